```python
import math
import jax, jax.numpy as jnp
from jax import lax
import numpy as np

D_MODEL = 1024
BATCH = 8
SEQ = 4096
DEPTH = 2

NORM_EPS = 1e-6
MLA_HEADS = 8
QK_NOPE_DIM = 64
QK_ROPE_DIM = 32
QK_HEAD_DIM = QK_NOPE_DIM + QK_ROPE_DIM
V_HEAD_DIM = 64
Q_LORA_RANK = 256
KV_LORA_RANK = 128
ROPE_THETA = 10000.0
Q_BLOCK = 128
S5_WIDTH = D_MODEL // 2
S5_GROUP = 16
S5_GROUPS = S5_WIDTH // S5_GROUP
S5_STATE = 64
S5_DT_MIN = 1e-3
S5_DT_MAX = 1e-1
POOL_WINDOWS = (2, 4, 8, 16)
POOL_WIDTH = D_MODEL // 2
POOL_GROUP = POOL_WIDTH // len(POOL_WINDOWS)
SGU_WIDTH = D_MODEL // 2
SGU_HEADS = 4
SGU_HEAD_DIM = SGU_WIDTH // SGU_HEADS
SGU_CHUNK = 128
N_EXPERTS = 32
TOP_K = 4
EXPERT_FF = D_MODEL
SWIGLU_ALPHA = 1.702
SWIGLU_LIMIT = 7.0
MOE_BLOCK = 128
EVEN_IN = Q_LORA_RANK + KV_LORA_RANK + QK_ROPE_DIM + S5_WIDTH
EVEN_MIX = MLA_HEADS * V_HEAD_DIM + S5_WIDTH
ODD_IN = POOL_WIDTH + 2 * SGU_WIDTH
ODD_MIX = POOL_WIDTH + SGU_WIDTH

kernel_name = "hybrid_mla_s5_pool_sgu_moe_adaln"


def rmsnorm(x, g):
    xf = x.astype(jnp.float32)
    y = xf * lax.rsqrt(jnp.mean(xf * xf, axis=-1, keepdims=True) + NORM_EPS)
    return (y * g.astype(jnp.float32)).astype(x.dtype)


def rope_tables(positions):
    half = QK_ROPE_DIM // 2
    inv_freq = 1.0 / (ROPE_THETA ** (jnp.arange(half, dtype=jnp.float32) / half))
    ang = positions.astype(jnp.float32)[..., None] * inv_freq
    return jnp.cos(ang), jnp.sin(ang)


def apply_rope(x, cos, sin):
    x1, x2 = jnp.split(x, 2, axis=-1)
    cs = cos[:, :, None, :]
    sn = sin[:, :, None, :]
    return jnp.concatenate([x1 * cs - x2 * sn, x1 * sn + x2 * cs], axis=-1).astype(x.dtype)


def causal_attention(q, k, v):
    B, S, H, Dq = q.shape
    nb = S // Q_BLOCK
    scale = Dq ** -0.5
    qb = q.reshape(B, nb, Q_BLOCK, H, Dq).swapaxes(0, 1)
    kpos = jnp.arange(S)

    def one_block(args):
        q_blk, i = args
        s = jnp.einsum('bqhd,bkhd->bhqk', q_blk, k, preferred_element_type=jnp.float32) * scale
        qpos = i * Q_BLOCK + jnp.arange(Q_BLOCK)
        s = jnp.where(kpos[None, :] <= qpos[:, None], s, -jnp.inf)
        p = jax.nn.softmax(s, axis=-1)
        return jnp.einsum('bhqk,bkhd->bqhd', p.astype(v.dtype), v)

    out = lax.map(one_block, (qb, jnp.arange(nb)))
    return out.swapaxes(0, 1).reshape(B, S, H, v.shape[-1])


def mla_mixer(q_c, kv_c, k_pe, cos, sin, q_norm_g, w_uq, kv_norm_g, w_ukv, q_head_g, k_head_g):
    B, S, _ = q_c.shape
    q = jnp.einsum('bsr,rhd->bshd', rmsnorm(q_c, q_norm_g), w_uq)
    kv = jnp.einsum('bsr,rhd->bshd', rmsnorm(kv_c, kv_norm_g), w_ukv)
    k_nope, v = kv[..., :QK_NOPE_DIM], kv[..., QK_NOPE_DIM:]
    k_rope = jnp.broadcast_to(k_pe[:, :, None, :], (B, S, MLA_HEADS, QK_ROPE_DIM))
    q = rmsnorm(q, q_head_g)
    k = rmsnorm(jnp.concatenate([k_nope, k_rope], axis=-1), k_head_g)
    q = jnp.concatenate([q[..., :QK_NOPE_DIM], apply_rope(q[..., QK_NOPE_DIM:], cos, sin)], axis=-1)
    k = jnp.concatenate([k[..., :QK_NOPE_DIM], apply_rope(k[..., QK_NOPE_DIM:], cos, sin)], axis=-1)
    return causal_attention(q, k, v).reshape(B, S, MLA_HEADS * V_HEAD_DIM)


def _complex_affine_combine(e1, e2):
    a1r, a1i, b1r, b1i = e1
    a2r, a2i, b2r, b2i = e2
    return (a1r * a2r - a1i * a2i,
            a1r * a2i + a1i * a2r,
            a2r * b1r - a2i * b1i + b2r,
            a2r * b1i + a2i * b1r + b2i)


def s5_mixer(u, a_re, a_im, log_dt, b_re, b_im, c_re, c_im, d_skip, glu_w, glu_b):
    B, S, W = u.shape
    f32 = jnp.float32
    uf = u.astype(f32)
    ug = uf.reshape(B, S, S5_GROUPS, S5_GROUP)
    dt = jnp.exp(log_dt.astype(f32))[:, None]
    lam_re = jnp.minimum(a_re.astype(f32), -1e-4)
    lam_im = a_im.astype(f32)
    mag = jnp.exp(lam_re * dt)
    ab_re = mag * jnp.cos(lam_im * dt)
    ab_im = mag * jnp.sin(lam_im * dt)
    den = lam_re * lam_re + lam_im * lam_im
    num_re = ab_re - 1.0
    f_re = (num_re * lam_re + ab_im * lam_im) / den
    f_im = (ab_im * lam_re - num_re * lam_im) / den
    br = b_re.astype(f32)
    bi = b_im.astype(f32)
    bb_re = f_re[..., None] * br - f_im[..., None] * bi
    bb_im = f_re[..., None] * bi + f_im[..., None] * br
    drive_re = jnp.einsum('bsgi,gpi->bsgp', ug, bb_re)
    drive_im = jnp.einsum('bsgi,gpi->bsgp', ug, bb_im)
    a_seq_re = jnp.broadcast_to(ab_re, (S,) + ab_re.shape)
    a_seq_im = jnp.broadcast_to(ab_im, (S,) + ab_im.shape)

    def scan_one(dr, di):
        _, _, xr, xi = lax.associative_scan(_complex_affine_combine, (a_seq_re, a_seq_im, dr, di), axis=0)
        return xr, xi

    st_re, st_im = jax.vmap(scan_one)(drive_re, drive_im)
    y = (jnp.einsum('gip,bsgp->bsgi', c_re.astype(f32), st_re)
         - jnp.einsum('gip,bsgp->bsgi', c_im.astype(f32), st_im))
    y = y.reshape(B, S, W) + d_skip.astype(f32) * uf
    g = jax.nn.gelu(y)
    out = g * jax.nn.sigmoid(g @ glu_w.astype(f32) + glu_b.astype(f32))
    return out.astype(u.dtype)


def pool_mixer(u, w_pool, pool_scale):
    B, S, _ = u.shape
    uf = u.astype(jnp.float32).reshape(B, S, len(POOL_WINDOWS), POOL_GROUP)
    csum = jnp.cumsum(uf, axis=1)
    t = jnp.arange(S)
    outs = []
    for gi, w in enumerate(POOL_WINDOWS):
        cg = csum[:, :, gi]
        lag = jnp.pad(cg, ((0, 0), (w, 0), (0, 0)))[:, :S]
        cnt = jnp.minimum(t + 1, w).astype(jnp.float32)[None, :, None]
        outs.append((cg - lag) / cnt - uf[:, :, gi])
    pooled = jnp.stack(outs, axis=2)
    mixed = jnp.einsum('bsgi,gio->bsgo', pooled, w_pool.astype(jnp.float32)).reshape(B, S, POOL_WIDTH)
    return (mixed * pool_scale.astype(jnp.float32)).astype(u.dtype)


def sgu_mixer(zu, zv, v_norm_g, w_sp, b_sp):
    B, S, _ = zu.shape
    nc = S // SGU_CHUNK
    u = jax.nn.gelu(zu)
    v = rmsnorm(jax.nn.gelu(zv), v_norm_g)
    v = v.reshape(B, nc, SGU_CHUNK, SGU_HEADS, SGU_HEAD_DIM)
    w_causal = jnp.tril(w_sp)
    mixed = jnp.einsum('hts,bnshd->bnthd', w_causal, v) + b_sp.T[None, None, :, :, None]
    return (u.reshape(B, nc, SGU_CHUNK, SGU_HEADS, SGU_HEAD_DIM) * mixed).reshape(B, S, SGU_WIDTH)


def clamped_swiglu(z):
    gate, lin = jnp.split(z, 2, axis=-1)
    gate = jnp.minimum(gate, SWIGLU_LIMIT)
    lin = jnp.clip(lin, -SWIGLU_LIMIT, SWIGLU_LIMIT)
    return gate * jax.nn.sigmoid(SWIGLU_ALPHA * gate) * (lin + 1.0)


def moe_ffn(h, router_w, router_b, w_gu, b_gu, w_dn, b_dn):
    B, S, Dm = h.shape
    x = h.reshape(-1, Dm)
    T = x.shape[0]
    logits = (x @ router_w + router_b).astype(jnp.float32)
    top_logit, top_e = lax.top_k(logits, TOP_K)
    gate = jax.nn.softmax(top_logit, axis=-1)
    flat_e = top_e.reshape(-1)
    flat_tok = jnp.repeat(jnp.arange(T, dtype=jnp.int32), TOP_K)
    flat_gate = gate.reshape(-1)
    order = jnp.argsort(flat_e)
    se, st, sg = flat_e[order], flat_tok[order], flat_gate[order]
    counts = jnp.bincount(flat_e, length=N_EXPERTS)
    padded = (counts + MOE_BLOCK - 1) // MOE_BLOCK * MOE_BLOCK
    pad_end = jnp.cumsum(padded)
    pad_start = pad_end - padded
    start = jnp.cumsum(counts) - counts
    dest = pad_start[se] + (jnp.arange(T * TOP_K) - start[se])
    n_blocks = -(-(T * TOP_K + N_EXPERTS * (MOE_BLOCK - 1)) // MOE_BLOCK)
    n_rows = n_blocks * MOE_BLOCK
    row_tok = jnp.zeros((n_rows,), jnp.int32).at[dest].set(st)
    row_gate = jnp.zeros((n_rows,), jnp.float32).at[dest].set(sg)
    block_e = jnp.minimum(jnp.searchsorted(pad_end, jnp.arange(n_blocks) * MOE_BLOCK, side='right'), N_EXPERTS - 1)
    xb = x[row_tok].reshape(n_blocks, MOE_BLOCK, Dm)

    def expert_block(args):
        xblk, e = args
        z = xblk @ w_gu[e] + b_gu[e]
        return clamped_swiglu(z) @ w_dn[e] + b_dn[e]

    yb = lax.map(expert_block, (xb, block_e)).reshape(n_rows, Dm)
    out = jnp.zeros((T, Dm), jnp.float32).at[row_tok].add(yb.astype(jnp.float32) * row_gate[:, None])
    return out.astype(h.dtype).reshape(B, S, Dm)


def setup_inputs(seed: int = 0) -> dict:
    key = jax.random.key(seed)
    ks = iter(jax.random.split(key, 64))
    f32 = jnp.float32
    D = D_MODEL
    ne = (DEPTH + 1) // 2
    no = DEPTH // 2

    def nrm(shape, s):
        return jax.random.normal(next(ks), shape, f32) * s

    def gain(shape):
        return 1.0 + nrm(shape, 0.1)

    x = nrm((BATCH, SEQ, D), 1.0)
    c = nrm((BATCH, D), 1.0)
    positions = (jax.random.randint(next(ks), (BATCH, 1), 0, 1024, jnp.int32)
                 + jnp.arange(SEQ, dtype=jnp.int32)[None, :])
    return {
        "x": x, "c": c, "positions": positions,
        "ada_w": nrm((DEPTH, D, 6 * D), 0.5 * D ** -0.5),
        "ada_b": nrm((DEPTH, 6 * D), 0.02),
        "norm_mix_g": gain((DEPTH, D)),
        "norm_ffn_g": gain((DEPTH, D)),
        "router_w": nrm((DEPTH, D, N_EXPERTS), D ** -0.5),
        "router_b": nrm((DEPTH, N_EXPERTS), 0.01),
        "moe_w_gu": nrm((DEPTH, N_EXPERTS, D, 2 * EXPERT_FF), D ** -0.5),
        "moe_b_gu": nrm((DEPTH, N_EXPERTS, 2 * EXPERT_FF), 0.01),
        "moe_w_dn": nrm((DEPTH, N_EXPERTS, EXPERT_FF, D), EXPERT_FF ** -0.5),
        "moe_b_dn": nrm((DEPTH, N_EXPERTS, D), 0.01),
        "even_w_in": nrm((ne, D, EVEN_IN), D ** -0.5),
        "mla_q_norm_g": gain((ne, Q_LORA_RANK)),
        "mla_w_uq": nrm((ne, Q_LORA_RANK, MLA_HEADS, QK_HEAD_DIM), Q_LORA_RANK ** -0.5),
        "mla_kv_norm_g": gain((ne, KV_LORA_RANK)),
        "mla_w_ukv": nrm((ne, KV_LORA_RANK, MLA_HEADS, QK_NOPE_DIM + V_HEAD_DIM), KV_LORA_RANK ** -0.5),
        "mla_q_head_g": gain((ne, QK_HEAD_DIM)),
        "mla_k_head_g": gain((ne, QK_HEAD_DIM)),
        "s5_a_re": -0.5 * (1.0 + nrm((ne, S5_GROUPS, S5_STATE), 0.01)),
        "s5_a_im": math.pi * jnp.arange(S5_STATE, dtype=f32) + nrm((ne, S5_GROUPS, S5_STATE), 0.01),
        "s5_log_dt": jax.random.uniform(next(ks), (ne, S5_GROUPS), f32, math.log(S5_DT_MIN), math.log(S5_DT_MAX)),
        "s5_b_re": nrm((ne, S5_GROUPS, S5_STATE, S5_GROUP), (2 * S5_GROUP) ** -0.5),
        "s5_b_im": nrm((ne, S5_GROUPS, S5_STATE, S5_GROUP), (2 * S5_GROUP) ** -0.5),
        "s5_c_re": nrm((ne, S5_GROUPS, S5_GROUP, S5_STATE), S5_STATE ** -0.5),
        "s5_c_im": nrm((ne, S5_GROUPS, S5_GROUP, S5_STATE), S5_STATE ** -0.5),
        "s5_d": nrm((ne, S5_WIDTH), 1.0),
        "s5_glu_w": nrm((ne, S5_WIDTH, S5_WIDTH), S5_WIDTH ** -0.5),
        "s5_glu_b": nrm((ne, S5_WIDTH), 0.02),
        "even_w_out": nrm((ne, EVEN_MIX, D), EVEN_MIX ** -0.5),
        "odd_w_in": nrm((no, D, ODD_IN), D ** -0.5),
        "pool_w": nrm((no, len(POOL_WINDOWS), POOL_GROUP, POOL_GROUP), POOL_GROUP ** -0.5),
        "pool_scale": gain((no, POOL_WIDTH)),
        "sgu_norm_g": gain((no, SGU_WIDTH)),
        "sgu_w": nrm((no, SGU_HEADS, SGU_CHUNK, SGU_CHUNK), SGU_CHUNK ** -0.5),
        "sgu_b": gain((no, SGU_HEADS, SGU_CHUNK)),
        "odd_w_out": nrm((no, ODD_MIX, D), ODD_MIX ** -0.5),
    }


def reference(x, c, positions, ada_w, ada_b, norm_mix_g, norm_ffn_g, router_w, router_b,
              moe_w_gu, moe_b_gu, moe_w_dn, moe_b_dn,
              even_w_in, mla_q_norm_g, mla_w_uq, mla_kv_norm_g, mla_w_ukv, mla_q_head_g, mla_k_head_g,
              s5_a_re, s5_a_im, s5_log_dt, s5_b_re, s5_b_im, s5_c_re, s5_c_im, s5_d, s5_glu_w, s5_glu_b,
              even_w_out,
              odd_w_in, pool_w, pool_scale, sgu_norm_g, sgu_w, sgu_b, odd_w_out):
    cos, sin = rope_tables(positions)
    c_act = jax.nn.silu(c)
    split_even = [Q_LORA_RANK, Q_LORA_RANK + KV_LORA_RANK, Q_LORA_RANK + KV_LORA_RANK + QK_ROPE_DIM]
    split_odd = [POOL_WIDTH, POOL_WIDTH + SGU_WIDTH]
    for layer in range(DEPTH):
        mod = (c_act @ ada_w[layer] + ada_b[layer])[:, None, :]
        sh_m, sc_m, g_m, sh_f, sc_f, g_f = jnp.split(mod, 6, axis=-1)
        h = rmsnorm(x, norm_mix_g[layer]) * (1.0 + sc_m) + sh_m
        i = layer // 2
        if layer % 2 == 0:
            z = h @ even_w_in[i]
            q_c, kv_c, k_pe, u_s5 = jnp.split(z, split_even, axis=-1)
            attn = mla_mixer(q_c, kv_c, k_pe, cos, sin, mla_q_norm_g[i], mla_w_uq[i],
                             mla_kv_norm_g[i], mla_w_ukv[i], mla_q_head_g[i], mla_k_head_g[i])
            ssm = s5_mixer(u_s5, s5_a_re[i], s5_a_im[i], s5_log_dt[i], s5_b_re[i], s5_b_im[i],
                           s5_c_re[i], s5_c_im[i], s5_d[i], s5_glu_w[i], s5_glu_b[i])
            mix = jnp.concatenate([attn, ssm.astype(attn.dtype)], axis=-1) @ even_w_out[i]
        else:
            z = h @ odd_w_in[i]
            u_pool, zu, zv = jnp.split(z, split_odd, axis=-1)
            pooled = pool_mixer(u_pool, pool_w[i], pool_scale[i])
            gated = sgu_mixer(zu, zv, sgu_norm_g[i], sgu_w[i], sgu_b[i])
            mix = jnp.concatenate([pooled, gated], axis=-1) @ odd_w_out[i]
        x = x + g_m * mix
        h = rmsnorm(x, norm_ffn_g[layer]) * (1.0 + sc_f) + sh_f
        x = x + g_f * moe_ffn(h, router_w[layer], router_b[layer], moe_w_gu[layer], moe_b_gu[layer],
                              moe_w_dn[layer], moe_b_dn[layer])
    return x
```

```python
import functools
import math

import jax
import jax.numpy as jnp
from jax import lax
from jax.experimental import pallas as pl
from jax.experimental.pallas import tpu as pltpu
from jax.experimental.pallas import tpu_sc as plsc

F32 = jnp.float32
BF16 = jnp.bfloat16
HIGHEST = lax.Precision.HIGHEST

NORM_EPS = 1e-6
MLA_HEADS = 8
QK_NOPE_DIM = 64
QK_ROPE_DIM = 32
QK_HEAD_DIM = QK_NOPE_DIM + QK_ROPE_DIM
V_HEAD_DIM = 64
Q_LORA_RANK = 256
KV_LORA_RANK = 128
ROPE_THETA = 10000.0
S5_GROUP = 16
S5_STATE = 64
POOL_WINDOWS = (2, 4, 8, 16)
SGU_HEADS = 4
SGU_CHUNK = 128
N_EXPERTS = 32
TOP_K = 4
SWIGLU_ALPHA = 1.702
SWIGLU_LIMIT = 7.0

LANES = 128
SUBLANES = 8
HEAD_SLAB = LANES
POOL_HALO = 16
ROW_TILE = 512
ATTN_TILE = 512
S5_STEPS = 64
MOE_ROWS = 512
SC_WORKERS = 32
SC_CHUNK = 64
VMEM_LIMIT = 56 * 1024 * 1024
NEG_BIG = -1e30


def _sigmoid(v):
    return 1.0 / (1.0 + jnp.exp(-v))


def _gelu(v):
    return 0.5 * v * (1.0 + jnp.tanh(math.sqrt(2.0 / math.pi) * (v + 0.044715 * (v * v * v))))


def _rms(v, width):
    return lax.rsqrt(jnp.sum(v * v, axis=-1, keepdims=True) * (1.0 / width) + NORM_EPS)


def _mod_norm(x, g, sc, sh):
    return x * _rms(x, x.shape[-1]) * g * (1.0 + sc) + sh


def _bdot(a, b):
    return jnp.dot(a.astype(BF16), b, preferred_element_type=F32)


def _pack_pairs(v):
    w = v.shape[-1] // 2
    bits = pltpu.bitcast(v.astype(BF16).astype(F32), jnp.uint32)
    return (bits[:, :w] >> 16) | bits[:, w:]


def _unpack_pairs(p):
    lo = pltpu.bitcast(p << 16, F32)
    hi = pltpu.bitcast(p & jnp.uint32(0xFFFF0000), F32)
    return lo.astype(BF16), hi.astype(BF16)


def _ada_kernel(c_ref, w_ref, b_ref, o_ref):
    c = c_ref[...]
    act = c * _sigmoid(c)
    o_ref[...] = jnp.dot(act, w_ref[...], precision=HIGHEST, preferred_element_type=F32) + b_ref[...]


def _ada_call(c, ada_w, ada_b):
    depth, d, n = ada_w.shape
    bsz = c.shape[0]
    tn = 1536
    return pl.pallas_call(
        _ada_kernel,
        grid=(depth, n // tn),
        in_specs=[
            pl.BlockSpec((bsz, d), lambda l, j: (0, 0)),
            pl.BlockSpec((None, d, tn), lambda l, j: (l, 0, j)),
            pl.BlockSpec((None, 1, tn), lambda l, j: (l, 0, j)),
        ],
        out_specs=pl.BlockSpec((None, bsz, tn), lambda l, j: (l, 0, j)),
        out_shape=jax.ShapeDtypeStruct((depth, bsz, n), F32),
        compiler_params=pltpu.CompilerParams(dimension_semantics=("parallel", "parallel"),
                                             vmem_limit_bytes=VMEM_LIMIT),
        name="ada_mod",
    )(c, ada_w, ada_b.reshape(depth, 1, n))


_C_Q = 0
_C_KV = Q_LORA_RANK
_C_PE = _C_KV + KV_LORA_RANK
_C_PESW = _C_PE + HEAD_SLAB
_C_U = _C_PESW + HEAD_SLAB


def _even_in_kernel(x_ref, mod_ref, pos_ref, g_ref, win_ref, gq_ref, wq_ref, gkv_ref, wk_ref, wv_ref,
                    tab_ref, q_ref, k_ref, v_ref, u_ref):
    x = x_ref[...]
    h = _mod_norm(x, g_ref[...], mod_ref[1:2, :], mod_ref[0:1, :])
    z = _bdot(h, win_ref[...])
    q_c = z[:, _C_Q:_C_KV]
    kv_c = z[:, _C_KV:_C_PE]
    kpe = z[:, _C_PE:_C_PESW]
    kpe_sw = z[:, _C_PESW:_C_U]
    u_ref[...] = z[:, _C_U:]

    ang = pos_ref[...] * tab_ref[0:1, :]
    cs = jnp.cos(ang)
    sn = jnp.sin(ang)
    gcq = cs * tab_ref[1:2, :]
    gsq = sn * tab_ref[2:3, :]
    gck = cs * tab_ref[3:4, :]
    gsk = sn * tab_ref[4:5, :]

    qn = q_c * _rms(q_c, Q_LORA_RANK) * gq_ref[...]
    qq = _bdot(qn, wq_ref[...])
    kvn = kv_c * _rms(kv_c, KV_LORA_RANK) * gkv_ref[...]
    kk = _bdot(kvn, wk_ref[...])
    v_ref[...] = _bdot(kvn, wv_ref[...]).astype(v_ref.dtype)

    pe_rot = kpe * gck + kpe_sw * gsk
    pe_ss = jnp.sum(kpe * kpe, axis=-1, keepdims=True)
    hw = MLA_HEADS * HEAD_SLAB
    for hd in range(MLA_HEADS):
        lo = hd * HEAD_SLAB
        qr = qq[:, lo:lo + HEAD_SLAB]
        qs = qq[:, hw + lo:hw + lo + HEAD_SLAB]
        rq = _rms(qr, QK_HEAD_DIM)
        q_ref[hd] = (rq * (qr * gcq + qs * gsq)).astype(q_ref.dtype)
        kr = kk[:, lo:lo + HEAD_SLAB]
        rk = lax.rsqrt((jnp.sum(kr * kr, axis=-1, keepdims=True) + pe_ss) * (1.0 / QK_HEAD_DIM) + NORM_EPS)
        k_ref[hd] = (rk * (kr * gck + pe_rot)).astype(k_ref.dtype)


def _even_in_call(x, mod, posf, g, prep):
    bsz, seq, d = x.shape
    tm = min(ROW_TILE, seq)
    hw = MLA_HEADS * HEAD_SLAB
    full = lambda a: pl.BlockSpec(a.shape, lambda b, s: (0,) * a.ndim)
    return pl.pallas_call(
        _even_in_kernel,
        grid=(bsz, seq // tm),
        in_specs=[
            pl.BlockSpec((None, tm, d), lambda b, s: (b, s, 0)),
            pl.BlockSpec((None, 6, d), lambda b, s: (b, 0, 0)),
            pl.BlockSpec((None, tm, 1), lambda b, s: (b, s, 0)),
            full(g), full(prep["w_in"]), full(prep["gq"]), full(prep["wq"]), full(prep["gkv"]),
            full(prep["wk"]), full(prep["wv"]), full(prep["tab"]),
        ],
        out_specs=[
            pl.BlockSpec((None, MLA_HEADS, tm, HEAD_SLAB), lambda b, s: (b, 0, s, 0)),
            pl.BlockSpec((None, MLA_HEADS, tm, HEAD_SLAB), lambda b, s: (b, 0, s, 0)),
            pl.BlockSpec((None, tm, MLA_HEADS * V_HEAD_DIM), lambda b, s: (b, s, 0)),
            pl.BlockSpec((tm, d // 2), lambda b, s: (s, b)),
        ],
        out_shape=[
            jax.ShapeDtypeStruct((bsz, MLA_HEADS, seq, HEAD_SLAB), BF16),
            jax.ShapeDtypeStruct((bsz, MLA_HEADS, seq, HEAD_SLAB), BF16),
            jax.ShapeDtypeStruct((bsz, seq, MLA_HEADS * V_HEAD_DIM), BF16),
            jax.ShapeDtypeStruct((seq, bsz * (d // 2)), F32),
        ],
        compiler_params=pltpu.CompilerParams(dimension_semantics=("parallel", "parallel"),
                                             vmem_limit_bytes=VMEM_LIMIT),
        name="even_in",
    )(x, mod, posf, g, prep["w_in"], prep["gq"], prep["wq"], prep["gkv"], prep["wk"], prep["wv"], prep["tab"])


def _prep_even(even_w_in, q_norm_g, w_uq, kv_norm_g, w_ukv, q_head_g, k_head_g):
    d = even_w_in.shape[0]
    half = QK_ROPE_DIM // 2
    nope = QK_NOPE_DIM
    c_pe = Q_LORA_RANK + KV_LORA_RANK
    w_pe = even_w_in[:, c_pe:c_pe + QK_ROPE_DIM]
    zeros = lambda n: jnp.zeros((d, n), F32)
    pe_slab = jnp.concatenate([zeros(nope), w_pe, zeros(HEAD_SLAB - QK_HEAD_DIM)], axis=1)
    pe_sw = jnp.concatenate([zeros(nope), -w_pe[:, half:], w_pe[:, :half], zeros(HEAD_SLAB - QK_HEAD_DIM)], axis=1)
    w_in = jnp.concatenate([even_w_in[:, :c_pe], pe_slab, pe_sw, even_w_in[:, c_pe + QK_ROPE_DIM:]], axis=1)

    r = w_uq.shape[0]
    padq = jnp.zeros((r, MLA_HEADS, HEAD_SLAB - QK_HEAD_DIM), F32)
    wq_plain = jnp.concatenate([w_uq, padq], axis=2).reshape(r, MLA_HEADS * HEAD_SLAB)
    wq_sw = jnp.concatenate([jnp.zeros((r, MLA_HEADS, nope), F32), -w_uq[:, :, nope + half:],
                             w_uq[:, :, nope:nope + half], padq], axis=2).reshape(r, MLA_HEADS * HEAD_SLAB)
    wq = jnp.concatenate([wq_plain, wq_sw], axis=1)

    rk = w_ukv.shape[0]
    wk = jnp.concatenate([w_ukv[:, :, :nope], jnp.zeros((rk, MLA_HEADS, HEAD_SLAB - nope), F32)],
                         axis=2).reshape(rk, MLA_HEADS * HEAD_SLAB)
    wv = w_ukv[:, :, nope:].reshape(rk, MLA_HEADS * V_HEAD_DIM)

    inv_freq = 1.0 / (ROPE_THETA ** (jnp.arange(half, dtype=F32) / half))
    pad_tail = jnp.zeros((HEAD_SLAB - QK_HEAD_DIM,), F32)
    freq_row = jnp.concatenate([jnp.zeros((nope,), F32), inv_freq, inv_freq, pad_tail])

    def gain_rows(gv, scale):
        plain = jnp.concatenate([gv, pad_tail]) * scale
        swapped = jnp.concatenate([jnp.zeros((nope,), F32), gv[nope + half:], gv[nope:nope + half], pad_tail]) * scale
        return plain, swapped

    gq_plain, gq_sw = gain_rows(q_head_g, QK_HEAD_DIM ** -0.5)
    gk_plain, gk_sw = gain_rows(k_head_g, 1.0)
    tab = jnp.stack([freq_row, gq_plain, gq_sw, gk_plain, gk_sw, freq_row * 0, freq_row * 0, freq_row * 0])
    return {
        "w_in": w_in.astype(BF16), "gq": q_norm_g.reshape(1, -1), "wq": wq.astype(BF16),
        "gkv": kv_norm_g.reshape(1, -1), "wk": wk.astype(BF16), "wv": wv.astype(BF16), "tab": tab,
    }


def _attn_kernel(q_ref, k_ref, v_ref, o_ref, m_sc, l_sc, acc_sc, *, tq, tk):
    i = pl.program_id(2)
    j = pl.program_id(3)

    @pl.when(j == 0)
    def _():
        m_sc[...] = jnp.full(m_sc.shape, -jnp.inf, F32)
        l_sc[...] = jnp.zeros(l_sc.shape, F32)
        acc_sc[...] = jnp.zeros(acc_sc.shape, F32)

    @pl.when(j <= i)
    def _():
        v = v_ref[...]
        lane = lax.broadcasted_iota(jnp.int32, (1, LANES), 1)
        row = i * tq + lax.broadcasted_iota(jnp.int32, (tq, tk), 0)
        col = j * tk + lax.broadcasted_iota(jnp.int32, (tq, tk), 1)
        keep = col <= row
        alphas = []
        pvs = []
        for hh in range(2):
            s = lax.dot_general(q_ref[hh], k_ref[hh], (((1,), (1,)), ((), ())), preferred_element_type=F32)
            s = jnp.where(keep, s, -jnp.inf)
            m_prev = m_sc[hh]
            m_new = jnp.maximum(m_prev, jnp.max(s, axis=-1, keepdims=True))
            alpha = jnp.exp(m_prev - m_new)
            p = jnp.exp(s - m_new[:, 0:1])
            l_sc[hh] = alpha * l_sc[hh] + jnp.sum(p, axis=-1, keepdims=True)
            m_sc[hh] = m_new
            vh = jnp.where((lane < V_HEAD_DIM) == (hh == 0), v, jnp.zeros_like(v))
            pvs.append(jnp.dot(p.astype(BF16), vh, preferred_element_type=F32))
            alphas.append(alpha)
        a_l = jnp.where(lane < V_HEAD_DIM, alphas[0], alphas[1])
        acc_sc[...] = acc_sc[...] * a_l + pvs[0] + pvs[1]

    @pl.when(j == pl.num_programs(3) - 1)
    def _():
        lane = lax.broadcasted_iota(jnp.int32, (1, LANES), 1)
        l_l = jnp.where(lane < V_HEAD_DIM, l_sc[0], l_sc[1])
        o_ref[...] = (acc_sc[...] / l_l).astype(o_ref.dtype)


def _attn_call(q, k, v):
    bsz, nh, seq, _ = q.shape
    tq = tk = min(ATTN_TILE, seq)
    kern = functools.partial(_attn_kernel, tq=tq, tk=tk)
    return pl.pallas_call(
        kern,
        grid=(bsz, nh // 2, seq // tq, seq // tk),
        in_specs=[
            pl.BlockSpec((None, 2, tq, HEAD_SLAB), lambda b, h, i, j: (b, h, i, 0)),
            pl.BlockSpec((None, 2, tk, HEAD_SLAB), lambda b, h, i, j: (b, h, jnp.minimum(i, j), 0)),
            pl.BlockSpec((None, tk, 2 * V_HEAD_DIM), lambda b, h, i, j: (b, jnp.minimum(i, j), h)),
        ],
        out_specs=pl.BlockSpec((None, tq, 2 * V_HEAD_DIM), lambda b, h, i, j: (b, i, h)),
        out_shape=jax.ShapeDtypeStruct((bsz, seq, nh * V_HEAD_DIM), BF16),
        scratch_shapes=[pltpu.VMEM((2, tq, LANES), F32), pltpu.VMEM((2, tq, LANES), F32),
                        pltpu.VMEM((tq, LANES), F32)],
        compiler_params=pltpu.CompilerParams(
            dimension_semantics=("parallel", "parallel", "parallel", "arbitrary"),
            vmem_limit_bytes=VMEM_LIMIT),
        name="mla_attention",
    )(q, k, v)


def _s5_disc_kernel(are_ref, aim_ref, ldt_ref, bre_ref, bim_ref, abre_ref, abim_ref, bbre_ref, bbim_ref):
    dt = jnp.exp(ldt_ref[...])
    lam_re = jnp.minimum(are_ref[...], -1e-4)
    lam_im = aim_ref[...]
    mag = jnp.exp(lam_re * dt)
    ab_re = mag * jnp.cos(lam_im * dt)
    ab_im = mag * jnp.sin(lam_im * dt)
    den = lam_re * lam_re + lam_im * lam_im
    num_re = ab_re - 1.0
    f_re = (num_re * lam_re + ab_im * lam_im) / den
    f_im = (ab_im * lam_re - num_re * lam_im) / den
    abre_ref[...] = ab_re
    abim_ref[...] = ab_im
    br = bre_ref[...]
    bi = bim_ref[...]
    bbre_ref[...] = f_re[:, None, :] * br - f_im[:, None, :] * bi
    bbim_ref[...] = f_re[:, None, :] * bi + f_im[:, None, :] * br


def _s5_disc_call(a_re, a_im, log_dt, b_re, b_im):
    g, p = a_re.shape
    bre_t = jnp.swapaxes(b_re, 1, 2)
    bim_t = jnp.swapaxes(b_im, 1, 2)
    return pl.pallas_call(
        _s5_disc_kernel,
        out_shape=[jax.ShapeDtypeStruct((g, p), F32), jax.ShapeDtypeStruct((g, p), F32),
                   jax.ShapeDtypeStruct(bre_t.shape, F32), jax.ShapeDtypeStruct(bre_t.shape, F32)],
        name="s5_discretize",
    )(a_re, a_im, log_dt.reshape(g, 1), bre_t, bim_t)


def _block_diag_halves(m):
    g, r, c = m.shape
    gh = g // 2
    eye = jnp.eye(gh, dtype=m.dtype)
    mh = m.reshape(2, gh, r, c)
    return (mh[:, :, :, None, :] * eye[None, :, None, :, None]).reshape(2, gh * r, gh * c)


def _s5_kernel(u_ref, bre_ref, bim_ref, are_ref, aim_ref, cre_ref, cim_ref, d_ref, gw_ref, gb_ref,
               o_ref, sre, sim, dre, dim, *, steps):
    @pl.when(pl.program_id(0) == 0)
    def _():
        sre[...] = jnp.zeros(sre.shape, F32)
        sim[...] = jnp.zeros(sim.shape, F32)

    rows = steps * SUBLANES
    w = u_ref.shape[-1]
    u = u_ref[...].reshape(rows, w)
    ub = u.astype(BF16)
    kh = w // 2
    nh = dre.shape[1] // 2
    for hf in range(2):
        dre[:, hf * nh:(hf + 1) * nh] = jnp.dot(ub[:, hf * kh:(hf + 1) * kh], bre_ref[hf], preferred_element_type=F32)
        dim[:, hf * nh:(hf + 1) * nh] = jnp.dot(ub[:, hf * kh:(hf + 1) * kh], bim_ref[hf], preferred_element_type=F32)

    a_r = are_ref[...]
    a_i = aim_ref[...]

    def body(t, carry):
        xr, xi = carry
        r0 = pl.multiple_of(t * SUBLANES, SUBLANES)
        nr = a_r * xr - a_i * xi + dre[pl.ds(r0, SUBLANES), :]
        ni = a_r * xi + a_i * xr + dim[pl.ds(r0, SUBLANES), :]
        dre[pl.ds(r0, SUBLANES), :] = nr
        dim[pl.ds(r0, SUBLANES), :] = ni
        return nr, ni

    xr, xi = lax.fori_loop(0, steps, body, (sre[...], sim[...]))
    sre[...] = xr
    sim[...] = xi

    ys = []
    for hf in range(2):
        yr = jnp.dot(dre[:, hf * nh:(hf + 1) * nh].astype(BF16), cre_ref[hf], preferred_element_type=F32)
        yi = jnp.dot(dim[:, hf * nh:(hf + 1) * nh].astype(BF16), cim_ref[hf], preferred_element_type=F32)
        ys.append(yr - yi)
    y = jnp.concatenate(ys, axis=1) + d_ref[...] * u
    g = _gelu(y)
    out = g * _sigmoid(_bdot(g, gw_ref[...]) + gb_ref[...])
    o_ref[...] = out.reshape(steps, SUBLANES, w).astype(o_ref.dtype)


def _s5_call(u_t, disc, c_re, c_im, d_skip, glu_w, glu_b):
    seq, bsz, w = u_t.shape
    assert bsz == SUBLANES
    ab_re, ab_im, bb_re, bb_im = disc
    g, p = ab_re.shape
    n_state = g * p
    bre = _block_diag_halves(bb_re).astype(BF16)
    bim = _block_diag_halves(bb_im).astype(BF16)
    cre = _block_diag_halves(jnp.swapaxes(c_re, 1, 2)).astype(BF16)
    cim = _block_diag_halves(jnp.swapaxes(c_im, 1, 2)).astype(BF16)
    steps = min(S5_STEPS, seq)
    full = lambda a: pl.BlockSpec(a.shape, lambda s: (0,) * a.ndim)
    args = (bre, bim, ab_re.reshape(1, n_state), ab_im.reshape(1, n_state), cre, cim,
            d_skip.reshape(1, w), glu_w.astype(BF16), glu_b.reshape(1, w))
    return pl.pallas_call(
        functools.partial(_s5_kernel, steps=steps),
        grid=(seq // steps,),
        in_specs=[pl.BlockSpec((steps, bsz, w), lambda s: (s, 0, 0))] + [full(a) for a in args],
        out_specs=pl.BlockSpec((steps, bsz, w), lambda s: (s, 0, 0)),
        out_shape=jax.ShapeDtypeStruct((seq, bsz, w), BF16),
        scratch_shapes=[pltpu.VMEM((bsz, n_state), F32), pltpu.VMEM((bsz, n_state), F32),
                        pltpu.VMEM((steps * bsz, n_state), F32), pltpu.VMEM((steps * bsz, n_state), F32)],
        compiler_params=pltpu.CompilerParams(dimension_semantics=("arbitrary",), vmem_limit_bytes=VMEM_LIMIT),
        name="s5_scan",
    )(u_t, *args)


def _router_tail(x_new, mod_ref, gf_ref, rw_ref, rb_ref, h2_ref, te_ref):
    h2 = _mod_norm(x_new, gf_ref[...], mod_ref[4:5, :], mod_ref[3:4, :])
    h2_ref[...] = _pack_pairs(h2)
    logits = jnp.dot(h2, rw_ref[...], precision=HIGHEST, preferred_element_type=F32) + rb_ref[...]
    lane = lax.broadcasted_iota(jnp.int32, logits.shape, 1).astype(F32)
    vals = []
    idxs = []
    work = logits
    for _ in range(TOP_K):
        m = jnp.max(work, axis=-1, keepdims=True)
        idx = jnp.min(jnp.where(work == m, lane, float(LANES)), axis=-1, keepdims=True)
        vals.append(m)
        idxs.append(idx)
        work = jnp.where(lane == idx, NEG_BIG * 2.0, work)
    exps = [jnp.exp(vv - vals[0]) for vv in vals]
    tot = exps[0] + exps[1] + exps[2] + exps[3]
    te = jnp.zeros(logits.shape, F32)
    for kk in range(TOP_K):
        te = jnp.where(lane == float(kk), idxs[kk], te)
        te = jnp.where(lane == float(TOP_K + kk), exps[kk] / tot, te)
    te_ref[...] = te[:, :2 * TOP_K]


def _mix_out_kernel(x_ref, a_ref, s_ref, mod_ref, wo_ref, gf_ref, rw_ref, rb_ref, xo_ref, h2_ref, te_ref):
    ka = a_ref.shape[-1]
    mix = jnp.dot(a_ref[...], wo_ref[:ka, :], preferred_element_type=F32)
    mix = mix + jnp.dot(s_ref[...], wo_ref[ka:, :], preferred_element_type=F32)
    x_new = x_ref[...] + mod_ref[2:3, :] * mix
    xo_ref[...] = x_new
    _router_tail(x_new, mod_ref, gf_ref, rw_ref, rb_ref, h2_ref, te_ref)


def _router_pad(router_w, router_b):
    d, e = router_w.shape
    rw = jnp.concatenate([router_w, jnp.zeros((d, LANES - e), F32)], axis=1)
    rb = jnp.concatenate([router_b, jnp.full((LANES - e,), NEG_BIG, F32)]).reshape(1, LANES)
    return rw, rb


def _mix_out_call(x, attn, ssm_t, mod, w_out, gf, rw, rb):
    bsz, seq, d = x.shape
    tm = min(ROW_TILE, seq)
    ka = attn.shape[-1]
    ks = ssm_t.shape[-1] // bsz
    full = lambda a: pl.BlockSpec(a.shape, lambda b, s: (0,) * a.ndim)
    return pl.pallas_call(
        _mix_out_kernel,
        grid=(bsz, seq // tm),
        in_specs=[
            pl.BlockSpec((None, tm, d), lambda b, s: (b, s, 0)),
            pl.BlockSpec((None, tm, ka), lambda b, s: (b, s, 0)),
            pl.BlockSpec((tm, ks), lambda b, s: (s, b)),
            pl.BlockSpec((None, 6, d), lambda b, s: (b, 0, 0)),
            full(w_out), full(gf), full(rw), full(rb),
        ],
        out_specs=[
            pl.BlockSpec((None, tm, d), lambda b, s: (b, s, 0)),
            pl.BlockSpec((tm, d // 2), lambda b, s: (b * (seq // tm) + s, 0)),
            pl.BlockSpec((tm, 2 * TOP_K), lambda b, s: (b * (seq // tm) + s, 0)),
        ],
        out_shape=[
            jax.ShapeDtypeStruct((bsz, seq, d), F32),
            jax.ShapeDtypeStruct((bsz * seq, d // 2), jnp.uint32),
            jax.ShapeDtypeStruct((bsz * seq, 2 * TOP_K), F32),
        ],
        compiler_params=pltpu.CompilerParams(dimension_semantics=("parallel", "parallel"),
                                             vmem_limit_bytes=VMEM_LIMIT),
        name="even_out",
    )(x, attn, ssm_t, mod, w_out, gf, rw, rb)


def _odd_kernel(x_ref, mod_ref, g_ref, win_ref, icnt_ref, wp_ref, ps_ref, gv_ref, wsp_ref, bsp_ref,
                wo_ref, gf_ref, rw_ref, rb_ref, xo_ref, h2_ref, te_ref, ext_sc):
    tm = x_ref.shape[0]
    pw = wp_ref.shape[-1]
    width = pw * len(POOL_WINDOWS)

    @pl.when(pl.program_id(1) == 0)
    def _():
        ext_sc[0:POOL_HALO, :] = jnp.zeros((POOL_HALO, width), F32)

    x = x_ref[...]
    h = _mod_norm(x, g_ref[...], mod_ref[1:2, :], mod_ref[0:1, :])
    z = _bdot(h, win_ref[...])
    up = z[:, :width]
    ext_sc[POOL_HALO:POOL_HALO + tm, :] = up

    pooled = []
    for gi, win in enumerate(POOL_WINDOWS):
        cols = slice(gi * pw, (gi + 1) * pw)
        acc = up[:, cols]
        for lag in range(1, win):
            acc = acc + ext_sc[POOL_HALO - lag:POOL_HALO - lag + tm, cols]
        pg = acc * icnt_ref[:, gi:gi + 1] - up[:, cols]
        pooled.append(_bdot(pg, wp_ref[gi]) * ps_ref[:, cols])
    ext_sc[0:POOL_HALO, :] = ext_sc[tm:tm + POOL_HALO, :]
    pooled = jnp.concatenate(pooled, axis=1)

    ug = _gelu(z[:, width:2 * width])
    vg = _gelu(z[:, 2 * width:])
    vn = (vg * _rms(vg, width) * gv_ref[...]).astype(BF16)
    hd = width // SGU_HEADS
    chunks = []
    for ci in range(tm // SGU_CHUNK):
        heads = []
        for hh in range(SGU_HEADS):
            blk = vn[ci * SGU_CHUNK:(ci + 1) * SGU_CHUNK, hh * hd:(hh + 1) * hd]
            heads.append(jnp.dot(wsp_ref[hh], blk, preferred_element_type=F32) + bsp_ref[hh])
        chunks.append(jnp.concatenate(heads, axis=1))
    gated = ug * jnp.concatenate(chunks, axis=0)

    mix = _bdot(pooled, wo_ref[:width, :]) + _bdot(gated, wo_ref[width:, :])
    x_new = x + mod_ref[2:3, :] * mix
    xo_ref[...] = x_new
    _router_tail(x_new, mod_ref, gf_ref, rw_ref, rb_ref, h2_ref, te_ref)


def _odd_call(x, mod, g, w_in, pool_w, pool_scale, sgu_norm_g, sgu_w, sgu_b, w_out, gf, rw, rb):
    bsz, seq, d = x.shape
    tm = min(ROW_TILE, seq)
    width = pool_scale.shape[0]
    hd = width // SGU_HEADS
    t = jnp.arange(seq, dtype=jnp.int32)
    icnt = jnp.stack([1.0 / jnp.minimum(t + 1, wn).astype(F32) for wn in POOL_WINDOWS], axis=1)
    wsp = jnp.tril(sgu_w).astype(BF16)
    bsp = jnp.broadcast_to(sgu_b[:, :, None], (SGU_HEADS, SGU_CHUNK, hd))
    args = (g, w_in.astype(BF16), icnt, pool_w.astype(BF16), pool_scale.reshape(1, width),
            sgu_norm_g.reshape(1, width), wsp, bsp, w_out.astype(BF16), gf, rw, rb)
    full = lambda a: pl.BlockSpec(a.shape, lambda b, s: (0,) * a.ndim)
    in_specs = [pl.BlockSpec((None, tm, d), lambda b, s: (b, s, 0)),
                pl.BlockSpec((None, 6, d), lambda b, s: (b, 0, 0))]
    for idx, a in enumerate(args):
        in_specs.append(pl.BlockSpec((tm, len(POOL_WINDOWS)), lambda b, s: (s, 0)) if idx == 2 else full(a))
    return pl.pallas_call(
        _odd_kernel,
        grid=(bsz, seq // tm),
        in_specs=in_specs,
        out_specs=[
            pl.BlockSpec((None, tm, d), lambda b, s: (b, s, 0)),
            pl.BlockSpec((tm, d // 2), lambda b, s: (b * (seq // tm) + s, 0)),
            pl.BlockSpec((tm, 2 * TOP_K), lambda b, s: (b * (seq // tm) + s, 0)),
        ],
        out_shape=[
            jax.ShapeDtypeStruct((bsz, seq, d), F32),
            jax.ShapeDtypeStruct((bsz * seq, d // 2), jnp.uint32),
            jax.ShapeDtypeStruct((bsz * seq, 2 * TOP_K), F32),
        ],
        scratch_shapes=[pltpu.VMEM((tm + POOL_HALO, width), F32)],
        compiler_params=pltpu.CompilerParams(dimension_semantics=("parallel", "arbitrary"),
                                             vmem_limit_bytes=VMEM_LIMIT),
        name="odd_mixer",
    )(x, mod, *args)


def _rank_kernel(te_ref, rk_ref, cnt_ref, carry):
    @pl.when(pl.program_id(0) == 0)
    def _():
        carry[...] = jnp.zeros(carry.shape, F32)

    tr = te_ref.shape[0]
    te = te_ref[...]
    lane = lax.broadcasted_iota(jnp.int32, (tr, LANES), 1).astype(F32)
    hots = [te[:, kk:kk + 1] == lane for kk in range(TOP_K)]
    oh = jnp.zeros((tr, LANES), F32)
    for hot in hots:
        oh = oh + jnp.where(hot, 1.0, 0.0)
    r_i = lax.broadcasted_iota(jnp.int32, (tr, tr), 0)
    c_i = lax.broadcasted_iota(jnp.int32, (tr, tr), 1)
    tri = jnp.where(c_i < r_i, 1.0, 0.0).astype(BF16)
    before = jnp.dot(tri, oh.astype(BF16), preferred_element_type=F32) + carry[...]
    lane8 = lax.broadcasted_iota(jnp.int32, (tr, LANES), 1)
    rk = jnp.zeros((tr, LANES), F32)
    for kk, hot in enumerate(hots):
        rk = jnp.where(lane8 == kk, jnp.sum(jnp.where(hot, before, 0.0), axis=-1, keepdims=True), rk)
    rk_ref[...] = rk[:, :2 * TOP_K]
    carry[...] = carry[...] + jnp.sum(oh, axis=0, keepdims=True)
    cnt_ref[...] = carry[...]


def _rank_call(te):
    t = te.shape[0]
    tr = min(ROW_TILE, t)
    return pl.pallas_call(
        _rank_kernel,
        grid=(t // tr,),
        in_specs=[pl.BlockSpec((tr, 2 * TOP_K), lambda i: (i, 0))],
        out_specs=[pl.BlockSpec((tr, 2 * TOP_K), lambda i: (i, 0)), pl.BlockSpec((1, LANES), lambda i: (0, 0))],
        out_shape=[jax.ShapeDtypeStruct((t, 2 * TOP_K), F32), jax.ShapeDtypeStruct((1, LANES), F32)],
        scratch_shapes=[pltpu.VMEM((1, LANES), F32)],
        compiler_params=pltpu.CompilerParams(dimension_semantics=("arbitrary",)),
        name="route_rank",
    )(te)


def _sc_gather(table, idx):
    n = idx.shape[0]
    per_w = n // SC_WORKERS
    assert per_w * SC_WORKERS == n and per_w % SC_CHUNK == 0
    n_chunks = per_w // SC_CHUNK
    row_shape = table.shape[1:]
    mesh = plsc.VectorSubcoreMesh(core_axis_name="c", subcore_axis_name="s")

    @functools.partial(
        pl.kernel, mesh=mesh,
        out_type=jax.ShapeDtypeStruct((n,) + row_shape, table.dtype),
        scratch_types=[pltpu.VMEM((SC_CHUNK,), jnp.int32), pltpu.VMEM((SC_CHUNK,) + row_shape, table.dtype),
                       pltpu.SemaphoreType.DMA],
        name="sc_row_gather",
    )
    def gather(table_hbm, idx_hbm, out_hbm, idx_v, rows_v, sem):
        wid = lax.axis_index("s") * 2 + lax.axis_index("c")
        base = wid * per_w

        @pl.loop(0, n_chunks)
        def _(ci):
            off = pl.multiple_of(base + ci * SC_CHUNK, SC_CHUNK)
            pltpu.sync_copy(idx_hbm.at[pl.ds(off, SC_CHUNK)], idx_v)
            pltpu.async_copy(table_hbm.at[idx_v], rows_v, sem).wait()
            pltpu.sync_copy(rows_v, out_hbm.at[pl.ds(off, SC_CHUNK)])

    return gather(table, idx)


def _expert_kernel(be_ref, nu_ref, x_ref, wgu_ref, bgu_ref, wdn_ref, bdn_ref, y_ref, wgu_bf, wdn_bf):
    i = pl.program_id(0)
    used = i < nu_ref[0]
    prev = be_ref[jnp.maximum(i - 1, 0)]
    fresh = jnp.logical_or(i == 0, be_ref[i] != prev)

    @pl.when(jnp.logical_and(used, fresh))
    def _():
        wgu_bf[...] = wgu_ref[...].astype(BF16)
        wdn_bf[...] = wdn_ref[...].astype(BF16)

    @pl.when(used)
    def _():
        x_lo, x_hi = _unpack_pairs(x_ref[...])
        dh = x_lo.shape[-1]
        z = jnp.dot(x_lo, wgu_bf[:dh, :], preferred_element_type=F32)
        z = z + jnp.dot(x_hi, wgu_bf[dh:, :], preferred_element_type=F32) + bgu_ref[...]
        ff = z.shape[-1] // 2
        gate = jnp.minimum(z[:, :ff], SWIGLU_LIMIT)
        lin = jnp.clip(z[:, ff:], -SWIGLU_LIMIT, SWIGLU_LIMIT)
        act = gate * _sigmoid(SWIGLU_ALPHA * gate) * (lin + 1.0)
        y = _bdot(act, wdn_bf[...]) + bdn_ref[...]
        y_ref[...] = _pack_pairs(y)


def _expert_call(block_e, n_used, xs, w_gu, b_gu, w_dn, b_dn):
    n_rows, dh = xs.shape
    e, d, ff2 = w_gu.shape
    ff = ff2 // 2
    nb = n_rows // MOE_ROWS
    row_map = lambda i, be, nu: (jnp.minimum(i, nu[0] - 1), 0)
    return pl.pallas_call(
        _expert_kernel,
        grid_spec=pltpu.PrefetchScalarGridSpec(
            num_scalar_prefetch=2,
            grid=(nb,),
            in_specs=[
                pl.BlockSpec((MOE_ROWS, dh), row_map),
                pl.BlockSpec((None, d, ff2), lambda i, be, nu: (be[i], 0, 0)),
                pl.BlockSpec((None, 1, ff2), lambda i, be, nu: (be[i], 0, 0)),
                pl.BlockSpec((None, ff, d), lambda i, be, nu: (be[i], 0, 0)),
                pl.BlockSpec((None, 1, d), lambda i, be, nu: (be[i], 0, 0)),
            ],
            out_specs=pl.BlockSpec((MOE_ROWS, dh), row_map),
            scratch_shapes=[pltpu.VMEM((d, ff2), BF16), pltpu.VMEM((ff, d), BF16)],
        ),
        out_shape=jax.ShapeDtypeStruct((n_rows, dh), jnp.uint32),
        compiler_params=pltpu.CompilerParams(dimension_semantics=("arbitrary",), vmem_limit_bytes=VMEM_LIMIT),
        name="moe_experts",
    )(block_e, n_used, xs, w_gu, b_gu.reshape(e, 1, ff2), w_dn, b_dn.reshape(e, 1, d))


def _combine_kernel(x_ref, yg_ref, te_ref, mod_ref, o_ref):
    te = te_ref[...]
    acc_lo = None
    for kk in range(TOP_K):
        lo, hi = _unpack_pairs(yg_ref[kk])
        gate = te[:, TOP_K + kk:TOP_K + kk + 1]
        acc_lo = gate * lo.astype(F32) if acc_lo is None else acc_lo + gate * lo.astype(F32)
        acc_hi = gate * hi.astype(F32) if kk == 0 else acc_hi + gate * hi.astype(F32)
    acc = jnp.concatenate([acc_lo, acc_hi], axis=1)
    o_ref[...] = x_ref[...] + mod_ref[5:6, :] * acc


def _combine_call(x, yg, te, mod):
    bsz, seq, d = x.shape
    tm = min(ROW_TILE, seq)
    nt = seq // tm
    return pl.pallas_call(
        _combine_kernel,
        grid=(bsz, nt),
        in_specs=[
            pl.BlockSpec((None, tm, d), lambda b, s: (b, s, 0)),
            pl.BlockSpec((TOP_K, tm, d // 2), lambda b, s: (0, b * nt + s, 0)),
            pl.BlockSpec((tm, 2 * TOP_K), lambda b, s: (b * nt + s, 0)),
            pl.BlockSpec((None, 6, d), lambda b, s: (b, 0, 0)),
        ],
        out_specs=pl.BlockSpec((None, tm, d), lambda b, s: (b, s, 0)),
        out_shape=jax.ShapeDtypeStruct((bsz, seq, d), F32),
        compiler_params=pltpu.CompilerParams(dimension_semantics=("parallel", "parallel"),
                                             vmem_limit_bytes=VMEM_LIMIT),
        name="moe_combine",
    )(x, yg, te, mod)


def _moe(x_new, h2, te, mod, w_gu, b_gu, w_dn, b_dn):
    t, dh = h2.shape
    rk, cnt = _rank_call(te)
    exp_id = te[:, :TOP_K].astype(jnp.int32)
    rank = rk[:, :TOP_K].astype(jnp.int32)
    counts = cnt[0, :N_EXPERTS].astype(jnp.int32)
    padded = (counts + MOE_ROWS - 1) // MOE_ROWS * MOE_ROWS
    pad_end = jnp.cumsum(padded)
    pad_start = pad_end - padded
    dest = pad_start[exp_id] + rank
    chunk = MOE_ROWS * SC_WORKERS * SC_CHUNK // math.gcd(MOE_ROWS, SC_WORKERS * SC_CHUNK)
    n_rows = -(-(t * TOP_K + N_EXPERTS * (MOE_ROWS - 1)) // chunk) * chunk
    nb = n_rows // MOE_ROWS
    n_used = (pad_end[-1] // MOE_ROWS).astype(jnp.int32).reshape(1)
    block_e = jnp.minimum(jnp.searchsorted(pad_end, jnp.arange(nb, dtype=jnp.int32) * MOE_ROWS, side="right"),
                          N_EXPERTS - 1).astype(jnp.int32)
    tok = jnp.broadcast_to(jnp.arange(t, dtype=jnp.int32)[:, None], (t, TOP_K))
    row_tok = jnp.zeros((n_rows,), jnp.int32).at[dest.reshape(-1)].set(tok.reshape(-1))
    xs = _sc_gather(h2, row_tok)
    y = _expert_call(block_e, n_used, xs, w_gu, b_gu, w_dn, b_dn)
    yg = _sc_gather(y, dest.T.reshape(-1)).reshape(TOP_K, t, dh)
    return _combine_call(x_new, yg, te, mod)


def kernel(x, c, positions, ada_w, ada_b, norm_mix_g, norm_ffn_g, router_w, router_b, moe_w_gu, moe_b_gu,
           moe_w_dn, moe_b_dn, even_w_in, mla_q_norm_g, mla_w_uq, mla_kv_norm_g, mla_w_ukv, mla_q_head_g,
           mla_k_head_g, s5_a_re, s5_a_im, s5_log_dt, s5_b_re, s5_b_im, s5_c_re, s5_c_im, s5_d, s5_glu_w,
           s5_glu_b, even_w_out, odd_w_in, pool_w, pool_scale, sgu_norm_g, sgu_w, sgu_b, odd_w_out):
    bsz, seq, d = x.shape
    depth = ada_w.shape[0]
    mods = _ada_call(c, ada_w, ada_b).reshape(depth, bsz, 6, d)
    posf = positions.astype(F32).reshape(bsz, seq, 1)
    for layer in range(depth):
        mod = mods[layer]
        i = layer // 2
        g_mix = norm_mix_g[layer].reshape(1, d)
        g_ffn = norm_ffn_g[layer].reshape(1, d)
        rw, rb = _router_pad(router_w[layer], router_b[layer])
        if layer % 2 == 0:
            prep = _prep_even(even_w_in[i], mla_q_norm_g[i], mla_w_uq[i], mla_kv_norm_g[i], mla_w_ukv[i],
                              mla_q_head_g[i], mla_k_head_g[i])
            q, k, v, u_t = _even_in_call(x, mod, posf, g_mix, prep)
            attn = _attn_call(q, k, v)
            disc = _s5_disc_call(s5_a_re[i], s5_a_im[i], s5_log_dt[i], s5_b_re[i], s5_b_im[i])
            ssm_t = _s5_call(u_t.reshape(seq, bsz, d // 2), disc, s5_c_re[i], s5_c_im[i], s5_d[i],
                             s5_glu_w[i], s5_glu_b[i])
            x_new, h2, te = _mix_out_call(x, attn, ssm_t.reshape(seq, bsz * (d // 2)), mod,
                                          even_w_out[i].astype(BF16), g_ffn, rw, rb)
        else:
            x_new, h2, te = _odd_call(x, mod, g_mix, odd_w_in[i], pool_w[i], pool_scale[i], sgu_norm_g[i],
                                      sgu_w[i], sgu_b[i], odd_w_out[i], g_ffn, rw, rb)
        x = _moe(x_new, h2, te, mod, moe_w_gu[layer], moe_b_gu[layer], moe_w_dn[layer], moe_b_dn[layer])
    return x
```

```python
import functools
import math

import jax
import jax.numpy as jnp
from jax import lax
from jax.experimental import pallas as pl
from jax.experimental.pallas import tpu as pltpu
from jax.experimental.pallas import tpu_sc as plsc

F32 = jnp.float32
BF16 = jnp.bfloat16
HIGHEST = lax.Precision.HIGHEST

NORM_EPS = 1e-6
MLA_HEADS = 8
QK_NOPE_DIM = 64
QK_ROPE_DIM = 32
QK_HEAD_DIM = QK_NOPE_DIM + QK_ROPE_DIM
V_HEAD_DIM = 64
Q_LORA_RANK = 256
KV_LORA_RANK = 128
ROPE_THETA = 10000.0
S5_GROUP = 16
S5_STATE = 64
POOL_WINDOWS = (2, 4, 8, 16)
SGU_HEADS = 4
SGU_CHUNK = 128
N_EXPERTS = 32
TOP_K = 4
SWIGLU_ALPHA = 1.702
SWIGLU_LIMIT = 7.0

LANES = 128
SUBLANES = 8
HEAD_SLAB = LANES
POOL_HALO = 16
ROW_TILE = 512
ATTN_TILE = 512
ATTN_ROWS = 32
S5_STEPS = 64
MOE_ROWS = 512
SC_WORKERS = 32
SC_CHUNK = 64
VMEM_LIMIT = 56 * 1024 * 1024
NEG_BIG = -1e30


def _sigmoid(v):
    return 1.0 / (1.0 + jnp.exp(-v))


def _gelu(v):
    return 0.5 * v * (1.0 + jnp.tanh(math.sqrt(2.0 / math.pi) * (v + 0.044715 * (v * v * v))))


def _rms(v, width):
    return lax.rsqrt(jnp.sum(v * v, axis=-1, keepdims=True) * (1.0 / width) + NORM_EPS)


def _mod_norm(x, g, sc, sh):
    return x * _rms(x, x.shape[-1]) * g * (1.0 + sc) + sh


def _bdot(a, b):
    return jnp.dot(a.astype(BF16), b, preferred_element_type=F32)


def _pack_pairs(v):
    w = v.shape[-1] // 2
    bits = pltpu.bitcast(v.astype(BF16).astype(F32), jnp.uint32)
    return (bits[:, :w] >> 16) | bits[:, w:]


def _unpack_pairs(p):
    lo = pltpu.bitcast(p << 16, F32)
    hi = pltpu.bitcast(p & jnp.uint32(0xFFFF0000), F32)
    return lo.astype(BF16), hi.astype(BF16)


def _ada_kernel(c_ref, w_ref, b_ref, o_ref):
    c = c_ref[...]
    act = c * _sigmoid(c)
    o_ref[...] = jnp.dot(act, w_ref[...], precision=HIGHEST, preferred_element_type=F32) + b_ref[...]


def _ada_call(c, ada_w, ada_b):
    depth, d, n = ada_w.shape
    bsz = c.shape[0]
    tn = 1536
    return pl.pallas_call(
        _ada_kernel,
        grid=(depth, n // tn),
        in_specs=[
            pl.BlockSpec((bsz, d), lambda l, j: (0, 0)),
            pl.BlockSpec((None, d, tn), lambda l, j: (l, 0, j)),
            pl.BlockSpec((None, 1, tn), lambda l, j: (l, 0, j)),
        ],
        out_specs=pl.BlockSpec((None, bsz, tn), lambda l, j: (l, 0, j)),
        out_shape=jax.ShapeDtypeStruct((depth, bsz, n), F32),
        compiler_params=pltpu.CompilerParams(dimension_semantics=("parallel", "parallel"),
                                             vmem_limit_bytes=VMEM_LIMIT),
        name="ada_mod",
    )(c, ada_w, ada_b.reshape(depth, 1, n))


_C_Q = 0
_C_KV = Q_LORA_RANK
_C_PE = _C_KV + KV_LORA_RANK
_C_PESW = _C_PE + HEAD_SLAB
_C_U = _C_PESW + HEAD_SLAB


def _even_in_kernel(x_ref, mod_ref, pos_ref, g_ref, win_ref, gq_ref, wq_ref, gkv_ref, wk_ref, wv_ref,
                    tab_ref, q_ref, k_ref, v_ref, u_ref):
    x = x_ref[...]
    h = _mod_norm(x, g_ref[...], mod_ref[1:2, :], mod_ref[0:1, :])
    z = _bdot(h, win_ref[...])
    q_c = z[:, _C_Q:_C_KV]
    kv_c = z[:, _C_KV:_C_PE]
    kpe = z[:, _C_PE:_C_PESW]
    kpe_sw = z[:, _C_PESW:_C_U]
    u_ref[...] = z[:, _C_U:]

    ang = pos_ref[...] * tab_ref[0:1, :]
    cs = jnp.cos(ang)
    sn = jnp.sin(ang)
    gcq = cs * tab_ref[1:2, :]
    gsq = sn * tab_ref[2:3, :]
    gck = cs * tab_ref[3:4, :]
    gsk = sn * tab_ref[4:5, :]

    qn = q_c * _rms(q_c, Q_LORA_RANK) * gq_ref[...]
    qq = _bdot(qn, wq_ref[...])
    kvn = kv_c * _rms(kv_c, KV_LORA_RANK) * gkv_ref[...]
    kk = _bdot(kvn, wk_ref[...])
    v_ref[...] = _bdot(kvn, wv_ref[...]).astype(v_ref.dtype)

    pe_rot = kpe * gck + kpe_sw * gsk
    pe_ss = jnp.sum(kpe * kpe, axis=-1, keepdims=True)
    hw = MLA_HEADS * HEAD_SLAB
    for hd in range(MLA_HEADS):
        lo = hd * HEAD_SLAB
        qr = qq[:, lo:lo + HEAD_SLAB]
        qs = qq[:, hw + lo:hw + lo + HEAD_SLAB]
        rq = _rms(qr, QK_HEAD_DIM)
        q_ref[hd] = (rq * (qr * gcq + qs * gsq)).astype(q_ref.dtype)
        kr = kk[:, lo:lo + HEAD_SLAB]
        rk = lax.rsqrt((jnp.sum(kr * kr, axis=-1, keepdims=True) + pe_ss) * (1.0 / QK_HEAD_DIM) + NORM_EPS)
        k_ref[hd] = (rk * (kr * gck + pe_rot)).astype(k_ref.dtype)


def _even_in_call(x, mod, posf, g, prep):
    bsz, seq, d = x.shape
    tm = min(ROW_TILE, seq)
    hw = MLA_HEADS * HEAD_SLAB
    full = lambda a: pl.BlockSpec(a.shape, lambda b, s: (0,) * a.ndim)
    return pl.pallas_call(
        _even_in_kernel,
        grid=(bsz, seq // tm),
        in_specs=[
            pl.BlockSpec((None, tm, d), lambda b, s: (b, s, 0)),
            pl.BlockSpec((None, 6, d), lambda b, s: (b, 0, 0)),
            pl.BlockSpec((None, tm, 1), lambda b, s: (b, s, 0)),
            full(g), full(prep["w_in"]), full(prep["gq"]), full(prep["wq"]), full(prep["gkv"]),
            full(prep["wk"]), full(prep["wv"]), full(prep["tab"]),
        ],
        out_specs=[
            pl.BlockSpec((None, MLA_HEADS, tm, HEAD_SLAB), lambda b, s: (b, 0, s, 0)),
            pl.BlockSpec((None, MLA_HEADS, tm, HEAD_SLAB), lambda b, s: (b, 0, s, 0)),
            pl.BlockSpec((None, tm, MLA_HEADS * V_HEAD_DIM), lambda b, s: (b, s, 0)),
            pl.BlockSpec((tm, d // 2), lambda b, s: (s, b)),
        ],
        out_shape=[
            jax.ShapeDtypeStruct((bsz, MLA_HEADS, seq, HEAD_SLAB), BF16),
            jax.ShapeDtypeStruct((bsz, MLA_HEADS, seq, HEAD_SLAB), BF16),
            jax.ShapeDtypeStruct((bsz, seq, MLA_HEADS * V_HEAD_DIM), BF16),
            jax.ShapeDtypeStruct((seq, bsz * (d // 2)), F32),
        ],
        compiler_params=pltpu.CompilerParams(dimension_semantics=("parallel", "parallel"),
                                             vmem_limit_bytes=VMEM_LIMIT),
        name="even_in",
    )(x, mod, posf, g, prep["w_in"], prep["gq"], prep["wq"], prep["gkv"], prep["wk"], prep["wv"], prep["tab"])


def _prep_even(even_w_in, q_norm_g, w_uq, kv_norm_g, w_ukv, q_head_g, k_head_g):
    d = even_w_in.shape[0]
    half = QK_ROPE_DIM // 2
    nope = QK_NOPE_DIM
    c_pe = Q_LORA_RANK + KV_LORA_RANK
    w_pe = even_w_in[:, c_pe:c_pe + QK_ROPE_DIM]
    zeros = lambda n: jnp.zeros((d, n), F32)
    pe_slab = jnp.concatenate([zeros(nope), w_pe, zeros(HEAD_SLAB - QK_HEAD_DIM)], axis=1)
    pe_sw = jnp.concatenate([zeros(nope), -w_pe[:, half:], w_pe[:, :half], zeros(HEAD_SLAB - QK_HEAD_DIM)], axis=1)
    w_in = jnp.concatenate([even_w_in[:, :c_pe], pe_slab, pe_sw, even_w_in[:, c_pe + QK_ROPE_DIM:]], axis=1)

    r = w_uq.shape[0]
    padq = jnp.zeros((r, MLA_HEADS, HEAD_SLAB - QK_HEAD_DIM), F32)
    wq_plain = jnp.concatenate([w_uq, padq], axis=2).reshape(r, MLA_HEADS * HEAD_SLAB)
    wq_sw = jnp.concatenate([jnp.zeros((r, MLA_HEADS, nope), F32), -w_uq[:, :, nope + half:],
                             w_uq[:, :, nope:nope + half], padq], axis=2).reshape(r, MLA_HEADS * HEAD_SLAB)
    wq = jnp.concatenate([wq_plain, wq_sw], axis=1)

    rk = w_ukv.shape[0]
    wk = jnp.concatenate([w_ukv[:, :, :nope], jnp.zeros((rk, MLA_HEADS, HEAD_SLAB - nope), F32)],
                         axis=2).reshape(rk, MLA_HEADS * HEAD_SLAB)
    wv = w_ukv[:, :, nope:].reshape(rk, MLA_HEADS * V_HEAD_DIM)

    inv_freq = 1.0 / (ROPE_THETA ** (jnp.arange(half, dtype=F32) / half))
    pad_tail = jnp.zeros((HEAD_SLAB - QK_HEAD_DIM,), F32)
    freq_row = jnp.concatenate([jnp.zeros((nope,), F32), inv_freq, inv_freq, pad_tail])

    def gain_rows(gv, scale):
        plain = jnp.concatenate([gv, pad_tail]) * scale
        swapped = jnp.concatenate([jnp.zeros((nope,), F32), gv[nope + half:], gv[nope:nope + half], pad_tail]) * scale
        return plain, swapped

    gq_plain, gq_sw = gain_rows(q_head_g, QK_HEAD_DIM ** -0.5 * math.log2(math.e))
    gk_plain, gk_sw = gain_rows(k_head_g, 1.0)
    tab = jnp.stack([freq_row, gq_plain, gq_sw, gk_plain, gk_sw, freq_row * 0, freq_row * 0, freq_row * 0])
    return {
        "w_in": w_in.astype(BF16), "gq": q_norm_g.reshape(1, -1), "wq": wq.astype(BF16),
        "gkv": kv_norm_g.reshape(1, -1), "wk": wk.astype(BF16), "wv": wv.astype(BF16), "tab": tab,
    }


def _attn_kernel(qi_ref, kj_ref, q_ref, k_ref, v_ref, o_ref, m_sc, l_sc, a_sc, acc_sc, s_sc, p_sc, *, tq, tk):
    step = pl.program_id(2)
    i = qi_ref[step]
    j = kj_ref[step]

    @pl.when(j == 0)
    def _():
        m_sc[...] = jnp.full(m_sc.shape, -jnp.inf, F32)
        l_sc[...] = jnp.zeros(l_sc.shape, F32)
        acc_sc[...] = jnp.zeros(acc_sc.shape, F32)

    def sweep(on_diagonal):
        v = v_ref[...]
        lane = lax.broadcasted_iota(jnp.int32, (1, LANES), 1)
        pvs = []
        for hh in range(2):
            s_sc[hh] = lax.dot_general(q_ref[hh], k_ref[hh], (((1,), (1,)), ((), ())),
                                       preferred_element_type=F32)
            for r0 in range(0, tq, ATTN_ROWS):
                rows = pl.ds(r0, ATTN_ROWS)
                s = s_sc[hh, rows, :]
                if on_diagonal:
                    row = r0 + lax.broadcasted_iota(jnp.int32, (ATTN_ROWS, tk), 0)
                    col = lax.broadcasted_iota(jnp.int32, (ATTN_ROWS, tk), 1)
                    s = jnp.where(col <= row, s, -jnp.inf)
                m_prev = m_sc[hh, rows, :]
                m_new = jnp.maximum(m_prev, jnp.max(s, axis=-1, keepdims=True))
                alpha = jnp.exp2(m_prev - m_new)
                p = jnp.exp2(s - jnp.concatenate([m_new] * (tk // LANES), axis=1))
                l_sc[hh, rows, :] = alpha * l_sc[hh, rows, :] + jnp.sum(p, axis=-1, keepdims=True)
                m_sc[hh, rows, :] = m_new
                a_sc[hh, rows, :] = alpha
                p_sc[hh, rows, :] = p.astype(BF16)
            vh = jnp.where((lane < V_HEAD_DIM) == (hh == 0), v, jnp.zeros_like(v))
            pvs.append(jnp.dot(p_sc[hh], vh, preferred_element_type=F32))
        a_l = jnp.where(lane < V_HEAD_DIM, a_sc[0], a_sc[1])
        acc_sc[...] = acc_sc[...] * a_l + pvs[0] + pvs[1]

    @pl.when(j < i)
    def _():
        sweep(False)

    @pl.when(j == i)
    def _():
        sweep(True)
        lane = lax.broadcasted_iota(jnp.int32, (1, LANES), 1)
        l_l = jnp.where(lane < V_HEAD_DIM, l_sc[0], l_sc[1])
        o_ref[...] = (acc_sc[...] / l_l).astype(o_ref.dtype)


def _attn_call(q, k, v):
    bsz, nh, seq, _ = q.shape
    tq = tk = min(ATTN_TILE, seq)
    nq = seq // tq
    pairs = [(i, j) for i in range(nq) for j in range(i + 1)]
    qi = jnp.asarray([p[0] for p in pairs], jnp.int32)
    kj = jnp.asarray([p[1] for p in pairs], jnp.int32)
    kern = functools.partial(_attn_kernel, tq=tq, tk=tk)
    return pl.pallas_call(
        kern,
        grid_spec=pltpu.PrefetchScalarGridSpec(
            num_scalar_prefetch=2,
            grid=(bsz, nh // 2, len(pairs)),
            in_specs=[
                pl.BlockSpec((None, 2, tq, HEAD_SLAB), lambda b, h, p, qi, kj: (b, h, qi[p], 0)),
                pl.BlockSpec((None, 2, tk, HEAD_SLAB), lambda b, h, p, qi, kj: (b, h, kj[p], 0)),
                pl.BlockSpec((None, tk, 2 * V_HEAD_DIM), lambda b, h, p, qi, kj: (b, kj[p], h)),
            ],
            out_specs=pl.BlockSpec((None, tq, 2 * V_HEAD_DIM), lambda b, h, p, qi, kj: (b, qi[p], h)),
            scratch_shapes=[pltpu.VMEM((2, tq, LANES), F32), pltpu.VMEM((2, tq, LANES), F32),
                            pltpu.VMEM((2, tq, LANES), F32), pltpu.VMEM((tq, LANES), F32),
                            pltpu.VMEM((2, tq, tk), F32), pltpu.VMEM((2, tq, tk), BF16)],
        ),
        out_shape=jax.ShapeDtypeStruct((bsz, seq, nh * V_HEAD_DIM), BF16),
        compiler_params=pltpu.CompilerParams(
            dimension_semantics=("parallel", "parallel", "arbitrary"),
            vmem_limit_bytes=VMEM_LIMIT),
        name="mla_attention",
    )(qi, kj, q, k, v)


def _s5_disc_kernel(are_ref, aim_ref, ldt_ref, bre_ref, bim_ref, abre_ref, abim_ref, bbre_ref, bbim_ref):
    dt = jnp.exp(ldt_ref[...])
    lam_re = jnp.minimum(are_ref[...], -1e-4)
    lam_im = aim_ref[...]
    mag = jnp.exp(lam_re * dt)
    ab_re = mag * jnp.cos(lam_im * dt)
    ab_im = mag * jnp.sin(lam_im * dt)
    den = lam_re * lam_re + lam_im * lam_im
    num_re = ab_re - 1.0
    f_re = (num_re * lam_re + ab_im * lam_im) / den
    f_im = (ab_im * lam_re - num_re * lam_im) / den
    abre_ref[...] = ab_re
    abim_ref[...] = ab_im
    br = bre_ref[...]
    bi = bim_ref[...]
    bbre_ref[...] = f_re[:, None, :] * br - f_im[:, None, :] * bi
    bbim_ref[...] = f_re[:, None, :] * bi + f_im[:, None, :] * br


def _s5_disc_call(a_re, a_im, log_dt, b_re, b_im):
    g, p = a_re.shape
    bre_t = jnp.swapaxes(b_re, 1, 2)
    bim_t = jnp.swapaxes(b_im, 1, 2)
    return pl.pallas_call(
        _s5_disc_kernel,
        out_shape=[jax.ShapeDtypeStruct((g, p), F32), jax.ShapeDtypeStruct((g, p), F32),
                   jax.ShapeDtypeStruct(bre_t.shape, F32), jax.ShapeDtypeStruct(bre_t.shape, F32)],
        name="s5_discretize",
    )(a_re, a_im, log_dt.reshape(g, 1), bre_t, bim_t)


def _block_diag_halves(m):
    g, r, c = m.shape
    gh = g // 2
    eye = jnp.eye(gh, dtype=m.dtype)
    mh = m.reshape(2, gh, r, c)
    return (mh[:, :, :, None, :] * eye[None, :, None, :, None]).reshape(2, gh * r, gh * c)


def _s5_kernel(u_ref, bre_ref, bim_ref, are_ref, aim_ref, cre_ref, cim_ref, d_ref, gw_ref, gb_ref,
               o_ref, sre, sim, dre, dim, *, steps):
    @pl.when(pl.program_id(0) == 0)
    def _():
        sre[...] = jnp.zeros(sre.shape, F32)
        sim[...] = jnp.zeros(sim.shape, F32)

    rows = steps * SUBLANES
    w = u_ref.shape[-1]
    u = u_ref[...].reshape(rows, w)
    ub = u.astype(BF16)
    kh = w // 2
    nh = dre.shape[1] // 2
    for hf in range(2):
        dre[:, hf * nh:(hf + 1) * nh] = jnp.dot(ub[:, hf * kh:(hf + 1) * kh], bre_ref[hf], preferred_element_type=F32)
        dim[:, hf * nh:(hf + 1) * nh] = jnp.dot(ub[:, hf * kh:(hf + 1) * kh], bim_ref[hf], preferred_element_type=F32)

    a_r = are_ref[...]
    a_i = aim_ref[...]

    def body(t, carry):
        xr, xi = carry
        r0 = pl.multiple_of(t * SUBLANES, SUBLANES)
        nr = a_r * xr - a_i * xi + dre[pl.ds(r0, SUBLANES), :]
        ni = a_r * xi + a_i * xr + dim[pl.ds(r0, SUBLANES), :]
        dre[pl.ds(r0, SUBLANES), :] = nr
        dim[pl.ds(r0, SUBLANES), :] = ni
        return nr, ni

    xr, xi = lax.fori_loop(0, steps, body, (sre[...], sim[...]))
    sre[...] = xr
    sim[...] = xi

    ys = []
    for hf in range(2):
        yr = jnp.dot(dre[:, hf * nh:(hf + 1) * nh].astype(BF16), cre_ref[hf], preferred_element_type=F32)
        yi = jnp.dot(dim[:, hf * nh:(hf + 1) * nh].astype(BF16), cim_ref[hf], preferred_element_type=F32)
        ys.append(yr - yi)
    y = jnp.concatenate(ys, axis=1) + d_ref[...] * u
    g = _gelu(y)
    out = g * _sigmoid(_bdot(g, gw_ref[...]) + gb_ref[...])
    o_ref[...] = out.reshape(steps, SUBLANES, w).astype(o_ref.dtype)


def _s5_call(u_t, disc, c_re, c_im, d_skip, glu_w, glu_b):
    seq, bsz, w = u_t.shape
    assert bsz == SUBLANES
    ab_re, ab_im, bb_re, bb_im = disc
    g, p = ab_re.shape
    n_state = g * p
    bre = _block_diag_halves(bb_re).astype(BF16)
    bim = _block_diag_halves(bb_im).astype(BF16)
    cre = _block_diag_halves(jnp.swapaxes(c_re, 1, 2)).astype(BF16)
    cim = _block_diag_halves(jnp.swapaxes(c_im, 1, 2)).astype(BF16)
    steps = min(S5_STEPS, seq)
    full = lambda a: pl.BlockSpec(a.shape, lambda s: (0,) * a.ndim)
    args = (bre, bim, ab_re.reshape(1, n_state), ab_im.reshape(1, n_state), cre, cim,
            d_skip.reshape(1, w), glu_w.astype(BF16), glu_b.reshape(1, w))
    return pl.pallas_call(
        functools.partial(_s5_kernel, steps=steps),
        grid=(seq // steps,),
        in_specs=[pl.BlockSpec((steps, bsz, w), lambda s: (s, 0, 0))] + [full(a) for a in args],
        out_specs=pl.BlockSpec((steps, bsz, w), lambda s: (s, 0, 0)),
        out_shape=jax.ShapeDtypeStruct((seq, bsz, w), BF16),
        scratch_shapes=[pltpu.VMEM((bsz, n_state), F32), pltpu.VMEM((bsz, n_state), F32),
                        pltpu.VMEM((steps * bsz, n_state), F32), pltpu.VMEM((steps * bsz, n_state), F32)],
        compiler_params=pltpu.CompilerParams(dimension_semantics=("arbitrary",), vmem_limit_bytes=VMEM_LIMIT),
        name="s5_scan",
    )(u_t, *args)


def _router_tail(x_new, mod_ref, gf_ref, rw_ref, rb_ref, h2_ref, te_ref, cnt_ref):
    h2 = _mod_norm(x_new, gf_ref[...], mod_ref[4:5, :], mod_ref[3:4, :])
    h2_ref[...] = _pack_pairs(h2)
    logits = jnp.dot(h2, rw_ref[...], precision=HIGHEST, preferred_element_type=F32) + rb_ref[...]
    lane = lax.broadcasted_iota(jnp.int32, logits.shape, 1).astype(F32)
    vals = []
    idxs = []
    work = logits
    for _ in range(TOP_K):
        m = jnp.max(work, axis=-1, keepdims=True)
        idx = jnp.min(jnp.where(work == m, lane, float(LANES)), axis=-1, keepdims=True)
        vals.append(m)
        idxs.append(idx)
        work = jnp.where(lane == idx, NEG_BIG * 2.0, work)
    exps = [jnp.exp(vv - vals[0]) for vv in vals]
    tot = exps[0] + exps[1] + exps[2] + exps[3]
    te = jnp.zeros(logits.shape, F32)
    picked = jnp.zeros(logits.shape, F32)
    for kk in range(TOP_K):
        te = jnp.where(lane == float(kk), idxs[kk], te)
        te = jnp.where(lane == float(TOP_K + kk), exps[kk] / tot, te)
        picked = picked + jnp.where(lane == idxs[kk], 1.0, 0.0)
    te_ref[...] = te[:, :2 * TOP_K]
    cnt_ref[...] = jnp.sum(picked, axis=0, keepdims=True)


def _mix_out_kernel(x_ref, a_ref, s_ref, mod_ref, wo_ref, gf_ref, rw_ref, rb_ref, xo_ref, h2_ref, te_ref,
                    cnt_ref):
    ka = a_ref.shape[-1]
    mix = jnp.dot(a_ref[...], wo_ref[:ka, :], preferred_element_type=F32)
    mix = mix + jnp.dot(s_ref[...], wo_ref[ka:, :], preferred_element_type=F32)
    x_new = x_ref[...] + mod_ref[2:3, :] * mix
    xo_ref[...] = x_new
    _router_tail(x_new, mod_ref, gf_ref, rw_ref, rb_ref, h2_ref, te_ref, cnt_ref)


def _tail_out_specs(bsz, seq, tm, d):
    nt = seq // tm
    specs = [
        pl.BlockSpec((None, tm, d), lambda b, s: (b, s, 0)),
        pl.BlockSpec((tm, d // 2), lambda b, s: (b * nt + s, 0)),
        pl.BlockSpec((tm, 2 * TOP_K), lambda b, s: (b * nt + s, 0)),
        pl.BlockSpec((None, 1, LANES), lambda b, s: (b * nt + s, 0, 0)),
    ]
    shapes = [
        jax.ShapeDtypeStruct((bsz, seq, d), F32),
        jax.ShapeDtypeStruct((bsz * seq, d // 2), jnp.uint32),
        jax.ShapeDtypeStruct((bsz * seq, 2 * TOP_K), F32),
        jax.ShapeDtypeStruct((bsz * nt, 1, LANES), F32),
    ]
    return specs, shapes


def _router_pad(router_w, router_b):
    d, e = router_w.shape
    rw = jnp.concatenate([router_w, jnp.zeros((d, LANES - e), F32)], axis=1)
    rb = jnp.concatenate([router_b, jnp.full((LANES - e,), NEG_BIG, F32)]).reshape(1, LANES)
    return rw, rb


def _mix_out_call(x, attn, ssm_t, mod, w_out, gf, rw, rb):
    bsz, seq, d = x.shape
    tm = min(ROW_TILE, seq)
    ka = attn.shape[-1]
    ks = ssm_t.shape[-1] // bsz
    full = lambda a: pl.BlockSpec(a.shape, lambda b, s: (0,) * a.ndim)
    out_specs, out_shape = _tail_out_specs(bsz, seq, tm, d)
    return pl.pallas_call(
        _mix_out_kernel,
        grid=(bsz, seq // tm),
        in_specs=[
            pl.BlockSpec((None, tm, d), lambda b, s: (b, s, 0)),
            pl.BlockSpec((None, tm, ka), lambda b, s: (b, s, 0)),
            pl.BlockSpec((tm, ks), lambda b, s: (s, b)),
            pl.BlockSpec((None, 6, d), lambda b, s: (b, 0, 0)),
            full(w_out), full(gf), full(rw), full(rb),
        ],
        out_specs=out_specs,
        out_shape=out_shape,
        compiler_params=pltpu.CompilerParams(dimension_semantics=("parallel", "parallel"),
                                             vmem_limit_bytes=VMEM_LIMIT),
        name="even_out",
    )(x, attn, ssm_t, mod, w_out, gf, rw, rb)


def _odd_kernel(x_ref, mod_ref, g_ref, win_ref, icnt_ref, wp_ref, ps_ref, gv_ref, wsp_ref, bsp_ref,
                wo_ref, gf_ref, rw_ref, rb_ref, xo_ref, h2_ref, te_ref, cnt_ref, ext_sc):
    tm = x_ref.shape[0]
    pw = wp_ref.shape[-1]
    width = pw * len(POOL_WINDOWS)

    @pl.when(pl.program_id(1) == 0)
    def _():
        ext_sc[0:POOL_HALO, :] = jnp.zeros((POOL_HALO, width), F32)

    x = x_ref[...]
    h = _mod_norm(x, g_ref[...], mod_ref[1:2, :], mod_ref[0:1, :])
    z = _bdot(h, win_ref[...])
    up = z[:, :width]
    ext_sc[POOL_HALO:POOL_HALO + tm, :] = up

    pooled = []
    for gi, win in enumerate(POOL_WINDOWS):
        cols = slice(gi * pw, (gi + 1) * pw)
        acc = up[:, cols]
        for lag in range(1, win):
            acc = acc + ext_sc[POOL_HALO - lag:POOL_HALO - lag + tm, cols]
        pg = acc * icnt_ref[:, gi:gi + 1] - up[:, cols]
        pooled.append(_bdot(pg, wp_ref[gi]) * ps_ref[:, cols])
    ext_sc[0:POOL_HALO, :] = ext_sc[tm:tm + POOL_HALO, :]
    pooled = jnp.concatenate(pooled, axis=1)

    ug = _gelu(z[:, width:2 * width])
    vg = _gelu(z[:, 2 * width:])
    vn = (vg * _rms(vg, width) * gv_ref[...]).astype(BF16)
    hd = width // SGU_HEADS
    chunks = []
    for ci in range(tm // SGU_CHUNK):
        heads = []
        for hh in range(SGU_HEADS):
            blk = vn[ci * SGU_CHUNK:(ci + 1) * SGU_CHUNK, hh * hd:(hh + 1) * hd]
            heads.append(jnp.dot(wsp_ref[hh], blk, preferred_element_type=F32) + bsp_ref[hh])
        chunks.append(jnp.concatenate(heads, axis=1))
    gated = ug * jnp.concatenate(chunks, axis=0)

    mix = _bdot(pooled, wo_ref[:width, :]) + _bdot(gated, wo_ref[width:, :])
    x_new = x + mod_ref[2:3, :] * mix
    xo_ref[...] = x_new
    _router_tail(x_new, mod_ref, gf_ref, rw_ref, rb_ref, h2_ref, te_ref, cnt_ref)


def _odd_call(x, mod, g, w_in, pool_w, pool_scale, sgu_norm_g, sgu_w, sgu_b, w_out, gf, rw, rb):
    bsz, seq, d = x.shape
    tm = min(ROW_TILE, seq)
    width = pool_scale.shape[0]
    hd = width // SGU_HEADS
    t = jnp.arange(seq, dtype=jnp.int32)
    icnt = jnp.stack([1.0 / jnp.minimum(t + 1, wn).astype(F32) for wn in POOL_WINDOWS], axis=1)
    wsp = jnp.tril(sgu_w).astype(BF16)
    bsp = jnp.broadcast_to(sgu_b[:, :, None], (SGU_HEADS, SGU_CHUNK, hd))
    args = (g, w_in.astype(BF16), icnt, pool_w.astype(BF16), pool_scale.reshape(1, width),
            sgu_norm_g.reshape(1, width), wsp, bsp, w_out.astype(BF16), gf, rw, rb)
    full = lambda a: pl.BlockSpec(a.shape, lambda b, s: (0,) * a.ndim)
    in_specs = [pl.BlockSpec((None, tm, d), lambda b, s: (b, s, 0)),
                pl.BlockSpec((None, 6, d), lambda b, s: (b, 0, 0))]
    for idx, a in enumerate(args):
        in_specs.append(pl.BlockSpec((tm, len(POOL_WINDOWS)), lambda b, s: (s, 0)) if idx == 2 else full(a))
    out_specs, out_shape = _tail_out_specs(bsz, seq, tm, d)
    return pl.pallas_call(
        _odd_kernel,
        grid=(bsz, seq // tm),
        in_specs=in_specs,
        out_specs=out_specs,
        out_shape=out_shape,
        scratch_shapes=[pltpu.VMEM((tm + POOL_HALO, width), F32)],
        compiler_params=pltpu.CompilerParams(dimension_semantics=("parallel", "arbitrary"),
                                             vmem_limit_bytes=VMEM_LIMIT),
        name="odd_mixer",
    )(x, mod, *args)


def _dest_kernel(te_ref, base_ref, dst_ref):
    tr = te_ref.shape[0]
    te = te_ref[...]
    lane = lax.broadcasted_iota(jnp.int32, (tr, LANES), 1).astype(F32)
    hots = [te[:, kk:kk + 1] == lane for kk in range(TOP_K)]
    oh = jnp.zeros((tr, LANES), F32)
    for hot in hots:
        oh = oh + jnp.where(hot, 1.0, 0.0)
    r_i = lax.broadcasted_iota(jnp.int32, (tr, tr), 0)
    c_i = lax.broadcasted_iota(jnp.int32, (tr, tr), 1)
    tri = jnp.where(c_i < r_i, 1.0, 0.0).astype(BF16)
    before = jnp.dot(tri, oh.astype(BF16), preferred_element_type=F32) + base_ref[...]
    dst = jnp.zeros((tr, LANES), F32)
    for kk, hot in enumerate(hots):
        dst = jnp.where(lane == float(kk), jnp.sum(jnp.where(hot, before, 0.0), axis=-1, keepdims=True), dst)
    dst_ref[...] = dst[:, :2 * TOP_K].astype(jnp.int32)


def _dest_call(te, base):
    t = te.shape[0]
    tr = t // base.shape[0]
    return pl.pallas_call(
        _dest_kernel,
        grid=(t // tr,),
        in_specs=[pl.BlockSpec((tr, 2 * TOP_K), lambda i: (i, 0)),
                  pl.BlockSpec((None, 1, LANES), lambda i: (i, 0, 0))],
        out_specs=pl.BlockSpec((tr, 2 * TOP_K), lambda i: (i, 0)),
        out_shape=jax.ShapeDtypeStruct((t, 2 * TOP_K), jnp.int32),
        compiler_params=pltpu.CompilerParams(dimension_semantics=("parallel",)),
        name="route_dest",
    )(te, base)


def _sc_gather(table, idx):
    n = idx.shape[0]
    per_w = n // SC_WORKERS
    assert per_w * SC_WORKERS == n and per_w % SC_CHUNK == 0
    n_chunks = per_w // SC_CHUNK
    row_shape = table.shape[1:]
    mesh = plsc.VectorSubcoreMesh(core_axis_name="c", subcore_axis_name="s")

    @functools.partial(
        pl.kernel, mesh=mesh,
        out_type=jax.ShapeDtypeStruct((n,) + row_shape, table.dtype),
        scratch_types=[pltpu.VMEM((SC_CHUNK,), jnp.int32), pltpu.VMEM((SC_CHUNK,) + row_shape, table.dtype),
                       pltpu.SemaphoreType.DMA],
        name="sc_row_gather",
    )
    def gather(table_hbm, idx_hbm, out_hbm, idx_v, rows_v, sem):
        wid = lax.axis_index("s") * 2 + lax.axis_index("c")
        base = wid * per_w

        @pl.loop(0, n_chunks)
        def _(ci):
            off = pl.multiple_of(base + ci * SC_CHUNK, SC_CHUNK)
            pltpu.sync_copy(idx_hbm.at[pl.ds(off, SC_CHUNK)], idx_v)
            pltpu.async_copy(table_hbm.at[idx_v], rows_v, sem).wait()
            pltpu.sync_copy(rows_v, out_hbm.at[pl.ds(off, SC_CHUNK)])

    return gather(table, idx)


def _sc_scatter(rows, dests, n_out):
    t = rows.shape[0]
    per_w = t // SC_WORKERS
    assert per_w * SC_WORKERS == t and per_w % SC_CHUNK == 0
    n_chunks = per_w // SC_CHUNK
    row_shape = rows.shape[1:]
    nk = len(dests)
    mesh = plsc.VectorSubcoreMesh(core_axis_name="c", subcore_axis_name="s")

    @functools.partial(
        pl.kernel, mesh=mesh,
        out_type=jax.ShapeDtypeStruct((n_out,) + row_shape, rows.dtype),
        scratch_types=[pltpu.VMEM((SC_CHUNK,), jnp.int32)] * nk
        + [pltpu.VMEM((SC_CHUNK,) + row_shape, rows.dtype), pltpu.SemaphoreType.DMA],
        name="sc_row_scatter",
    )
    def scatter(rows_hbm, *rest):
        dest_hbm = rest[:nk]
        out_hbm = rest[nk]
        idx_v = rest[nk + 1:2 * nk + 1]
        rows_v, sem = rest[2 * nk + 1:]
        wid = lax.axis_index("s") * 2 + lax.axis_index("c")
        base = wid * per_w

        @pl.loop(0, n_chunks)
        def _(ci):
            off = pl.multiple_of(base + ci * SC_CHUNK, SC_CHUNK)
            pltpu.sync_copy(rows_hbm.at[pl.ds(off, SC_CHUNK)], rows_v)
            for kk in range(nk):
                pltpu.sync_copy(dest_hbm[kk].at[pl.ds(off, SC_CHUNK)], idx_v[kk])
            copies = [pltpu.async_copy(rows_v, out_hbm.at[idx_v[kk]], sem) for kk in range(nk)]
            for cp in copies:
                cp.wait()

    return scatter(rows, *dests)


def _expert_kernel(be_ref, nu_ref, x_ref, wgu_ref, bgu_ref, wdn_ref, bdn_ref, y_ref, wgu_bf, wdn_bf):
    i = pl.program_id(0)
    used = i < nu_ref[0]
    prev = be_ref[jnp.maximum(i - 1, 0)]
    fresh = jnp.logical_or(i == 0, be_ref[i] != prev)

    @pl.when(jnp.logical_and(used, fresh))
    def _():
        wgu_bf[...] = wgu_ref[...].astype(BF16)
        wdn_bf[...] = wdn_ref[...].astype(BF16)

    @pl.when(used)
    def _():
        x_lo, x_hi = _unpack_pairs(x_ref[...])
        dh = x_lo.shape[-1]
        z = jnp.dot(x_lo, wgu_bf[:dh, :], preferred_element_type=F32)
        z = z + jnp.dot(x_hi, wgu_bf[dh:, :], preferred_element_type=F32) + bgu_ref[...]
        ff = z.shape[-1] // 2
        gate = jnp.minimum(z[:, :ff], SWIGLU_LIMIT)
        lin = jnp.clip(z[:, ff:], -SWIGLU_LIMIT, SWIGLU_LIMIT)
        act = gate * _sigmoid(SWIGLU_ALPHA * gate) * (lin + 1.0)
        y = _bdot(act, wdn_bf[...]) + bdn_ref[...]
        y_ref[...] = _pack_pairs(y)


def _expert_call(layer, block_e, n_used, xs, w_gu, b_gu, w_dn, b_dn):
    n_rows, dh = xs.shape
    depth, e, d, ff2 = w_gu.shape
    ff = ff2 // 2
    nb = n_rows // MOE_ROWS
    row_map = lambda i, be, nu: (jnp.minimum(i, nu[0] - 1), 0)
    w_map = lambda i, be, nu: (layer, be[i], 0, 0)
    return pl.pallas_call(
        _expert_kernel,
        grid_spec=pltpu.PrefetchScalarGridSpec(
            num_scalar_prefetch=2,
            grid=(nb,),
            in_specs=[
                pl.BlockSpec((MOE_ROWS, dh), row_map),
                pl.BlockSpec((None, None, d, ff2), w_map),
                pl.BlockSpec((None, None, 1, ff2), w_map),
                pl.BlockSpec((None, None, ff, d), w_map),
                pl.BlockSpec((None, None, 1, d), w_map),
            ],
            out_specs=pl.BlockSpec((MOE_ROWS, dh), row_map),
            scratch_shapes=[pltpu.VMEM((d, ff2), BF16), pltpu.VMEM((ff, d), BF16)],
        ),
        out_shape=jax.ShapeDtypeStruct((n_rows, dh), jnp.uint32),
        compiler_params=pltpu.CompilerParams(dimension_semantics=("arbitrary",), vmem_limit_bytes=VMEM_LIMIT),
        name="moe_experts",
    )(block_e, n_used, xs, w_gu, b_gu.reshape(depth, e, 1, ff2), w_dn, b_dn.reshape(depth, e, 1, d))


def _combine_kernel(x_ref, yg_ref, te_ref, mod_ref, o_ref):
    te = te_ref[...]
    acc_lo = None
    for kk in range(TOP_K):
        lo, hi = _unpack_pairs(yg_ref[kk])
        gate = te[:, TOP_K + kk:TOP_K + kk + 1]
        acc_lo = gate * lo.astype(F32) if acc_lo is None else acc_lo + gate * lo.astype(F32)
        acc_hi = gate * hi.astype(F32) if kk == 0 else acc_hi + gate * hi.astype(F32)
    acc = jnp.concatenate([acc_lo, acc_hi], axis=1)
    o_ref[...] = x_ref[...] + mod_ref[5:6, :] * acc


def _combine_call(x, yg, te, mod):
    bsz, seq, d = x.shape
    tm = min(ROW_TILE, seq)
    nt = seq // tm
    return pl.pallas_call(
        _combine_kernel,
        grid=(bsz, nt),
        in_specs=[
            pl.BlockSpec((None, tm, d), lambda b, s: (b, s, 0)),
            pl.BlockSpec((TOP_K, tm, d // 2), lambda b, s: (0, b * nt + s, 0)),
            pl.BlockSpec((tm, 2 * TOP_K), lambda b, s: (b * nt + s, 0)),
            pl.BlockSpec((None, 6, d), lambda b, s: (b, 0, 0)),
        ],
        out_specs=pl.BlockSpec((None, tm, d), lambda b, s: (b, s, 0)),
        out_shape=jax.ShapeDtypeStruct((bsz, seq, d), F32),
        compiler_params=pltpu.CompilerParams(dimension_semantics=("parallel", "parallel"),
                                             vmem_limit_bytes=VMEM_LIMIT),
        name="moe_combine",
    )(x, yg, te, mod)


def _moe(layer, x_new, h2, te, tile_cnt, mod, w_gu, b_gu, w_dn, b_dn):
    t, dh = h2.shape
    tile_cnt = tile_cnt[:, 0, :].astype(jnp.int32)
    counts = jnp.sum(tile_cnt, axis=0)
    padded = (counts + MOE_ROWS - 1) // MOE_ROWS * MOE_ROWS
    upto = jnp.arange(LANES)[:, None] <= jnp.arange(LANES)[None, :]
    pad_end = jnp.sum(jnp.where(upto, padded[:, None], 0), axis=0)
    pad_start = pad_end - padded
    tile_base = pad_start[None, :] + jnp.cumsum(tile_cnt, axis=0) - tile_cnt
    dest = _dest_call(te, tile_base.astype(F32)[:, None, :])
    dests = [dest[:, kk] for kk in range(TOP_K)]
    n_rows = -(-(t * TOP_K + N_EXPERTS * (MOE_ROWS - 1)) // MOE_ROWS) * MOE_ROWS
    nb = n_rows // MOE_ROWS
    n_used = (pad_end[N_EXPERTS - 1] // MOE_ROWS).astype(jnp.int32).reshape(1)
    first_row = jnp.arange(nb, dtype=jnp.int32) * MOE_ROWS
    block_e = jnp.minimum(jnp.sum(pad_end[None, :N_EXPERTS] <= first_row[:, None], axis=1),
                          N_EXPERTS - 1).astype(jnp.int32)
    xs = _sc_scatter(h2, dests, n_rows)
    y = _expert_call(layer, block_e, n_used, xs, w_gu, b_gu, w_dn, b_dn)
    yg = _sc_gather(y, jnp.concatenate(dests)).reshape(TOP_K, t, dh)
    return _combine_call(x_new, yg, te, mod)


def kernel(x, c, positions, ada_w, ada_b, norm_mix_g, norm_ffn_g, router_w, router_b, moe_w_gu, moe_b_gu,
           moe_w_dn, moe_b_dn, even_w_in, mla_q_norm_g, mla_w_uq, mla_kv_norm_g, mla_w_ukv, mla_q_head_g,
           mla_k_head_g, s5_a_re, s5_a_im, s5_log_dt, s5_b_re, s5_b_im, s5_c_re, s5_c_im, s5_d, s5_glu_w,
           s5_glu_b, even_w_out, odd_w_in, pool_w, pool_scale, sgu_norm_g, sgu_w, sgu_b, odd_w_out):
    bsz, seq, d = x.shape
    depth = ada_w.shape[0]
    mods = _ada_call(c, ada_w, ada_b).reshape(depth, bsz, 6, d)
    posf = positions.astype(F32).reshape(bsz, seq, 1)
    for layer in range(depth):
        mod = mods[layer]
        i = layer // 2
        g_mix = norm_mix_g[layer].reshape(1, d)
        g_ffn = norm_ffn_g[layer].reshape(1, d)
        rw, rb = _router_pad(router_w[layer], router_b[layer])
        if layer % 2 == 0:
            prep = _prep_even(even_w_in[i], mla_q_norm_g[i], mla_w_uq[i], mla_kv_norm_g[i], mla_w_ukv[i],
                              mla_q_head_g[i], mla_k_head_g[i])
            q, k, v, u_t = _even_in_call(x, mod, posf, g_mix, prep)
            attn = _attn_call(q, k, v)
            disc = _s5_disc_call(s5_a_re[i], s5_a_im[i], s5_log_dt[i], s5_b_re[i], s5_b_im[i])
            ssm_t = _s5_call(u_t.reshape(seq, bsz, d // 2), disc, s5_c_re[i], s5_c_im[i], s5_d[i],
                             s5_glu_w[i], s5_glu_b[i])
            x_new, h2, te, tile_cnt = _mix_out_call(x, attn, ssm_t.reshape(seq, bsz * (d // 2)), mod,
                                                    even_w_out[i].astype(BF16), g_ffn, rw, rb)
        else:
            x_new, h2, te, tile_cnt = _odd_call(x, mod, g_mix, odd_w_in[i], pool_w[i], pool_scale[i],
                                                sgu_norm_g[i], sgu_w[i], sgu_b[i], odd_w_out[i], g_ffn, rw, rb)
        x = _moe(layer, x_new, h2, te, tile_cnt, mod, moe_w_gu, moe_b_gu, moe_w_dn, moe_b_dn)
    return x
```

```python
import functools
import math

import jax
import jax.numpy as jnp
from jax import lax
from jax.experimental import pallas as pl
from jax.experimental.pallas import tpu as pltpu
from jax.experimental.pallas import tpu_sc as plsc

F32 = jnp.float32
BF16 = jnp.bfloat16
HIGHEST = lax.Precision.HIGHEST

NORM_EPS = 1e-6
MLA_HEADS = 8
QK_NOPE_DIM = 64
QK_ROPE_DIM = 32
QK_HEAD_DIM = QK_NOPE_DIM + QK_ROPE_DIM
V_HEAD_DIM = 64
Q_LORA_RANK = 256
KV_LORA_RANK = 128
ROPE_THETA = 10000.0
S5_GROUP = 16
S5_STATE = 64
POOL_WINDOWS = (2, 4, 8, 16)
SGU_HEADS = 4
SGU_CHUNK = 128
N_EXPERTS = 32
TOP_K = 4
SWIGLU_ALPHA = 1.702
SWIGLU_LIMIT = 7.0

LANES = 128
SUBLANES = 8
HEAD_SLAB = LANES
POOL_HALO = 16
ROW_TILE = 512
ATTN_TILE = 512
ATTN_ROWS = 32
S5_STEPS = 64
MOE_ROWS = 512
SC_WORKERS = 32
SC_CHUNK = 64
VMEM_LIMIT = 56 * 1024 * 1024
NEG_BIG = -1e30


def _sigmoid(v):
    return 1.0 / (1.0 + jnp.exp(-v))


def _gelu(v):
    return 0.5 * v * (1.0 + jnp.tanh(math.sqrt(2.0 / math.pi) * (v + 0.044715 * (v * v * v))))


def _rms(v, width):
    return lax.rsqrt(jnp.sum(v * v, axis=-1, keepdims=True) * (1.0 / width) + NORM_EPS)


def _mod_norm(x, g, sc, sh):
    return x * _rms(x, x.shape[-1]) * g * (1.0 + sc) + sh


def _bdot(a, b):
    return jnp.dot(a.astype(BF16), b, preferred_element_type=F32)


def _pack_pairs(v):
    w = v.shape[-1] // 2
    bits = pltpu.bitcast(v.astype(BF16).astype(F32), jnp.uint32)
    return (bits[:, :w] >> 16) | bits[:, w:]


def _unpack_pairs(p):
    lo = pltpu.bitcast(p << 16, F32)
    hi = pltpu.bitcast(p & jnp.uint32(0xFFFF0000), F32)
    return lo.astype(BF16), hi.astype(BF16)


def _ada_kernel(c_ref, w_ref, b_ref, o_ref):
    c = c_ref[...]
    act = c * _sigmoid(c)
    o_ref[...] = jnp.dot(act, w_ref[...], precision=HIGHEST, preferred_element_type=F32) + b_ref[...]


def _ada_call(c, ada_w, ada_b):
    depth, d, n = ada_w.shape
    bsz = c.shape[0]
    tn = 1536
    return pl.pallas_call(
        _ada_kernel,
        grid=(depth, n // tn),
        in_specs=[
            pl.BlockSpec((bsz, d), lambda l, j: (0, 0)),
            pl.BlockSpec((None, d, tn), lambda l, j: (l, 0, j)),
            pl.BlockSpec((None, 1, tn), lambda l, j: (l, 0, j)),
        ],
        out_specs=pl.BlockSpec((None, bsz, tn), lambda l, j: (l, 0, j)),
        out_shape=jax.ShapeDtypeStruct((depth, bsz, n), F32),
        compiler_params=pltpu.CompilerParams(dimension_semantics=("parallel", "parallel"),
                                             vmem_limit_bytes=VMEM_LIMIT),
        name="ada_mod",
    )(c, ada_w, ada_b.reshape(depth, 1, n))


_C_Q = 0
_C_KV = Q_LORA_RANK
_C_PE = _C_KV + KV_LORA_RANK
_C_PESW = _C_PE + HEAD_SLAB
_C_U = _C_PESW + HEAD_SLAB


def _even_in_kernel(x_ref, mod_ref, pos_ref, g_ref, win_ref, gq_ref, wq_ref, gkv_ref, wk_ref, wv_ref,
                    tab_ref, q_ref, k_ref, v_ref, u_ref):
    x = x_ref[...]
    h = _mod_norm(x, g_ref[...], mod_ref[1:2, :], mod_ref[0:1, :])
    z = _bdot(h, win_ref[...])
    q_c = z[:, _C_Q:_C_KV]
    kv_c = z[:, _C_KV:_C_PE]
    kpe = z[:, _C_PE:_C_PESW]
    kpe_sw = z[:, _C_PESW:_C_U]
    u_ref[...] = z[:, _C_U:]

    ang = pos_ref[...] * tab_ref[0:1, :]
    cs = jnp.cos(ang)
    sn = jnp.sin(ang)
    gcq = cs * tab_ref[1:2, :]
    gsq = sn * tab_ref[2:3, :]
    gck = cs * tab_ref[3:4, :]
    gsk = sn * tab_ref[4:5, :]

    qn = q_c * _rms(q_c, Q_LORA_RANK) * gq_ref[...]
    qq = _bdot(qn, wq_ref[...])
    kvn = kv_c * _rms(kv_c, KV_LORA_RANK) * gkv_ref[...]
    kk = _bdot(kvn, wk_ref[...])
    v_ref[...] = _bdot(kvn, wv_ref[...]).astype(v_ref.dtype)

    pe_rot = kpe * gck + kpe_sw * gsk
    pe_ss = jnp.sum(kpe * kpe, axis=-1, keepdims=True)
    hw = MLA_HEADS * HEAD_SLAB
    for hd in range(MLA_HEADS):
        lo = hd * HEAD_SLAB
        qr = qq[:, lo:lo + HEAD_SLAB]
        qs = qq[:, hw + lo:hw + lo + HEAD_SLAB]
        rq = _rms(qr, QK_HEAD_DIM)
        q_ref[hd] = (rq * (qr * gcq + qs * gsq)).astype(q_ref.dtype)
        kr = kk[:, lo:lo + HEAD_SLAB]
        rk = lax.rsqrt((jnp.sum(kr * kr, axis=-1, keepdims=True) + pe_ss) * (1.0 / QK_HEAD_DIM) + NORM_EPS)
        k_ref[hd] = (rk * (kr * gck + pe_rot)).astype(k_ref.dtype)


def _even_in_call(x, mod, posf, g, prep):
    bsz, seq, d = x.shape
    tm = min(ROW_TILE, seq)
    hw = MLA_HEADS * HEAD_SLAB
    full = lambda a: pl.BlockSpec(a.shape, lambda b, s: (0,) * a.ndim)
    return pl.pallas_call(
        _even_in_kernel,
        grid=(bsz, seq // tm),
        in_specs=[
            pl.BlockSpec((None, tm, d), lambda b, s: (b, s, 0)),
            pl.BlockSpec((None, 6, d), lambda b, s: (b, 0, 0)),
            pl.BlockSpec((None, tm, 1), lambda b, s: (b, s, 0)),
            full(g), full(prep["w_in"]), full(prep["gq"]), full(prep["wq"]), full(prep["gkv"]),
            full(prep["wk"]), full(prep["wv"]), full(prep["tab"]),
        ],
        out_specs=[
            pl.BlockSpec((None, MLA_HEADS, tm, HEAD_SLAB), lambda b, s: (b, 0, s, 0)),
            pl.BlockSpec((None, MLA_HEADS, tm, HEAD_SLAB), lambda b, s: (b, 0, s, 0)),
            pl.BlockSpec((None, tm, MLA_HEADS * V_HEAD_DIM), lambda b, s: (b, s, 0)),
            pl.BlockSpec((tm, d // 2), lambda b, s: (s, b)),
        ],
        out_shape=[
            jax.ShapeDtypeStruct((bsz, MLA_HEADS, seq, HEAD_SLAB), BF16),
            jax.ShapeDtypeStruct((bsz, MLA_HEADS, seq, HEAD_SLAB), BF16),
            jax.ShapeDtypeStruct((bsz, seq, MLA_HEADS * V_HEAD_DIM), BF16),
            jax.ShapeDtypeStruct((seq, bsz * (d // 2)), F32),
        ],
        compiler_params=pltpu.CompilerParams(dimension_semantics=("parallel", "parallel"),
                                             vmem_limit_bytes=VMEM_LIMIT),
        name="even_in",
    )(x, mod, posf, g, prep["w_in"], prep["gq"], prep["wq"], prep["gkv"], prep["wk"], prep["wv"], prep["tab"])


def _prep_even(even_w_in, q_norm_g, w_uq, kv_norm_g, w_ukv, q_head_g, k_head_g):
    d = even_w_in.shape[0]
    half = QK_ROPE_DIM // 2
    nope = QK_NOPE_DIM
    c_pe = Q_LORA_RANK + KV_LORA_RANK
    w_pe = even_w_in[:, c_pe:c_pe + QK_ROPE_DIM]
    zeros = lambda n: jnp.zeros((d, n), F32)
    pe_slab = jnp.concatenate([zeros(nope), w_pe, zeros(HEAD_SLAB - QK_HEAD_DIM)], axis=1)
    pe_sw = jnp.concatenate([zeros(nope), -w_pe[:, half:], w_pe[:, :half], zeros(HEAD_SLAB - QK_HEAD_DIM)], axis=1)
    w_in = jnp.concatenate([even_w_in[:, :c_pe], pe_slab, pe_sw, even_w_in[:, c_pe + QK_ROPE_DIM:]], axis=1)

    r = w_uq.shape[0]
    padq = jnp.zeros((r, MLA_HEADS, HEAD_SLAB - QK_HEAD_DIM), F32)
    wq_plain = jnp.concatenate([w_uq, padq], axis=2).reshape(r, MLA_HEADS * HEAD_SLAB)
    wq_sw = jnp.concatenate([jnp.zeros((r, MLA_HEADS, nope), F32), -w_uq[:, :, nope + half:],
                             w_uq[:, :, nope:nope + half], padq], axis=2).reshape(r, MLA_HEADS * HEAD_SLAB)
    wq = jnp.concatenate([wq_plain, wq_sw], axis=1)

    rk = w_ukv.shape[0]
    wk = jnp.concatenate([w_ukv[:, :, :nope], jnp.zeros((rk, MLA_HEADS, HEAD_SLAB - nope), F32)],
                         axis=2).reshape(rk, MLA_HEADS * HEAD_SLAB)
    wv = w_ukv[:, :, nope:].reshape(rk, MLA_HEADS * V_HEAD_DIM)

    inv_freq = 1.0 / (ROPE_THETA ** (jnp.arange(half, dtype=F32) / half))
    pad_tail = jnp.zeros((HEAD_SLAB - QK_HEAD_DIM,), F32)
    freq_row = jnp.concatenate([jnp.zeros((nope,), F32), inv_freq, inv_freq, pad_tail])

    def gain_rows(gv, scale):
        plain = jnp.concatenate([gv, pad_tail]) * scale
        swapped = jnp.concatenate([jnp.zeros((nope,), F32), gv[nope + half:], gv[nope:nope + half], pad_tail]) * scale
        return plain, swapped

    gq_plain, gq_sw = gain_rows(q_head_g, QK_HEAD_DIM ** -0.5 * math.log2(math.e))
    gk_plain, gk_sw = gain_rows(k_head_g, 1.0)
    tab = jnp.stack([freq_row, gq_plain, gq_sw, gk_plain, gk_sw, freq_row * 0, freq_row * 0, freq_row * 0])
    return {
        "w_in": w_in.astype(BF16), "gq": q_norm_g.reshape(1, -1), "wq": wq.astype(BF16),
        "gkv": kv_norm_g.reshape(1, -1), "wk": wk.astype(BF16), "wv": wv.astype(BF16), "tab": tab,
    }


def _attn_kernel(qi_ref, kj_ref, q_ref, k_ref, v_ref, o_ref, m_sc, a_sc, acc_sc, s_sc, p_sc, *, tq, tk):
    step = pl.program_id(2)
    i = qi_ref[step]
    j = kj_ref[step]
    sum_lane = (V_HEAD_DIM, 0)

    @pl.when(j == 0)
    def _():
        m_sc[...] = jnp.full(m_sc.shape, -jnp.inf, F32)
        acc_sc[...] = jnp.zeros(acc_sc.shape, F32)

    def sweep(on_diagonal):
        v = v_ref[...]
        lane = lax.broadcasted_iota(jnp.int32, (1, LANES), 1)
        for hh in range(2):
            s_sc[hh] = lax.dot_general(q_ref[hh], k_ref[hh], (((1,), (1,)), ((), ())),
                                       preferred_element_type=F32)
        for hh in range(2):
            for r0 in range(0, tq, ATTN_ROWS):
                rows = pl.ds(r0, ATTN_ROWS)
                s = s_sc[hh, rows, :]
                if on_diagonal:
                    row = r0 + lax.broadcasted_iota(jnp.int32, (ATTN_ROWS, tk), 0)
                    col = lax.broadcasted_iota(jnp.int32, (ATTN_ROWS, tk), 1)
                    s = jnp.where(col <= row, s, -jnp.inf)
                m_prev = m_sc[hh, rows, :]
                m_new = jnp.maximum(m_prev, jnp.max(s, axis=-1, keepdims=True))
                a_sc[hh, rows, :] = jnp.exp2(m_prev - m_new)
                m_sc[hh, rows, :] = m_new
                shifted = s - jnp.concatenate([m_new] * (tk // LANES), axis=1)
                p_sc[hh, rows, :] = jnp.exp2(shifted.astype(BF16))
            own = (lane < V_HEAD_DIM) == (hh == 0)
            ones = jnp.where(lane == sum_lane[hh], 1.0, 0.0).astype(v.dtype)
            vh = jnp.where(own, v, jnp.broadcast_to(ones, v.shape))
            acc_sc[hh] = acc_sc[hh] * a_sc[hh] + jnp.dot(p_sc[hh], vh, preferred_element_type=F32)

    @pl.when(j < i)
    def _():
        sweep(False)

    @pl.when(j == i)
    def _():
        sweep(True)
        lane = lax.broadcasted_iota(jnp.int32, (1, LANES), 1)
        acc0 = acc_sc[0]
        acc1 = acc_sc[1]
        l0 = acc0[:, sum_lane[0]:sum_lane[0] + 1]
        l1 = acc1[:, sum_lane[1]:sum_lane[1] + 1]
        o_ref[...] = jnp.where(lane < V_HEAD_DIM, acc0 / l0, acc1 / l1).astype(o_ref.dtype)


def _attn_call(q, k, v):
    bsz, nh, seq, _ = q.shape
    tq = tk = min(ATTN_TILE, seq)
    nq = seq // tq
    pairs = [(i, j) for i in range(nq) for j in range(i + 1)]
    qi = jnp.asarray([p[0] for p in pairs], jnp.int32)
    kj = jnp.asarray([p[1] for p in pairs], jnp.int32)
    kern = functools.partial(_attn_kernel, tq=tq, tk=tk)
    return pl.pallas_call(
        kern,
        grid_spec=pltpu.PrefetchScalarGridSpec(
            num_scalar_prefetch=2,
            grid=(bsz, nh // 2, len(pairs)),
            in_specs=[
                pl.BlockSpec((None, 2, tq, HEAD_SLAB), lambda b, h, p, qi, kj: (b, h, qi[p], 0)),
                pl.BlockSpec((None, 2, tk, HEAD_SLAB), lambda b, h, p, qi, kj: (b, h, kj[p], 0)),
                pl.BlockSpec((None, tk, 2 * V_HEAD_DIM), lambda b, h, p, qi, kj: (b, kj[p], h)),
            ],
            out_specs=pl.BlockSpec((None, tq, 2 * V_HEAD_DIM), lambda b, h, p, qi, kj: (b, qi[p], h)),
            scratch_shapes=[pltpu.VMEM((2, tq, LANES), F32), pltpu.VMEM((2, tq, LANES), F32),
                            pltpu.VMEM((2, tq, LANES), F32),
                            pltpu.VMEM((2, tq, tk), F32), pltpu.VMEM((2, tq, tk), BF16)],
        ),
        out_shape=jax.ShapeDtypeStruct((bsz, seq, nh * V_HEAD_DIM), BF16),
        compiler_params=pltpu.CompilerParams(
            dimension_semantics=("parallel", "parallel", "arbitrary"),
            vmem_limit_bytes=VMEM_LIMIT),
        name="mla_attention",
    )(qi, kj, q, k, v)


def _s5_disc_kernel(are_ref, aim_ref, ldt_ref, bre_ref, bim_ref, abre_ref, abim_ref, bbre_ref, bbim_ref):
    dt = jnp.exp(ldt_ref[...])
    lam_re = jnp.minimum(are_ref[...], -1e-4)
    lam_im = aim_ref[...]
    mag = jnp.exp(lam_re * dt)
    ab_re = mag * jnp.cos(lam_im * dt)
    ab_im = mag * jnp.sin(lam_im * dt)
    den = lam_re * lam_re + lam_im * lam_im
    num_re = ab_re - 1.0
    f_re = (num_re * lam_re + ab_im * lam_im) / den
    f_im = (ab_im * lam_re - num_re * lam_im) / den
    abre_ref[...] = ab_re
    abim_ref[...] = ab_im
    br = bre_ref[...]
    bi = bim_ref[...]
    bbre_ref[...] = f_re[:, None, :] * br - f_im[:, None, :] * bi
    bbim_ref[...] = f_re[:, None, :] * bi + f_im[:, None, :] * br


def _s5_disc_call(a_re, a_im, log_dt, b_re, b_im):
    g, p = a_re.shape
    bre_t = jnp.swapaxes(b_re, 1, 2)
    bim_t = jnp.swapaxes(b_im, 1, 2)
    return pl.pallas_call(
        _s5_disc_kernel,
        out_shape=[jax.ShapeDtypeStruct((g, p), F32), jax.ShapeDtypeStruct((g, p), F32),
                   jax.ShapeDtypeStruct(bre_t.shape, F32), jax.ShapeDtypeStruct(bre_t.shape, F32)],
        name="s5_discretize",
    )(a_re, a_im, log_dt.reshape(g, 1), bre_t, bim_t)


def _block_diag_halves(m):
    g, r, c = m.shape
    gh = g // 2
    eye = jnp.eye(gh, dtype=m.dtype)
    mh = m.reshape(2, gh, r, c)
    return (mh[:, :, :, None, :] * eye[None, :, None, :, None]).reshape(2, gh * r, gh * c)


def _s5_kernel(u_ref, bre_ref, bim_ref, are_ref, aim_ref, cre_ref, cim_ref, d_ref, gw_ref, gb_ref,
               o_ref, sre, sim, dre, dim, *, steps):
    @pl.when(pl.program_id(0) == 0)
    def _():
        sre[...] = jnp.zeros(sre.shape, F32)
        sim[...] = jnp.zeros(sim.shape, F32)

    rows = steps * SUBLANES
    w = u_ref.shape[-1]
    u = u_ref[...].reshape(rows, w)
    ub = u.astype(BF16)
    kh = w // 2
    nh = dre.shape[1] // 2
    for hf in range(2):
        dre[:, hf * nh:(hf + 1) * nh] = jnp.dot(ub[:, hf * kh:(hf + 1) * kh], bre_ref[hf], preferred_element_type=F32)
        dim[:, hf * nh:(hf + 1) * nh] = jnp.dot(ub[:, hf * kh:(hf + 1) * kh], bim_ref[hf], preferred_element_type=F32)

    a_r = are_ref[...]
    a_i = aim_ref[...]

    def body(t, carry):
        xr, xi = carry
        r0 = pl.multiple_of(t * SUBLANES, SUBLANES)
        nr = a_r * xr - a_i * xi + dre[pl.ds(r0, SUBLANES), :]
        ni = a_r * xi + a_i * xr + dim[pl.ds(r0, SUBLANES), :]
        dre[pl.ds(r0, SUBLANES), :] = nr
        dim[pl.ds(r0, SUBLANES), :] = ni
        return nr, ni

    xr, xi = lax.fori_loop(0, steps, body, (sre[...], sim[...]))
    sre[...] = xr
    sim[...] = xi

    ys = []
    for hf in range(2):
        yr = jnp.dot(dre[:, hf * nh:(hf + 1) * nh].astype(BF16), cre_ref[hf], preferred_element_type=F32)
        yi = jnp.dot(dim[:, hf * nh:(hf + 1) * nh].astype(BF16), cim_ref[hf], preferred_element_type=F32)
        ys.append(yr - yi)
    y = jnp.concatenate(ys, axis=1) + d_ref[...] * u
    g = _gelu(y)
    out = g * _sigmoid(_bdot(g, gw_ref[...]) + gb_ref[...])
    o_ref[...] = out.reshape(steps, SUBLANES, w).astype(o_ref.dtype)


def _s5_call(u_t, disc, c_re, c_im, d_skip, glu_w, glu_b):
    seq, bsz, w = u_t.shape
    assert bsz == SUBLANES
    ab_re, ab_im, bb_re, bb_im = disc
    g, p = ab_re.shape
    n_state = g * p
    bre = _block_diag_halves(bb_re).astype(BF16)
    bim = _block_diag_halves(bb_im).astype(BF16)
    cre = _block_diag_halves(jnp.swapaxes(c_re, 1, 2)).astype(BF16)
    cim = _block_diag_halves(jnp.swapaxes(c_im, 1, 2)).astype(BF16)
    steps = min(S5_STEPS, seq)
    full = lambda a: pl.BlockSpec(a.shape, lambda s: (0,) * a.ndim)
    args = (bre, bim, ab_re.reshape(1, n_state), ab_im.reshape(1, n_state), cre, cim,
            d_skip.reshape(1, w), glu_w.astype(BF16), glu_b.reshape(1, w))
    return pl.pallas_call(
        functools.partial(_s5_kernel, steps=steps),
        grid=(seq // steps,),
        in_specs=[pl.BlockSpec((steps, bsz, w), lambda s: (s, 0, 0))] + [full(a) for a in args],
        out_specs=pl.BlockSpec((steps, bsz, w), lambda s: (s, 0, 0)),
        out_shape=jax.ShapeDtypeStruct((seq, bsz, w), BF16),
        scratch_shapes=[pltpu.VMEM((bsz, n_state), F32), pltpu.VMEM((bsz, n_state), F32),
                        pltpu.VMEM((steps * bsz, n_state), F32), pltpu.VMEM((steps * bsz, n_state), F32)],
        compiler_params=pltpu.CompilerParams(dimension_semantics=("arbitrary",), vmem_limit_bytes=VMEM_LIMIT),
        name="s5_scan",
    )(u_t, *args)


def _router_tail(x_new, mod_ref, gf_ref, rw_ref, rb_ref, h2_ref, te_ref, cnt_ref):
    h2 = _mod_norm(x_new, gf_ref[...], mod_ref[4:5, :], mod_ref[3:4, :])
    h2_ref[...] = _pack_pairs(h2)
    h_hi = h2.astype(BF16)
    h_lo = (h2 - h_hi.astype(F32)).astype(BF16)
    r_hi = jnp.dot(h_hi, rw_ref[...], preferred_element_type=F32)
    r_lo = jnp.dot(h_lo, rw_ref[...], preferred_element_type=F32)
    logits = r_hi[:, :LANES] + r_hi[:, LANES:] + r_lo[:, :LANES] + rb_ref[...]
    lane = lax.broadcasted_iota(jnp.int32, logits.shape, 1).astype(F32)
    vals = []
    idxs = []
    work = logits
    for _ in range(TOP_K):
        m = jnp.max(work, axis=-1, keepdims=True)
        idx = jnp.min(jnp.where(work == m, lane, float(LANES)), axis=-1, keepdims=True)
        vals.append(m)
        idxs.append(idx)
        work = jnp.where(lane == idx, NEG_BIG * 2.0, work)
    exps = [jnp.exp(vv - vals[0]) for vv in vals]
    tot = exps[0] + exps[1] + exps[2] + exps[3]
    te = jnp.zeros(logits.shape, F32)
    picked = jnp.zeros(logits.shape, F32)
    for kk in range(TOP_K):
        te = jnp.where(lane == float(kk), idxs[kk], te)
        te = jnp.where(lane == float(TOP_K + kk), exps[kk] / tot, te)
        picked = picked + jnp.where(lane == idxs[kk], 1.0, 0.0)
    te_ref[...] = te[:, :2 * TOP_K]
    cnt_ref[...] = jnp.sum(picked, axis=0, keepdims=True)


def _mix_out_kernel(x_ref, a_ref, s_ref, mod_ref, wo_ref, gf_ref, rw_ref, rb_ref, xo_ref, h2_ref, te_ref,
                    cnt_ref):
    ka = a_ref.shape[-1]
    mix = jnp.dot(a_ref[...], wo_ref[:ka, :], preferred_element_type=F32)
    mix = mix + jnp.dot(s_ref[...], wo_ref[ka:, :], preferred_element_type=F32)
    x_new = x_ref[...] + mod_ref[2:3, :] * mix
    xo_ref[...] = x_new
    _router_tail(x_new, mod_ref, gf_ref, rw_ref, rb_ref, h2_ref, te_ref, cnt_ref)


def _tail_out_specs(bsz, seq, tm, d):
    nt = seq // tm
    specs = [
        pl.BlockSpec((None, tm, d), lambda b, s: (b, s, 0)),
        pl.BlockSpec((tm, d // 2), lambda b, s: (b * nt + s, 0)),
        pl.BlockSpec((tm, 2 * TOP_K), lambda b, s: (b * nt + s, 0)),
        pl.BlockSpec((None, 1, LANES), lambda b, s: (b * nt + s, 0, 0)),
    ]
    shapes = [
        jax.ShapeDtypeStruct((bsz, seq, d), F32),
        jax.ShapeDtypeStruct((bsz * seq, d // 2), jnp.uint32),
        jax.ShapeDtypeStruct((bsz * seq, 2 * TOP_K), F32),
        jax.ShapeDtypeStruct((bsz * nt, 1, LANES), F32),
    ]
    return specs, shapes


def _router_pad(router_w, router_b):
    d, e = router_w.shape
    rw = jnp.concatenate([router_w, jnp.zeros((d, LANES - e), F32)], axis=1)
    rw_hi = rw.astype(BF16)
    rw_lo = (rw - rw_hi.astype(F32)).astype(BF16)
    rb = jnp.concatenate([router_b, jnp.full((LANES - e,), NEG_BIG, F32)]).reshape(1, LANES)
    return jnp.concatenate([rw_hi, rw_lo], axis=1), rb


def _mix_out_call(x, attn, ssm_t, mod, w_out, gf, rw, rb):
    bsz, seq, d = x.shape
    tm = min(ROW_TILE, seq)
    ka = attn.shape[-1]
    ks = ssm_t.shape[-1] // bsz
    full = lambda a: pl.BlockSpec(a.shape, lambda b, s: (0,) * a.ndim)
    out_specs, out_shape = _tail_out_specs(bsz, seq, tm, d)
    return pl.pallas_call(
        _mix_out_kernel,
        grid=(bsz, seq // tm),
        in_specs=[
            pl.BlockSpec((None, tm, d), lambda b, s: (b, s, 0)),
            pl.BlockSpec((None, tm, ka), lambda b, s: (b, s, 0)),
            pl.BlockSpec((tm, ks), lambda b, s: (s, b)),
            pl.BlockSpec((None, 6, d), lambda b, s: (b, 0, 0)),
            full(w_out), full(gf), full(rw), full(rb),
        ],
        out_specs=out_specs,
        out_shape=out_shape,
        compiler_params=pltpu.CompilerParams(dimension_semantics=("parallel", "parallel"),
                                             vmem_limit_bytes=VMEM_LIMIT),
        name="even_out",
    )(x, attn, ssm_t, mod, w_out, gf, rw, rb)


def _odd_kernel(x_ref, mod_ref, g_ref, win_ref, icnt_ref, wp_ref, ps_ref, gv_ref, wsp_ref, bsp_ref,
                wo_ref, gf_ref, rw_ref, rb_ref, xo_ref, h2_ref, te_ref, cnt_ref, ext_sc):
    tm = x_ref.shape[0]
    pw = wp_ref.shape[-1]
    width = pw * len(POOL_WINDOWS)

    @pl.when(pl.program_id(1) == 0)
    def _():
        ext_sc[0:POOL_HALO, :] = jnp.zeros((POOL_HALO, width), F32)

    x = x_ref[...]
    h = _mod_norm(x, g_ref[...], mod_ref[1:2, :], mod_ref[0:1, :])
    z = _bdot(h, win_ref[...])
    up = z[:, :width]
    ext_sc[POOL_HALO:POOL_HALO + tm, :] = up

    pooled = []
    for gi, win in enumerate(POOL_WINDOWS):
        cols = slice(gi * pw, (gi + 1) * pw)
        acc = up[:, cols]
        for lag in range(1, win):
            acc = acc + ext_sc[POOL_HALO - lag:POOL_HALO - lag + tm, cols]
        pg = acc * icnt_ref[:, gi:gi + 1] - up[:, cols]
        pooled.append(_bdot(pg, wp_ref[gi]) * ps_ref[:, cols])
    ext_sc[0:POOL_HALO, :] = ext_sc[tm:tm + POOL_HALO, :]
    pooled = jnp.concatenate(pooled, axis=1)

    ug = _gelu(z[:, width:2 * width])
    vg = _gelu(z[:, 2 * width:])
    vn = (vg * _rms(vg, width) * gv_ref[...]).astype(BF16)
    hd = width // SGU_HEADS
    chunks = []
    for ci in range(tm // SGU_CHUNK):
        heads = []
        for hh in range(SGU_HEADS):
            blk = vn[ci * SGU_CHUNK:(ci + 1) * SGU_CHUNK, hh * hd:(hh + 1) * hd]
            heads.append(jnp.dot(wsp_ref[hh], blk, preferred_element_type=F32) + bsp_ref[hh])
        chunks.append(jnp.concatenate(heads, axis=1))
    gated = ug * jnp.concatenate(chunks, axis=0)

    mix = _bdot(pooled, wo_ref[:width, :]) + _bdot(gated, wo_ref[width:, :])
    x_new = x + mod_ref[2:3, :] * mix
    xo_ref[...] = x_new
    _router_tail(x_new, mod_ref, gf_ref, rw_ref, rb_ref, h2_ref, te_ref, cnt_ref)


def _odd_call(x, mod, g, w_in, pool_w, pool_scale, sgu_norm_g, sgu_w, sgu_b, w_out, gf, rw, rb):
    bsz, seq, d = x.shape
    tm = min(ROW_TILE, seq)
    width = pool_scale.shape[0]
    hd = width // SGU_HEADS
    t = jnp.arange(seq, dtype=jnp.int32)
    icnt = jnp.stack([1.0 / jnp.minimum(t + 1, wn).astype(F32) for wn in POOL_WINDOWS], axis=1)
    wsp = jnp.tril(sgu_w).astype(BF16)
    bsp = jnp.broadcast_to(sgu_b[:, :, None], (SGU_HEADS, SGU_CHUNK, hd))
    args = (g, w_in.astype(BF16), icnt, pool_w.astype(BF16), pool_scale.reshape(1, width),
            sgu_norm_g.reshape(1, width), wsp, bsp, w_out.astype(BF16), gf, rw, rb)
    full = lambda a: pl.BlockSpec(a.shape, lambda b, s: (0,) * a.ndim)
    in_specs = [pl.BlockSpec((None, tm, d), lambda b, s: (b, s, 0)),
                pl.BlockSpec((None, 6, d), lambda b, s: (b, 0, 0))]
    for idx, a in enumerate(args):
        in_specs.append(pl.BlockSpec((tm, len(POOL_WINDOWS)), lambda b, s: (s, 0)) if idx == 2 else full(a))
    out_specs, out_shape = _tail_out_specs(bsz, seq, tm, d)
    return pl.pallas_call(
        _odd_kernel,
        grid=(bsz, seq // tm),
        in_specs=in_specs,
        out_specs=out_specs,
        out_shape=out_shape,
        scratch_shapes=[pltpu.VMEM((tm + POOL_HALO, width), F32)],
        compiler_params=pltpu.CompilerParams(dimension_semantics=("parallel", "arbitrary"),
                                             vmem_limit_bytes=VMEM_LIMIT),
        name="odd_mixer",
    )(x, mod, *args)


def _dest_kernel(te_ref, base_ref, dst_ref):
    tr = te_ref.shape[0]
    te = te_ref[...]
    lane = lax.broadcasted_iota(jnp.int32, (tr, LANES), 1).astype(F32)
    hots = [te[:, kk:kk + 1] == lane for kk in range(TOP_K)]
    oh = jnp.zeros((tr, LANES), F32)
    for hot in hots:
        oh = oh + jnp.where(hot, 1.0, 0.0)
    r_i = lax.broadcasted_iota(jnp.int32, (tr, tr), 0)
    c_i = lax.broadcasted_iota(jnp.int32, (tr, tr), 1)
    tri = jnp.where(c_i < r_i, 1.0, 0.0).astype(BF16)
    before = jnp.dot(tri, oh.astype(BF16), preferred_element_type=F32) + base_ref[...]
    dst = jnp.zeros((tr, LANES), F32)
    for kk, hot in enumerate(hots):
        dst = jnp.where(lane == float(kk), jnp.sum(jnp.where(hot, before, 0.0), axis=-1, keepdims=True), dst)
    dst_ref[...] = dst[:, :2 * TOP_K].astype(jnp.int32)


def _dest_call(te, base):
    t = te.shape[0]
    tr = t // base.shape[0]
    return pl.pallas_call(
        _dest_kernel,
        grid=(t // tr,),
        in_specs=[pl.BlockSpec((tr, 2 * TOP_K), lambda i: (i, 0)),
                  pl.BlockSpec((None, 1, LANES), lambda i: (i, 0, 0))],
        out_specs=pl.BlockSpec((tr, 2 * TOP_K), lambda i: (i, 0)),
        out_shape=jax.ShapeDtypeStruct((t, 2 * TOP_K), jnp.int32),
        compiler_params=pltpu.CompilerParams(dimension_semantics=("parallel",)),
        name="route_dest",
    )(te, base)


def _sc_gather(table, idx):
    n = idx.shape[0]
    per_w = n // SC_WORKERS
    assert per_w * SC_WORKERS == n and per_w % SC_CHUNK == 0
    n_chunks = per_w // SC_CHUNK
    row_shape = table.shape[1:]
    mesh = plsc.VectorSubcoreMesh(core_axis_name="c", subcore_axis_name="s")

    @functools.partial(
        pl.kernel, mesh=mesh,
        out_type=jax.ShapeDtypeStruct((n,) + row_shape, table.dtype),
        scratch_types=[pltpu.VMEM((SC_CHUNK,), jnp.int32), pltpu.VMEM((SC_CHUNK,) + row_shape, table.dtype),
                       pltpu.SemaphoreType.DMA],
        name="sc_row_gather",
    )
    def gather(table_hbm, idx_hbm, out_hbm, idx_v, rows_v, sem):
        wid = lax.axis_index("s") * 2 + lax.axis_index("c")
        base = wid * per_w

        @pl.loop(0, n_chunks)
        def _(ci):
            off = pl.multiple_of(base + ci * SC_CHUNK, SC_CHUNK)
            pltpu.sync_copy(idx_hbm.at[pl.ds(off, SC_CHUNK)], idx_v)
            pltpu.async_copy(table_hbm.at[idx_v], rows_v, sem).wait()
            pltpu.sync_copy(rows_v, out_hbm.at[pl.ds(off, SC_CHUNK)])

    return gather(table, idx)


def _sc_scatter(rows, dests, n_out):
    t = rows.shape[0]
    per_w = t // SC_WORKERS
    assert per_w * SC_WORKERS == t and per_w % SC_CHUNK == 0
    n_chunks = per_w // SC_CHUNK
    row_shape = rows.shape[1:]
    nk = len(dests)
    mesh = plsc.VectorSubcoreMesh(core_axis_name="c", subcore_axis_name="s")

    @functools.partial(
        pl.kernel, mesh=mesh,
        out_type=jax.ShapeDtypeStruct((n_out,) + row_shape, rows.dtype),
        scratch_types=[pltpu.VMEM((SC_CHUNK,), jnp.int32)] * nk
        + [pltpu.VMEM((SC_CHUNK,) + row_shape, rows.dtype), pltpu.SemaphoreType.DMA],
        name="sc_row_scatter",
    )
    def scatter(rows_hbm, *rest):
        dest_hbm = rest[:nk]
        out_hbm = rest[nk]
        idx_v = rest[nk + 1:2 * nk + 1]
        rows_v, sem = rest[2 * nk + 1:]
        wid = lax.axis_index("s") * 2 + lax.axis_index("c")
        base = wid * per_w

        @pl.loop(0, n_chunks)
        def _(ci):
            off = pl.multiple_of(base + ci * SC_CHUNK, SC_CHUNK)
            pltpu.sync_copy(rows_hbm.at[pl.ds(off, SC_CHUNK)], rows_v)
            for kk in range(nk):
                pltpu.sync_copy(dest_hbm[kk].at[pl.ds(off, SC_CHUNK)], idx_v[kk])
            copies = [pltpu.async_copy(rows_v, out_hbm.at[idx_v[kk]], sem) for kk in range(nk)]
            for cp in copies:
                cp.wait()

    return scatter(rows, *dests)


def _expert_kernel(be_ref, nu_ref, x_ref, wgu_ref, bgu_ref, wdn_ref, bdn_ref, y_ref, wgu_bf, wdn_bf):
    i = pl.program_id(0)
    used = i < nu_ref[0]
    prev = be_ref[jnp.maximum(i - 1, 0)]
    fresh = jnp.logical_or(i == 0, be_ref[i] != prev)

    @pl.when(jnp.logical_and(used, fresh))
    def _():
        wgu_bf[...] = wgu_ref[...].astype(BF16)
        wdn_bf[...] = wdn_ref[...].astype(BF16)

    @pl.when(used)
    def _():
        x_lo, x_hi = _unpack_pairs(x_ref[...])
        dh = x_lo.shape[-1]
        z = jnp.dot(x_lo, wgu_bf[:dh, :], preferred_element_type=F32)
        z = z + jnp.dot(x_hi, wgu_bf[dh:, :], preferred_element_type=F32) + bgu_ref[...]
        ff = z.shape[-1] // 2
        gate = jnp.minimum(z[:, :ff], SWIGLU_LIMIT)
        lin = jnp.clip(z[:, ff:], -SWIGLU_LIMIT, SWIGLU_LIMIT)
        act = gate * _sigmoid(SWIGLU_ALPHA * gate) * (lin + 1.0)
        y = _bdot(act, wdn_bf[...]) + bdn_ref[...]
        y_ref[...] = _pack_pairs(y)


def _expert_call(layer, block_e, n_used, xs, w_gu, b_gu, w_dn, b_dn):
    n_rows, dh = xs.shape
    depth, e, d, ff2 = w_gu.shape
    ff = ff2 // 2
    nb = n_rows // MOE_ROWS
    row_map = lambda i, be, nu: (jnp.minimum(i, nu[0] - 1), 0)
    w_map = lambda i, be, nu: (layer, be[i], 0, 0)
    return pl.pallas_call(
        _expert_kernel,
        grid_spec=pltpu.PrefetchScalarGridSpec(
            num_scalar_prefetch=2,
            grid=(nb,),
            in_specs=[
                pl.BlockSpec((MOE_ROWS, dh), row_map),
                pl.BlockSpec((None, None, d, ff2), w_map),
                pl.BlockSpec((None, None, 1, ff2), w_map),
                pl.BlockSpec((None, None, ff, d), w_map),
                pl.BlockSpec((None, None, 1, d), w_map),
            ],
            out_specs=pl.BlockSpec((MOE_ROWS, dh), row_map),
            scratch_shapes=[pltpu.VMEM((d, ff2), BF16), pltpu.VMEM((ff, d), BF16)],
        ),
        out_shape=jax.ShapeDtypeStruct((n_rows, dh), jnp.uint32),
        compiler_params=pltpu.CompilerParams(dimension_semantics=("arbitrary",), vmem_limit_bytes=VMEM_LIMIT),
        name="moe_experts",
    )(block_e, n_used, xs, w_gu, b_gu.reshape(depth, e, 1, ff2), w_dn, b_dn.reshape(depth, e, 1, d))


def _combine_kernel(x_ref, yg_ref, te_ref, mod_ref, o_ref):
    te = te_ref[...]
    acc_lo = None
    for kk in range(TOP_K):
        lo, hi = _unpack_pairs(yg_ref[kk])
        gate = te[:, TOP_K + kk:TOP_K + kk + 1]
        acc_lo = gate * lo.astype(F32) if acc_lo is None else acc_lo + gate * lo.astype(F32)
        acc_hi = gate * hi.astype(F32) if kk == 0 else acc_hi + gate * hi.astype(F32)
    acc = jnp.concatenate([acc_lo, acc_hi], axis=1)
    o_ref[...] = x_ref[...] + mod_ref[5:6, :] * acc


def _combine_call(x, yg, te, mod):
    bsz, seq, d = x.shape
    tm = min(ROW_TILE, seq)
    nt = seq // tm
    return pl.pallas_call(
        _combine_kernel,
        grid=(bsz, nt),
        in_specs=[
            pl.BlockSpec((None, tm, d), lambda b, s: (b, s, 0)),
            pl.BlockSpec((TOP_K, tm, d // 2), lambda b, s: (0, b * nt + s, 0)),
            pl.BlockSpec((tm, 2 * TOP_K), lambda b, s: (b * nt + s, 0)),
            pl.BlockSpec((None, 6, d), lambda b, s: (b, 0, 0)),
        ],
        out_specs=pl.BlockSpec((None, tm, d), lambda b, s: (b, s, 0)),
        out_shape=jax.ShapeDtypeStruct((bsz, seq, d), F32),
        compiler_params=pltpu.CompilerParams(dimension_semantics=("parallel", "parallel"),
                                             vmem_limit_bytes=VMEM_LIMIT),
        name="moe_combine",
    )(x, yg, te, mod)


def _moe(layer, x_new, h2, te, tile_cnt, mod, w_gu, b_gu, w_dn, b_dn):
    t, dh = h2.shape
    tile_cnt = tile_cnt[:, 0, :].astype(jnp.int32)
    counts = jnp.sum(tile_cnt, axis=0)
    padded = (counts + MOE_ROWS - 1) // MOE_ROWS * MOE_ROWS
    upto = jnp.arange(LANES)[:, None] <= jnp.arange(LANES)[None, :]
    pad_end = jnp.sum(jnp.where(upto, padded[:, None], 0), axis=0)
    pad_start = pad_end - padded
    tile_base = pad_start[None, :] + jnp.cumsum(tile_cnt, axis=0) - tile_cnt
    dest = _dest_call(te, tile_base.astype(F32)[:, None, :])
    dests = [dest[:, kk] for kk in range(TOP_K)]
    n_rows = -(-(t * TOP_K + N_EXPERTS * (MOE_ROWS - 1)) // MOE_ROWS) * MOE_ROWS
    nb = n_rows // MOE_ROWS
    n_used = (pad_end[N_EXPERTS - 1] // MOE_ROWS).astype(jnp.int32).reshape(1)
    first_row = jnp.arange(nb, dtype=jnp.int32) * MOE_ROWS
    block_e = jnp.minimum(jnp.sum(pad_end[None, :N_EXPERTS] <= first_row[:, None], axis=1),
                          N_EXPERTS - 1).astype(jnp.int32)
    xs = _sc_scatter(h2, dests, n_rows)
    y = _expert_call(layer, block_e, n_used, xs, w_gu, b_gu, w_dn, b_dn)
    yg = _sc_gather(y, jnp.concatenate(dests)).reshape(TOP_K, t, dh)
    return _combine_call(x_new, yg, te, mod)


def kernel(x, c, positions, ada_w, ada_b, norm_mix_g, norm_ffn_g, router_w, router_b, moe_w_gu, moe_b_gu,
           moe_w_dn, moe_b_dn, even_w_in, mla_q_norm_g, mla_w_uq, mla_kv_norm_g, mla_w_ukv, mla_q_head_g,
           mla_k_head_g, s5_a_re, s5_a_im, s5_log_dt, s5_b_re, s5_b_im, s5_c_re, s5_c_im, s5_d, s5_glu_w,
           s5_glu_b, even_w_out, odd_w_in, pool_w, pool_scale, sgu_norm_g, sgu_w, sgu_b, odd_w_out):
    bsz, seq, d = x.shape
    depth = ada_w.shape[0]
    mods = _ada_call(c, ada_w, ada_b).reshape(depth, bsz, 6, d)
    posf = positions.astype(F32).reshape(bsz, seq, 1)
    for layer in range(depth):
        mod = mods[layer]
        i = layer // 2
        g_mix = norm_mix_g[layer].reshape(1, d)
        g_ffn = norm_ffn_g[layer].reshape(1, d)
        rw, rb = _router_pad(router_w[layer], router_b[layer])
        if layer % 2 == 0:
            prep = _prep_even(even_w_in[i], mla_q_norm_g[i], mla_w_uq[i], mla_kv_norm_g[i], mla_w_ukv[i],
                              mla_q_head_g[i], mla_k_head_g[i])
            q, k, v, u_t = _even_in_call(x, mod, posf, g_mix, prep)
            attn = _attn_call(q, k, v)
            disc = _s5_disc_call(s5_a_re[i], s5_a_im[i], s5_log_dt[i], s5_b_re[i], s5_b_im[i])
            ssm_t = _s5_call(u_t.reshape(seq, bsz, d // 2), disc, s5_c_re[i], s5_c_im[i], s5_d[i],
                             s5_glu_w[i], s5_glu_b[i])
            x_new, h2, te, tile_cnt = _mix_out_call(x, attn, ssm_t.reshape(seq, bsz * (d // 2)), mod,
                                                    even_w_out[i].astype(BF16), g_ffn, rw, rb)
        else:
            x_new, h2, te, tile_cnt = _odd_call(x, mod, g_mix, odd_w_in[i], pool_w[i], pool_scale[i],
                                                sgu_norm_g[i], sgu_w[i], sgu_b[i], odd_w_out[i], g_ffn, rw, rb)
        x = _moe(layer, x_new, h2, te, tile_cnt, mod, moe_w_gu, moe_b_gu, moe_w_dn, moe_b_dn)
    return x
```

```python
import functools
import math

import jax
import jax.numpy as jnp
from jax import lax
from jax.experimental import pallas as pl
from jax.experimental.pallas import tpu as pltpu
from jax.experimental.pallas import tpu_sc as plsc

F32 = jnp.float32
BF16 = jnp.bfloat16
HIGHEST = lax.Precision.HIGHEST

NORM_EPS = 1e-6
MLA_HEADS = 8
QK_NOPE_DIM = 64
QK_ROPE_DIM = 32
QK_HEAD_DIM = QK_NOPE_DIM + QK_ROPE_DIM
V_HEAD_DIM = 64
Q_LORA_RANK = 256
KV_LORA_RANK = 128
ROPE_THETA = 10000.0
S5_GROUP = 16
S5_STATE = 64
POOL_WINDOWS = (2, 4, 8, 16)
SGU_HEADS = 4
SGU_CHUNK = 128
N_EXPERTS = 32
TOP_K = 4
SWIGLU_ALPHA = 1.702
SWIGLU_LIMIT = 7.0

LANES = 128
SUBLANES = 8
HEAD_SLAB = LANES
POOL_HALO = 16
ROW_TILE = 512
ATTN_TILE = 512
ATTN_ROWS = 32
ATTN_HEADS = 4
S5_STEPS = 64
MOE_ROWS = 512
MOE_SPLITS = 2
SC_WORKERS = 32
SC_CHUNK = 64
VMEM_LIMIT = 56 * 1024 * 1024
NEG_BIG = -1e30


def _sigmoid(v):
    return 1.0 / (1.0 + jnp.exp(-v))


def _gelu(v):
    return 0.5 * v * (1.0 + jnp.tanh(math.sqrt(2.0 / math.pi) * (v + 0.044715 * (v * v * v))))


def _rms(v, width):
    return lax.rsqrt(jnp.sum(v * v, axis=-1, keepdims=True) * (1.0 / width) + NORM_EPS)


def _mod_norm(x, g, sc, sh):
    return x * _rms(x, x.shape[-1]) * g * (1.0 + sc) + sh


def _bdot(a, b):
    return jnp.dot(a.astype(BF16), b, preferred_element_type=F32)


def _pack_pairs(v):
    w = v.shape[-1] // 2
    bits = pltpu.bitcast(v.astype(BF16).astype(F32), jnp.uint32)
    return (bits[:, :w] >> 16) | bits[:, w:]


def _unpack_pairs(p):
    lo = pltpu.bitcast(p << 16, F32)
    hi = pltpu.bitcast(p & jnp.uint32(0xFFFF0000), F32)
    return lo.astype(BF16), hi.astype(BF16)


def _ada_kernel(c_ref, w_ref, b_ref, o_ref):
    c = c_ref[...]
    act = c * _sigmoid(c)
    o_ref[...] = jnp.dot(act, w_ref[...], precision=HIGHEST, preferred_element_type=F32) + b_ref[...]


def _ada_call(c, ada_w, ada_b):
    depth, d, n = ada_w.shape
    bsz = c.shape[0]
    tn = 1536
    return pl.pallas_call(
        _ada_kernel,
        grid=(depth, n // tn),
        in_specs=[
            pl.BlockSpec((bsz, d), lambda l, j: (0, 0)),
            pl.BlockSpec((None, d, tn), lambda l, j: (l, 0, j)),
            pl.BlockSpec((None, 1, tn), lambda l, j: (l, 0, j)),
        ],
        out_specs=pl.BlockSpec((None, bsz, tn), lambda l, j: (l, 0, j)),
        out_shape=jax.ShapeDtypeStruct((depth, bsz, n), F32),
        compiler_params=pltpu.CompilerParams(dimension_semantics=("parallel", "parallel"),
                                             vmem_limit_bytes=VMEM_LIMIT),
        name="ada_mod",
    )(c, ada_w, ada_b.reshape(depth, 1, n))


_C_Q = 0
_C_KV = Q_LORA_RANK
_C_PE = _C_KV + KV_LORA_RANK
_C_PESW = _C_PE + HEAD_SLAB
_C_U = _C_PESW + HEAD_SLAB


def _even_in_kernel(x_ref, mod_ref, pos_ref, g_ref, win_ref, gq_ref, wq_ref, gkv_ref, wk_ref, wv_ref,
                    tab_ref, q_ref, k_ref, v_ref, u_ref):
    x = x_ref[...]
    h = _mod_norm(x, g_ref[...], mod_ref[1:2, :], mod_ref[0:1, :])
    z = _bdot(h, win_ref[...])
    q_c = z[:, _C_Q:_C_KV]
    kv_c = z[:, _C_KV:_C_PE]
    kpe = z[:, _C_PE:_C_PESW]
    kpe_sw = z[:, _C_PESW:_C_U]
    u_ref[...] = z[:, _C_U:]

    ang = pos_ref[...] * tab_ref[0:1, :]
    cs = jnp.cos(ang)
    sn = jnp.sin(ang)
    gcq = cs * tab_ref[1:2, :]
    gsq = sn * tab_ref[2:3, :]
    gck = cs * tab_ref[3:4, :]
    gsk = sn * tab_ref[4:5, :]

    qn = q_c * _rms(q_c, Q_LORA_RANK) * gq_ref[...]
    qq = _bdot(qn, wq_ref[...])
    kvn = kv_c * _rms(kv_c, KV_LORA_RANK) * gkv_ref[...]
    kk = _bdot(kvn, wk_ref[...])
    v_ref[...] = _bdot(kvn, wv_ref[...]).astype(v_ref.dtype)

    pe_rot = kpe * gck + kpe_sw * gsk
    pe_ss = jnp.sum(kpe * kpe, axis=-1, keepdims=True)
    hw = MLA_HEADS * HEAD_SLAB
    for hd in range(MLA_HEADS):
        lo = hd * HEAD_SLAB
        qr = qq[:, lo:lo + HEAD_SLAB]
        qs = qq[:, hw + lo:hw + lo + HEAD_SLAB]
        rq = _rms(qr, QK_HEAD_DIM)
        q_ref[hd] = (rq * (qr * gcq + qs * gsq)).astype(q_ref.dtype)
        kr = kk[:, lo:lo + HEAD_SLAB]
        rk = lax.rsqrt((jnp.sum(kr * kr, axis=-1, keepdims=True) + pe_ss) * (1.0 / QK_HEAD_DIM) + NORM_EPS)
        k_ref[hd] = (rk * (kr * gck + pe_rot)).astype(k_ref.dtype)


def _even_in_call(x, mod, posf, g, prep):
    bsz, seq, d = x.shape
    tm = min(ROW_TILE, seq)
    hw = MLA_HEADS * HEAD_SLAB
    full = lambda a: pl.BlockSpec(a.shape, lambda b, s: (0,) * a.ndim)
    return pl.pallas_call(
        _even_in_kernel,
        grid=(bsz, seq // tm),
        in_specs=[
            pl.BlockSpec((None, tm, d), lambda b, s: (b, s, 0)),
            pl.BlockSpec((None, 6, d), lambda b, s: (b, 0, 0)),
            pl.BlockSpec((None, tm, 1), lambda b, s: (b, s, 0)),
            full(g), full(prep["w_in"]), full(prep["gq"]), full(prep["wq"]), full(prep["gkv"]),
            full(prep["wk"]), full(prep["wv"]), full(prep["tab"]),
        ],
        out_specs=[
            pl.BlockSpec((None, MLA_HEADS, tm, HEAD_SLAB), lambda b, s: (b, 0, s, 0)),
            pl.BlockSpec((None, MLA_HEADS, tm, HEAD_SLAB), lambda b, s: (b, 0, s, 0)),
            pl.BlockSpec((None, tm, MLA_HEADS * V_HEAD_DIM), lambda b, s: (b, s, 0)),
            pl.BlockSpec((tm, d // 2), lambda b, s: (s, b)),
        ],
        out_shape=[
            jax.ShapeDtypeStruct((bsz, MLA_HEADS, seq, HEAD_SLAB), BF16),
            jax.ShapeDtypeStruct((bsz, MLA_HEADS, seq, HEAD_SLAB), BF16),
            jax.ShapeDtypeStruct((bsz, seq, MLA_HEADS * V_HEAD_DIM), BF16),
            jax.ShapeDtypeStruct((seq, bsz * (d // 2)), F32),
        ],
        compiler_params=pltpu.CompilerParams(dimension_semantics=("parallel", "parallel"),
                                             vmem_limit_bytes=VMEM_LIMIT),
        name="even_in",
    )(x, mod, posf, g, prep["w_in"], prep["gq"], prep["wq"], prep["gkv"], prep["wk"], prep["wv"], prep["tab"])


def _prep_even(even_w_in, q_norm_g, w_uq, kv_norm_g, w_ukv, q_head_g, k_head_g):
    d = even_w_in.shape[0]
    half = QK_ROPE_DIM // 2
    nope = QK_NOPE_DIM
    c_pe = Q_LORA_RANK + KV_LORA_RANK
    w_pe = even_w_in[:, c_pe:c_pe + QK_ROPE_DIM]
    zeros = lambda n: jnp.zeros((d, n), F32)
    pe_slab = jnp.concatenate([zeros(nope), w_pe, zeros(HEAD_SLAB - QK_HEAD_DIM)], axis=1)
    pe_sw = jnp.concatenate([zeros(nope), -w_pe[:, half:], w_pe[:, :half], zeros(HEAD_SLAB - QK_HEAD_DIM)], axis=1)
    w_in = jnp.concatenate([even_w_in[:, :c_pe], pe_slab, pe_sw, even_w_in[:, c_pe + QK_ROPE_DIM:]], axis=1)

    r = w_uq.shape[0]
    padq = jnp.zeros((r, MLA_HEADS, HEAD_SLAB - QK_HEAD_DIM), F32)
    wq_plain = jnp.concatenate([w_uq, padq], axis=2).reshape(r, MLA_HEADS * HEAD_SLAB)
    wq_sw = jnp.concatenate([jnp.zeros((r, MLA_HEADS, nope), F32), -w_uq[:, :, nope + half:],
                             w_uq[:, :, nope:nope + half], padq], axis=2).reshape(r, MLA_HEADS * HEAD_SLAB)
    wq = jnp.concatenate([wq_plain, wq_sw], axis=1)

    rk = w_ukv.shape[0]
    wk = jnp.concatenate([w_ukv[:, :, :nope], jnp.zeros((rk, MLA_HEADS, HEAD_SLAB - nope), F32)],
                         axis=2).reshape(rk, MLA_HEADS * HEAD_SLAB)
    wv = w_ukv[:, :, nope:].reshape(rk, MLA_HEADS * V_HEAD_DIM)

    inv_freq = 1.0 / (ROPE_THETA ** (jnp.arange(half, dtype=F32) / half))
    pad_tail = jnp.zeros((HEAD_SLAB - QK_HEAD_DIM,), F32)
    freq_row = jnp.concatenate([jnp.zeros((nope,), F32), inv_freq, inv_freq, pad_tail])

    def gain_rows(gv, scale):
        plain = jnp.concatenate([gv, pad_tail]) * scale
        swapped = jnp.concatenate([jnp.zeros((nope,), F32), gv[nope + half:], gv[nope:nope + half], pad_tail]) * scale
        return plain, swapped

    gq_plain, gq_sw = gain_rows(q_head_g, QK_HEAD_DIM ** -0.5 * math.log2(math.e))
    gk_plain, gk_sw = gain_rows(k_head_g, 1.0)
    tab = jnp.stack([freq_row, gq_plain, gq_sw, gk_plain, gk_sw, freq_row * 0, freq_row * 0, freq_row * 0])
    return {
        "w_in": w_in.astype(BF16), "gq": q_norm_g.reshape(1, -1), "wq": wq.astype(BF16),
        "gkv": kv_norm_g.reshape(1, -1), "wk": wk.astype(BF16), "wv": wv.astype(BF16), "tab": tab,
    }


def _attn_kernel(qi_ref, kj_ref, q_ref, k_ref, v_ref, o_ref, m_sc, a_sc, acc_sc, s_sc, p_sc, *, tq, tk):
    step = pl.program_id(2)
    i = qi_ref[step]
    j = kj_ref[step]
    sum_lane = (V_HEAD_DIM, 0)

    @pl.when(j == 0)
    def _():
        m_sc[...] = jnp.full(m_sc.shape, -jnp.inf, F32)
        acc_sc[...] = jnp.zeros(acc_sc.shape, F32)

    def sweep(on_diagonal):
        lane = lax.broadcasted_iota(jnp.int32, (1, LANES), 1)
        for hh in range(ATTN_HEADS):
            s_sc[hh] = lax.dot_general(q_ref[hh], k_ref[hh], (((1,), (1,)), ((), ())),
                                       preferred_element_type=F32)
        for hh in range(ATTN_HEADS):
            v = v_ref[:, (hh // 2) * LANES:(hh // 2 + 1) * LANES]
            for r0 in range(0, tq, ATTN_ROWS):
                rows = pl.ds(r0, ATTN_ROWS)
                s = s_sc[hh, rows, :]
                if on_diagonal:
                    row = r0 + lax.broadcasted_iota(jnp.int32, (ATTN_ROWS, tk), 0)
                    col = lax.broadcasted_iota(jnp.int32, (ATTN_ROWS, tk), 1)
                    s = jnp.where(col <= row, s, -jnp.inf)
                m_prev = m_sc[hh, rows, :]
                m_new = jnp.maximum(m_prev, jnp.max(s, axis=-1, keepdims=True))
                a_sc[hh, rows, :] = jnp.exp2(m_prev - m_new)
                m_sc[hh, rows, :] = m_new
                shifted = s - jnp.concatenate([m_new] * (tk // LANES), axis=1)
                p_sc[hh, rows, :] = jnp.exp2(shifted.astype(BF16))
            own = (lane < V_HEAD_DIM) == (hh % 2 == 0)
            ones = jnp.where(lane == sum_lane[hh % 2], 1.0, 0.0).astype(v.dtype)
            vh = jnp.where(own, v, jnp.broadcast_to(ones, v.shape))
            acc_sc[hh] = acc_sc[hh] * a_sc[hh] + jnp.dot(p_sc[hh], vh, preferred_element_type=F32)

    @pl.when(j < i)
    def _():
        sweep(False)

    @pl.when(j == i)
    def _():
        sweep(True)
        lane = lax.broadcasted_iota(jnp.int32, (1, LANES), 1)
        for pp in range(ATTN_HEADS // 2):
            acc0 = acc_sc[2 * pp]
            acc1 = acc_sc[2 * pp + 1]
            l0 = acc0[:, sum_lane[0]:sum_lane[0] + 1]
            l1 = acc1[:, sum_lane[1]:sum_lane[1] + 1]
            o_ref[:, pp * LANES:(pp + 1) * LANES] = jnp.where(lane < V_HEAD_DIM, acc0 / l0,
                                                              acc1 / l1).astype(o_ref.dtype)


def _attn_call(q, k, v):
    bsz, nh, seq, _ = q.shape
    tq = tk = min(ATTN_TILE, seq)
    nq = seq // tq
    pairs = [(i, j) for i in range(nq) for j in range(i + 1)]
    qi = jnp.asarray([p[0] for p in pairs], jnp.int32)
    kj = jnp.asarray([p[1] for p in pairs], jnp.int32)
    kern = functools.partial(_attn_kernel, tq=tq, tk=tk)
    hp = ATTN_HEADS
    assert nh % hp == 0
    return pl.pallas_call(
        kern,
        grid_spec=pltpu.PrefetchScalarGridSpec(
            num_scalar_prefetch=2,
            grid=(bsz, nh // hp, len(pairs)),
            in_specs=[
                pl.BlockSpec((None, hp, tq, HEAD_SLAB), lambda b, h, p, qi, kj: (b, h, qi[p], 0)),
                pl.BlockSpec((None, hp, tk, HEAD_SLAB), lambda b, h, p, qi, kj: (b, h, kj[p], 0)),
                pl.BlockSpec((None, tk, hp * V_HEAD_DIM), lambda b, h, p, qi, kj: (b, kj[p], h)),
            ],
            out_specs=pl.BlockSpec((None, tq, hp * V_HEAD_DIM), lambda b, h, p, qi, kj: (b, qi[p], h)),
            scratch_shapes=[pltpu.VMEM((hp, tq, LANES), F32), pltpu.VMEM((hp, tq, LANES), F32),
                            pltpu.VMEM((hp, tq, LANES), F32),
                            pltpu.VMEM((hp, tq, tk), F32), pltpu.VMEM((hp, tq, tk), BF16)],
        ),
        out_shape=jax.ShapeDtypeStruct((bsz, seq, nh * V_HEAD_DIM), BF16),
        compiler_params=pltpu.CompilerParams(
            dimension_semantics=("parallel", "parallel", "arbitrary"),
            vmem_limit_bytes=VMEM_LIMIT),
        name="mla_attention",
    )(qi, kj, q, k, v)


def _s5_disc_kernel(are_ref, aim_ref, ldt_ref, bre_ref, bim_ref, abre_ref, abim_ref, bbre_ref, bbim_ref):
    dt = jnp.exp(ldt_ref[...])
    lam_re = jnp.minimum(are_ref[...], -1e-4)
    lam_im = aim_ref[...]
    mag = jnp.exp(lam_re * dt)
    ab_re = mag * jnp.cos(lam_im * dt)
    ab_im = mag * jnp.sin(lam_im * dt)
    den = lam_re * lam_re + lam_im * lam_im
    num_re = ab_re - 1.0
    f_re = (num_re * lam_re + ab_im * lam_im) / den
    f_im = (ab_im * lam_re - num_re * lam_im) / den
    abre_ref[...] = ab_re
    abim_ref[...] = ab_im
    br = bre_ref[...]
    bi = bim_ref[...]
    bbre_ref[...] = f_re[:, None, :] * br - f_im[:, None, :] * bi
    bbim_ref[...] = f_re[:, None, :] * bi + f_im[:, None, :] * br


def _s5_disc_call(a_re, a_im, log_dt, b_re, b_im):
    g, p = a_re.shape
    bre_t = jnp.swapaxes(b_re, 1, 2)
    bim_t = jnp.swapaxes(b_im, 1, 2)
    return pl.pallas_call(
        _s5_disc_kernel,
        out_shape=[jax.ShapeDtypeStruct((g, p), F32), jax.ShapeDtypeStruct((g, p), F32),
                   jax.ShapeDtypeStruct(bre_t.shape, F32), jax.ShapeDtypeStruct(bre_t.shape, F32)],
        name="s5_discretize",
    )(a_re, a_im, log_dt.reshape(g, 1), bre_t, bim_t)


def _block_diag_halves(m):
    g, r, c = m.shape
    gh = g // 2
    eye = jnp.eye(gh, dtype=m.dtype)
    mh = m.reshape(2, gh, r, c)
    return (mh[:, :, :, None, :] * eye[None, :, None, :, None]).reshape(2, gh * r, gh * c)


def _s5_kernel(u_ref, bre_ref, bim_ref, are_ref, aim_ref, cre_ref, cim_ref, d_ref, gw_ref, gb_ref,
               o_ref, sre, sim, dre, dim, *, steps):
    @pl.when(pl.program_id(0) == 0)
    def _():
        sre[...] = jnp.zeros(sre.shape, F32)
        sim[...] = jnp.zeros(sim.shape, F32)

    rows = steps * SUBLANES
    w = u_ref.shape[-1]
    u = u_ref[...].reshape(rows, w)
    ub = u.astype(BF16)
    kh = w // 2
    nh = dre.shape[1] // 2
    for hf in range(2):
        dre[:, hf * nh:(hf + 1) * nh] = jnp.dot(ub[:, hf * kh:(hf + 1) * kh], bre_ref[hf], preferred_element_type=F32)
        dim[:, hf * nh:(hf + 1) * nh] = jnp.dot(ub[:, hf * kh:(hf + 1) * kh], bim_ref[hf], preferred_element_type=F32)

    a_r = are_ref[...]
    a_i = aim_ref[...]

    def body(t, carry):
        xr, xi = carry
        r0 = pl.multiple_of(t * SUBLANES, SUBLANES)
        nr = a_r * xr - a_i * xi + dre[pl.ds(r0, SUBLANES), :]
        ni = a_r * xi + a_i * xr + dim[pl.ds(r0, SUBLANES), :]
        dre[pl.ds(r0, SUBLANES), :] = nr
        dim[pl.ds(r0, SUBLANES), :] = ni
        return nr, ni

    xr, xi = lax.fori_loop(0, steps, body, (sre[...], sim[...]))
    sre[...] = xr
    sim[...] = xi

    ys = []
    for hf in range(2):
        yr = jnp.dot(dre[:, hf * nh:(hf + 1) * nh].astype(BF16), cre_ref[hf], preferred_element_type=F32)
        yi = jnp.dot(dim[:, hf * nh:(hf + 1) * nh].astype(BF16), cim_ref[hf], preferred_element_type=F32)
        ys.append(yr - yi)
    y = jnp.concatenate(ys, axis=1) + d_ref[...] * u
    g = _gelu(y)
    out = g * _sigmoid(_bdot(g, gw_ref[...]) + gb_ref[...])
    o_ref[...] = out.reshape(steps, SUBLANES, w).astype(o_ref.dtype)


def _s5_call(u_t, disc, c_re, c_im, d_skip, glu_w, glu_b):
    seq, bsz, w = u_t.shape
    assert bsz == SUBLANES
    ab_re, ab_im, bb_re, bb_im = disc
    g, p = ab_re.shape
    n_state = g * p
    bre = _block_diag_halves(bb_re).astype(BF16)
    bim = _block_diag_halves(bb_im).astype(BF16)
    cre = _block_diag_halves(jnp.swapaxes(c_re, 1, 2)).astype(BF16)
    cim = _block_diag_halves(jnp.swapaxes(c_im, 1, 2)).astype(BF16)
    steps = min(S5_STEPS, seq)
    full = lambda a: pl.BlockSpec(a.shape, lambda s: (0,) * a.ndim)
    args = (bre, bim, ab_re.reshape(1, n_state), ab_im.reshape(1, n_state), cre, cim,
            d_skip.reshape(1, w), glu_w.astype(BF16), glu_b.reshape(1, w))
    return pl.pallas_call(
        functools.partial(_s5_kernel, steps=steps),
        grid=(seq // steps,),
        in_specs=[pl.BlockSpec((steps, bsz, w), lambda s: (s, 0, 0))] + [full(a) for a in args],
        out_specs=pl.BlockSpec((steps, bsz, w), lambda s: (s, 0, 0)),
        out_shape=jax.ShapeDtypeStruct((seq, bsz, w), BF16),
        scratch_shapes=[pltpu.VMEM((bsz, n_state), F32), pltpu.VMEM((bsz, n_state), F32),
                        pltpu.VMEM((steps * bsz, n_state), F32), pltpu.VMEM((steps * bsz, n_state), F32)],
        compiler_params=pltpu.CompilerParams(dimension_semantics=("arbitrary",), vmem_limit_bytes=VMEM_LIMIT),
        name="s5_scan",
    )(u_t, *args)


def _router_tail(x_new, mod_ref, gf_ref, rw_ref, rb_ref, h2_ref, te_ref, cnt_ref):
    h2 = _mod_norm(x_new, gf_ref[...], mod_ref[4:5, :], mod_ref[3:4, :])
    h2_ref[...] = _pack_pairs(h2)
    h_hi = h2.astype(BF16)
    h_lo = (h2 - h_hi.astype(F32)).astype(BF16)
    r_hi = jnp.dot(h_hi, rw_ref[...], preferred_element_type=F32)
    r_lo = jnp.dot(h_lo, rw_ref[...], preferred_element_type=F32)
    logits = r_hi[:, :LANES] + r_hi[:, LANES:] + r_lo[:, :LANES] + rb_ref[...]
    lane = lax.broadcasted_iota(jnp.int32, logits.shape, 1).astype(F32)
    vals = []
    idxs = []
    work = logits
    for _ in range(TOP_K):
        m = jnp.max(work, axis=-1, keepdims=True)
        idx = jnp.min(jnp.where(work == m, lane, float(LANES)), axis=-1, keepdims=True)
        vals.append(m)
        idxs.append(idx)
        work = jnp.where(lane == idx, NEG_BIG * 2.0, work)
    exps = [jnp.exp(vv - vals[0]) for vv in vals]
    tot = exps[0] + exps[1] + exps[2] + exps[3]
    te = jnp.zeros(logits.shape, F32)
    picked = jnp.zeros(logits.shape, F32)
    for kk in range(TOP_K):
        te = jnp.where(lane == float(kk), idxs[kk], te)
        te = jnp.where(lane == float(TOP_K + kk), exps[kk] / tot, te)
        picked = picked + jnp.where(lane == idxs[kk], 1.0, 0.0)
    te_ref[...] = te[:, :2 * TOP_K]
    cnt_ref[...] = jnp.sum(picked, axis=0, keepdims=True)


def _mix_out_kernel(x_ref, a_ref, s_ref, mod_ref, wo_ref, gf_ref, rw_ref, rb_ref, xo_ref, h2_ref, te_ref,
                    cnt_ref):
    ka = a_ref.shape[-1]
    mix = jnp.dot(a_ref[...], wo_ref[:ka, :], preferred_element_type=F32)
    mix = mix + jnp.dot(s_ref[...], wo_ref[ka:, :], preferred_element_type=F32)
    x_new = x_ref[...] + mod_ref[2:3, :] * mix
    xo_ref[...] = x_new
    _router_tail(x_new, mod_ref, gf_ref, rw_ref, rb_ref, h2_ref, te_ref, cnt_ref)


def _tail_out_specs(bsz, seq, tm, d):
    nt = seq // tm
    specs = [
        pl.BlockSpec((None, tm, d), lambda b, s: (b, s, 0)),
        pl.BlockSpec((tm, d // 2), lambda b, s: (b * nt + s, 0)),
        pl.BlockSpec((tm, 2 * TOP_K), lambda b, s: (b * nt + s, 0)),
        pl.BlockSpec((None, 1, LANES), lambda b, s: (b * nt + s, 0, 0)),
    ]
    shapes = [
        jax.ShapeDtypeStruct((bsz, seq, d), F32),
        jax.ShapeDtypeStruct((bsz * seq, d // 2), jnp.uint32),
        jax.ShapeDtypeStruct((bsz * seq, 2 * TOP_K), F32),
        jax.ShapeDtypeStruct((bsz * nt, 1, LANES), F32),
    ]
    return specs, shapes


def _router_pad(router_w, router_b):
    d, e = router_w.shape
    rw = jnp.concatenate([router_w, jnp.zeros((d, LANES - e), F32)], axis=1)
    rw_hi = rw.astype(BF16)
    rw_lo = (rw - rw_hi.astype(F32)).astype(BF16)
    rb = jnp.concatenate([router_b, jnp.full((LANES - e,), NEG_BIG, F32)]).reshape(1, LANES)
    return jnp.concatenate([rw_hi, rw_lo], axis=1), rb


def _mix_out_call(x, attn, ssm_t, mod, w_out, gf, rw, rb):
    bsz, seq, d = x.shape
    tm = min(ROW_TILE, seq)
    ka = attn.shape[-1]
    ks = ssm_t.shape[-1] // bsz
    full = lambda a: pl.BlockSpec(a.shape, lambda b, s: (0,) * a.ndim)
    out_specs, out_shape = _tail_out_specs(bsz, seq, tm, d)
    return pl.pallas_call(
        _mix_out_kernel,
        grid=(bsz, seq // tm),
        in_specs=[
            pl.BlockSpec((None, tm, d), lambda b, s: (b, s, 0)),
            pl.BlockSpec((None, tm, ka), lambda b, s: (b, s, 0)),
            pl.BlockSpec((tm, ks), lambda b, s: (s, b)),
            pl.BlockSpec((None, 6, d), lambda b, s: (b, 0, 0)),
            full(w_out), full(gf), full(rw), full(rb),
        ],
        out_specs=out_specs,
        out_shape=out_shape,
        compiler_params=pltpu.CompilerParams(dimension_semantics=("parallel", "parallel"),
                                             vmem_limit_bytes=VMEM_LIMIT),
        name="even_out",
    )(x, attn, ssm_t, mod, w_out, gf, rw, rb)


def _odd_kernel(x_ref, mod_ref, g_ref, win_ref, icnt_ref, wp_ref, ps_ref, gv_ref, wsp_ref, bsp_ref,
                wo_ref, gf_ref, rw_ref, rb_ref, xo_ref, h2_ref, te_ref, cnt_ref, ext_sc):
    tm = x_ref.shape[0]
    pw = wp_ref.shape[-1]
    width = pw * len(POOL_WINDOWS)

    @pl.when(pl.program_id(1) == 0)
    def _():
        ext_sc[0:POOL_HALO, :] = jnp.zeros((POOL_HALO, width), F32)

    x = x_ref[...]
    h = _mod_norm(x, g_ref[...], mod_ref[1:2, :], mod_ref[0:1, :])
    z = _bdot(h, win_ref[...])
    up = z[:, :width]
    ext_sc[POOL_HALO:POOL_HALO + tm, :] = up

    pooled = []
    for gi, win in enumerate(POOL_WINDOWS):
        cols = slice(gi * pw, (gi + 1) * pw)
        acc = up[:, cols]
        for lag in range(1, win):
            acc = acc + ext_sc[POOL_HALO - lag:POOL_HALO - lag + tm, cols]
        pg = acc * icnt_ref[:, gi:gi + 1] - up[:, cols]
        pooled.append(_bdot(pg, wp_ref[gi]) * ps_ref[:, cols])
    ext_sc[0:POOL_HALO, :] = ext_sc[tm:tm + POOL_HALO, :]
    pooled = jnp.concatenate(pooled, axis=1)

    ug = _gelu(z[:, width:2 * width])
    vg = _gelu(z[:, 2 * width:])
    vn = (vg * _rms(vg, width) * gv_ref[...]).astype(BF16)
    hd = width // SGU_HEADS
    chunks = []
    for ci in range(tm // SGU_CHUNK):
        heads = []
        for hh in range(SGU_HEADS):
            blk = vn[ci * SGU_CHUNK:(ci + 1) * SGU_CHUNK, hh * hd:(hh + 1) * hd]
            heads.append(jnp.dot(wsp_ref[hh], blk, preferred_element_type=F32) + bsp_ref[hh])
        chunks.append(jnp.concatenate(heads, axis=1))
    gated = ug * jnp.concatenate(chunks, axis=0)

    mix = _bdot(pooled, wo_ref[:width, :]) + _bdot(gated, wo_ref[width:, :])
    x_new = x + mod_ref[2:3, :] * mix
    xo_ref[...] = x_new
    _router_tail(x_new, mod_ref, gf_ref, rw_ref, rb_ref, h2_ref, te_ref, cnt_ref)


def _odd_call(x, mod, g, w_in, pool_w, pool_scale, sgu_norm_g, sgu_w, sgu_b, w_out, gf, rw, rb):
    bsz, seq, d = x.shape
    tm = min(ROW_TILE, seq)
    width = pool_scale.shape[0]
    hd = width // SGU_HEADS
    t = jnp.arange(seq, dtype=jnp.int32)
    icnt = jnp.stack([1.0 / jnp.minimum(t + 1, wn).astype(F32) for wn in POOL_WINDOWS], axis=1)
    wsp = jnp.tril(sgu_w).astype(BF16)
    bsp = jnp.broadcast_to(sgu_b[:, :, None], (SGU_HEADS, SGU_CHUNK, hd))
    args = (g, w_in.astype(BF16), icnt, pool_w.astype(BF16), pool_scale.reshape(1, width),
            sgu_norm_g.reshape(1, width), wsp, bsp, w_out.astype(BF16), gf, rw, rb)
    full = lambda a: pl.BlockSpec(a.shape, lambda b, s: (0,) * a.ndim)
    in_specs = [pl.BlockSpec((None, tm, d), lambda b, s: (b, s, 0)),
                pl.BlockSpec((None, 6, d), lambda b, s: (b, 0, 0))]
    for idx, a in enumerate(args):
        in_specs.append(pl.BlockSpec((tm, len(POOL_WINDOWS)), lambda b, s: (s, 0)) if idx == 2 else full(a))
    out_specs, out_shape = _tail_out_specs(bsz, seq, tm, d)
    return pl.pallas_call(
        _odd_kernel,
        grid=(bsz, seq // tm),
        in_specs=in_specs,
        out_specs=out_specs,
        out_shape=out_shape,
        scratch_shapes=[pltpu.VMEM((tm + POOL_HALO, width), F32)],
        compiler_params=pltpu.CompilerParams(dimension_semantics=("parallel", "arbitrary"),
                                             vmem_limit_bytes=VMEM_LIMIT),
        name="odd_mixer",
    )(x, mod, *args)


def _dest_kernel(te_ref, base_ref, dst_ref):
    tr = te_ref.shape[0]
    te = te_ref[...]
    lane = lax.broadcasted_iota(jnp.int32, (tr, LANES), 1).astype(F32)
    hots = [te[:, kk:kk + 1] == lane for kk in range(TOP_K)]
    oh = jnp.zeros((tr, LANES), F32)
    for hot in hots:
        oh = oh + jnp.where(hot, 1.0, 0.0)
    r_i = lax.broadcasted_iota(jnp.int32, (tr, tr), 0)
    c_i = lax.broadcasted_iota(jnp.int32, (tr, tr), 1)
    tri = jnp.where(c_i < r_i, 1.0, 0.0).astype(BF16)
    before = jnp.dot(tri, oh.astype(BF16), preferred_element_type=F32) + base_ref[...]
    dst = jnp.zeros((tr, LANES), F32)
    for kk, hot in enumerate(hots):
        dst = jnp.where(lane == float(kk), jnp.sum(jnp.where(hot, before, 0.0), axis=-1, keepdims=True), dst)
    dst_ref[...] = dst.T[:2 * TOP_K, :].astype(jnp.int32)


def _dest_call(te, base, tok0, n_tok):
    tiles = base.shape[0]
    tr = n_tok // tiles
    tile0 = tok0 // tr
    return pl.pallas_call(
        _dest_kernel,
        grid=(tiles,),
        in_specs=[pl.BlockSpec((tr, 2 * TOP_K), lambda i: (tile0 + i, 0)),
                  pl.BlockSpec((None, 1, LANES), lambda i: (i, 0, 0))],
        out_specs=pl.BlockSpec((2 * TOP_K, tr), lambda i: (0, i)),
        out_shape=jax.ShapeDtypeStruct((2 * TOP_K, n_tok), jnp.int32),
        compiler_params=pltpu.CompilerParams(dimension_semantics=("parallel",)),
        name="route_dest",
    )(te, base)


def _sc_gather(table, idx):
    n = idx.shape[0]
    per_w = n // SC_WORKERS
    assert per_w * SC_WORKERS == n and per_w % SC_CHUNK == 0
    n_chunks = per_w // SC_CHUNK
    row_shape = table.shape[1:]
    mesh = plsc.VectorSubcoreMesh(core_axis_name="c", subcore_axis_name="s")

    @functools.partial(
        pl.kernel, mesh=mesh,
        out_type=jax.ShapeDtypeStruct((n,) + row_shape, table.dtype),
        scratch_types=[pltpu.VMEM((SC_CHUNK,), jnp.int32), pltpu.VMEM((SC_CHUNK,) + row_shape, table.dtype),
                       pltpu.SemaphoreType.DMA],
        name="sc_row_gather",
    )
    def gather(table_hbm, idx_hbm, out_hbm, idx_v, rows_v, sem):
        wid = lax.axis_index("s") * 2 + lax.axis_index("c")
        base = wid * per_w

        @pl.loop(0, n_chunks)
        def _(ci):
            off = pl.multiple_of(base + ci * SC_CHUNK, SC_CHUNK)
            pltpu.sync_copy(idx_hbm.at[pl.ds(off, SC_CHUNK)], idx_v)
            pltpu.async_copy(table_hbm.at[idx_v], rows_v, sem).wait()
            pltpu.sync_copy(rows_v, out_hbm.at[pl.ds(off, SC_CHUNK)])

    return gather(table, idx)


def _sc_scatter(rows, dests, n_out, tok0):
    t = dests[0].shape[0]
    per_w = t // SC_WORKERS
    assert per_w * SC_WORKERS == t and per_w % SC_CHUNK == 0
    n_chunks = per_w // SC_CHUNK
    row_shape = rows.shape[1:]
    nk = len(dests)
    mesh = plsc.VectorSubcoreMesh(core_axis_name="c", subcore_axis_name="s")

    @functools.partial(
        pl.kernel, mesh=mesh,
        out_type=jax.ShapeDtypeStruct((n_out,) + row_shape, rows.dtype),
        scratch_types=[pltpu.VMEM((SC_CHUNK,), jnp.int32)] * nk
        + [pltpu.VMEM((SC_CHUNK,) + row_shape, rows.dtype), pltpu.SemaphoreType.DMA],
        name="sc_row_scatter",
    )
    def scatter(rows_hbm, *rest):
        dest_hbm = rest[:nk]
        out_hbm = rest[nk]
        idx_v = rest[nk + 1:2 * nk + 1]
        rows_v, sem = rest[2 * nk + 1:]
        wid = lax.axis_index("s") * 2 + lax.axis_index("c")
        base = wid * per_w

        @pl.loop(0, n_chunks)
        def _(ci):
            off = pl.multiple_of(base + ci * SC_CHUNK, SC_CHUNK)
            src = pl.multiple_of(tok0 + off, SC_CHUNK)
            pltpu.sync_copy(rows_hbm.at[pl.ds(src, SC_CHUNK)], rows_v)
            for kk in range(nk):
                pltpu.sync_copy(dest_hbm[kk].at[pl.ds(off, SC_CHUNK)], idx_v[kk])
            copies = [pltpu.async_copy(rows_v, out_hbm.at[idx_v[kk]], sem) for kk in range(nk)]
            for cp in copies:
                cp.wait()

    return scatter(rows, *dests)


def _expert_kernel(be_ref, nv_ref, nu_ref, x_ref, wgu_ref, bgu_ref, wdn_ref, bdn_ref, y_ref, wgu_bf, wdn_bf):
    i = pl.program_id(0)
    used = i < nu_ref[0]
    prev = be_ref[jnp.maximum(i - 1, 0)]
    fresh = jnp.logical_or(i == 0, be_ref[i] != prev)

    @pl.when(jnp.logical_and(used, fresh))
    def _():
        wgu_bf[...] = wgu_ref[...].astype(BF16)
        wdn_bf[...] = wdn_ref[...].astype(BF16)

    def ffn(rows):
        x_lo, x_hi = _unpack_pairs(x_ref[0:rows, :])
        dh = x_lo.shape[-1]
        z = jnp.dot(x_lo, wgu_bf[:dh, :], preferred_element_type=F32)
        z = z + jnp.dot(x_hi, wgu_bf[dh:, :], preferred_element_type=F32) + bgu_ref[...]
        ff = z.shape[-1] // 2
        gate = jnp.minimum(z[:, :ff], SWIGLU_LIMIT)
        lin = jnp.clip(z[:, ff:], -SWIGLU_LIMIT, SWIGLU_LIMIT)
        act = gate * _sigmoid(SWIGLU_ALPHA * gate) * (lin + 1.0)
        y = _bdot(act, wdn_bf[...]) + bdn_ref[...]
        y_ref[0:rows, :] = _pack_pairs(y)

    half = x_ref.shape[0] // 2
    few = nv_ref[i] <= half

    @pl.when(jnp.logical_and(used, jnp.logical_not(few)))
    def _():
        ffn(x_ref.shape[0])

    @pl.when(jnp.logical_and(used, few))
    def _():
        ffn(half)


def _expert_call(layer, block_e, block_valid, n_used, xs, w_gu, b_gu, w_dn, b_dn):
    n_rows, dh = xs.shape
    depth, e, d, ff2 = w_gu.shape
    ff = ff2 // 2
    nb = n_rows // MOE_ROWS
    row_map = lambda i, be, nv, nu: (jnp.minimum(i, nu[0] - 1), 0)
    w_map = lambda i, be, nv, nu: (layer, be[i], 0, 0)
    return pl.pallas_call(
        _expert_kernel,
        grid_spec=pltpu.PrefetchScalarGridSpec(
            num_scalar_prefetch=3,
            grid=(nb,),
            in_specs=[
                pl.BlockSpec((MOE_ROWS, dh), row_map),
                pl.BlockSpec((None, None, d, ff2), w_map),
                pl.BlockSpec((None, None, 1, ff2), w_map),
                pl.BlockSpec((None, None, ff, d), w_map),
                pl.BlockSpec((None, None, 1, d), w_map),
            ],
            out_specs=pl.BlockSpec((MOE_ROWS, dh), row_map),
            scratch_shapes=[pltpu.VMEM((d, ff2), BF16), pltpu.VMEM((ff, d), BF16)],
        ),
        out_shape=jax.ShapeDtypeStruct((n_rows, dh), jnp.uint32),
        compiler_params=pltpu.CompilerParams(dimension_semantics=("arbitrary",), vmem_limit_bytes=VMEM_LIMIT),
        name="moe_experts",
    )(block_e, block_valid, n_used, xs, w_gu, b_gu.reshape(depth, e, 1, ff2), w_dn, b_dn.reshape(depth, e, 1, d))


def _combine_kernel(x_ref, yg_ref, te_ref, mod_ref, *rest):
    o_ref = rest[-1]
    te = te_ref[...]
    acc_lo = None
    for kk in range(TOP_K):
        lo, hi = _unpack_pairs(yg_ref[kk])
        gate = te[:, TOP_K + kk:TOP_K + kk + 1]
        acc_lo = gate * lo.astype(F32) if acc_lo is None else acc_lo + gate * lo.astype(F32)
        acc_hi = gate * hi.astype(F32) if kk == 0 else acc_hi + gate * hi.astype(F32)
    acc = jnp.concatenate([acc_lo, acc_hi], axis=1)
    o_ref[...] = x_ref[...] + mod_ref[5:6, :] * acc


def _combine_call(x, yg, te, mod, prev, b0, nb):
    bsz, seq, d = x.shape
    tm = min(ROW_TILE, seq)
    nt = seq // tm
    in_specs = [
        pl.BlockSpec((None, tm, d), lambda b, s: (b0 + b, s, 0)),
        pl.BlockSpec((TOP_K, tm, d // 2), lambda b, s: (0, b * nt + s, 0)),
        pl.BlockSpec((tm, 2 * TOP_K), lambda b, s: ((b0 + b) * nt + s, 0)),
        pl.BlockSpec((None, 6, d), lambda b, s: (b0 + b, 0, 0)),
    ]
    args = [x, yg, te, mod]
    aliases = {}
    if prev is not None:
        in_specs.append(pl.BlockSpec(memory_space=pl.ANY))
        args.append(prev)
        aliases = {len(args) - 1: 0}
    return pl.pallas_call(
        _combine_kernel,
        grid=(nb, nt),
        in_specs=in_specs,
        out_specs=pl.BlockSpec((None, tm, d), lambda b, s: (b0 + b, s, 0)),
        out_shape=jax.ShapeDtypeStruct((bsz, seq, d), F32),
        input_output_aliases=aliases,
        compiler_params=pltpu.CompilerParams(dimension_semantics=("parallel", "parallel"),
                                             vmem_limit_bytes=VMEM_LIMIT),
        name="moe_combine",
    )(*args)


def _moe(layer, x_new, h2, te, tile_cnt, mod, w_gu, b_gu, w_dn, b_dn):
    bsz, seq, _ = x_new.shape
    t, dh = h2.shape
    tiles = tile_cnt.shape[0]
    splits = MOE_SPLITS if bsz % MOE_SPLITS == 0 else 1
    gt = t // splits
    gtiles = tiles // splits
    gb = bsz // splits
    n_rows = -(-(gt * TOP_K + N_EXPERTS * (MOE_ROWS - 1)) // MOE_ROWS) * MOE_ROWS
    nb = n_rows // MOE_ROWS
    first_row = jnp.arange(nb, dtype=jnp.int32) * MOE_ROWS
    upto = jnp.arange(LANES)[:, None] <= jnp.arange(LANES)[None, :]
    all_cnt = tile_cnt[:, 0, :].astype(jnp.int32)
    out = None
    for gi in range(splits):
        g_cnt = all_cnt[gi * gtiles:(gi + 1) * gtiles]
        counts = jnp.sum(g_cnt, axis=0)
        padded = (counts + MOE_ROWS - 1) // MOE_ROWS * MOE_ROWS
        pad_end = jnp.sum(jnp.where(upto, padded[:, None], 0), axis=0)
        pad_start = pad_end - padded
        tile_base = pad_start[None, :] + jnp.cumsum(g_cnt, axis=0) - g_cnt
        dest = _dest_call(te, tile_base.astype(F32)[:, None, :], gi * gt, gt)
        dests = [dest[kk] for kk in range(TOP_K)]
        n_used = (pad_end[N_EXPERTS - 1] // MOE_ROWS).astype(jnp.int32).reshape(1)
        block_e = jnp.minimum(jnp.sum(pad_end[None, :N_EXPERTS] <= first_row[:, None], axis=1),
                              N_EXPERTS - 1).astype(jnp.int32)
        valid_end = (pad_start + counts)[block_e]
        block_valid = jnp.clip(valid_end - first_row, 0, MOE_ROWS).astype(jnp.int32)
        xs = _sc_scatter(h2, dests, n_rows, gi * gt)
        y = _expert_call(layer, block_e, block_valid, n_used, xs, w_gu, b_gu, w_dn, b_dn)
        yg = _sc_gather(y, dest[:TOP_K].reshape(-1)).reshape(TOP_K, gt, dh)
        out = _combine_call(x_new, yg, te, mod, out, gi * gb, gb)
    return out


def kernel(x, c, positions, ada_w, ada_b, norm_mix_g, norm_ffn_g, router_w, router_b, moe_w_gu, moe_b_gu,
           moe_w_dn, moe_b_dn, even_w_in, mla_q_norm_g, mla_w_uq, mla_kv_norm_g, mla_w_ukv, mla_q_head_g,
           mla_k_head_g, s5_a_re, s5_a_im, s5_log_dt, s5_b_re, s5_b_im, s5_c_re, s5_c_im, s5_d, s5_glu_w,
           s5_glu_b, even_w_out, odd_w_in, pool_w, pool_scale, sgu_norm_g, sgu_w, sgu_b, odd_w_out):
    bsz, seq, d = x.shape
    depth = ada_w.shape[0]
    mods = _ada_call(c, ada_w, ada_b).reshape(depth, bsz, 6, d)
    posf = positions.astype(F32).reshape(bsz, seq, 1)
    for layer in range(depth):
        mod = mods[layer]
        i = layer // 2
        g_mix = norm_mix_g[layer].reshape(1, d)
        g_ffn = norm_ffn_g[layer].reshape(1, d)
        rw, rb = _router_pad(router_w[layer], router_b[layer])
        if layer % 2 == 0:
            prep = _prep_even(even_w_in[i], mla_q_norm_g[i], mla_w_uq[i], mla_kv_norm_g[i], mla_w_ukv[i],
                              mla_q_head_g[i], mla_k_head_g[i])
            q, k, v, u_t = _even_in_call(x, mod, posf, g_mix, prep)
            attn = _attn_call(q, k, v)
            disc = _s5_disc_call(s5_a_re[i], s5_a_im[i], s5_log_dt[i], s5_b_re[i], s5_b_im[i])
            ssm_t = _s5_call(u_t.reshape(seq, bsz, d // 2), disc, s5_c_re[i], s5_c_im[i], s5_d[i],
                             s5_glu_w[i], s5_glu_b[i])
            x_new, h2, te, tile_cnt = _mix_out_call(x, attn, ssm_t.reshape(seq, bsz * (d // 2)), mod,
                                                    even_w_out[i].astype(BF16), g_ffn, rw, rb)
        else:
            x_new, h2, te, tile_cnt = _odd_call(x, mod, g_mix, odd_w_in[i], pool_w[i], pool_scale[i],
                                                sgu_norm_g[i], sgu_w[i], sgu_b[i], odd_w_out[i], g_ffn, rw, rb)
        x = _moe(layer, x_new, h2, te, tile_cnt, mod, moe_w_gu, moe_b_gu, moe_w_dn, moe_b_dn)
    return x
```

```python
import functools
import math

import jax
import jax.numpy as jnp
from jax import lax
from jax.experimental import pallas as pl
from jax.experimental.pallas import tpu as pltpu
from jax.experimental.pallas import tpu_sc as plsc

F32 = jnp.float32
BF16 = jnp.bfloat16
HIGHEST = lax.Precision.HIGHEST

NORM_EPS = 1e-6
MLA_HEADS = 8
QK_NOPE_DIM = 64
QK_ROPE_DIM = 32
QK_HEAD_DIM = QK_NOPE_DIM + QK_ROPE_DIM
V_HEAD_DIM = 64
Q_LORA_RANK = 256
KV_LORA_RANK = 128
ROPE_THETA = 10000.0
S5_GROUP = 16
S5_STATE = 64
POOL_WINDOWS = (2, 4, 8, 16)
SGU_HEADS = 4
SGU_CHUNK = 128
N_EXPERTS = 32
TOP_K = 4
SWIGLU_ALPHA = 1.702
SWIGLU_LIMIT = 7.0

LANES = 128
SUBLANES = 8
HEAD_SLAB = LANES
POOL_HALO = 16
ROW_TILE = 512
ATTN_TILE = 512
ATTN_ROWS = 32
ATTN_HEADS = 4
S5_STEPS = 64
MOE_ROWS = 512
MOE_SPLITS = 2
SC_WORKERS = 32
SC_CHUNK = 64
VMEM_LIMIT = 56 * 1024 * 1024
NEG_BIG = -1e30


def _sigmoid(v):
    return 1.0 / (1.0 + jnp.exp(-v))


def _gelu(v):
    return 0.5 * v * (1.0 + jnp.tanh(math.sqrt(2.0 / math.pi) * (v + 0.044715 * (v * v * v))))


def _rms(v, width):
    return lax.rsqrt(jnp.sum(v * v, axis=-1, keepdims=True) * (1.0 / width) + NORM_EPS)


def _mod_norm(x, g, sc, sh):
    return x * _rms(x, x.shape[-1]) * g * (1.0 + sc) + sh


def _bdot(a, b):
    return jnp.dot(a.astype(BF16), b, preferred_element_type=F32)


def _pack_pairs(v):
    w = v.shape[-1] // 2
    bits = pltpu.bitcast(v.astype(BF16).astype(F32), jnp.uint32)
    return (bits[:, :w] >> 16) | bits[:, w:]


def _unpack_pairs(p):
    lo = pltpu.bitcast(p << 16, F32)
    hi = pltpu.bitcast(p & jnp.uint32(0xFFFF0000), F32)
    return lo.astype(BF16), hi.astype(BF16)


def _ada_kernel(c_ref, w_ref, b_ref, o_ref):
    c = c_ref[...]
    act = c * _sigmoid(c)
    o_ref[...] = jnp.dot(act, w_ref[...], precision=HIGHEST, preferred_element_type=F32) + b_ref[...]


def _ada_call(c, ada_w, ada_b):
    depth, d, n = ada_w.shape
    bsz = c.shape[0]
    tn = 1536
    return pl.pallas_call(
        _ada_kernel,
        grid=(depth, n // tn),
        in_specs=[
            pl.BlockSpec((bsz, d), lambda l, j: (0, 0)),
            pl.BlockSpec((None, d, tn), lambda l, j: (l, 0, j)),
            pl.BlockSpec((None, 1, tn), lambda l, j: (l, 0, j)),
        ],
        out_specs=pl.BlockSpec((None, bsz, tn), lambda l, j: (l, 0, j)),
        out_shape=jax.ShapeDtypeStruct((depth, bsz, n), F32),
        compiler_params=pltpu.CompilerParams(dimension_semantics=("parallel", "parallel"),
                                             vmem_limit_bytes=VMEM_LIMIT),
        name="ada_mod",
    )(c, ada_w, ada_b.reshape(depth, 1, n))


_C_Q = 0
_C_KV = Q_LORA_RANK
_C_PE = _C_KV + KV_LORA_RANK
_C_PESW = _C_PE + HEAD_SLAB
_C_U = _C_PESW + HEAD_SLAB


def _even_in_kernel(x_ref, mod_ref, pos_ref, g_ref, win_ref, gq_ref, wq_ref, gkv_ref, wk_ref, wv_ref,
                    tab_ref, q_ref, k_ref, v_ref, u_ref):
    x = x_ref[...]
    h = _mod_norm(x, g_ref[...], mod_ref[1:2, :], mod_ref[0:1, :])
    z = _bdot(h, win_ref[...])
    q_c = z[:, _C_Q:_C_KV]
    kv_c = z[:, _C_KV:_C_PE]
    kpe = z[:, _C_PE:_C_PESW]
    kpe_sw = z[:, _C_PESW:_C_U]
    u_ref[...] = z[:, _C_U:]

    ang = pos_ref[...] * tab_ref[0:1, :]
    cs = jnp.cos(ang)
    sn = jnp.sin(ang)
    gcq = cs * tab_ref[1:2, :]
    gsq = sn * tab_ref[2:3, :]
    gck = cs * tab_ref[3:4, :]
    gsk = sn * tab_ref[4:5, :]

    qn = q_c * _rms(q_c, Q_LORA_RANK) * gq_ref[...]
    qq = _bdot(qn, wq_ref[...])
    kvn = kv_c * _rms(kv_c, KV_LORA_RANK) * gkv_ref[...]
    kk = _bdot(kvn, wk_ref[...])
    v_ref[...] = _bdot(kvn, wv_ref[...]).astype(v_ref.dtype)

    pe_rot = kpe * gck + kpe_sw * gsk
    pe_ss = jnp.sum(kpe * kpe, axis=-1, keepdims=True)
    hw = MLA_HEADS * HEAD_SLAB
    for hd in range(MLA_HEADS):
        lo = hd * HEAD_SLAB
        qr = qq[:, lo:lo + HEAD_SLAB]
        qs = qq[:, hw + lo:hw + lo + HEAD_SLAB]
        rq = _rms(qr, QK_HEAD_DIM)
        q_ref[hd] = (rq * (qr * gcq + qs * gsq)).astype(q_ref.dtype)
        kr = kk[:, lo:lo + HEAD_SLAB]
        rk = lax.rsqrt((jnp.sum(kr * kr, axis=-1, keepdims=True) + pe_ss) * (1.0 / QK_HEAD_DIM) + NORM_EPS)
        k_ref[hd] = (rk * (kr * gck + pe_rot)).astype(k_ref.dtype)


def _even_in_call(x, mod, posf, g, prep):
    bsz, seq, d = x.shape
    tm = min(ROW_TILE, seq)
    hw = MLA_HEADS * HEAD_SLAB
    full = lambda a: pl.BlockSpec(a.shape, lambda b, s: (0,) * a.ndim)
    return pl.pallas_call(
        _even_in_kernel,
        grid=(bsz, seq // tm),
        in_specs=[
            pl.BlockSpec((None, tm, d), lambda b, s: (b, s, 0)),
            pl.BlockSpec((None, 6, d), lambda b, s: (b, 0, 0)),
            pl.BlockSpec((None, tm, 1), lambda b, s: (b, s, 0)),
            full(g), full(prep["w_in"]), full(prep["gq"]), full(prep["wq"]), full(prep["gkv"]),
            full(prep["wk"]), full(prep["wv"]), full(prep["tab"]),
        ],
        out_specs=[
            pl.BlockSpec((None, MLA_HEADS, tm, HEAD_SLAB), lambda b, s: (b, 0, s, 0)),
            pl.BlockSpec((None, MLA_HEADS, tm, HEAD_SLAB), lambda b, s: (b, 0, s, 0)),
            pl.BlockSpec((None, tm, MLA_HEADS * V_HEAD_DIM), lambda b, s: (b, s, 0)),
            pl.BlockSpec((tm, d // 2), lambda b, s: (s, b)),
        ],
        out_shape=[
            jax.ShapeDtypeStruct((bsz, MLA_HEADS, seq, HEAD_SLAB), BF16),
            jax.ShapeDtypeStruct((bsz, MLA_HEADS, seq, HEAD_SLAB), BF16),
            jax.ShapeDtypeStruct((bsz, seq, MLA_HEADS * V_HEAD_DIM), BF16),
            jax.ShapeDtypeStruct((seq, bsz * (d // 2)), F32),
        ],
        compiler_params=pltpu.CompilerParams(dimension_semantics=("parallel", "parallel"),
                                             vmem_limit_bytes=VMEM_LIMIT),
        name="even_in",
    )(x, mod, posf, g, prep["w_in"], prep["gq"], prep["wq"], prep["gkv"], prep["wk"], prep["wv"], prep["tab"])


def _prep_even(even_w_in, q_norm_g, w_uq, kv_norm_g, w_ukv, q_head_g, k_head_g):
    d = even_w_in.shape[0]
    half = QK_ROPE_DIM // 2
    nope = QK_NOPE_DIM
    c_pe = Q_LORA_RANK + KV_LORA_RANK
    w_pe = even_w_in[:, c_pe:c_pe + QK_ROPE_DIM]
    zeros = lambda n: jnp.zeros((d, n), F32)
    pe_slab = jnp.concatenate([zeros(nope), w_pe, zeros(HEAD_SLAB - QK_HEAD_DIM)], axis=1)
    pe_sw = jnp.concatenate([zeros(nope), -w_pe[:, half:], w_pe[:, :half], zeros(HEAD_SLAB - QK_HEAD_DIM)], axis=1)
    w_in = jnp.concatenate([even_w_in[:, :c_pe], pe_slab, pe_sw, even_w_in[:, c_pe + QK_ROPE_DIM:]], axis=1)

    r = w_uq.shape[0]
    padq = jnp.zeros((r, MLA_HEADS, HEAD_SLAB - QK_HEAD_DIM), F32)
    wq_plain = jnp.concatenate([w_uq, padq], axis=2).reshape(r, MLA_HEADS * HEAD_SLAB)
    wq_sw = jnp.concatenate([jnp.zeros((r, MLA_HEADS, nope), F32), -w_uq[:, :, nope + half:],
                             w_uq[:, :, nope:nope + half], padq], axis=2).reshape(r, MLA_HEADS * HEAD_SLAB)
    wq = jnp.concatenate([wq_plain, wq_sw], axis=1)

    rk = w_ukv.shape[0]
    wk = jnp.concatenate([w_ukv[:, :, :nope], jnp.zeros((rk, MLA_HEADS, HEAD_SLAB - nope), F32)],
                         axis=2).reshape(rk, MLA_HEADS * HEAD_SLAB)
    wv = w_ukv[:, :, nope:].reshape(rk, MLA_HEADS * V_HEAD_DIM)

    inv_freq = 1.0 / (ROPE_THETA ** (jnp.arange(half, dtype=F32) / half))
    pad_tail = jnp.zeros((HEAD_SLAB - QK_HEAD_DIM,), F32)
    freq_row = jnp.concatenate([jnp.zeros((nope,), F32), inv_freq, inv_freq, pad_tail])

    def gain_rows(gv, scale):
        plain = jnp.concatenate([gv, pad_tail]) * scale
        swapped = jnp.concatenate([jnp.zeros((nope,), F32), gv[nope + half:], gv[nope:nope + half], pad_tail]) * scale
        return plain, swapped

    gq_plain, gq_sw = gain_rows(q_head_g, QK_HEAD_DIM ** -0.5 * math.log2(math.e))
    gk_plain, gk_sw = gain_rows(k_head_g, 1.0)
    tab = jnp.stack([freq_row, gq_plain, gq_sw, gk_plain, gk_sw, freq_row * 0, freq_row * 0, freq_row * 0])
    return {
        "w_in": w_in.astype(BF16), "gq": q_norm_g.reshape(1, -1), "wq": wq.astype(BF16),
        "gkv": kv_norm_g.reshape(1, -1), "wk": wk.astype(BF16), "wv": wv.astype(BF16), "tab": tab,
    }


def _attn_kernel(qi_ref, kj_ref, q_ref, k_ref, v_ref, o_ref, m_sc, a_sc, acc_sc, s_sc, p_sc, *, tq, tk):
    step = pl.program_id(2)
    i = qi_ref[step]
    j = kj_ref[step]
    sum_lane = (V_HEAD_DIM, 0)

    @pl.when(j == 0)
    def _():
        m_sc[...] = jnp.full(m_sc.shape, -jnp.inf, F32)
        acc_sc[...] = jnp.zeros(acc_sc.shape, F32)

    def sweep(on_diagonal):
        lane = lax.broadcasted_iota(jnp.int32, (1, LANES), 1)
        for hh in range(ATTN_HEADS):
            s_sc[hh] = lax.dot_general(q_ref[hh], k_ref[hh], (((1,), (1,)), ((), ())),
                                       preferred_element_type=F32)
        for hh in range(ATTN_HEADS):
            v = v_ref[:, (hh // 2) * LANES:(hh // 2 + 1) * LANES]
            for r0 in range(0, tq, ATTN_ROWS):
                rows = pl.ds(r0, ATTN_ROWS)
                s = s_sc[hh, rows, :]
                if on_diagonal:
                    row = r0 + lax.broadcasted_iota(jnp.int32, (ATTN_ROWS, tk), 0)
                    col = lax.broadcasted_iota(jnp.int32, (ATTN_ROWS, tk), 1)
                    s = jnp.where(col <= row, s, -jnp.inf)
                m_prev = m_sc[hh, rows, :]
                m_new = jnp.maximum(m_prev, jnp.max(s, axis=-1, keepdims=True))
                a_sc[hh, rows, :] = jnp.exp2(m_prev - m_new)
                m_sc[hh, rows, :] = m_new
                shifted = s - jnp.concatenate([m_new] * (tk // LANES), axis=1)
                p_sc[hh, rows, :] = jnp.exp2(shifted.astype(BF16))
            own = (lane < V_HEAD_DIM) == (hh % 2 == 0)
            ones = jnp.where(lane == sum_lane[hh % 2], 1.0, 0.0).astype(v.dtype)
            vh = jnp.where(own, v, jnp.broadcast_to(ones, v.shape))
            acc_sc[hh] = acc_sc[hh] * a_sc[hh] + jnp.dot(p_sc[hh], vh, preferred_element_type=F32)

    @pl.when(j < i)
    def _():
        sweep(False)

    @pl.when(j == i)
    def _():
        sweep(True)
        lane = lax.broadcasted_iota(jnp.int32, (1, LANES), 1)
        for pp in range(ATTN_HEADS // 2):
            acc0 = acc_sc[2 * pp]
            acc1 = acc_sc[2 * pp + 1]
            l0 = acc0[:, sum_lane[0]:sum_lane[0] + 1]
            l1 = acc1[:, sum_lane[1]:sum_lane[1] + 1]
            o_ref[:, pp * LANES:(pp + 1) * LANES] = jnp.where(lane < V_HEAD_DIM, acc0 / l0,
                                                              acc1 / l1).astype(o_ref.dtype)


def _attn_call(q, k, v):
    bsz, nh, seq, _ = q.shape
    tq = tk = min(ATTN_TILE, seq)
    nq = seq // tq
    pairs = [(i, j) for i in range(nq) for j in range(i + 1)]
    qi = jnp.asarray([p[0] for p in pairs], jnp.int32)
    kj = jnp.asarray([p[1] for p in pairs], jnp.int32)
    kern = functools.partial(_attn_kernel, tq=tq, tk=tk)
    hp = ATTN_HEADS
    assert nh % hp == 0
    return pl.pallas_call(
        kern,
        grid_spec=pltpu.PrefetchScalarGridSpec(
            num_scalar_prefetch=2,
            grid=(bsz, nh // hp, len(pairs)),
            in_specs=[
                pl.BlockSpec((None, hp, tq, HEAD_SLAB), lambda b, h, p, qi, kj: (b, h, qi[p], 0)),
                pl.BlockSpec((None, hp, tk, HEAD_SLAB), lambda b, h, p, qi, kj: (b, h, kj[p], 0)),
                pl.BlockSpec((None, tk, hp * V_HEAD_DIM), lambda b, h, p, qi, kj: (b, kj[p], h)),
            ],
            out_specs=pl.BlockSpec((None, tq, hp * V_HEAD_DIM), lambda b, h, p, qi, kj: (b, qi[p], h)),
            scratch_shapes=[pltpu.VMEM((hp, tq, LANES), F32), pltpu.VMEM((hp, tq, LANES), F32),
                            pltpu.VMEM((hp, tq, LANES), F32),
                            pltpu.VMEM((hp, tq, tk), F32), pltpu.VMEM((hp, tq, tk), BF16)],
        ),
        out_shape=jax.ShapeDtypeStruct((bsz, seq, nh * V_HEAD_DIM), BF16),
        compiler_params=pltpu.CompilerParams(
            dimension_semantics=("parallel", "parallel", "arbitrary"),
            vmem_limit_bytes=VMEM_LIMIT),
        name="mla_attention",
    )(qi, kj, q, k, v)


def _s5_disc_kernel(are_ref, aim_ref, ldt_ref, bre_ref, bim_ref, abre_ref, abim_ref, bbre_ref, bbim_ref):
    dt = jnp.exp(ldt_ref[...])
    lam_re = jnp.minimum(are_ref[...], -1e-4)
    lam_im = aim_ref[...]
    mag = jnp.exp(lam_re * dt)
    ab_re = mag * jnp.cos(lam_im * dt)
    ab_im = mag * jnp.sin(lam_im * dt)
    den = lam_re * lam_re + lam_im * lam_im
    num_re = ab_re - 1.0
    f_re = (num_re * lam_re + ab_im * lam_im) / den
    f_im = (ab_im * lam_re - num_re * lam_im) / den
    abre_ref[...] = ab_re
    abim_ref[...] = ab_im
    br = bre_ref[...]
    bi = bim_ref[...]
    bbre_ref[...] = f_re[:, None, :] * br - f_im[:, None, :] * bi
    bbim_ref[...] = f_re[:, None, :] * bi + f_im[:, None, :] * br


def _s5_disc_call(a_re, a_im, log_dt, b_re, b_im):
    g, p = a_re.shape
    bre_t = jnp.swapaxes(b_re, 1, 2)
    bim_t = jnp.swapaxes(b_im, 1, 2)
    return pl.pallas_call(
        _s5_disc_kernel,
        out_shape=[jax.ShapeDtypeStruct((g, p), F32), jax.ShapeDtypeStruct((g, p), F32),
                   jax.ShapeDtypeStruct(bre_t.shape, F32), jax.ShapeDtypeStruct(bre_t.shape, F32)],
        name="s5_discretize",
    )(a_re, a_im, log_dt.reshape(g, 1), bre_t, bim_t)


def _block_diag_halves(m):
    g, r, c = m.shape
    gh = g // 2
    eye = jnp.eye(gh, dtype=m.dtype)
    mh = m.reshape(2, gh, r, c)
    return (mh[:, :, :, None, :] * eye[None, :, None, :, None]).reshape(2, gh * r, gh * c)


def _s5_kernel(u_ref, bre_ref, bim_ref, are_ref, aim_ref, cre_ref, cim_ref, d_ref, gw_ref, gb_ref,
               o_ref, sre, sim, dre, dim, xbr, xbi, *, steps):
    @pl.when(pl.program_id(0) == 0)
    def _():
        sre[...] = jnp.zeros(sre.shape, F32)
        sim[...] = jnp.zeros(sim.shape, F32)

    rows = steps * SUBLANES
    w = u_ref.shape[-1]
    u = u_ref[...].reshape(rows, w)
    ub = u.astype(BF16)
    kh = w // 2
    nh = dre.shape[1] // 2
    for hf in range(2):
        dre[:, hf * nh:(hf + 1) * nh] = jnp.dot(ub[:, hf * kh:(hf + 1) * kh], bre_ref[hf], preferred_element_type=F32)
        dim[:, hf * nh:(hf + 1) * nh] = jnp.dot(ub[:, hf * kh:(hf + 1) * kh], bim_ref[hf], preferred_element_type=F32)

    xr = sre[...]
    xi = sim[...]
    for t in range(0, steps, 2):
        pair_r = []
        pair_i = []
        for r0 in (t * SUBLANES, (t + 1) * SUBLANES):
            nr = are_ref[...] * xr - aim_ref[...] * xi + dre[r0:r0 + SUBLANES, :]
            ni = are_ref[...] * xi + aim_ref[...] * xr + dim[r0:r0 + SUBLANES, :]
            xr, xi = nr, ni
            pair_r.append(nr)
            pair_i.append(ni)
        xbr[t * SUBLANES:(t + 2) * SUBLANES, :] = jnp.concatenate(pair_r, axis=0).astype(BF16)
        xbi[t * SUBLANES:(t + 2) * SUBLANES, :] = jnp.concatenate(pair_i, axis=0).astype(BF16)
    sre[...] = xr
    sim[...] = xi

    ys = []
    for hf in range(2):
        yr = jnp.dot(xbr[:, hf * nh:(hf + 1) * nh], cre_ref[hf], preferred_element_type=F32)
        yi = jnp.dot(xbi[:, hf * nh:(hf + 1) * nh], cim_ref[hf], preferred_element_type=F32)
        ys.append(yr - yi)
    y = jnp.concatenate(ys, axis=1) + d_ref[...] * u
    g = _gelu(y)
    out = g * _sigmoid(_bdot(g, gw_ref[...]) + gb_ref[...])
    o_ref[...] = out.reshape(steps, SUBLANES, w).astype(o_ref.dtype)


def _s5_call(u_t, disc, c_re, c_im, d_skip, glu_w, glu_b):
    seq, bsz, w = u_t.shape
    assert bsz == SUBLANES
    ab_re, ab_im, bb_re, bb_im = disc
    g, p = ab_re.shape
    n_state = g * p
    bre = _block_diag_halves(bb_re).astype(BF16)
    bim = _block_diag_halves(bb_im).astype(BF16)
    cre = _block_diag_halves(jnp.swapaxes(c_re, 1, 2)).astype(BF16)
    cim = _block_diag_halves(jnp.swapaxes(c_im, 1, 2)).astype(BF16)
    steps = min(S5_STEPS, seq)
    full = lambda a: pl.BlockSpec(a.shape, lambda s: (0,) * a.ndim)
    rep = lambda a: jnp.broadcast_to(a.reshape(1, n_state), (bsz, n_state))
    args = (bre, bim, rep(ab_re), rep(ab_im), cre, cim,
            d_skip.reshape(1, w), glu_w.astype(BF16), glu_b.reshape(1, w))
    return pl.pallas_call(
        functools.partial(_s5_kernel, steps=steps),
        grid=(seq // steps,),
        in_specs=[pl.BlockSpec((steps, bsz, w), lambda s: (s, 0, 0))] + [full(a) for a in args],
        out_specs=pl.BlockSpec((steps, bsz, w), lambda s: (s, 0, 0)),
        out_shape=jax.ShapeDtypeStruct((seq, bsz, w), BF16),
        scratch_shapes=[pltpu.VMEM((bsz, n_state), F32), pltpu.VMEM((bsz, n_state), F32),
                        pltpu.VMEM((steps * bsz, n_state), F32), pltpu.VMEM((steps * bsz, n_state), F32),
                        pltpu.VMEM((steps * bsz, n_state), BF16), pltpu.VMEM((steps * bsz, n_state), BF16)],
        compiler_params=pltpu.CompilerParams(dimension_semantics=("arbitrary",), vmem_limit_bytes=VMEM_LIMIT),
        name="s5_scan",
    )(u_t, *args)


def _router_tail(x_new, mod_ref, gf_ref, rw_ref, rb_ref, h2_ref, te_ref, cnt_ref):
    h2 = _mod_norm(x_new, gf_ref[...], mod_ref[4:5, :], mod_ref[3:4, :])
    h2_ref[...] = _pack_pairs(h2)
    h_hi = h2.astype(BF16)
    h_lo = (h2 - h_hi.astype(F32)).astype(BF16)
    r_hi = jnp.dot(h_hi, rw_ref[...], preferred_element_type=F32)
    r_lo = jnp.dot(h_lo, rw_ref[...], preferred_element_type=F32)
    logits = r_hi[:, :LANES] + r_hi[:, LANES:] + r_lo[:, :LANES] + rb_ref[...]
    lane = lax.broadcasted_iota(jnp.int32, logits.shape, 1).astype(F32)
    vals = []
    idxs = []
    work = logits
    for _ in range(TOP_K):
        m = jnp.max(work, axis=-1, keepdims=True)
        idx = jnp.min(jnp.where(work == m, lane, float(LANES)), axis=-1, keepdims=True)
        vals.append(m)
        idxs.append(idx)
        work = jnp.where(lane == idx, NEG_BIG * 2.0, work)
    exps = [jnp.exp(vv - vals[0]) for vv in vals]
    tot = exps[0] + exps[1] + exps[2] + exps[3]
    te = jnp.zeros(logits.shape, F32)
    picked = jnp.zeros(logits.shape, F32)
    for kk in range(TOP_K):
        te = jnp.where(lane == float(kk), idxs[kk], te)
        te = jnp.where(lane == float(TOP_K + kk), exps[kk] / tot, te)
        picked = picked + jnp.where(lane == idxs[kk], 1.0, 0.0)
    te_ref[...] = te[:, :2 * TOP_K]
    cnt_ref[...] = jnp.sum(picked, axis=0, keepdims=True)


def _mix_out_kernel(x_ref, a_ref, s_ref, mod_ref, wo_ref, gf_ref, rw_ref, rb_ref, xo_ref, h2_ref, te_ref,
                    cnt_ref):
    ka = a_ref.shape[-1]
    mix = jnp.dot(a_ref[...], wo_ref[:ka, :], preferred_element_type=F32)
    mix = mix + jnp.dot(s_ref[...], wo_ref[ka:, :], preferred_element_type=F32)
    x_new = x_ref[...] + mod_ref[2:3, :] * mix
    xo_ref[...] = x_new
    _router_tail(x_new, mod_ref, gf_ref, rw_ref, rb_ref, h2_ref, te_ref, cnt_ref)


def _tail_out_specs(bsz, seq, tm, d):
    nt = seq // tm
    specs = [
        pl.BlockSpec((None, tm, d), lambda b, s: (b, s, 0)),
        pl.BlockSpec((tm, d // 2), lambda b, s: (b * nt + s, 0)),
        pl.BlockSpec((tm, 2 * TOP_K), lambda b, s: (b * nt + s, 0)),
        pl.BlockSpec((None, 1, LANES), lambda b, s: (b * nt + s, 0, 0)),
    ]
    shapes = [
        jax.ShapeDtypeStruct((bsz, seq, d), F32),
        jax.ShapeDtypeStruct((bsz * seq, d // 2), jnp.uint32),
        jax.ShapeDtypeStruct((bsz * seq, 2 * TOP_K), F32),
        jax.ShapeDtypeStruct((bsz * nt, 1, LANES), F32),
    ]
    return specs, shapes


def _router_pad(router_w, router_b):
    d, e = router_w.shape
    rw = jnp.concatenate([router_w, jnp.zeros((d, LANES - e), F32)], axis=1)
    rw_hi = rw.astype(BF16)
    rw_lo = (rw - rw_hi.astype(F32)).astype(BF16)
    rb = jnp.concatenate([router_b, jnp.full((LANES - e,), NEG_BIG, F32)]).reshape(1, LANES)
    return jnp.concatenate([rw_hi, rw_lo], axis=1), rb


def _mix_out_call(x, attn, ssm_t, mod, w_out, gf, rw, rb):
    bsz, seq, d = x.shape
    tm = min(ROW_TILE, seq)
    ka = attn.shape[-1]
    ks = ssm_t.shape[-1] // bsz
    full = lambda a: pl.BlockSpec(a.shape, lambda b, s: (0,) * a.ndim)
    out_specs, out_shape = _tail_out_specs(bsz, seq, tm, d)
    return pl.pallas_call(
        _mix_out_kernel,
        grid=(bsz, seq // tm),
        in_specs=[
            pl.BlockSpec((None, tm, d), lambda b, s: (b, s, 0)),
            pl.BlockSpec((None, tm, ka), lambda b, s: (b, s, 0)),
            pl.BlockSpec((tm, ks), lambda b, s: (s, b)),
            pl.BlockSpec((None, 6, d), lambda b, s: (b, 0, 0)),
            full(w_out), full(gf), full(rw), full(rb),
        ],
        out_specs=out_specs,
        out_shape=out_shape,
        compiler_params=pltpu.CompilerParams(dimension_semantics=("parallel", "parallel"),
                                             vmem_limit_bytes=VMEM_LIMIT),
        name="even_out",
    )(x, attn, ssm_t, mod, w_out, gf, rw, rb)


def _odd_kernel(x_ref, mod_ref, g_ref, win_ref, icnt_ref, wp_ref, ps_ref, gv_ref, wsp_ref, bsp_ref,
                wo_ref, gf_ref, rw_ref, rb_ref, xo_ref, h2_ref, te_ref, cnt_ref, ext_sc):
    tm = x_ref.shape[0]
    pw = wp_ref.shape[-1]
    width = pw * len(POOL_WINDOWS)

    @pl.when(pl.program_id(1) == 0)
    def _():
        ext_sc[0:POOL_HALO, :] = jnp.zeros((POOL_HALO, width), F32)

    x = x_ref[...]
    h = _mod_norm(x, g_ref[...], mod_ref[1:2, :], mod_ref[0:1, :])
    z = _bdot(h, win_ref[...])
    up = z[:, :width]
    ext_sc[POOL_HALO:POOL_HALO + tm, :] = up

    pooled = []
    for gi, win in enumerate(POOL_WINDOWS):
        cols = slice(gi * pw, (gi + 1) * pw)
        acc = up[:, cols]
        for lag in range(1, win):
            acc = acc + ext_sc[POOL_HALO - lag:POOL_HALO - lag + tm, cols]
        pg = acc * icnt_ref[:, gi:gi + 1] - up[:, cols]
        pooled.append(_bdot(pg, wp_ref[gi]) * ps_ref[:, cols])
    ext_sc[0:POOL_HALO, :] = ext_sc[tm:tm + POOL_HALO, :]
    pooled = jnp.concatenate(pooled, axis=1)

    ug = _gelu(z[:, width:2 * width])
    vg = _gelu(z[:, 2 * width:])
    vn = (vg * _rms(vg, width) * gv_ref[...]).astype(BF16)
    hd = width // SGU_HEADS
    chunks = []
    for ci in range(tm // SGU_CHUNK):
        heads = []
        for hh in range(SGU_HEADS):
            blk = vn[ci * SGU_CHUNK:(ci + 1) * SGU_CHUNK, hh * hd:(hh + 1) * hd]
            heads.append(jnp.dot(wsp_ref[hh], blk, preferred_element_type=F32) + bsp_ref[hh])
        chunks.append(jnp.concatenate(heads, axis=1))
    gated = ug * jnp.concatenate(chunks, axis=0)

    mix = _bdot(pooled, wo_ref[:width, :]) + _bdot(gated, wo_ref[width:, :])
    x_new = x + mod_ref[2:3, :] * mix
    xo_ref[...] = x_new
    _router_tail(x_new, mod_ref, gf_ref, rw_ref, rb_ref, h2_ref, te_ref, cnt_ref)


def _odd_call(x, mod, g, w_in, pool_w, pool_scale, sgu_norm_g, sgu_w, sgu_b, w_out, gf, rw, rb):
    bsz, seq, d = x.shape
    tm = min(ROW_TILE, seq)
    width = pool_scale.shape[0]
    hd = width // SGU_HEADS
    t = jnp.arange(seq, dtype=jnp.int32)
    icnt = jnp.stack([1.0 / jnp.minimum(t + 1, wn).astype(F32) for wn in POOL_WINDOWS], axis=1)
    wsp = jnp.tril(sgu_w).astype(BF16)
    bsp = jnp.broadcast_to(sgu_b[:, :, None], (SGU_HEADS, SGU_CHUNK, hd))
    args = (g, w_in.astype(BF16), icnt, pool_w.astype(BF16), pool_scale.reshape(1, width),
            sgu_norm_g.reshape(1, width), wsp, bsp, w_out.astype(BF16), gf, rw, rb)
    full = lambda a: pl.BlockSpec(a.shape, lambda b, s: (0,) * a.ndim)
    in_specs = [pl.BlockSpec((None, tm, d), lambda b, s: (b, s, 0)),
                pl.BlockSpec((None, 6, d), lambda b, s: (b, 0, 0))]
    for idx, a in enumerate(args):
        in_specs.append(pl.BlockSpec((tm, len(POOL_WINDOWS)), lambda b, s: (s, 0)) if idx == 2 else full(a))
    out_specs, out_shape = _tail_out_specs(bsz, seq, tm, d)
    return pl.pallas_call(
        _odd_kernel,
        grid=(bsz, seq // tm),
        in_specs=in_specs,
        out_specs=out_specs,
        out_shape=out_shape,
        scratch_shapes=[pltpu.VMEM((tm + POOL_HALO, width), F32)],
        compiler_params=pltpu.CompilerParams(dimension_semantics=("parallel", "arbitrary"),
                                             vmem_limit_bytes=VMEM_LIMIT),
        name="odd_mixer",
    )(x, mod, *args)


def _dest_kernel(te_ref, base_ref, dst_ref):
    tr = te_ref.shape[0]
    te = te_ref[...]
    lane = lax.broadcasted_iota(jnp.int32, (tr, LANES), 1).astype(F32)
    hots = [te[:, kk:kk + 1] == lane for kk in range(TOP_K)]
    oh = jnp.zeros((tr, LANES), F32)
    for hot in hots:
        oh = oh + jnp.where(hot, 1.0, 0.0)
    r_i = lax.broadcasted_iota(jnp.int32, (tr, tr), 0)
    c_i = lax.broadcasted_iota(jnp.int32, (tr, tr), 1)
    tri = jnp.where(c_i < r_i, 1.0, 0.0).astype(BF16)
    before = jnp.dot(tri, oh.astype(BF16), preferred_element_type=F32) + base_ref[...]
    dst = jnp.zeros((tr, LANES), F32)
    for kk, hot in enumerate(hots):
        dst = jnp.where(lane == float(kk), jnp.sum(jnp.where(hot, before, 0.0), axis=-1, keepdims=True), dst)
    dst_ref[...] = dst.T[:2 * TOP_K, :].astype(jnp.int32)


def _dest_call(te, base, tok0, n_tok):
    tiles = base.shape[0]
    tr = n_tok // tiles
    tile0 = tok0 // tr
    return pl.pallas_call(
        _dest_kernel,
        grid=(tiles,),
        in_specs=[pl.BlockSpec((tr, 2 * TOP_K), lambda i: (tile0 + i, 0)),
                  pl.BlockSpec((None, 1, LANES), lambda i: (i, 0, 0))],
        out_specs=pl.BlockSpec((2 * TOP_K, tr), lambda i: (0, i)),
        out_shape=jax.ShapeDtypeStruct((2 * TOP_K, n_tok), jnp.int32),
        compiler_params=pltpu.CompilerParams(dimension_semantics=("parallel",)),
        name="route_dest",
    )(te, base)


def _sc_gather(table, idx):
    n = idx.shape[0]
    per_w = n // SC_WORKERS
    assert per_w * SC_WORKERS == n and per_w % SC_CHUNK == 0
    n_chunks = per_w // SC_CHUNK
    row_shape = table.shape[1:]
    mesh = plsc.VectorSubcoreMesh(core_axis_name="c", subcore_axis_name="s")

    @functools.partial(
        pl.kernel, mesh=mesh,
        out_type=jax.ShapeDtypeStruct((n,) + row_shape, table.dtype),
        scratch_types=[pltpu.VMEM((SC_CHUNK,), jnp.int32), pltpu.VMEM((SC_CHUNK,) + row_shape, table.dtype),
                       pltpu.SemaphoreType.DMA],
        name="sc_row_gather",
    )
    def gather(table_hbm, idx_hbm, out_hbm, idx_v, rows_v, sem):
        wid = lax.axis_index("s") * 2 + lax.axis_index("c")
        base = wid * per_w

        @pl.loop(0, n_chunks)
        def _(ci):
            off = pl.multiple_of(base + ci * SC_CHUNK, SC_CHUNK)
            pltpu.sync_copy(idx_hbm.at[pl.ds(off, SC_CHUNK)], idx_v)
            pltpu.async_copy(table_hbm.at[idx_v], rows_v, sem).wait()
            pltpu.sync_copy(rows_v, out_hbm.at[pl.ds(off, SC_CHUNK)])

    return gather(table, idx)


def _sc_scatter(rows, dests, n_out, tok0):
    t = dests[0].shape[0]
    per_w = t // SC_WORKERS
    assert per_w * SC_WORKERS == t and per_w % SC_CHUNK == 0
    n_chunks = per_w // SC_CHUNK
    row_shape = rows.shape[1:]
    nk = len(dests)
    mesh = plsc.VectorSubcoreMesh(core_axis_name="c", subcore_axis_name="s")

    @functools.partial(
        pl.kernel, mesh=mesh,
        out_type=jax.ShapeDtypeStruct((n_out,) + row_shape, rows.dtype),
        scratch_types=[pltpu.VMEM((SC_CHUNK,), jnp.int32)] * nk
        + [pltpu.VMEM((SC_CHUNK,) + row_shape, rows.dtype), pltpu.SemaphoreType.DMA],
        name="sc_row_scatter",
    )
    def scatter(rows_hbm, *rest):
        dest_hbm = rest[:nk]
        out_hbm = rest[nk]
        idx_v = rest[nk + 1:2 * nk + 1]
        rows_v, sem = rest[2 * nk + 1:]
        wid = lax.axis_index("s") * 2 + lax.axis_index("c")
        base = wid * per_w

        @pl.loop(0, n_chunks)
        def _(ci):
            off = pl.multiple_of(base + ci * SC_CHUNK, SC_CHUNK)
            src = pl.multiple_of(tok0 + off, SC_CHUNK)
            pltpu.sync_copy(rows_hbm.at[pl.ds(src, SC_CHUNK)], rows_v)
            for kk in range(nk):
                pltpu.sync_copy(dest_hbm[kk].at[pl.ds(off, SC_CHUNK)], idx_v[kk])
            copies = [pltpu.async_copy(rows_v, out_hbm.at[idx_v[kk]], sem) for kk in range(nk)]
            for cp in copies:
                cp.wait()

    return scatter(rows, *dests)


def _expert_kernel(be_ref, nv_ref, ord_ref, ue_ref, nu_ref, x_ref, wgu_hbm, bgu_ref, wdn_hbm, bdn_ref, y_ref,
                   wgu_f32, wdn_f32, wgu_bf, wdn_bf, sem, *, layer):
    i = pl.program_id(0)
    used = i < nu_ref[0]
    pos = ord_ref[i]
    fresh = jnp.logical_or(i == 0, ord_ref[jnp.maximum(i - 1, 0)] != pos)

    def weight_copies(expert):
        return (pltpu.make_async_copy(wgu_hbm.at[layer, expert], wgu_f32, sem.at[0]),
                pltpu.make_async_copy(wdn_hbm.at[layer, expert], wdn_f32, sem.at[1]))

    @pl.when(i == 0)
    def _():
        for cp in weight_copies(ue_ref[0]):
            cp.start()

    @pl.when(jnp.logical_and(used, fresh))
    def _():
        for cp in weight_copies(ue_ref[pos]):
            cp.wait()
        wgu_bf[...] = wgu_f32[...].astype(BF16)
        wdn_bf[...] = wdn_f32[...].astype(BF16)

        @pl.when(pos + 1 < nu_ref[1])
        def _():
            for cp in weight_copies(ue_ref[pos + 1]):
                cp.start()

    def ffn(rows):
        x_lo, x_hi = _unpack_pairs(x_ref[0:rows, :])
        dh = x_lo.shape[-1]
        z = jnp.dot(x_lo, wgu_bf[:dh, :], preferred_element_type=F32)
        z = z + jnp.dot(x_hi, wgu_bf[dh:, :], preferred_element_type=F32) + bgu_ref[...]
        ff = z.shape[-1] // 2
        gate = jnp.minimum(z[:, :ff], SWIGLU_LIMIT)
        lin = jnp.clip(z[:, ff:], -SWIGLU_LIMIT, SWIGLU_LIMIT)
        act = gate * _sigmoid(SWIGLU_ALPHA * gate) * (lin + 1.0)
        y = _bdot(act, wdn_bf[...]) + bdn_ref[...]
        y_ref[0:rows, :] = _pack_pairs(y)

    half = x_ref.shape[0] // 2
    few = nv_ref[i] <= half

    @pl.when(jnp.logical_and(used, jnp.logical_not(few)))
    def _():
        ffn(x_ref.shape[0])

    @pl.when(jnp.logical_and(used, few))
    def _():
        ffn(half)


def _expert_call(layer, block_e, block_valid, block_pos, used_experts, n_used, xs, w_gu, b_gu, w_dn, b_dn):
    n_rows, dh = xs.shape
    depth, e, d, ff2 = w_gu.shape
    ff = ff2 // 2
    nb = n_rows // MOE_ROWS
    row_map = lambda i, be, nv, po, ue, nu: (jnp.minimum(i, nu[0] - 1), 0)
    b_map = lambda i, be, nv, po, ue, nu: (layer, be[i], 0, 0)
    return pl.pallas_call(
        functools.partial(_expert_kernel, layer=layer),
        grid_spec=pltpu.PrefetchScalarGridSpec(
            num_scalar_prefetch=5,
            grid=(nb,),
            in_specs=[
                pl.BlockSpec((MOE_ROWS, dh), row_map),
                pl.BlockSpec(memory_space=pl.ANY),
                pl.BlockSpec((None, None, 1, ff2), b_map),
                pl.BlockSpec(memory_space=pl.ANY),
                pl.BlockSpec((None, None, 1, d), b_map),
            ],
            out_specs=pl.BlockSpec((MOE_ROWS, dh), row_map),
            scratch_shapes=[pltpu.VMEM((d, ff2), F32), pltpu.VMEM((ff, d), F32),
                            pltpu.VMEM((d, ff2), BF16), pltpu.VMEM((ff, d), BF16),
                            pltpu.SemaphoreType.DMA((2,))],
        ),
        out_shape=jax.ShapeDtypeStruct((n_rows, dh), jnp.uint32),
        compiler_params=pltpu.CompilerParams(dimension_semantics=("arbitrary",), vmem_limit_bytes=VMEM_LIMIT),
        name="moe_experts",
    )(block_e, block_valid, block_pos, used_experts, n_used, xs, w_gu, b_gu.reshape(depth, e, 1, ff2), w_dn,
      b_dn.reshape(depth, e, 1, d))


def _combine_kernel(x_ref, yg_ref, te_ref, mod_ref, *rest):
    o_ref = rest[-1]
    te = te_ref[...]
    acc_lo = None
    for kk in range(TOP_K):
        lo, hi = _unpack_pairs(yg_ref[kk])
        gate = te[:, TOP_K + kk:TOP_K + kk + 1]
        acc_lo = gate * lo.astype(F32) if acc_lo is None else acc_lo + gate * lo.astype(F32)
        acc_hi = gate * hi.astype(F32) if kk == 0 else acc_hi + gate * hi.astype(F32)
    acc = jnp.concatenate([acc_lo, acc_hi], axis=1)
    o_ref[...] = x_ref[...] + mod_ref[5:6, :] * acc


def _combine_call(x, yg, te, mod, prev, b0, nb):
    bsz, seq, d = x.shape
    tm = min(ROW_TILE, seq)
    nt = seq // tm
    in_specs = [
        pl.BlockSpec((None, tm, d), lambda b, s: (b0 + b, s, 0)),
        pl.BlockSpec((TOP_K, tm, d // 2), lambda b, s: (0, b * nt + s, 0)),
        pl.BlockSpec((tm, 2 * TOP_K), lambda b, s: ((b0 + b) * nt + s, 0)),
        pl.BlockSpec((None, 6, d), lambda b, s: (b0 + b, 0, 0)),
    ]
    args = [x, yg, te, mod]
    aliases = {}
    if prev is not None:
        in_specs.append(pl.BlockSpec(memory_space=pl.ANY))
        args.append(prev)
        aliases = {len(args) - 1: 0}
    return pl.pallas_call(
        _combine_kernel,
        grid=(nb, nt),
        in_specs=in_specs,
        out_specs=pl.BlockSpec((None, tm, d), lambda b, s: (b0 + b, s, 0)),
        out_shape=jax.ShapeDtypeStruct((bsz, seq, d), F32),
        input_output_aliases=aliases,
        compiler_params=pltpu.CompilerParams(dimension_semantics=("parallel", "parallel"),
                                             vmem_limit_bytes=VMEM_LIMIT),
        name="moe_combine",
    )(*args)


def _moe(layer, x_new, h2, te, tile_cnt, mod, w_gu, b_gu, w_dn, b_dn):
    bsz, seq, _ = x_new.shape
    t, dh = h2.shape
    tiles = tile_cnt.shape[0]
    splits = MOE_SPLITS if bsz % MOE_SPLITS == 0 else 1
    gt = t // splits
    gtiles = tiles // splits
    gb = bsz // splits
    n_rows = -(-(gt * TOP_K + N_EXPERTS * (MOE_ROWS - 1)) // MOE_ROWS) * MOE_ROWS
    nb = n_rows // MOE_ROWS
    first_row = jnp.arange(nb, dtype=jnp.int32) * MOE_ROWS
    upto = jnp.arange(LANES)[:, None] <= jnp.arange(LANES)[None, :]
    all_cnt = tile_cnt[:, 0, :].astype(jnp.int32)
    out = None
    for gi in range(splits):
        g_cnt = all_cnt[gi * gtiles:(gi + 1) * gtiles]
        counts = jnp.sum(g_cnt, axis=0)
        padded = (counts + MOE_ROWS - 1) // MOE_ROWS * MOE_ROWS
        pad_end = jnp.sum(jnp.where(upto, padded[:, None], 0), axis=0)
        pad_start = pad_end - padded
        tile_base = pad_start[None, :] + jnp.cumsum(g_cnt, axis=0) - g_cnt
        dest = _dest_call(te, tile_base.astype(F32)[:, None, :], gi * gt, gt)
        dests = [dest[kk] for kk in range(TOP_K)]
        block_e = jnp.minimum(jnp.sum(pad_end[None, :N_EXPERTS] <= first_row[:, None], axis=1),
                              N_EXPERTS - 1).astype(jnp.int32)
        valid_end = (pad_start + counts)[block_e]
        block_valid = jnp.clip(valid_end - first_row, 0, MOE_ROWS).astype(jnp.int32)
        owns = counts[:N_EXPERTS] > 0
        expert_pos = jnp.cumsum(owns.astype(jnp.int32)) - 1
        slot = jnp.arange(N_EXPERTS, dtype=jnp.int32)
        used_experts = jnp.sum(jnp.where(owns[None, :] & (expert_pos[None, :] == slot[:, None]),
                                         slot[None, :], 0), axis=1).astype(jnp.int32)
        block_pos = expert_pos[block_e].astype(jnp.int32)
        n_used = jnp.stack([pad_end[N_EXPERTS - 1] // MOE_ROWS, jnp.sum(owns)]).astype(jnp.int32)
        xs = _sc_scatter(h2, dests, n_rows, gi * gt)
        y = _expert_call(layer, block_e, block_valid, block_pos, used_experts, n_used, xs, w_gu, b_gu, w_dn, b_dn)
        yg = _sc_gather(y, dest[:TOP_K].reshape(-1)).reshape(TOP_K, gt, dh)
        out = _combine_call(x_new, yg, te, mod, out, gi * gb, gb)
    return out


def kernel(x, c, positions, ada_w, ada_b, norm_mix_g, norm_ffn_g, router_w, router_b, moe_w_gu, moe_b_gu,
           moe_w_dn, moe_b_dn, even_w_in, mla_q_norm_g, mla_w_uq, mla_kv_norm_g, mla_w_ukv, mla_q_head_g,
           mla_k_head_g, s5_a_re, s5_a_im, s5_log_dt, s5_b_re, s5_b_im, s5_c_re, s5_c_im, s5_d, s5_glu_w,
           s5_glu_b, even_w_out, odd_w_in, pool_w, pool_scale, sgu_norm_g, sgu_w, sgu_b, odd_w_out):
    bsz, seq, d = x.shape
    depth = ada_w.shape[0]
    mods = _ada_call(c, ada_w, ada_b).reshape(depth, bsz, 6, d)
    posf = positions.astype(F32).reshape(bsz, seq, 1)
    for layer in range(depth):
        mod = mods[layer]
        i = layer // 2
        g_mix = norm_mix_g[layer].reshape(1, d)
        g_ffn = norm_ffn_g[layer].reshape(1, d)
        rw, rb = _router_pad(router_w[layer], router_b[layer])
        if layer % 2 == 0:
            prep = _prep_even(even_w_in[i], mla_q_norm_g[i], mla_w_uq[i], mla_kv_norm_g[i], mla_w_ukv[i],
                              mla_q_head_g[i], mla_k_head_g[i])
            q, k, v, u_t = _even_in_call(x, mod, posf, g_mix, prep)
            attn = _attn_call(q, k, v)
            disc = _s5_disc_call(s5_a_re[i], s5_a_im[i], s5_log_dt[i], s5_b_re[i], s5_b_im[i])
            ssm_t = _s5_call(u_t.reshape(seq, bsz, d // 2), disc, s5_c_re[i], s5_c_im[i], s5_d[i],
                             s5_glu_w[i], s5_glu_b[i])
            x_new, h2, te, tile_cnt = _mix_out_call(x, attn, ssm_t.reshape(seq, bsz * (d // 2)), mod,
                                                    even_w_out[i].astype(BF16), g_ffn, rw, rb)
        else:
            x_new, h2, te, tile_cnt = _odd_call(x, mod, g_mix, odd_w_in[i], pool_w[i], pool_scale[i],
                                                sgu_norm_g[i], sgu_w[i], sgu_b[i], odd_w_out[i], g_ffn, rw, rb)
        x = _moe(layer, x_new, h2, te, tile_cnt, mod, moe_w_gu, moe_b_gu, moe_w_dn, moe_b_dn)
    return x
```

```python
import functools
import math

import jax
import jax.numpy as jnp
from jax import lax
from jax.experimental import pallas as pl
from jax.experimental.pallas import tpu as pltpu
from jax.experimental.pallas import tpu_sc as plsc

F32 = jnp.float32
BF16 = jnp.bfloat16
HIGHEST = lax.Precision.HIGHEST

NORM_EPS = 1e-6
MLA_HEADS = 8
QK_NOPE_DIM = 64
QK_ROPE_DIM = 32
QK_HEAD_DIM = QK_NOPE_DIM + QK_ROPE_DIM
V_HEAD_DIM = 64
Q_LORA_RANK = 256
KV_LORA_RANK = 128
ROPE_THETA = 10000.0
S5_GROUP = 16
S5_STATE = 64
POOL_WINDOWS = (2, 4, 8, 16)
SGU_HEADS = 4
SGU_CHUNK = 128
N_EXPERTS = 32
TOP_K = 4
SWIGLU_ALPHA = 1.702
SWIGLU_LIMIT = 7.0

LANES = 128
SUBLANES = 8
HEAD_SLAB = LANES
POOL_HALO = 16
ROW_TILE = 512
ATTN_TILE = 512
ATTN_ROWS = 32
ATTN_HEADS = 4
S5_STEPS = 64
MOE_ROWS = 1024
MOE_TAIL_PARTS = 4
MOE_SPLITS = 2
SC_WORKERS = 32
SC_CHUNK = 64
VMEM_LIMIT = 56 * 1024 * 1024
NEG_BIG = -1e30


def _sigmoid(v):
    return 1.0 / (1.0 + jnp.exp(-v))


def _gelu(v):
    return 0.5 * v * (1.0 + jnp.tanh(math.sqrt(2.0 / math.pi) * (v + 0.044715 * (v * v * v))))


def _rms(v, width):
    return lax.rsqrt(jnp.sum(v * v, axis=-1, keepdims=True) * (1.0 / width) + NORM_EPS)


def _mod_norm(x, g, sc, sh):
    return x * _rms(x, x.shape[-1]) * g * (1.0 + sc) + sh


def _bdot(a, b):
    return jnp.dot(a.astype(BF16), b, preferred_element_type=F32)


def _pack_pairs(v):
    w = v.shape[-1] // 2
    bits = pltpu.bitcast(v.astype(BF16).astype(F32), jnp.uint32)
    return (bits[:, :w] >> 16) | bits[:, w:]


def _unpack_pairs(p):
    lo = pltpu.bitcast(p << 16, F32)
    hi = pltpu.bitcast(p & jnp.uint32(0xFFFF0000), F32)
    return lo.astype(BF16), hi.astype(BF16)


def _ada_kernel(c_ref, w_ref, b_ref, o_ref):
    c = c_ref[...]
    act = c * _sigmoid(c)
    o_ref[...] = jnp.dot(act, w_ref[...], precision=HIGHEST, preferred_element_type=F32) + b_ref[...]


def _ada_call(c, ada_w, ada_b):
    depth, d, n = ada_w.shape
    bsz = c.shape[0]
    tn = 1536
    return pl.pallas_call(
        _ada_kernel,
        grid=(depth, n // tn),
        in_specs=[
            pl.BlockSpec((bsz, d), lambda l, j: (0, 0)),
            pl.BlockSpec((None, d, tn), lambda l, j: (l, 0, j)),
            pl.BlockSpec((None, 1, tn), lambda l, j: (l, 0, j)),
        ],
        out_specs=pl.BlockSpec((None, bsz, tn), lambda l, j: (l, 0, j)),
        out_shape=jax.ShapeDtypeStruct((depth, bsz, n), F32),
        compiler_params=pltpu.CompilerParams(dimension_semantics=("parallel", "parallel"),
                                             vmem_limit_bytes=VMEM_LIMIT),
        name="ada_mod",
    )(c, ada_w, ada_b.reshape(depth, 1, n))


_C_Q = 0
_C_KV = Q_LORA_RANK
_C_PE = _C_KV + KV_LORA_RANK
_C_PESW = _C_PE + HEAD_SLAB
_C_U = _C_PESW + HEAD_SLAB


def _even_in_kernel(x_ref, mod_ref, pos_ref, g_ref, win_ref, gq_ref, wq_ref, gkv_ref, wk_ref, wv_ref,
                    tab_ref, q_ref, k_ref, v_ref, u_ref):
    x = x_ref[...]
    h = _mod_norm(x, g_ref[...], mod_ref[1:2, :], mod_ref[0:1, :])
    z = _bdot(h, win_ref[...])
    q_c = z[:, _C_Q:_C_KV]
    kv_c = z[:, _C_KV:_C_PE]
    kpe = z[:, _C_PE:_C_PESW]
    kpe_sw = z[:, _C_PESW:_C_U]
    u_ref[...] = z[:, _C_U:]

    ang = pos_ref[...] * tab_ref[0:1, :]
    cs = jnp.cos(ang)
    sn = jnp.sin(ang)
    gcq = cs * tab_ref[1:2, :]
    gsq = sn * tab_ref[2:3, :]
    gck = cs * tab_ref[3:4, :]
    gsk = sn * tab_ref[4:5, :]

    qn = q_c * _rms(q_c, Q_LORA_RANK) * gq_ref[...]
    qq = _bdot(qn, wq_ref[...])
    kvn = kv_c * _rms(kv_c, KV_LORA_RANK) * gkv_ref[...]
    kk = _bdot(kvn, wk_ref[...])
    v_ref[...] = _bdot(kvn, wv_ref[...]).astype(v_ref.dtype)

    pe_rot = kpe * gck + kpe_sw * gsk
    pe_ss = jnp.sum(kpe * kpe, axis=-1, keepdims=True)
    hw = MLA_HEADS * HEAD_SLAB
    for hd in range(MLA_HEADS):
        lo = hd * HEAD_SLAB
        qr = qq[:, lo:lo + HEAD_SLAB]
        qs = qq[:, hw + lo:hw + lo + HEAD_SLAB]
        rq = _rms(qr, QK_HEAD_DIM)
        q_ref[hd] = (rq * (qr * gcq + qs * gsq)).astype(q_ref.dtype)
        kr = kk[:, lo:lo + HEAD_SLAB]
        rk = lax.rsqrt((jnp.sum(kr * kr, axis=-1, keepdims=True) + pe_ss) * (1.0 / QK_HEAD_DIM) + NORM_EPS)
        k_ref[hd] = (rk * (kr * gck + pe_rot)).astype(k_ref.dtype)


def _even_in_call(x, mod, posf, g, prep):
    bsz, seq, d = x.shape
    tm = min(ROW_TILE, seq)
    hw = MLA_HEADS * HEAD_SLAB
    full = lambda a: pl.BlockSpec(a.shape, lambda b, s: (0,) * a.ndim)
    return pl.pallas_call(
        _even_in_kernel,
        grid=(bsz, seq // tm),
        in_specs=[
            pl.BlockSpec((None, tm, d), lambda b, s: (b, s, 0)),
            pl.BlockSpec((None, 6, d), lambda b, s: (b, 0, 0)),
            pl.BlockSpec((None, tm, 1), lambda b, s: (b, s, 0)),
            full(g), full(prep["w_in"]), full(prep["gq"]), full(prep["wq"]), full(prep["gkv"]),
            full(prep["wk"]), full(prep["wv"]), full(prep["tab"]),
        ],
        out_specs=[
            pl.BlockSpec((None, MLA_HEADS, tm, HEAD_SLAB), lambda b, s: (b, 0, s, 0)),
            pl.BlockSpec((None, MLA_HEADS, tm, HEAD_SLAB), lambda b, s: (b, 0, s, 0)),
            pl.BlockSpec((None, tm, MLA_HEADS * V_HEAD_DIM), lambda b, s: (b, s, 0)),
            pl.BlockSpec((tm, d // 2), lambda b, s: (s, b)),
        ],
        out_shape=[
            jax.ShapeDtypeStruct((bsz, MLA_HEADS, seq, HEAD_SLAB), BF16),
            jax.ShapeDtypeStruct((bsz, MLA_HEADS, seq, HEAD_SLAB), BF16),
            jax.ShapeDtypeStruct((bsz, seq, MLA_HEADS * V_HEAD_DIM), BF16),
            jax.ShapeDtypeStruct((seq, bsz * (d // 2)), F32),
        ],
        compiler_params=pltpu.CompilerParams(dimension_semantics=("parallel", "parallel"),
                                             vmem_limit_bytes=VMEM_LIMIT),
        name="even_in",
    )(x, mod, posf, g, prep["w_in"], prep["gq"], prep["wq"], prep["gkv"], prep["wk"], prep["wv"], prep["tab"])


def _prep_even(even_w_in, q_norm_g, w_uq, kv_norm_g, w_ukv, q_head_g, k_head_g):
    d = even_w_in.shape[0]
    half = QK_ROPE_DIM // 2
    nope = QK_NOPE_DIM
    c_pe = Q_LORA_RANK + KV_LORA_RANK
    w_pe = even_w_in[:, c_pe:c_pe + QK_ROPE_DIM]
    zeros = lambda n: jnp.zeros((d, n), F32)
    pe_slab = jnp.concatenate([zeros(nope), w_pe, zeros(HEAD_SLAB - QK_HEAD_DIM)], axis=1)
    pe_sw = jnp.concatenate([zeros(nope), -w_pe[:, half:], w_pe[:, :half], zeros(HEAD_SLAB - QK_HEAD_DIM)], axis=1)
    w_in = jnp.concatenate([even_w_in[:, :c_pe], pe_slab, pe_sw, even_w_in[:, c_pe + QK_ROPE_DIM:]], axis=1)

    r = w_uq.shape[0]
    padq = jnp.zeros((r, MLA_HEADS, HEAD_SLAB - QK_HEAD_DIM), F32)
    wq_plain = jnp.concatenate([w_uq, padq], axis=2).reshape(r, MLA_HEADS * HEAD_SLAB)
    wq_sw = jnp.concatenate([jnp.zeros((r, MLA_HEADS, nope), F32), -w_uq[:, :, nope + half:],
                             w_uq[:, :, nope:nope + half], padq], axis=2).reshape(r, MLA_HEADS * HEAD_SLAB)
    wq = jnp.concatenate([wq_plain, wq_sw], axis=1)

    rk = w_ukv.shape[0]
    wk = jnp.concatenate([w_ukv[:, :, :nope], jnp.zeros((rk, MLA_HEADS, HEAD_SLAB - nope), F32)],
                         axis=2).reshape(rk, MLA_HEADS * HEAD_SLAB)
    wv = w_ukv[:, :, nope:].reshape(rk, MLA_HEADS * V_HEAD_DIM)

    inv_freq = 1.0 / (ROPE_THETA ** (jnp.arange(half, dtype=F32) / half))
    pad_tail = jnp.zeros((HEAD_SLAB - QK_HEAD_DIM,), F32)
    freq_row = jnp.concatenate([jnp.zeros((nope,), F32), inv_freq, inv_freq, pad_tail])

    def gain_rows(gv, scale):
        plain = jnp.concatenate([gv, pad_tail]) * scale
        swapped = jnp.concatenate([jnp.zeros((nope,), F32), gv[nope + half:], gv[nope:nope + half], pad_tail]) * scale
        return plain, swapped

    gq_plain, gq_sw = gain_rows(q_head_g, QK_HEAD_DIM ** -0.5 * math.log2(math.e))
    gk_plain, gk_sw = gain_rows(k_head_g, 1.0)
    tab = jnp.stack([freq_row, gq_plain, gq_sw, gk_plain, gk_sw, freq_row * 0, freq_row * 0, freq_row * 0])
    return {
        "w_in": w_in.astype(BF16), "gq": q_norm_g.reshape(1, -1), "wq": wq.astype(BF16),
        "gkv": kv_norm_g.reshape(1, -1), "wk": wk.astype(BF16), "wv": wv.astype(BF16), "tab": tab,
    }


def _attn_kernel(qi_ref, kj_ref, q_ref, k_ref, v_ref, o_ref, m_sc, a_sc, acc_sc, s_sc, p_sc, *, tq, tk):
    step = pl.program_id(2)
    i = qi_ref[step]
    j = kj_ref[step]
    sum_lane = (V_HEAD_DIM, 0)

    @pl.when(j == 0)
    def _():
        m_sc[...] = jnp.full(m_sc.shape, -jnp.inf, F32)
        acc_sc[...] = jnp.zeros(acc_sc.shape, F32)

    def sweep(on_diagonal):
        lane = lax.broadcasted_iota(jnp.int32, (1, LANES), 1)
        for hh in range(ATTN_HEADS):
            s_sc[hh] = lax.dot_general(q_ref[hh], k_ref[hh], (((1,), (1,)), ((), ())),
                                       preferred_element_type=F32)
        for hh in range(ATTN_HEADS):
            v = v_ref[:, (hh // 2) * LANES:(hh // 2 + 1) * LANES]
            for r0 in range(0, tq, ATTN_ROWS):
                rows = pl.ds(r0, ATTN_ROWS)
                s = s_sc[hh, rows, :]
                if on_diagonal:
                    row = r0 + lax.broadcasted_iota(jnp.int32, (ATTN_ROWS, tk), 0)
                    col = lax.broadcasted_iota(jnp.int32, (ATTN_ROWS, tk), 1)
                    s = jnp.where(col <= row, s, -jnp.inf)
                m_prev = m_sc[hh, rows, :]
                m_new = jnp.maximum(m_prev, jnp.max(s, axis=-1, keepdims=True))
                a_sc[hh, rows, :] = jnp.exp2(m_prev - m_new)
                m_sc[hh, rows, :] = m_new
                shifted = s - jnp.concatenate([m_new] * (tk // LANES), axis=1)
                p_sc[hh, rows, :] = jnp.exp2(shifted.astype(BF16))
            own = (lane < V_HEAD_DIM) == (hh % 2 == 0)
            ones = jnp.where(lane == sum_lane[hh % 2], 1.0, 0.0).astype(v.dtype)
            vh = jnp.where(own, v, jnp.broadcast_to(ones, v.shape))
            acc_sc[hh] = acc_sc[hh] * a_sc[hh] + jnp.dot(p_sc[hh], vh, preferred_element_type=F32)

    @pl.when(j < i)
    def _():
        sweep(False)

    @pl.when(j == i)
    def _():
        sweep(True)
        lane = lax.broadcasted_iota(jnp.int32, (1, LANES), 1)
        for pp in range(ATTN_HEADS // 2):
            acc0 = acc_sc[2 * pp]
            acc1 = acc_sc[2 * pp + 1]
            l0 = acc0[:, sum_lane[0]:sum_lane[0] + 1]
            l1 = acc1[:, sum_lane[1]:sum_lane[1] + 1]
            o_ref[:, pp * LANES:(pp + 1) * LANES] = jnp.where(lane < V_HEAD_DIM, acc0 / l0,
                                                              acc1 / l1).astype(o_ref.dtype)


def _attn_call(q, k, v):
    bsz, nh, seq, _ = q.shape
    tq = tk = min(ATTN_TILE, seq)
    nq = seq // tq
    pairs = [(i, j) for i in range(nq) for j in range(i + 1)]
    qi = jnp.asarray([p[0] for p in pairs], jnp.int32)
    kj = jnp.asarray([p[1] for p in pairs], jnp.int32)
    kern = functools.partial(_attn_kernel, tq=tq, tk=tk)
    hp = ATTN_HEADS
    assert nh % hp == 0
    return pl.pallas_call(
        kern,
        grid_spec=pltpu.PrefetchScalarGridSpec(
            num_scalar_prefetch=2,
            grid=(bsz, nh // hp, len(pairs)),
            in_specs=[
                pl.BlockSpec((None, hp, tq, HEAD_SLAB), lambda b, h, p, qi, kj: (b, h, qi[p], 0)),
                pl.BlockSpec((None, hp, tk, HEAD_SLAB), lambda b, h, p, qi, kj: (b, h, kj[p], 0)),
                pl.BlockSpec((None, tk, hp * V_HEAD_DIM), lambda b, h, p, qi, kj: (b, kj[p], h)),
            ],
            out_specs=pl.BlockSpec((None, tq, hp * V_HEAD_DIM), lambda b, h, p, qi, kj: (b, qi[p], h)),
            scratch_shapes=[pltpu.VMEM((hp, tq, LANES), F32), pltpu.VMEM((hp, tq, LANES), F32),
                            pltpu.VMEM((hp, tq, LANES), F32),
                            pltpu.VMEM((hp, tq, tk), F32), pltpu.VMEM((hp, tq, tk), BF16)],
        ),
        out_shape=jax.ShapeDtypeStruct((bsz, seq, nh * V_HEAD_DIM), BF16),
        compiler_params=pltpu.CompilerParams(
            dimension_semantics=("parallel", "parallel", "arbitrary"),
            vmem_limit_bytes=VMEM_LIMIT),
        name="mla_attention",
    )(qi, kj, q, k, v)


def _s5_disc_kernel(are_ref, aim_ref, ldt_ref, bre_ref, bim_ref, abre_ref, abim_ref, bbre_ref, bbim_ref):
    dt = jnp.exp(ldt_ref[...])
    lam_re = jnp.minimum(are_ref[...], -1e-4)
    lam_im = aim_ref[...]
    mag = jnp.exp(lam_re * dt)
    ab_re = mag * jnp.cos(lam_im * dt)
    ab_im = mag * jnp.sin(lam_im * dt)
    den = lam_re * lam_re + lam_im * lam_im
    num_re = ab_re - 1.0
    f_re = (num_re * lam_re + ab_im * lam_im) / den
    f_im = (ab_im * lam_re - num_re * lam_im) / den
    abre_ref[...] = ab_re
    abim_ref[...] = ab_im
    br = bre_ref[...]
    bi = bim_ref[...]
    bbre_ref[...] = f_re[:, None, :] * br - f_im[:, None, :] * bi
    bbim_ref[...] = f_re[:, None, :] * bi + f_im[:, None, :] * br


def _s5_disc_call(a_re, a_im, log_dt, b_re, b_im):
    g, p = a_re.shape
    bre_t = jnp.swapaxes(b_re, 1, 2)
    bim_t = jnp.swapaxes(b_im, 1, 2)
    return pl.pallas_call(
        _s5_disc_kernel,
        out_shape=[jax.ShapeDtypeStruct((g, p), F32), jax.ShapeDtypeStruct((g, p), F32),
                   jax.ShapeDtypeStruct(bre_t.shape, F32), jax.ShapeDtypeStruct(bre_t.shape, F32)],
        name="s5_discretize",
    )(a_re, a_im, log_dt.reshape(g, 1), bre_t, bim_t)


def _block_diag_halves(m):
    g, r, c = m.shape
    gh = g // 2
    eye = jnp.eye(gh, dtype=m.dtype)
    mh = m.reshape(2, gh, r, c)
    return (mh[:, :, :, None, :] * eye[None, :, None, :, None]).reshape(2, gh * r, gh * c)


def _s5_kernel(u_ref, bre_ref, bim_ref, are_ref, aim_ref, cre_ref, cim_ref, d_ref, gw_ref, gb_ref,
               o_ref, sre, sim, dre, dim, xbr, xbi, *, steps):
    @pl.when(pl.program_id(0) == 0)
    def _():
        sre[...] = jnp.zeros(sre.shape, F32)
        sim[...] = jnp.zeros(sim.shape, F32)

    rows = steps * SUBLANES
    w = u_ref.shape[-1]
    u = u_ref[...].reshape(rows, w)
    ub = u.astype(BF16)
    kh = w // 2
    nh = dre.shape[1] // 2
    for hf in range(2):
        dre[:, hf * nh:(hf + 1) * nh] = jnp.dot(ub[:, hf * kh:(hf + 1) * kh], bre_ref[hf], preferred_element_type=F32)
        dim[:, hf * nh:(hf + 1) * nh] = jnp.dot(ub[:, hf * kh:(hf + 1) * kh], bim_ref[hf], preferred_element_type=F32)

    xr = sre[...]
    xi = sim[...]
    for t in range(0, steps, 2):
        pair_r = []
        pair_i = []
        for r0 in (t * SUBLANES, (t + 1) * SUBLANES):
            nr = are_ref[...] * xr - aim_ref[...] * xi + dre[r0:r0 + SUBLANES, :]
            ni = are_ref[...] * xi + aim_ref[...] * xr + dim[r0:r0 + SUBLANES, :]
            xr, xi = nr, ni
            pair_r.append(nr)
            pair_i.append(ni)
        xbr[t * SUBLANES:(t + 2) * SUBLANES, :] = jnp.concatenate(pair_r, axis=0).astype(BF16)
        xbi[t * SUBLANES:(t + 2) * SUBLANES, :] = jnp.concatenate(pair_i, axis=0).astype(BF16)
    sre[...] = xr
    sim[...] = xi

    ys = []
    for hf in range(2):
        yr = jnp.dot(xbr[:, hf * nh:(hf + 1) * nh], cre_ref[hf], preferred_element_type=F32)
        yi = jnp.dot(xbi[:, hf * nh:(hf + 1) * nh], cim_ref[hf], preferred_element_type=F32)
        ys.append(yr - yi)
    y = jnp.concatenate(ys, axis=1) + d_ref[...] * u
    g = _gelu(y)
    out = g * _sigmoid(_bdot(g, gw_ref[...]) + gb_ref[...])
    o_ref[...] = out.reshape(steps, SUBLANES, w).astype(o_ref.dtype)


def _s5_call(u_t, disc, c_re, c_im, d_skip, glu_w, glu_b):
    seq, bsz, w = u_t.shape
    assert bsz == SUBLANES
    ab_re, ab_im, bb_re, bb_im = disc
    g, p = ab_re.shape
    n_state = g * p
    bre = _block_diag_halves(bb_re).astype(BF16)
    bim = _block_diag_halves(bb_im).astype(BF16)
    cre = _block_diag_halves(jnp.swapaxes(c_re, 1, 2)).astype(BF16)
    cim = _block_diag_halves(jnp.swapaxes(c_im, 1, 2)).astype(BF16)
    steps = min(S5_STEPS, seq)
    full = lambda a: pl.BlockSpec(a.shape, lambda s: (0,) * a.ndim)
    rep = lambda a: jnp.broadcast_to(a.reshape(1, n_state), (bsz, n_state))
    args = (bre, bim, rep(ab_re), rep(ab_im), cre, cim,
            d_skip.reshape(1, w), glu_w.astype(BF16), glu_b.reshape(1, w))
    return pl.pallas_call(
        functools.partial(_s5_kernel, steps=steps),
        grid=(seq // steps,),
        in_specs=[pl.BlockSpec((steps, bsz, w), lambda s: (s, 0, 0))] + [full(a) for a in args],
        out_specs=pl.BlockSpec((steps, bsz, w), lambda s: (s, 0, 0)),
        out_shape=jax.ShapeDtypeStruct((seq, bsz, w), BF16),
        scratch_shapes=[pltpu.VMEM((bsz, n_state), F32), pltpu.VMEM((bsz, n_state), F32),
                        pltpu.VMEM((steps * bsz, n_state), F32), pltpu.VMEM((steps * bsz, n_state), F32),
                        pltpu.VMEM((steps * bsz, n_state), BF16), pltpu.VMEM((steps * bsz, n_state), BF16)],
        compiler_params=pltpu.CompilerParams(dimension_semantics=("arbitrary",), vmem_limit_bytes=VMEM_LIMIT),
        name="s5_scan",
    )(u_t, *args)


def _router_tail(x_new, mod_ref, gf_ref, rw_ref, rb_ref, h2_ref, te_ref, cnt_ref):
    h2 = _mod_norm(x_new, gf_ref[...], mod_ref[4:5, :], mod_ref[3:4, :])
    h2_ref[...] = _pack_pairs(h2)
    h_hi = h2.astype(BF16)
    h_lo = (h2 - h_hi.astype(F32)).astype(BF16)
    r_hi = jnp.dot(h_hi, rw_ref[...], preferred_element_type=F32)
    r_lo = jnp.dot(h_lo, rw_ref[...], preferred_element_type=F32)
    logits = r_hi[:, :LANES] + r_hi[:, LANES:] + r_lo[:, :LANES] + rb_ref[...]
    lane = lax.broadcasted_iota(jnp.int32, logits.shape, 1).astype(F32)
    vals = []
    idxs = []
    work = logits
    for _ in range(TOP_K):
        m = jnp.max(work, axis=-1, keepdims=True)
        idx = jnp.min(jnp.where(work == m, lane, float(LANES)), axis=-1, keepdims=True)
        vals.append(m)
        idxs.append(idx)
        work = jnp.where(lane == idx, NEG_BIG * 2.0, work)
    exps = [jnp.exp(vv - vals[0]) for vv in vals]
    tot = exps[0] + exps[1] + exps[2] + exps[3]
    te = jnp.zeros(logits.shape, F32)
    picked = jnp.zeros(logits.shape, F32)
    for kk in range(TOP_K):
        te = jnp.where(lane == float(kk), idxs[kk], te)
        te = jnp.where(lane == float(TOP_K + kk), exps[kk] / tot, te)
        picked = picked + jnp.where(lane == idxs[kk], 1.0, 0.0)
    te_ref[...] = te[:, :2 * TOP_K]
    cnt_ref[...] = jnp.sum(picked, axis=0, keepdims=True)


def _mix_out_kernel(x_ref, a_ref, s_ref, mod_ref, wo_ref, gf_ref, rw_ref, rb_ref, xo_ref, h2_ref, te_ref,
                    cnt_ref):
    ka = a_ref.shape[-1]
    mix = jnp.dot(a_ref[...], wo_ref[:ka, :], preferred_element_type=F32)
    mix = mix + jnp.dot(s_ref[...], wo_ref[ka:, :], preferred_element_type=F32)
    x_new = x_ref[...] + mod_ref[2:3, :] * mix
    xo_ref[...] = x_new
    _router_tail(x_new, mod_ref, gf_ref, rw_ref, rb_ref, h2_ref, te_ref, cnt_ref)


def _tail_out_specs(bsz, seq, tm, d):
    nt = seq // tm
    specs = [
        pl.BlockSpec((None, tm, d), lambda b, s: (b, s, 0)),
        pl.BlockSpec((tm, d // 2), lambda b, s: (b * nt + s, 0)),
        pl.BlockSpec((tm, 2 * TOP_K), lambda b, s: (b * nt + s, 0)),
        pl.BlockSpec((None, 1, LANES), lambda b, s: (b * nt + s, 0, 0)),
    ]
    shapes = [
        jax.ShapeDtypeStruct((bsz, seq, d), F32),
        jax.ShapeDtypeStruct((bsz * seq, d // 2), jnp.uint32),
        jax.ShapeDtypeStruct((bsz * seq, 2 * TOP_K), F32),
        jax.ShapeDtypeStruct((bsz * nt, 1, LANES), F32),
    ]
    return specs, shapes


def _router_pad(router_w, router_b):
    d, e = router_w.shape
    rw = jnp.concatenate([router_w, jnp.zeros((d, LANES - e), F32)], axis=1)
    rw_hi = rw.astype(BF16)
    rw_lo = (rw - rw_hi.astype(F32)).astype(BF16)
    rb = jnp.concatenate([router_b, jnp.full((LANES - e,), NEG_BIG, F32)]).reshape(1, LANES)
    return jnp.concatenate([rw_hi, rw_lo], axis=1), rb


def _mix_out_call(x, attn, ssm_t, mod, w_out, gf, rw, rb):
    bsz, seq, d = x.shape
    tm = min(ROW_TILE, seq)
    ka = attn.shape[-1]
    ks = ssm_t.shape[-1] // bsz
    full = lambda a: pl.BlockSpec(a.shape, lambda b, s: (0,) * a.ndim)
    out_specs, out_shape = _tail_out_specs(bsz, seq, tm, d)
    return pl.pallas_call(
        _mix_out_kernel,
        grid=(bsz, seq // tm),
        in_specs=[
            pl.BlockSpec((None, tm, d), lambda b, s: (b, s, 0)),
            pl.BlockSpec((None, tm, ka), lambda b, s: (b, s, 0)),
            pl.BlockSpec((tm, ks), lambda b, s: (s, b)),
            pl.BlockSpec((None, 6, d), lambda b, s: (b, 0, 0)),
            full(w_out), full(gf), full(rw), full(rb),
        ],
        out_specs=out_specs,
        out_shape=out_shape,
        compiler_params=pltpu.CompilerParams(dimension_semantics=("parallel", "parallel"),
                                             vmem_limit_bytes=VMEM_LIMIT),
        name="even_out",
    )(x, attn, ssm_t, mod, w_out, gf, rw, rb)


def _odd_kernel(x_ref, mod_ref, g_ref, win_ref, icnt_ref, wp_ref, ps_ref, gv_ref, wsp_ref, bsp_ref,
                wo_ref, gf_ref, rw_ref, rb_ref, xo_ref, h2_ref, te_ref, cnt_ref, ext_sc):
    tm = x_ref.shape[0]
    pw = wp_ref.shape[-1]
    width = pw * len(POOL_WINDOWS)

    @pl.when(pl.program_id(1) == 0)
    def _():
        ext_sc[0:POOL_HALO, :] = jnp.zeros((POOL_HALO, width), F32)

    x = x_ref[...]
    h = _mod_norm(x, g_ref[...], mod_ref[1:2, :], mod_ref[0:1, :])
    z = _bdot(h, win_ref[...])
    up = z[:, :width]
    ext_sc[POOL_HALO:POOL_HALO + tm, :] = up

    pooled = []
    for gi, win in enumerate(POOL_WINDOWS):
        cols = slice(gi * pw, (gi + 1) * pw)
        acc = up[:, cols]
        for lag in range(1, win):
            acc = acc + ext_sc[POOL_HALO - lag:POOL_HALO - lag + tm, cols]
        pg = acc * icnt_ref[:, gi:gi + 1] - up[:, cols]
        pooled.append(_bdot(pg, wp_ref[gi]) * ps_ref[:, cols])
    ext_sc[0:POOL_HALO, :] = ext_sc[tm:tm + POOL_HALO, :]
    pooled = jnp.concatenate(pooled, axis=1)

    ug = _gelu(z[:, width:2 * width])
    vg = _gelu(z[:, 2 * width:])
    vn = (vg * _rms(vg, width) * gv_ref[...]).astype(BF16)
    hd = width // SGU_HEADS
    chunks = []
    for ci in range(tm // SGU_CHUNK):
        heads = []
        for hh in range(SGU_HEADS):
            blk = vn[ci * SGU_CHUNK:(ci + 1) * SGU_CHUNK, hh * hd:(hh + 1) * hd]
            heads.append(jnp.dot(wsp_ref[hh], blk, preferred_element_type=F32) + bsp_ref[hh])
        chunks.append(jnp.concatenate(heads, axis=1))
    gated = ug * jnp.concatenate(chunks, axis=0)

    mix = _bdot(pooled, wo_ref[:width, :]) + _bdot(gated, wo_ref[width:, :])
    x_new = x + mod_ref[2:3, :] * mix
    xo_ref[...] = x_new
    _router_tail(x_new, mod_ref, gf_ref, rw_ref, rb_ref, h2_ref, te_ref, cnt_ref)


def _odd_call(x, mod, g, w_in, pool_w, pool_scale, sgu_norm_g, sgu_w, sgu_b, w_out, gf, rw, rb):
    bsz, seq, d = x.shape
    tm = min(ROW_TILE, seq)
    width = pool_scale.shape[0]
    hd = width // SGU_HEADS
    t = jnp.arange(seq, dtype=jnp.int32)
    icnt = jnp.stack([1.0 / jnp.minimum(t + 1, wn).astype(F32) for wn in POOL_WINDOWS], axis=1)
    wsp = jnp.tril(sgu_w).astype(BF16)
    bsp = jnp.broadcast_to(sgu_b[:, :, None], (SGU_HEADS, SGU_CHUNK, hd))
    args = (g, w_in.astype(BF16), icnt, pool_w.astype(BF16), pool_scale.reshape(1, width),
            sgu_norm_g.reshape(1, width), wsp, bsp, w_out.astype(BF16), gf, rw, rb)
    full = lambda a: pl.BlockSpec(a.shape, lambda b, s: (0,) * a.ndim)
    in_specs = [pl.BlockSpec((None, tm, d), lambda b, s: (b, s, 0)),
                pl.BlockSpec((None, 6, d), lambda b, s: (b, 0, 0))]
    for idx, a in enumerate(args):
        in_specs.append(pl.BlockSpec((tm, len(POOL_WINDOWS)), lambda b, s: (s, 0)) if idx == 2 else full(a))
    out_specs, out_shape = _tail_out_specs(bsz, seq, tm, d)
    return pl.pallas_call(
        _odd_kernel,
        grid=(bsz, seq // tm),
        in_specs=in_specs,
        out_specs=out_specs,
        out_shape=out_shape,
        scratch_shapes=[pltpu.VMEM((tm + POOL_HALO, width), F32)],
        compiler_params=pltpu.CompilerParams(dimension_semantics=("parallel", "arbitrary"),
                                             vmem_limit_bytes=VMEM_LIMIT),
        name="odd_mixer",
    )(x, mod, *args)


def _dest_kernel(te_ref, base_ref, dst_ref):
    tr = te_ref.shape[0]
    te = te_ref[...]
    lane = lax.broadcasted_iota(jnp.int32, (tr, LANES), 1).astype(F32)
    hots = [te[:, kk:kk + 1] == lane for kk in range(TOP_K)]
    oh = jnp.zeros((tr, LANES), F32)
    for hot in hots:
        oh = oh + jnp.where(hot, 1.0, 0.0)
    r_i = lax.broadcasted_iota(jnp.int32, (tr, tr), 0)
    c_i = lax.broadcasted_iota(jnp.int32, (tr, tr), 1)
    tri = jnp.where(c_i < r_i, 1.0, 0.0).astype(BF16)
    before = jnp.dot(tri, oh.astype(BF16), preferred_element_type=F32) + base_ref[...]
    dst = jnp.zeros((tr, LANES), F32)
    for kk, hot in enumerate(hots):
        dst = jnp.where(lane == float(kk), jnp.sum(jnp.where(hot, before, 0.0), axis=-1, keepdims=True), dst)
    dst_ref[...] = dst.T[:2 * TOP_K, :].astype(jnp.int32)


def _dest_call(te, base, tok0, n_tok):
    tiles = base.shape[0]
    tr = n_tok // tiles
    tile0 = tok0 // tr
    return pl.pallas_call(
        _dest_kernel,
        grid=(tiles,),
        in_specs=[pl.BlockSpec((tr, 2 * TOP_K), lambda i: (tile0 + i, 0)),
                  pl.BlockSpec((None, 1, LANES), lambda i: (i, 0, 0))],
        out_specs=pl.BlockSpec((2 * TOP_K, tr), lambda i: (0, i)),
        out_shape=jax.ShapeDtypeStruct((2 * TOP_K, n_tok), jnp.int32),
        compiler_params=pltpu.CompilerParams(dimension_semantics=("parallel",)),
        name="route_dest",
    )(te, base)


def _sc_gather(table, idx):
    n = idx.shape[0]
    per_w = n // SC_WORKERS
    assert per_w * SC_WORKERS == n and per_w % SC_CHUNK == 0
    n_chunks = per_w // SC_CHUNK
    row_shape = table.shape[1:]
    mesh = plsc.VectorSubcoreMesh(core_axis_name="c", subcore_axis_name="s")

    @functools.partial(
        pl.kernel, mesh=mesh,
        out_type=jax.ShapeDtypeStruct((n,) + row_shape, table.dtype),
        scratch_types=[pltpu.VMEM((SC_CHUNK,), jnp.int32), pltpu.VMEM((SC_CHUNK,) + row_shape, table.dtype),
                       pltpu.SemaphoreType.DMA],
        name="sc_row_gather",
    )
    def gather(table_hbm, idx_hbm, out_hbm, idx_v, rows_v, sem):
        wid = lax.axis_index("s") * 2 + lax.axis_index("c")
        base = wid * per_w

        @pl.loop(0, n_chunks)
        def _(ci):
            off = pl.multiple_of(base + ci * SC_CHUNK, SC_CHUNK)
            pltpu.sync_copy(idx_hbm.at[pl.ds(off, SC_CHUNK)], idx_v)
            pltpu.async_copy(table_hbm.at[idx_v], rows_v, sem).wait()
            pltpu.sync_copy(rows_v, out_hbm.at[pl.ds(off, SC_CHUNK)])

    return gather(table, idx)


def _sc_scatter(rows, dests, n_out, tok0):
    t = dests[0].shape[0]
    per_w = t // SC_WORKERS
    assert per_w * SC_WORKERS == t and per_w % SC_CHUNK == 0
    n_chunks = per_w // SC_CHUNK
    row_shape = rows.shape[1:]
    nk = len(dests)
    mesh = plsc.VectorSubcoreMesh(core_axis_name="c", subcore_axis_name="s")

    @functools.partial(
        pl.kernel, mesh=mesh,
        out_type=jax.ShapeDtypeStruct((n_out,) + row_shape, rows.dtype),
        scratch_types=[pltpu.VMEM((SC_CHUNK,), jnp.int32)] * nk
        + [pltpu.VMEM((SC_CHUNK,) + row_shape, rows.dtype), pltpu.SemaphoreType.DMA],
        name="sc_row_scatter",
    )
    def scatter(rows_hbm, *rest):
        dest_hbm = rest[:nk]
        out_hbm = rest[nk]
        idx_v = rest[nk + 1:2 * nk + 1]
        rows_v, sem = rest[2 * nk + 1:]
        wid = lax.axis_index("s") * 2 + lax.axis_index("c")
        base = wid * per_w

        @pl.loop(0, n_chunks)
        def _(ci):
            off = pl.multiple_of(base + ci * SC_CHUNK, SC_CHUNK)
            src = pl.multiple_of(tok0 + off, SC_CHUNK)
            pltpu.sync_copy(rows_hbm.at[pl.ds(src, SC_CHUNK)], rows_v)
            for kk in range(nk):
                pltpu.sync_copy(dest_hbm[kk].at[pl.ds(off, SC_CHUNK)], idx_v[kk])
            copies = [pltpu.async_copy(rows_v, out_hbm.at[idx_v[kk]], sem) for kk in range(nk)]
            for cp in copies:
                cp.wait()

    return scatter(rows, *dests)


def _expert_kernel(be_ref, nv_ref, ord_ref, ue_ref, nu_ref, x_ref, wgu_hbm, bgu_ref, wdn_hbm, bdn_ref, y_ref,
                   wgu_f32, wdn_f32, wgu_bf, wdn_bf, sem, *, layer):
    i = pl.program_id(0)
    used = i < nu_ref[0]
    pos = ord_ref[i]
    fresh = jnp.logical_or(i == 0, ord_ref[jnp.maximum(i - 1, 0)] != pos)

    def weight_copies(expert):
        return (pltpu.make_async_copy(wgu_hbm.at[layer, expert], wgu_f32, sem.at[0]),
                pltpu.make_async_copy(wdn_hbm.at[layer, expert], wdn_f32, sem.at[1]))

    @pl.when(i == 0)
    def _():
        for cp in weight_copies(ue_ref[0]):
            cp.start()

    @pl.when(jnp.logical_and(used, fresh))
    def _():
        for cp in weight_copies(ue_ref[pos]):
            cp.wait()
        wgu_bf[...] = wgu_f32[...].astype(BF16)
        wdn_bf[...] = wdn_f32[...].astype(BF16)

        @pl.when(pos + 1 < nu_ref[1])
        def _():
            for cp in weight_copies(ue_ref[pos + 1]):
                cp.start()

    def ffn(rows):
        x = jnp.concatenate(_unpack_pairs(x_ref[0:rows, :]), axis=1)
        z = jnp.dot(x, wgu_bf[...], preferred_element_type=F32) + bgu_ref[...]
        ff = z.shape[-1] // 2
        gate = jnp.minimum(z[:, :ff], SWIGLU_LIMIT)
        lin = jnp.clip(z[:, ff:], -SWIGLU_LIMIT, SWIGLU_LIMIT)
        act = gate * _sigmoid(SWIGLU_ALPHA * gate) * (lin + 1.0)
        y = _bdot(act, wdn_bf[...]) + bdn_ref[...]
        y_ref[0:rows, :] = _pack_pairs(y)

    quarter = x_ref.shape[0] // MOE_TAIL_PARTS
    parts = (nv_ref[i] + quarter - 1) // quarter
    for np_ in range(1, MOE_TAIL_PARTS + 1):
        @pl.when(jnp.logical_and(used, parts == np_))
        def _(np_=np_):
            ffn(np_ * quarter)


def _expert_call(layer, block_e, block_valid, block_pos, used_experts, n_used, xs, w_gu, b_gu, w_dn, b_dn):
    n_rows, dh = xs.shape
    depth, e, d, ff2 = w_gu.shape
    ff = ff2 // 2
    nb = n_rows // MOE_ROWS
    row_map = lambda i, be, nv, po, ue, nu: (jnp.minimum(i, nu[0] - 1), 0)
    b_map = lambda i, be, nv, po, ue, nu: (layer, be[i], 0, 0)
    return pl.pallas_call(
        functools.partial(_expert_kernel, layer=layer),
        grid_spec=pltpu.PrefetchScalarGridSpec(
            num_scalar_prefetch=5,
            grid=(nb,),
            in_specs=[
                pl.BlockSpec((MOE_ROWS, dh), row_map),
                pl.BlockSpec(memory_space=pl.ANY),
                pl.BlockSpec((None, None, 1, ff2), b_map),
                pl.BlockSpec(memory_space=pl.ANY),
                pl.BlockSpec((None, None, 1, d), b_map),
            ],
            out_specs=pl.BlockSpec((MOE_ROWS, dh), row_map),
            scratch_shapes=[pltpu.VMEM((d, ff2), F32), pltpu.VMEM((ff, d), F32),
                            pltpu.VMEM((d, ff2), BF16), pltpu.VMEM((ff, d), BF16),
                            pltpu.SemaphoreType.DMA((2,))],
        ),
        out_shape=jax.ShapeDtypeStruct((n_rows, dh), jnp.uint32),
        compiler_params=pltpu.CompilerParams(dimension_semantics=("arbitrary",), vmem_limit_bytes=VMEM_LIMIT),
        name="moe_experts",
    )(block_e, block_valid, block_pos, used_experts, n_used, xs, w_gu, b_gu.reshape(depth, e, 1, ff2), w_dn,
      b_dn.reshape(depth, e, 1, d))


def _combine_kernel(x_ref, yg_ref, te_ref, mod_ref, *rest):
    o_ref = rest[-1]
    te = te_ref[...]
    acc_lo = None
    for kk in range(TOP_K):
        lo, hi = _unpack_pairs(yg_ref[kk])
        gate = te[:, TOP_K + kk:TOP_K + kk + 1]
        acc_lo = gate * lo.astype(F32) if acc_lo is None else acc_lo + gate * lo.astype(F32)
        acc_hi = gate * hi.astype(F32) if kk == 0 else acc_hi + gate * hi.astype(F32)
    acc = jnp.concatenate([acc_lo, acc_hi], axis=1)
    o_ref[...] = x_ref[...] + mod_ref[5:6, :] * acc


def _combine_call(x, yg, te, mod, prev, b0, nb):
    bsz, seq, d = x.shape
    tm = min(ROW_TILE, seq)
    nt = seq // tm
    in_specs = [
        pl.BlockSpec((None, tm, d), lambda b, s: (b0 + b, s, 0)),
        pl.BlockSpec((TOP_K, tm, d // 2), lambda b, s: (0, b * nt + s, 0)),
        pl.BlockSpec((tm, 2 * TOP_K), lambda b, s: ((b0 + b) * nt + s, 0)),
        pl.BlockSpec((None, 6, d), lambda b, s: (b0 + b, 0, 0)),
    ]
    args = [x, yg, te, mod]
    aliases = {}
    if prev is not None:
        in_specs.append(pl.BlockSpec(memory_space=pl.ANY))
        args.append(prev)
        aliases = {len(args) - 1: 0}
    return pl.pallas_call(
        _combine_kernel,
        grid=(nb, nt),
        in_specs=in_specs,
        out_specs=pl.BlockSpec((None, tm, d), lambda b, s: (b0 + b, s, 0)),
        out_shape=jax.ShapeDtypeStruct((bsz, seq, d), F32),
        input_output_aliases=aliases,
        compiler_params=pltpu.CompilerParams(dimension_semantics=("parallel", "parallel"),
                                             vmem_limit_bytes=VMEM_LIMIT),
        name="moe_combine",
    )(*args)


def _moe(layer, x_new, h2, te, tile_cnt, mod, w_gu, b_gu, w_dn, b_dn):
    bsz, seq, _ = x_new.shape
    t, dh = h2.shape
    tiles = tile_cnt.shape[0]
    splits = MOE_SPLITS if bsz % MOE_SPLITS == 0 else 1
    gt = t // splits
    gtiles = tiles // splits
    gb = bsz // splits
    n_rows = -(-(gt * TOP_K + N_EXPERTS * (MOE_ROWS - 1)) // MOE_ROWS) * MOE_ROWS
    nb = n_rows // MOE_ROWS
    first_row = jnp.arange(nb, dtype=jnp.int32) * MOE_ROWS
    upto = jnp.arange(LANES)[:, None] <= jnp.arange(LANES)[None, :]
    all_cnt = tile_cnt[:, 0, :].astype(jnp.int32)
    out = None
    for gi in range(splits):
        g_cnt = all_cnt[gi * gtiles:(gi + 1) * gtiles]
        counts = jnp.sum(g_cnt, axis=0)
        padded = (counts + MOE_ROWS - 1) // MOE_ROWS * MOE_ROWS
        pad_end = jnp.sum(jnp.where(upto, padded[:, None], 0), axis=0)
        pad_start = pad_end - padded
        tile_base = pad_start[None, :] + jnp.cumsum(g_cnt, axis=0) - g_cnt
        dest = _dest_call(te, tile_base.astype(F32)[:, None, :], gi * gt, gt)
        dests = [dest[kk] for kk in range(TOP_K)]
        block_e = jnp.minimum(jnp.sum(pad_end[None, :N_EXPERTS] <= first_row[:, None], axis=1),
                              N_EXPERTS - 1).astype(jnp.int32)
        valid_end = (pad_start + counts)[block_e]
        block_valid = jnp.clip(valid_end - first_row, 0, MOE_ROWS).astype(jnp.int32)
        owns = counts[:N_EXPERTS] > 0
        expert_pos = jnp.cumsum(owns.astype(jnp.int32)) - 1
        slot = jnp.arange(N_EXPERTS, dtype=jnp.int32)
        used_experts = jnp.sum(jnp.where(owns[None, :] & (expert_pos[None, :] == slot[:, None]),
                                         slot[None, :], 0), axis=1).astype(jnp.int32)
        block_pos = expert_pos[block_e].astype(jnp.int32)
        n_used = jnp.stack([pad_end[N_EXPERTS - 1] // MOE_ROWS, jnp.sum(owns)]).astype(jnp.int32)
        xs = _sc_scatter(h2, dests, n_rows, gi * gt)
        y = _expert_call(layer, block_e, block_valid, block_pos, used_experts, n_used, xs, w_gu, b_gu, w_dn, b_dn)
        yg = _sc_gather(y, dest[:TOP_K].reshape(-1)).reshape(TOP_K, gt, dh)
        out = _combine_call(x_new, yg, te, mod, out, gi * gb, gb)
    return out


def kernel(x, c, positions, ada_w, ada_b, norm_mix_g, norm_ffn_g, router_w, router_b, moe_w_gu, moe_b_gu,
           moe_w_dn, moe_b_dn, even_w_in, mla_q_norm_g, mla_w_uq, mla_kv_norm_g, mla_w_ukv, mla_q_head_g,
           mla_k_head_g, s5_a_re, s5_a_im, s5_log_dt, s5_b_re, s5_b_im, s5_c_re, s5_c_im, s5_d, s5_glu_w,
           s5_glu_b, even_w_out, odd_w_in, pool_w, pool_scale, sgu_norm_g, sgu_w, sgu_b, odd_w_out):
    bsz, seq, d = x.shape
    depth = ada_w.shape[0]
    mods = _ada_call(c, ada_w, ada_b).reshape(depth, bsz, 6, d)
    posf = positions.astype(F32).reshape(bsz, seq, 1)
    for layer in range(depth):
        mod = mods[layer]
        i = layer // 2
        g_mix = norm_mix_g[layer].reshape(1, d)
        g_ffn = norm_ffn_g[layer].reshape(1, d)
        rw, rb = _router_pad(router_w[layer], router_b[layer])
        if layer % 2 == 0:
            prep = _prep_even(even_w_in[i], mla_q_norm_g[i], mla_w_uq[i], mla_kv_norm_g[i], mla_w_ukv[i],
                              mla_q_head_g[i], mla_k_head_g[i])
            q, k, v, u_t = _even_in_call(x, mod, posf, g_mix, prep)
            attn = _attn_call(q, k, v)
            disc = _s5_disc_call(s5_a_re[i], s5_a_im[i], s5_log_dt[i], s5_b_re[i], s5_b_im[i])
            ssm_t = _s5_call(u_t.reshape(seq, bsz, d // 2), disc, s5_c_re[i], s5_c_im[i], s5_d[i],
                             s5_glu_w[i], s5_glu_b[i])
            x_new, h2, te, tile_cnt = _mix_out_call(x, attn, ssm_t.reshape(seq, bsz * (d // 2)), mod,
                                                    even_w_out[i].astype(BF16), g_ffn, rw, rb)
        else:
            x_new, h2, te, tile_cnt = _odd_call(x, mod, g_mix, odd_w_in[i], pool_w[i], pool_scale[i],
                                                sgu_norm_g[i], sgu_w[i], sgu_b[i], odd_w_out[i], g_ffn, rw, rb)
        x = _moe(layer, x_new, h2, te, tile_cnt, mod, moe_w_gu, moe_b_gu, moe_w_dn, moe_b_dn)
    return x
```

```python
import functools
import math

import jax
import jax.numpy as jnp
from jax import lax
from jax.experimental import pallas as pl
from jax.experimental.pallas import tpu as pltpu
from jax.experimental.pallas import tpu_sc as plsc

F32 = jnp.float32
BF16 = jnp.bfloat16
HIGHEST = lax.Precision.HIGHEST

NORM_EPS = 1e-6
MLA_HEADS = 8
QK_NOPE_DIM = 64
QK_ROPE_DIM = 32
QK_HEAD_DIM = QK_NOPE_DIM + QK_ROPE_DIM
V_HEAD_DIM = 64
Q_LORA_RANK = 256
KV_LORA_RANK = 128
ROPE_THETA = 10000.0
S5_GROUP = 16
S5_STATE = 64
POOL_WINDOWS = (2, 4, 8, 16)
SGU_HEADS = 4
SGU_CHUNK = 128
N_EXPERTS = 32
TOP_K = 4
SWIGLU_ALPHA = 1.702
SWIGLU_LIMIT = 7.0

LANES = 128
SUBLANES = 8
HEAD_SLAB = LANES
POOL_HALO = 16
ROW_TILE = 512
ATTN_TILE = 512
ATTN_Q_BLOCKS = 2
ATTN_ROWS = 32
ATTN_HEADS = 4
S5_STEPS = 64
MOE_ROWS = 1024
MOE_TAIL_PARTS = 4
MOE_SPLITS = 2
SC_WORKERS = 32
SC_CHUNK = 64
VMEM_LIMIT = 56 * 1024 * 1024
NEG_BIG = -1e30


def _sigmoid(v):
    return 1.0 / (1.0 + jnp.exp(-v))


def _gelu(v):
    return 0.5 * v * (1.0 + jnp.tanh(math.sqrt(2.0 / math.pi) * (v + 0.044715 * (v * v * v))))


def _rms(v, width):
    return lax.rsqrt(jnp.sum(v * v, axis=-1, keepdims=True) * (1.0 / width) + NORM_EPS)


def _mod_norm(x, g, sc, sh):
    return x * _rms(x, x.shape[-1]) * g * (1.0 + sc) + sh


def _bdot(a, b):
    return jnp.dot(a.astype(BF16), b, preferred_element_type=F32)


def _pack_pairs(v):
    w = v.shape[-1] // 2
    bits = pltpu.bitcast(v.astype(BF16).astype(F32), jnp.uint32)
    return (bits[:, :w] >> 16) | bits[:, w:]


def _unpack_pairs(p):
    lo = pltpu.bitcast(p << 16, F32)
    hi = pltpu.bitcast(p & jnp.uint32(0xFFFF0000), F32)
    return lo.astype(BF16), hi.astype(BF16)


def _ada_kernel(c_ref, w_ref, b_ref, o_ref):
    c = c_ref[...]
    act = c * _sigmoid(c)
    o_ref[...] = jnp.dot(act, w_ref[...], precision=HIGHEST, preferred_element_type=F32) + b_ref[...]


def _ada_call(c, ada_w, ada_b):
    depth, d, n = ada_w.shape
    bsz = c.shape[0]
    tn = 1536
    return pl.pallas_call(
        _ada_kernel,
        grid=(depth, n // tn),
        in_specs=[
            pl.BlockSpec((bsz, d), lambda l, j: (0, 0)),
            pl.BlockSpec((None, d, tn), lambda l, j: (l, 0, j)),
            pl.BlockSpec((None, 1, tn), lambda l, j: (l, 0, j)),
        ],
        out_specs=pl.BlockSpec((None, bsz, tn), lambda l, j: (l, 0, j)),
        out_shape=jax.ShapeDtypeStruct((depth, bsz, n), F32),
        compiler_params=pltpu.CompilerParams(dimension_semantics=("parallel", "parallel"),
                                             vmem_limit_bytes=VMEM_LIMIT),
        name="ada_mod",
    )(c, ada_w, ada_b.reshape(depth, 1, n))


_C_Q = 0
_C_KV = Q_LORA_RANK
_C_PE = _C_KV + KV_LORA_RANK
_C_PESW = _C_PE + HEAD_SLAB
_C_U = _C_PESW + HEAD_SLAB


def _even_in_kernel(x_ref, mod_ref, pos_ref, g_ref, win_ref, gq_ref, wq_ref, gkv_ref, wk_ref, wv_ref,
                    tab_ref, q_ref, k_ref, v_ref, u_ref):
    x = x_ref[...]
    h = _mod_norm(x, g_ref[...], mod_ref[1:2, :], mod_ref[0:1, :])
    z = _bdot(h, win_ref[...])
    q_c = z[:, _C_Q:_C_KV]
    kv_c = z[:, _C_KV:_C_PE]
    kpe = z[:, _C_PE:_C_PESW]
    kpe_sw = z[:, _C_PESW:_C_U]
    u_ref[...] = z[:, _C_U:]

    ang = pos_ref[...] * tab_ref[0:1, :]
    cs = jnp.cos(ang)
    sn = jnp.sin(ang)
    gcq = cs * tab_ref[1:2, :]
    gsq = sn * tab_ref[2:3, :]
    gck = cs * tab_ref[3:4, :]
    gsk = sn * tab_ref[4:5, :]

    qn = q_c * _rms(q_c, Q_LORA_RANK) * gq_ref[...]
    qq = _bdot(qn, wq_ref[...])
    kvn = kv_c * _rms(kv_c, KV_LORA_RANK) * gkv_ref[...]
    kk = _bdot(kvn, wk_ref[...])
    v_ref[...] = _bdot(kvn, wv_ref[...]).astype(v_ref.dtype)

    pe_rot = kpe * gck + kpe_sw * gsk
    pe_ss = jnp.sum(kpe * kpe, axis=-1, keepdims=True)
    hw = MLA_HEADS * HEAD_SLAB
    for hd in range(MLA_HEADS):
        lo = hd * HEAD_SLAB
        qr = qq[:, lo:lo + HEAD_SLAB]
        qs = qq[:, hw + lo:hw + lo + HEAD_SLAB]
        rq = _rms(qr, QK_HEAD_DIM)
        q_ref[hd] = (rq * (qr * gcq + qs * gsq)).astype(q_ref.dtype)
        kr = kk[:, lo:lo + HEAD_SLAB]
        rk = lax.rsqrt((jnp.sum(kr * kr, axis=-1, keepdims=True) + pe_ss) * (1.0 / QK_HEAD_DIM) + NORM_EPS)
        k_ref[hd] = (rk * (kr * gck + pe_rot)).astype(k_ref.dtype)


def _even_in_call(x, mod, posf, g, prep):
    bsz, seq, d = x.shape
    tm = min(ROW_TILE, seq)
    hw = MLA_HEADS * HEAD_SLAB
    full = lambda a: pl.BlockSpec(a.shape, lambda b, s: (0,) * a.ndim)
    return pl.pallas_call(
        _even_in_kernel,
        grid=(bsz, seq // tm),
        in_specs=[
            pl.BlockSpec((None, tm, d), lambda b, s: (b, s, 0)),
            pl.BlockSpec((None, 6, d), lambda b, s: (b, 0, 0)),
            pl.BlockSpec((None, tm, 1), lambda b, s: (b, s, 0)),
            full(g), full(prep["w_in"]), full(prep["gq"]), full(prep["wq"]), full(prep["gkv"]),
            full(prep["wk"]), full(prep["wv"]), full(prep["tab"]),
        ],
        out_specs=[
            pl.BlockSpec((None, MLA_HEADS, tm, HEAD_SLAB), lambda b, s: (b, 0, s, 0)),
            pl.BlockSpec((None, MLA_HEADS, tm, HEAD_SLAB), lambda b, s: (b, 0, s, 0)),
            pl.BlockSpec((None, tm, MLA_HEADS * V_HEAD_DIM), lambda b, s: (b, s, 0)),
            pl.BlockSpec((tm, d // 2), lambda b, s: (s, b)),
        ],
        out_shape=[
            jax.ShapeDtypeStruct((bsz, MLA_HEADS, seq, HEAD_SLAB), BF16),
            jax.ShapeDtypeStruct((bsz, MLA_HEADS, seq, HEAD_SLAB), BF16),
            jax.ShapeDtypeStruct((bsz, seq, MLA_HEADS * V_HEAD_DIM), BF16),
            jax.ShapeDtypeStruct((seq, bsz * (d // 2)), F32),
        ],
        compiler_params=pltpu.CompilerParams(dimension_semantics=("parallel", "parallel"),
                                             vmem_limit_bytes=VMEM_LIMIT),
        name="even_in",
    )(x, mod, posf, g, prep["w_in"], prep["gq"], prep["wq"], prep["gkv"], prep["wk"], prep["wv"], prep["tab"])


def _prep_even(even_w_in, q_norm_g, w_uq, kv_norm_g, w_ukv, q_head_g, k_head_g):
    d = even_w_in.shape[0]
    half = QK_ROPE_DIM // 2
    nope = QK_NOPE_DIM
    c_pe = Q_LORA_RANK + KV_LORA_RANK
    w_pe = even_w_in[:, c_pe:c_pe + QK_ROPE_DIM]
    zeros = lambda n: jnp.zeros((d, n), F32)
    pe_slab = jnp.concatenate([zeros(nope), w_pe, zeros(HEAD_SLAB - QK_HEAD_DIM)], axis=1)
    pe_sw = jnp.concatenate([zeros(nope), -w_pe[:, half:], w_pe[:, :half], zeros(HEAD_SLAB - QK_HEAD_DIM)], axis=1)
    w_in = jnp.concatenate([even_w_in[:, :c_pe], pe_slab, pe_sw, even_w_in[:, c_pe + QK_ROPE_DIM:]], axis=1)

    r = w_uq.shape[0]
    padq = jnp.zeros((r, MLA_HEADS, HEAD_SLAB - QK_HEAD_DIM), F32)
    wq_plain = jnp.concatenate([w_uq, padq], axis=2).reshape(r, MLA_HEADS * HEAD_SLAB)
    wq_sw = jnp.concatenate([jnp.zeros((r, MLA_HEADS, nope), F32), -w_uq[:, :, nope + half:],
                             w_uq[:, :, nope:nope + half], padq], axis=2).reshape(r, MLA_HEADS * HEAD_SLAB)
    wq = jnp.concatenate([wq_plain, wq_sw], axis=1)

    rk = w_ukv.shape[0]
    wk = jnp.concatenate([w_ukv[:, :, :nope], jnp.zeros((rk, MLA_HEADS, HEAD_SLAB - nope), F32)],
                         axis=2).reshape(rk, MLA_HEADS * HEAD_SLAB)
    wv = w_ukv[:, :, nope:].reshape(rk, MLA_HEADS * V_HEAD_DIM)

    inv_freq = 1.0 / (ROPE_THETA ** (jnp.arange(half, dtype=F32) / half))
    pad_tail = jnp.zeros((HEAD_SLAB - QK_HEAD_DIM,), F32)
    freq_row = jnp.concatenate([jnp.zeros((nope,), F32), inv_freq, inv_freq, pad_tail])

    def gain_rows(gv, scale):
        plain = jnp.concatenate([gv, pad_tail]) * scale
        swapped = jnp.concatenate([jnp.zeros((nope,), F32), gv[nope + half:], gv[nope:nope + half], pad_tail]) * scale
        return plain, swapped

    gq_plain, gq_sw = gain_rows(q_head_g, QK_HEAD_DIM ** -0.5 * math.log2(math.e))
    gk_plain, gk_sw = gain_rows(k_head_g, 1.0)
    tab = jnp.stack([freq_row, gq_plain, gq_sw, gk_plain, gk_sw, freq_row * 0, freq_row * 0, freq_row * 0])
    return {
        "w_in": w_in.astype(BF16), "gq": q_norm_g.reshape(1, -1), "wq": wq.astype(BF16),
        "gkv": kv_norm_g.reshape(1, -1), "wk": wk.astype(BF16), "wv": wv.astype(BF16), "tab": tab,
    }


def _attn_kernel(qi_ref, kj_ref, q_ref, k_ref, v_ref, o_ref, m_sc, a_sc, acc_sc, s_sc, p_sc, *, tq, tk):
    step = pl.program_id(2)
    i = qi_ref[step]
    j = kj_ref[step]
    sum_lane = (V_HEAD_DIM, 0)
    subs = tq // tk

    @pl.when(j == 0)
    def _():
        m_sc[...] = jnp.full(m_sc.shape, -jnp.inf, F32)
        acc_sc[...] = jnp.zeros(acc_sc.shape, F32)

    def sweep(diag_sub):
        lo = 0 if diag_sub is None else diag_sub * tk
        live = pl.ds(lo, tq - lo)
        lane = lax.broadcasted_iota(jnp.int32, (1, LANES), 1)
        for hh in range(ATTN_HEADS):
            s_sc[hh, live, :] = lax.dot_general(q_ref[hh, live, :], k_ref[hh], (((1,), (1,)), ((), ())),
                                                preferred_element_type=F32)
        for hh in range(ATTN_HEADS):
            v = v_ref[:, (hh // 2) * LANES:(hh // 2 + 1) * LANES]
            for r0 in range(lo, tq, ATTN_ROWS):
                rows = pl.ds(r0, ATTN_ROWS)
                s = s_sc[hh, rows, :]
                if diag_sub is not None and r0 < lo + tk:
                    row = (r0 - lo) + lax.broadcasted_iota(jnp.int32, (ATTN_ROWS, tk), 0)
                    col = lax.broadcasted_iota(jnp.int32, (ATTN_ROWS, tk), 1)
                    s = jnp.where(col <= row, s, -jnp.inf)
                m_prev = m_sc[hh, rows, :]
                m_new = jnp.maximum(m_prev, jnp.max(s, axis=-1, keepdims=True))
                a_sc[hh, rows, :] = jnp.exp2(m_prev - m_new)
                m_sc[hh, rows, :] = m_new
                shifted = s - jnp.concatenate([m_new] * (tk // LANES), axis=1)
                p_sc[hh, rows, :] = jnp.exp2(shifted.astype(BF16))
            own = (lane < V_HEAD_DIM) == (hh % 2 == 0)
            ones = jnp.where(lane == sum_lane[hh % 2], 1.0, 0.0).astype(v.dtype)
            vh = jnp.where(own, v, jnp.broadcast_to(ones, v.shape))
            acc_sc[hh, live, :] = (acc_sc[hh, live, :] * a_sc[hh, live, :]
                                   + jnp.dot(p_sc[hh, live, :], vh, preferred_element_type=F32))

    diag = j - subs * i

    @pl.when(diag < 0)
    def _():
        sweep(None)

    for ds_ in range(subs):
        @pl.when(diag == ds_)
        def _(ds_=ds_):
            sweep(ds_)

    @pl.when(diag == subs - 1)
    def _():
        lane = lax.broadcasted_iota(jnp.int32, (1, LANES), 1)
        for pp in range(ATTN_HEADS // 2):
            acc0 = acc_sc[2 * pp]
            acc1 = acc_sc[2 * pp + 1]
            l0 = acc0[:, sum_lane[0]:sum_lane[0] + 1]
            l1 = acc1[:, sum_lane[1]:sum_lane[1] + 1]
            o_ref[:, pp * LANES:(pp + 1) * LANES] = jnp.where(lane < V_HEAD_DIM, acc0 / l0,
                                                              acc1 / l1).astype(o_ref.dtype)


def _attn_call(q, k, v):
    bsz, nh, seq, _ = q.shape
    tk = min(ATTN_TILE, seq)
    tq = min(ATTN_Q_BLOCKS * tk, seq)
    subs = tq // tk
    nq = seq // tq
    pairs = [(i, j) for i in range(nq) for j in range(subs * (i + 1))]
    qi = jnp.asarray([p[0] for p in pairs], jnp.int32)
    kj = jnp.asarray([p[1] for p in pairs], jnp.int32)
    kern = functools.partial(_attn_kernel, tq=tq, tk=tk)
    hp = ATTN_HEADS
    assert nh % hp == 0
    return pl.pallas_call(
        kern,
        grid_spec=pltpu.PrefetchScalarGridSpec(
            num_scalar_prefetch=2,
            grid=(bsz, nh // hp, len(pairs)),
            in_specs=[
                pl.BlockSpec((None, hp, tq, HEAD_SLAB), lambda b, h, p, qi, kj: (b, h, qi[p], 0)),
                pl.BlockSpec((None, hp, tk, HEAD_SLAB), lambda b, h, p, qi, kj: (b, h, kj[p], 0)),
                pl.BlockSpec((None, tk, hp * V_HEAD_DIM), lambda b, h, p, qi, kj: (b, kj[p], h)),
            ],
            out_specs=pl.BlockSpec((None, tq, hp * V_HEAD_DIM), lambda b, h, p, qi, kj: (b, qi[p], h)),
            scratch_shapes=[pltpu.VMEM((hp, tq, LANES), F32), pltpu.VMEM((hp, tq, LANES), F32),
                            pltpu.VMEM((hp, tq, LANES), F32),
                            pltpu.VMEM((hp, tq, tk), F32), pltpu.VMEM((hp, tq, tk), BF16)],
        ),
        out_shape=jax.ShapeDtypeStruct((bsz, seq, nh * V_HEAD_DIM), BF16),
        compiler_params=pltpu.CompilerParams(
            dimension_semantics=("parallel", "parallel", "arbitrary"),
            vmem_limit_bytes=VMEM_LIMIT),
        name="mla_attention",
    )(qi, kj, q, k, v)


def _s5_disc_kernel(are_ref, aim_ref, ldt_ref, bre_ref, bim_ref, abre_ref, abim_ref, bbre_ref, bbim_ref):
    dt = jnp.exp(ldt_ref[...])
    lam_re = jnp.minimum(are_ref[...], -1e-4)
    lam_im = aim_ref[...]
    mag = jnp.exp(lam_re * dt)
    ab_re = mag * jnp.cos(lam_im * dt)
    ab_im = mag * jnp.sin(lam_im * dt)
    den = lam_re * lam_re + lam_im * lam_im
    num_re = ab_re - 1.0
    f_re = (num_re * lam_re + ab_im * lam_im) / den
    f_im = (ab_im * lam_re - num_re * lam_im) / den
    abre_ref[...] = ab_re
    abim_ref[...] = ab_im
    br = bre_ref[...]
    bi = bim_ref[...]
    bbre_ref[...] = f_re[:, None, :] * br - f_im[:, None, :] * bi
    bbim_ref[...] = f_re[:, None, :] * bi + f_im[:, None, :] * br


def _s5_disc_call(a_re, a_im, log_dt, b_re, b_im):
    g, p = a_re.shape
    bre_t = jnp.swapaxes(b_re, 1, 2)
    bim_t = jnp.swapaxes(b_im, 1, 2)
    return pl.pallas_call(
        _s5_disc_kernel,
        out_shape=[jax.ShapeDtypeStruct((g, p), F32), jax.ShapeDtypeStruct((g, p), F32),
                   jax.ShapeDtypeStruct(bre_t.shape, F32), jax.ShapeDtypeStruct(bre_t.shape, F32)],
        name="s5_discretize",
    )(a_re, a_im, log_dt.reshape(g, 1), bre_t, bim_t)


def _block_diag_halves(m):
    g, r, c = m.shape
    gh = g // 2
    eye = jnp.eye(gh, dtype=m.dtype)
    mh = m.reshape(2, gh, r, c)
    return (mh[:, :, :, None, :] * eye[None, :, None, :, None]).reshape(2, gh * r, gh * c)


def _s5_kernel(u_ref, bre_ref, bim_ref, are_ref, aim_ref, cre_ref, cim_ref, d_ref, gw_ref, gb_ref,
               o_ref, sre, sim, dre, dim, xbr, xbi, *, steps):
    @pl.when(pl.program_id(0) == 0)
    def _():
        sre[...] = jnp.zeros(sre.shape, F32)
        sim[...] = jnp.zeros(sim.shape, F32)

    rows = steps * SUBLANES
    w = u_ref.shape[-1]
    u = u_ref[...].reshape(rows, w)
    ub = u.astype(BF16)
    kh = w // 2
    nh = dre.shape[1] // 2
    for hf in range(2):
        dre[:, hf * nh:(hf + 1) * nh] = jnp.dot(ub[:, hf * kh:(hf + 1) * kh], bre_ref[hf], preferred_element_type=F32)
        dim[:, hf * nh:(hf + 1) * nh] = jnp.dot(ub[:, hf * kh:(hf + 1) * kh], bim_ref[hf], preferred_element_type=F32)

    xr = sre[...]
    xi = sim[...]
    for t in range(0, steps, 2):
        pair_r = []
        pair_i = []
        for r0 in (t * SUBLANES, (t + 1) * SUBLANES):
            nr = are_ref[...] * xr - aim_ref[...] * xi + dre[r0:r0 + SUBLANES, :]
            ni = are_ref[...] * xi + aim_ref[...] * xr + dim[r0:r0 + SUBLANES, :]
            xr, xi = nr, ni
            pair_r.append(nr)
            pair_i.append(ni)
        xbr[t * SUBLANES:(t + 2) * SUBLANES, :] = jnp.concatenate(pair_r, axis=0).astype(BF16)
        xbi[t * SUBLANES:(t + 2) * SUBLANES, :] = jnp.concatenate(pair_i, axis=0).astype(BF16)
    sre[...] = xr
    sim[...] = xi

    ys = []
    for hf in range(2):
        yr = jnp.dot(xbr[:, hf * nh:(hf + 1) * nh], cre_ref[hf], preferred_element_type=F32)
        yi = jnp.dot(xbi[:, hf * nh:(hf + 1) * nh], cim_ref[hf], preferred_element_type=F32)
        ys.append(yr - yi)
    y = jnp.concatenate(ys, axis=1) + d_ref[...] * u
    g = _gelu(y)
    out = g * _sigmoid(_bdot(g, gw_ref[...]) + gb_ref[...])
    o_ref[...] = out.reshape(steps, SUBLANES, w).astype(o_ref.dtype)


def _s5_call(u_t, disc, c_re, c_im, d_skip, glu_w, glu_b):
    seq, bsz, w = u_t.shape
    assert bsz == SUBLANES
    ab_re, ab_im, bb_re, bb_im = disc
    g, p = ab_re.shape
    n_state = g * p
    bre = _block_diag_halves(bb_re).astype(BF16)
    bim = _block_diag_halves(bb_im).astype(BF16)
    cre = _block_diag_halves(jnp.swapaxes(c_re, 1, 2)).astype(BF16)
    cim = _block_diag_halves(jnp.swapaxes(c_im, 1, 2)).astype(BF16)
    steps = min(S5_STEPS, seq)
    full = lambda a: pl.BlockSpec(a.shape, lambda s: (0,) * a.ndim)
    rep = lambda a: jnp.broadcast_to(a.reshape(1, n_state), (bsz, n_state))
    args = (bre, bim, rep(ab_re), rep(ab_im), cre, cim,
            d_skip.reshape(1, w), glu_w.astype(BF16), glu_b.reshape(1, w))
    return pl.pallas_call(
        functools.partial(_s5_kernel, steps=steps),
        grid=(seq // steps,),
        in_specs=[pl.BlockSpec((steps, bsz, w), lambda s: (s, 0, 0))] + [full(a) for a in args],
        out_specs=pl.BlockSpec((steps, bsz, w), lambda s: (s, 0, 0)),
        out_shape=jax.ShapeDtypeStruct((seq, bsz, w), BF16),
        scratch_shapes=[pltpu.VMEM((bsz, n_state), F32), pltpu.VMEM((bsz, n_state), F32),
                        pltpu.VMEM((steps * bsz, n_state), F32), pltpu.VMEM((steps * bsz, n_state), F32),
                        pltpu.VMEM((steps * bsz, n_state), BF16), pltpu.VMEM((steps * bsz, n_state), BF16)],
        compiler_params=pltpu.CompilerParams(dimension_semantics=("arbitrary",), vmem_limit_bytes=VMEM_LIMIT),
        name="s5_scan",
    )(u_t, *args)


def _router_tail(x_new, mod_ref, gf_ref, rw_ref, rb_ref, h2_ref, te_ref, cnt_ref):
    h2 = _mod_norm(x_new, gf_ref[...], mod_ref[4:5, :], mod_ref[3:4, :])
    h2_ref[...] = _pack_pairs(h2)
    h_hi = h2.astype(BF16)
    h_lo = (h2 - h_hi.astype(F32)).astype(BF16)
    r_hi = jnp.dot(h_hi, rw_ref[...], preferred_element_type=F32)
    r_lo = jnp.dot(h_lo, rw_ref[...], preferred_element_type=F32)
    logits = r_hi[:, :LANES] + r_hi[:, LANES:] + r_lo[:, :LANES] + rb_ref[...]
    lane = lax.broadcasted_iota(jnp.int32, logits.shape, 1).astype(F32)
    vals = []
    idxs = []
    work = logits
    for _ in range(TOP_K):
        m = jnp.max(work, axis=-1, keepdims=True)
        idx = jnp.min(jnp.where(work == m, lane, float(LANES)), axis=-1, keepdims=True)
        vals.append(m)
        idxs.append(idx)
        work = jnp.where(lane == idx, NEG_BIG * 2.0, work)
    exps = [jnp.exp(vv - vals[0]) for vv in vals]
    tot = exps[0] + exps[1] + exps[2] + exps[3]
    te = jnp.zeros(logits.shape, F32)
    picked = jnp.zeros(logits.shape, F32)
    for kk in range(TOP_K):
        te = jnp.where(lane == float(kk), idxs[kk], te)
        te = jnp.where(lane == float(TOP_K + kk), exps[kk] / tot, te)
        picked = picked + jnp.where(lane == idxs[kk], 1.0, 0.0)
    te_ref[...] = te[:, :2 * TOP_K]
    cnt_ref[...] = jnp.sum(picked, axis=0, keepdims=True)


def _mix_out_kernel(x_ref, a_ref, s_ref, mod_ref, wo_ref, gf_ref, rw_ref, rb_ref, xo_ref, h2_ref, te_ref,
                    cnt_ref):
    ka = a_ref.shape[-1]
    mix = jnp.dot(a_ref[...], wo_ref[:ka, :], preferred_element_type=F32)
    mix = mix + jnp.dot(s_ref[...], wo_ref[ka:, :], preferred_element_type=F32)
    x_new = x_ref[...] + mod_ref[2:3, :] * mix
    xo_ref[...] = x_new
    _router_tail(x_new, mod_ref, gf_ref, rw_ref, rb_ref, h2_ref, te_ref, cnt_ref)


def _tail_out_specs(bsz, seq, tm, d):
    nt = seq // tm
    specs = [
        pl.BlockSpec((None, tm, d), lambda b, s: (b, s, 0)),
        pl.BlockSpec((tm, d // 2), lambda b, s: (b * nt + s, 0)),
        pl.BlockSpec((tm, 2 * TOP_K), lambda b, s: (b * nt + s, 0)),
        pl.BlockSpec((None, 1, LANES), lambda b, s: (b * nt + s, 0, 0)),
    ]
    shapes = [
        jax.ShapeDtypeStruct((bsz, seq, d), F32),
        jax.ShapeDtypeStruct((bsz * seq, d // 2), jnp.uint32),
        jax.ShapeDtypeStruct((bsz * seq, 2 * TOP_K), F32),
        jax.ShapeDtypeStruct((bsz * nt, 1, LANES), F32),
    ]
    return specs, shapes


def _router_pad(router_w, router_b):
    d, e = router_w.shape
    rw = jnp.concatenate([router_w, jnp.zeros((d, LANES - e), F32)], axis=1)
    rw_hi = rw.astype(BF16)
    rw_lo = (rw - rw_hi.astype(F32)).astype(BF16)
    rb = jnp.concatenate([router_b, jnp.full((LANES - e,), NEG_BIG, F32)]).reshape(1, LANES)
    return jnp.concatenate([rw_hi, rw_lo], axis=1), rb


def _mix_out_call(x, attn, ssm_t, mod, w_out, gf, rw, rb):
    bsz, seq, d = x.shape
    tm = min(ROW_TILE, seq)
    ka = attn.shape[-1]
    ks = ssm_t.shape[-1] // bsz
    full = lambda a: pl.BlockSpec(a.shape, lambda b, s: (0,) * a.ndim)
    out_specs, out_shape = _tail_out_specs(bsz, seq, tm, d)
    return pl.pallas_call(
        _mix_out_kernel,
        grid=(bsz, seq // tm),
        in_specs=[
            pl.BlockSpec((None, tm, d), lambda b, s: (b, s, 0)),
            pl.BlockSpec((None, tm, ka), lambda b, s: (b, s, 0)),
            pl.BlockSpec((tm, ks), lambda b, s: (s, b)),
            pl.BlockSpec((None, 6, d), lambda b, s: (b, 0, 0)),
            full(w_out), full(gf), full(rw), full(rb),
        ],
        out_specs=out_specs,
        out_shape=out_shape,
        compiler_params=pltpu.CompilerParams(dimension_semantics=("parallel", "parallel"),
                                             vmem_limit_bytes=VMEM_LIMIT),
        name="even_out",
    )(x, attn, ssm_t, mod, w_out, gf, rw, rb)


def _moe_sum(yg_ref, te_ref):
    te = te_ref[...]
    acc_lo = acc_hi = None
    for kk in range(TOP_K):
        lo, hi = _unpack_pairs(yg_ref[kk])
        gate = te[:, TOP_K + kk:TOP_K + kk + 1]
        acc_lo = gate * lo.astype(F32) if kk == 0 else acc_lo + gate * lo.astype(F32)
        acc_hi = gate * hi.astype(F32) if kk == 0 else acc_hi + gate * hi.astype(F32)
    return jnp.concatenate([acc_lo, acc_hi], axis=1)


def _odd_kernel(x_ref, yg_ref, tep_ref, modp_ref, mod_ref, g_ref, win_ref, icnt_ref, wp_ref, ps_ref, gv_ref,
                wsp_ref, bsp_ref, wo_ref, gf_ref, rw_ref, rb_ref, xo_ref, h2_ref, te_ref, cnt_ref, ext_sc):
    tm = x_ref.shape[0]
    pw = wp_ref.shape[-1]
    width = pw * len(POOL_WINDOWS)

    @pl.when(pl.program_id(1) == 0)
    def _():
        ext_sc[0:POOL_HALO, :] = jnp.zeros((POOL_HALO, width), F32)

    x = x_ref[...] + modp_ref[5:6, :] * _moe_sum(yg_ref, tep_ref)
    h = _mod_norm(x, g_ref[...], mod_ref[1:2, :], mod_ref[0:1, :])
    z = _bdot(h, win_ref[...])
    up = z[:, :width]
    ext_sc[POOL_HALO:POOL_HALO + tm, :] = up

    pooled = []
    for gi, win in enumerate(POOL_WINDOWS):
        cols = slice(gi * pw, (gi + 1) * pw)
        acc = up[:, cols]
        for lag in range(1, win):
            acc = acc + ext_sc[POOL_HALO - lag:POOL_HALO - lag + tm, cols]
        pg = acc * icnt_ref[:, gi:gi + 1] - up[:, cols]
        pooled.append(_bdot(pg, wp_ref[gi]) * ps_ref[:, cols])
    ext_sc[0:POOL_HALO, :] = ext_sc[tm:tm + POOL_HALO, :]
    pooled = jnp.concatenate(pooled, axis=1)

    ug = _gelu(z[:, width:2 * width])
    vg = _gelu(z[:, 2 * width:])
    vn = (vg * _rms(vg, width) * gv_ref[...]).astype(BF16)
    hd = width // SGU_HEADS
    chunks = []
    for ci in range(tm // SGU_CHUNK):
        heads = []
        for hh in range(SGU_HEADS):
            blk = vn[ci * SGU_CHUNK:(ci + 1) * SGU_CHUNK, hh * hd:(hh + 1) * hd]
            heads.append(jnp.dot(wsp_ref[hh], blk, preferred_element_type=F32) + bsp_ref[hh])
        chunks.append(jnp.concatenate(heads, axis=1))
    gated = ug * jnp.concatenate(chunks, axis=0)

    mix = _bdot(pooled, wo_ref[:width, :]) + _bdot(gated, wo_ref[width:, :])
    x_new = x + mod_ref[2:3, :] * mix
    xo_ref[...] = x_new
    _router_tail(x_new, mod_ref, gf_ref, rw_ref, rb_ref, h2_ref, te_ref, cnt_ref)


def _odd_call(grp, mod_prev, mod, nb, g, w_in, pool_w, pool_scale, sgu_norm_g, sgu_w, sgu_b, w_out, gf, rw, rb):
    x = grp["x"]
    _, seq, d = x.shape
    tm = min(ROW_TILE, seq)
    nt = seq // tm
    xb0, tok0, mb0 = grp["x_b0"], grp["tok0"], grp["mod_b0"]
    tile0 = tok0 // tm
    yg = grp["yg"]
    width = pool_scale.shape[0]
    hd = width // SGU_HEADS
    t = jnp.arange(seq, dtype=jnp.int32)
    icnt = jnp.stack([1.0 / jnp.minimum(t + 1, wn).astype(F32) for wn in POOL_WINDOWS], axis=1)
    wsp = jnp.tril(sgu_w).astype(BF16)
    bsp = jnp.broadcast_to(sgu_b[:, :, None], (SGU_HEADS, SGU_CHUNK, hd))
    args = (g, w_in.astype(BF16), icnt, pool_w.astype(BF16), pool_scale.reshape(1, width),
            sgu_norm_g.reshape(1, width), wsp, bsp, w_out.astype(BF16), gf, rw, rb)
    full = lambda a: pl.BlockSpec(a.shape, lambda b, s: (0,) * a.ndim)
    in_specs = [pl.BlockSpec((None, tm, d), lambda b, s: (xb0 + b, s, 0)),
                pl.BlockSpec((TOP_K, tm, yg.shape[-1]), lambda b, s: (0, b * nt + s, 0)),
                pl.BlockSpec((tm, 2 * TOP_K), lambda b, s: (tile0 + b * nt + s, 0)),
                pl.BlockSpec((None, 6, d), lambda b, s: (mb0 + b, 0, 0)),
                pl.BlockSpec((None, 6, d), lambda b, s: (mb0 + b, 0, 0))]
    for idx, a in enumerate(args):
        in_specs.append(pl.BlockSpec((tm, len(POOL_WINDOWS)), lambda b, s: (s, 0)) if idx == 2 else full(a))
    out_specs, out_shape = _tail_out_specs(nb, seq, tm, d)
    return pl.pallas_call(
        _odd_kernel,
        grid=(nb, nt),
        in_specs=in_specs,
        out_specs=out_specs,
        out_shape=out_shape,
        scratch_shapes=[pltpu.VMEM((tm + POOL_HALO, width), F32)],
        compiler_params=pltpu.CompilerParams(dimension_semantics=("parallel", "arbitrary"),
                                             vmem_limit_bytes=VMEM_LIMIT),
        name="odd_mixer",
    )(x, yg, grp["te"], mod_prev, mod, *args)


def _dest_kernel(te_ref, base_ref, dst_ref):
    tr = te_ref.shape[0]
    te = te_ref[...]
    lane = lax.broadcasted_iota(jnp.int32, (tr, LANES), 1).astype(F32)
    hots = [te[:, kk:kk + 1] == lane for kk in range(TOP_K)]
    oh = jnp.zeros((tr, LANES), F32)
    for hot in hots:
        oh = oh + jnp.where(hot, 1.0, 0.0)
    r_i = lax.broadcasted_iota(jnp.int32, (tr, tr), 0)
    c_i = lax.broadcasted_iota(jnp.int32, (tr, tr), 1)
    tri = jnp.where(c_i < r_i, 1.0, 0.0).astype(BF16)
    before = jnp.dot(tri, oh.astype(BF16), preferred_element_type=F32) + base_ref[...]
    dst = jnp.zeros((tr, LANES), F32)
    for kk, hot in enumerate(hots):
        dst = jnp.where(lane == float(kk), jnp.sum(jnp.where(hot, before, 0.0), axis=-1, keepdims=True), dst)
    dst_ref[...] = dst.T[:2 * TOP_K, :].astype(jnp.int32)


def _dest_call(te, base, tok0, n_tok):
    tiles = base.shape[0]
    tr = n_tok // tiles
    tile0 = tok0 // tr
    return pl.pallas_call(
        _dest_kernel,
        grid=(tiles,),
        in_specs=[pl.BlockSpec((tr, 2 * TOP_K), lambda i: (tile0 + i, 0)),
                  pl.BlockSpec((None, 1, LANES), lambda i: (i, 0, 0))],
        out_specs=pl.BlockSpec((2 * TOP_K, tr), lambda i: (0, i)),
        out_shape=jax.ShapeDtypeStruct((2 * TOP_K, n_tok), jnp.int32),
        compiler_params=pltpu.CompilerParams(dimension_semantics=("parallel",)),
        name="route_dest",
    )(te, base)


def _sc_gather(table, idx):
    n = idx.shape[0]
    per_w = n // SC_WORKERS
    assert per_w * SC_WORKERS == n and per_w % SC_CHUNK == 0
    n_chunks = per_w // SC_CHUNK
    row_shape = table.shape[1:]
    mesh = plsc.VectorSubcoreMesh(core_axis_name="c", subcore_axis_name="s")

    @functools.partial(
        pl.kernel, mesh=mesh,
        out_type=jax.ShapeDtypeStruct((n,) + row_shape, table.dtype),
        scratch_types=[pltpu.VMEM((SC_CHUNK,), jnp.int32), pltpu.VMEM((SC_CHUNK,) + row_shape, table.dtype),
                       pltpu.SemaphoreType.DMA],
        name="sc_row_gather",
    )
    def gather(table_hbm, idx_hbm, out_hbm, idx_v, rows_v, sem):
        wid = lax.axis_index("s") * 2 + lax.axis_index("c")
        base = wid * per_w

        @pl.loop(0, n_chunks)
        def _(ci):
            off = pl.multiple_of(base + ci * SC_CHUNK, SC_CHUNK)
            pltpu.sync_copy(idx_hbm.at[pl.ds(off, SC_CHUNK)], idx_v)
            pltpu.async_copy(table_hbm.at[idx_v], rows_v, sem).wait()
            pltpu.sync_copy(rows_v, out_hbm.at[pl.ds(off, SC_CHUNK)])

    return gather(table, idx)


def _sc_scatter(rows, dests, n_out, tok0):
    t = dests[0].shape[0]
    per_w = t // SC_WORKERS
    assert per_w * SC_WORKERS == t and per_w % SC_CHUNK == 0
    n_chunks = per_w // SC_CHUNK
    row_shape = rows.shape[1:]
    nk = len(dests)
    mesh = plsc.VectorSubcoreMesh(core_axis_name="c", subcore_axis_name="s")

    @functools.partial(
        pl.kernel, mesh=mesh,
        out_type=jax.ShapeDtypeStruct((n_out,) + row_shape, rows.dtype),
        scratch_types=[pltpu.VMEM((SC_CHUNK,), jnp.int32)] * nk
        + [pltpu.VMEM((SC_CHUNK,) + row_shape, rows.dtype), pltpu.SemaphoreType.DMA],
        name="sc_row_scatter",
    )
    def scatter(rows_hbm, *rest):
        dest_hbm = rest[:nk]
        out_hbm = rest[nk]
        idx_v = rest[nk + 1:2 * nk + 1]
        rows_v, sem = rest[2 * nk + 1:]
        wid = lax.axis_index("s") * 2 + lax.axis_index("c")
        base = wid * per_w

        @pl.loop(0, n_chunks)
        def _(ci):
            off = pl.multiple_of(base + ci * SC_CHUNK, SC_CHUNK)
            src = pl.multiple_of(tok0 + off, SC_CHUNK)
            pltpu.sync_copy(rows_hbm.at[pl.ds(src, SC_CHUNK)], rows_v)
            for kk in range(nk):
                pltpu.sync_copy(dest_hbm[kk].at[pl.ds(off, SC_CHUNK)], idx_v[kk])
            copies = [pltpu.async_copy(rows_v, out_hbm.at[idx_v[kk]], sem) for kk in range(nk)]
            for cp in copies:
                cp.wait()

    return scatter(rows, *dests)


def _expert_kernel(be_ref, nv_ref, ord_ref, ue_ref, nu_ref, x_ref, wgu_hbm, bgu_ref, wdn_hbm, bdn_ref, y_ref,
                   wgu_f32, wdn_f32, wgu_bf, wdn_bf, sem, *, layer):
    i = pl.program_id(0)
    used = i < nu_ref[0]
    pos = ord_ref[i]
    fresh = jnp.logical_or(i == 0, ord_ref[jnp.maximum(i - 1, 0)] != pos)

    def weight_copies(expert):
        return (pltpu.make_async_copy(wgu_hbm.at[layer, expert], wgu_f32, sem.at[0]),
                pltpu.make_async_copy(wdn_hbm.at[layer, expert], wdn_f32, sem.at[1]))

    @pl.when(i == 0)
    def _():
        for cp in weight_copies(ue_ref[0]):
            cp.start()

    @pl.when(jnp.logical_and(used, fresh))
    def _():
        for cp in weight_copies(ue_ref[pos]):
            cp.wait()
        wgu_bf[...] = wgu_f32[...].astype(BF16)
        wdn_bf[...] = wdn_f32[...].astype(BF16)

        @pl.when(pos + 1 < nu_ref[1])
        def _():
            for cp in weight_copies(ue_ref[pos + 1]):
                cp.start()

    def ffn(rows):
        x = jnp.concatenate(_unpack_pairs(x_ref[0:rows, :]), axis=1)
        z = jnp.dot(x, wgu_bf[...], preferred_element_type=F32) + bgu_ref[...]
        ff = z.shape[-1] // 2
        gate = jnp.minimum(z[:, :ff], SWIGLU_LIMIT)
        lin = jnp.clip(z[:, ff:], -SWIGLU_LIMIT, SWIGLU_LIMIT)
        act = gate * _sigmoid(SWIGLU_ALPHA * gate) * (lin + 1.0)
        y = _bdot(act, wdn_bf[...]) + bdn_ref[...]
        y_ref[0:rows, :] = _pack_pairs(y)

    quarter = x_ref.shape[0] // MOE_TAIL_PARTS
    parts = (nv_ref[i] + quarter - 1) // quarter
    for np_ in range(1, MOE_TAIL_PARTS + 1):
        @pl.when(jnp.logical_and(used, parts == np_))
        def _(np_=np_):
            ffn(np_ * quarter)


def _expert_call(layer, block_e, block_valid, block_pos, used_experts, n_used, xs, w_gu, b_gu, w_dn, b_dn):
    n_rows, dh = xs.shape
    depth, e, d, ff2 = w_gu.shape
    ff = ff2 // 2
    nb = n_rows // MOE_ROWS
    row_map = lambda i, be, nv, po, ue, nu: (jnp.minimum(i, nu[0] - 1), 0)
    b_map = lambda i, be, nv, po, ue, nu: (layer, be[i], 0, 0)
    return pl.pallas_call(
        functools.partial(_expert_kernel, layer=layer),
        grid_spec=pltpu.PrefetchScalarGridSpec(
            num_scalar_prefetch=5,
            grid=(nb,),
            in_specs=[
                pl.BlockSpec((MOE_ROWS, dh), row_map),
                pl.BlockSpec(memory_space=pl.ANY),
                pl.BlockSpec((None, None, 1, ff2), b_map),
                pl.BlockSpec(memory_space=pl.ANY),
                pl.BlockSpec((None, None, 1, d), b_map),
            ],
            out_specs=pl.BlockSpec((MOE_ROWS, dh), row_map),
            scratch_shapes=[pltpu.VMEM((d, ff2), F32), pltpu.VMEM((ff, d), F32),
                            pltpu.VMEM((d, ff2), BF16), pltpu.VMEM((ff, d), BF16),
                            pltpu.SemaphoreType.DMA((2,))],
        ),
        out_shape=jax.ShapeDtypeStruct((n_rows, dh), jnp.uint32),
        compiler_params=pltpu.CompilerParams(dimension_semantics=("arbitrary",), vmem_limit_bytes=VMEM_LIMIT),
        name="moe_experts",
    )(block_e, block_valid, block_pos, used_experts, n_used, xs, w_gu, b_gu.reshape(depth, e, 1, ff2), w_dn,
      b_dn.reshape(depth, e, 1, d))


def _combine_kernel(x_ref, yg_ref, te_ref, mod_ref, *rest):
    o_ref = rest[-1]
    o_ref[...] = x_ref[...] + mod_ref[5:6, :] * _moe_sum(yg_ref, te_ref)


def _combine_call(grp, mod, prev, bsz, nb):
    x = grp["x"]
    _, seq, d = x.shape
    tm = min(ROW_TILE, seq)
    nt = seq // tm
    xb0, mb0 = grp["x_b0"], grp["mod_b0"]
    tile0 = grp["tok0"] // tm
    yg = grp["yg"]
    in_specs = [
        pl.BlockSpec((None, tm, d), lambda b, s: (xb0 + b, s, 0)),
        pl.BlockSpec((TOP_K, tm, yg.shape[-1]), lambda b, s: (0, b * nt + s, 0)),
        pl.BlockSpec((tm, 2 * TOP_K), lambda b, s: (tile0 + b * nt + s, 0)),
        pl.BlockSpec((None, 6, d), lambda b, s: (mb0 + b, 0, 0)),
    ]
    args = [x, yg, grp["te"], mod]
    aliases = {}
    if prev is not None:
        in_specs.append(pl.BlockSpec(memory_space=pl.ANY))
        args.append(prev)
        aliases = {len(args) - 1: 0}
    return pl.pallas_call(
        _combine_kernel,
        grid=(nb, nt),
        in_specs=in_specs,
        out_specs=pl.BlockSpec((None, tm, d), lambda b, s: (mb0 + b, s, 0)),
        out_shape=jax.ShapeDtypeStruct((bsz, seq, d), F32),
        input_output_aliases=aliases,
        compiler_params=pltpu.CompilerParams(dimension_semantics=("parallel", "parallel"),
                                             vmem_limit_bytes=VMEM_LIMIT),
        name="moe_combine",
    )(*args)


def _moe_rows(layer, grp, gt, w_gu, b_gu, w_dn, b_dn):
    h2, te = grp["h2"], grp["te"]
    dh = h2.shape[-1]
    n_rows = -(-(gt * TOP_K + N_EXPERTS * (MOE_ROWS - 1)) // MOE_ROWS) * MOE_ROWS
    nb = n_rows // MOE_ROWS
    first_row = jnp.arange(nb, dtype=jnp.int32) * MOE_ROWS
    upto = jnp.arange(LANES)[:, None] <= jnp.arange(LANES)[None, :]
    g_cnt = grp["cnt"][:, 0, :].astype(jnp.int32)
    counts = jnp.sum(g_cnt, axis=0)
    padded = (counts + MOE_ROWS - 1) // MOE_ROWS * MOE_ROWS
    pad_end = jnp.sum(jnp.where(upto, padded[:, None], 0), axis=0)
    pad_start = pad_end - padded
    tile_base = pad_start[None, :] + jnp.cumsum(g_cnt, axis=0) - g_cnt
    dest = _dest_call(te, tile_base.astype(F32)[:, None, :], grp["tok0"], gt)
    dests = [dest[kk] for kk in range(TOP_K)]
    block_e = jnp.minimum(jnp.sum(pad_end[None, :N_EXPERTS] <= first_row[:, None], axis=1),
                          N_EXPERTS - 1).astype(jnp.int32)
    valid_end = (pad_start + counts)[block_e]
    block_valid = jnp.clip(valid_end - first_row, 0, MOE_ROWS).astype(jnp.int32)
    owns = counts[:N_EXPERTS] > 0
    expert_pos = jnp.cumsum(owns.astype(jnp.int32)) - 1
    slot = jnp.arange(N_EXPERTS, dtype=jnp.int32)
    used_experts = jnp.sum(jnp.where(owns[None, :] & (expert_pos[None, :] == slot[:, None]),
                                     slot[None, :], 0), axis=1).astype(jnp.int32)
    block_pos = expert_pos[block_e].astype(jnp.int32)
    n_used = jnp.stack([pad_end[N_EXPERTS - 1] // MOE_ROWS, jnp.sum(owns)]).astype(jnp.int32)
    xs = _sc_scatter(h2, dests, n_rows, grp["tok0"])
    y = _expert_call(layer, block_e, block_valid, block_pos, used_experts, n_used, xs, w_gu, b_gu, w_dn, b_dn)
    return _sc_gather(y, dest[:TOP_K].reshape(-1)).reshape(TOP_K, gt, dh)


def kernel(x, c, positions, ada_w, ada_b, norm_mix_g, norm_ffn_g, router_w, router_b, moe_w_gu, moe_b_gu,
           moe_w_dn, moe_b_dn, even_w_in, mla_q_norm_g, mla_w_uq, mla_kv_norm_g, mla_w_ukv, mla_q_head_g,
           mla_k_head_g, s5_a_re, s5_a_im, s5_log_dt, s5_b_re, s5_b_im, s5_c_re, s5_c_im, s5_d, s5_glu_w,
           s5_glu_b, even_w_out, odd_w_in, pool_w, pool_scale, sgu_norm_g, sgu_w, sgu_b, odd_w_out):
    bsz, seq, d = x.shape
    depth = ada_w.shape[0]
    mods = _ada_call(c, ada_w, ada_b).reshape(depth, bsz, 6, d)
    posf = positions.astype(F32).reshape(bsz, seq, 1)
    splits = MOE_SPLITS if bsz % MOE_SPLITS == 0 else 1
    gb = bsz // splits
    gt = gb * seq
    tiles_g = gt // min(ROW_TILE, seq)

    def settle(groups, mod):
        out = None
        for grp in groups:
            out = _combine_call(grp, mod, out, bsz, gb)
        return out

    pending = None
    for layer in range(depth):
        mod = mods[layer]
        i = layer // 2
        g_mix = norm_mix_g[layer].reshape(1, d)
        g_ffn = norm_ffn_g[layer].reshape(1, d)
        rw, rb = _router_pad(router_w[layer], router_b[layer])
        if layer % 2 == 0:
            if pending is not None:
                x = settle(pending, mods[layer - 1])
            prep = _prep_even(even_w_in[i], mla_q_norm_g[i], mla_w_uq[i], mla_kv_norm_g[i], mla_w_ukv[i],
                              mla_q_head_g[i], mla_k_head_g[i])
            q, k, v, u_t = _even_in_call(x, mod, posf, g_mix, prep)
            attn = _attn_call(q, k, v)
            disc = _s5_disc_call(s5_a_re[i], s5_a_im[i], s5_log_dt[i], s5_b_re[i], s5_b_im[i])
            ssm_t = _s5_call(u_t.reshape(seq, bsz, d // 2), disc, s5_c_re[i], s5_c_im[i], s5_d[i],
                             s5_glu_w[i], s5_glu_b[i])
            x_new, h2, te, tile_cnt = _mix_out_call(x, attn, ssm_t.reshape(seq, bsz * (d // 2)), mod,
                                                    even_w_out[i].astype(BF16), g_ffn, rw, rb)
            groups = [dict(x=x_new, x_b0=gi * gb, h2=h2, te=te, tok0=gi * gt, mod_b0=gi * gb,
                           cnt=tile_cnt[gi * tiles_g:(gi + 1) * tiles_g]) for gi in range(splits)]
        else:
            groups = []
            for gi, grp in enumerate(pending):
                x_new, h2, te, tile_cnt = _odd_call(grp, mods[layer - 1], mod, gb, g_mix, odd_w_in[i], pool_w[i],
                                                    pool_scale[i], sgu_norm_g[i], sgu_w[i], sgu_b[i],
                                                    odd_w_out[i], g_ffn, rw, rb)
                groups.append(dict(x=x_new, x_b0=0, h2=h2, te=te, tok0=0, mod_b0=gi * gb, cnt=tile_cnt))
        for grp in groups:
            grp["yg"] = _moe_rows(layer, grp, gt, moe_w_gu, moe_b_gu, moe_w_dn, moe_b_dn)
        pending = groups
    return settle(pending, mods[depth - 1])
```

```python
import functools
import math

import jax
import jax.numpy as jnp
from jax import lax
from jax.experimental import pallas as pl
from jax.experimental.pallas import tpu as pltpu
from jax.experimental.pallas import tpu_sc as plsc

F32 = jnp.float32
BF16 = jnp.bfloat16
HIGHEST = lax.Precision.HIGHEST

NORM_EPS = 1e-6
MLA_HEADS = 8
QK_NOPE_DIM = 64
QK_ROPE_DIM = 32
QK_HEAD_DIM = QK_NOPE_DIM + QK_ROPE_DIM
V_HEAD_DIM = 64
Q_LORA_RANK = 256
KV_LORA_RANK = 128
ROPE_THETA = 10000.0
S5_GROUP = 16
S5_STATE = 64
POOL_WINDOWS = (2, 4, 8, 16)
SGU_HEADS = 4
SGU_CHUNK = 128
N_EXPERTS = 32
TOP_K = 4
SWIGLU_ALPHA = 1.702
SWIGLU_LIMIT = 7.0

LANES = 128
SUBLANES = 8
HEAD_SLAB = LANES
POOL_HALO = 16
ROW_TILE = 512
ATTN_TILE = 512
ATTN_Q_BLOCKS = 2
ATTN_ROWS = 32
ATTN_HEADS = 4
S5_STEPS = 64
MOE_ROWS = 1024
MOE_PATHS = (128, 256, 512, 768, MOE_ROWS)
MOE_SPLITS = 2
SC_WORKERS = 32
SC_CHUNK = 64
VMEM_LIMIT = 56 * 1024 * 1024
NEG_BIG = -1e30


def _sigmoid(v):
    return 1.0 / (1.0 + jnp.exp(-v))


def _gelu(v):
    return 0.5 * v * (1.0 + jnp.tanh(math.sqrt(2.0 / math.pi) * (v + 0.044715 * (v * v * v))))


def _rms(v, width):
    return lax.rsqrt(jnp.sum(v * v, axis=-1, keepdims=True) * (1.0 / width) + NORM_EPS)


def _mod_norm(x, g, sc, sh):
    return x * _rms(x, x.shape[-1]) * (g * (1.0 + sc)) + sh


def _bdot(a, b):
    return jnp.dot(a.astype(BF16), b, preferred_element_type=F32)


def _pack_pairs(v):
    w = v.shape[-1] // 2
    bits = pltpu.bitcast(v.astype(BF16).astype(F32), jnp.uint32)
    return (bits[:, :w] >> 16) | bits[:, w:]


def _unpack_pairs(p):
    lo = pltpu.bitcast(p << 16, F32)
    hi = pltpu.bitcast(p & jnp.uint32(0xFFFF0000), F32)
    return lo.astype(BF16), hi.astype(BF16)


def _ada_kernel(c_ref, w_ref, b_ref, o_ref):
    c = c_ref[...]
    act = c * _sigmoid(c)
    o_ref[...] = jnp.dot(act, w_ref[...], precision=HIGHEST, preferred_element_type=F32) + b_ref[...]


def _ada_call(c, ada_w, ada_b):
    depth, d, n = ada_w.shape
    bsz = c.shape[0]
    tn = 1536
    return pl.pallas_call(
        _ada_kernel,
        grid=(depth, n // tn),
        in_specs=[
            pl.BlockSpec((bsz, d), lambda l, j: (0, 0)),
            pl.BlockSpec((None, d, tn), lambda l, j: (l, 0, j)),
            pl.BlockSpec((None, 1, tn), lambda l, j: (l, 0, j)),
        ],
        out_specs=pl.BlockSpec((None, bsz, tn), lambda l, j: (l, 0, j)),
        out_shape=jax.ShapeDtypeStruct((depth, bsz, n), F32),
        compiler_params=pltpu.CompilerParams(dimension_semantics=("parallel", "parallel"),
                                             vmem_limit_bytes=VMEM_LIMIT),
        name="ada_mod",
    )(c, ada_w, ada_b.reshape(depth, 1, n))


_C_Q = 0
_C_KV = Q_LORA_RANK
_C_PE = _C_KV + KV_LORA_RANK
_C_PESW = _C_PE + HEAD_SLAB
_C_U = _C_PESW + HEAD_SLAB


def _even_in_kernel(x_ref, mod_ref, pos_ref, g_ref, win_ref, gq_ref, wq_ref, gkv_ref, wk_ref, wv_ref,
                    tab_ref, q_ref, k_ref, v_ref, u_ref):
    x = x_ref[...]
    h = _mod_norm(x, g_ref[...], mod_ref[1:2, :], mod_ref[0:1, :])
    z = _bdot(h, win_ref[...])
    q_c = z[:, _C_Q:_C_KV]
    kv_c = z[:, _C_KV:_C_PE]
    kpe = z[:, _C_PE:_C_PESW]
    kpe_sw = z[:, _C_PESW:_C_U]
    u_ref[...] = z[:, _C_U:]

    ang = pos_ref[...] * tab_ref[0:1, :]
    cs = jnp.cos(ang)
    sn = jnp.sin(ang)
    gcq = cs * tab_ref[1:2, :]
    gsq = sn * tab_ref[2:3, :]
    gck = cs * tab_ref[3:4, :]
    gsk = sn * tab_ref[4:5, :]

    qn = q_c * _rms(q_c, Q_LORA_RANK) * gq_ref[...]
    qq = _bdot(qn, wq_ref[...])
    kvn = kv_c * _rms(kv_c, KV_LORA_RANK) * gkv_ref[...]
    kk = _bdot(kvn, wk_ref[...])
    v_ref[...] = _bdot(kvn, wv_ref[...]).astype(v_ref.dtype)

    pe_rot = kpe * gck + kpe_sw * gsk
    pe_ss = jnp.sum(kpe * kpe, axis=-1, keepdims=True)
    hw = MLA_HEADS * HEAD_SLAB
    for hd in range(MLA_HEADS):
        lo = hd * HEAD_SLAB
        qr = qq[:, lo:lo + HEAD_SLAB]
        qs = qq[:, hw + lo:hw + lo + HEAD_SLAB]
        rq = _rms(qr, QK_HEAD_DIM)
        q_ref[hd] = (rq * (qr * gcq + qs * gsq)).astype(q_ref.dtype)
        kr = kk[:, lo:lo + HEAD_SLAB]
        rk = lax.rsqrt((jnp.sum(kr * kr, axis=-1, keepdims=True) + pe_ss) * (1.0 / QK_HEAD_DIM) + NORM_EPS)
        k_ref[hd] = (rk * (kr * gck + pe_rot)).astype(k_ref.dtype)


def _even_in_call(x, mod, posf, g, prep):
    bsz, seq, d = x.shape
    tm = min(ROW_TILE, seq)
    hw = MLA_HEADS * HEAD_SLAB
    full = lambda a: pl.BlockSpec(a.shape, lambda b, s: (0,) * a.ndim)
    return pl.pallas_call(
        _even_in_kernel,
        grid=(bsz, seq // tm),
        in_specs=[
            pl.BlockSpec((None, tm, d), lambda b, s: (b, s, 0)),
            pl.BlockSpec((None, 6, d), lambda b, s: (b, 0, 0)),
            pl.BlockSpec((None, tm, 1), lambda b, s: (b, s, 0)),
            full(g), full(prep["w_in"]), full(prep["gq"]), full(prep["wq"]), full(prep["gkv"]),
            full(prep["wk"]), full(prep["wv"]), full(prep["tab"]),
        ],
        out_specs=[
            pl.BlockSpec((None, MLA_HEADS, tm, HEAD_SLAB), lambda b, s: (b, 0, s, 0)),
            pl.BlockSpec((None, MLA_HEADS, tm, HEAD_SLAB), lambda b, s: (b, 0, s, 0)),
            pl.BlockSpec((None, tm, MLA_HEADS * V_HEAD_DIM), lambda b, s: (b, s, 0)),
            pl.BlockSpec((tm, d // 2), lambda b, s: (s, b)),
        ],
        out_shape=[
            jax.ShapeDtypeStruct((bsz, MLA_HEADS, seq, HEAD_SLAB), BF16),
            jax.ShapeDtypeStruct((bsz, MLA_HEADS, seq, HEAD_SLAB), BF16),
            jax.ShapeDtypeStruct((bsz, seq, MLA_HEADS * V_HEAD_DIM), BF16),
            jax.ShapeDtypeStruct((seq, bsz * (d // 2)), F32),
        ],
        compiler_params=pltpu.CompilerParams(dimension_semantics=("parallel", "parallel"),
                                             vmem_limit_bytes=VMEM_LIMIT),
        name="even_in",
    )(x, mod, posf, g, prep["w_in"], prep["gq"], prep["wq"], prep["gkv"], prep["wk"], prep["wv"], prep["tab"])


def _prep_even(even_w_in, q_norm_g, w_uq, kv_norm_g, w_ukv, q_head_g, k_head_g):
    d = even_w_in.shape[0]
    half = QK_ROPE_DIM // 2
    nope = QK_NOPE_DIM
    c_pe = Q_LORA_RANK + KV_LORA_RANK
    w_pe = even_w_in[:, c_pe:c_pe + QK_ROPE_DIM]
    zeros = lambda n: jnp.zeros((d, n), F32)
    pe_slab = jnp.concatenate([zeros(nope), w_pe, zeros(HEAD_SLAB - QK_HEAD_DIM)], axis=1)
    pe_sw = jnp.concatenate([zeros(nope), -w_pe[:, half:], w_pe[:, :half], zeros(HEAD_SLAB - QK_HEAD_DIM)], axis=1)
    w_in = jnp.concatenate([even_w_in[:, :c_pe], pe_slab, pe_sw, even_w_in[:, c_pe + QK_ROPE_DIM:]], axis=1)

    r = w_uq.shape[0]
    padq = jnp.zeros((r, MLA_HEADS, HEAD_SLAB - QK_HEAD_DIM), F32)
    wq_plain = jnp.concatenate([w_uq, padq], axis=2).reshape(r, MLA_HEADS * HEAD_SLAB)
    wq_sw = jnp.concatenate([jnp.zeros((r, MLA_HEADS, nope), F32), -w_uq[:, :, nope + half:],
                             w_uq[:, :, nope:nope + half], padq], axis=2).reshape(r, MLA_HEADS * HEAD_SLAB)
    wq = jnp.concatenate([wq_plain, wq_sw], axis=1)

    rk = w_ukv.shape[0]
    wk = jnp.concatenate([w_ukv[:, :, :nope], jnp.zeros((rk, MLA_HEADS, HEAD_SLAB - nope), F32)],
                         axis=2).reshape(rk, MLA_HEADS * HEAD_SLAB)
    wv = w_ukv[:, :, nope:].reshape(rk, MLA_HEADS * V_HEAD_DIM)

    inv_freq = 1.0 / (ROPE_THETA ** (jnp.arange(half, dtype=F32) / half))
    pad_tail = jnp.zeros((HEAD_SLAB - QK_HEAD_DIM,), F32)
    freq_row = jnp.concatenate([jnp.zeros((nope,), F32), inv_freq, inv_freq, pad_tail])

    def gain_rows(gv, scale):
        plain = jnp.concatenate([gv, pad_tail]) * scale
        swapped = jnp.concatenate([jnp.zeros((nope,), F32), gv[nope + half:], gv[nope:nope + half], pad_tail]) * scale
        return plain, swapped

    gq_plain, gq_sw = gain_rows(q_head_g, QK_HEAD_DIM ** -0.5 * math.log2(math.e))
    gk_plain, gk_sw = gain_rows(k_head_g, 1.0)
    tab = jnp.stack([freq_row, gq_plain, gq_sw, gk_plain, gk_sw, freq_row * 0, freq_row * 0, freq_row * 0])
    return {
        "w_in": w_in.astype(BF16), "gq": q_norm_g.reshape(1, -1), "wq": wq.astype(BF16),
        "gkv": kv_norm_g.reshape(1, -1), "wk": wk.astype(BF16), "wv": wv.astype(BF16), "tab": tab,
    }


def _attn_kernel(qi_ref, kj_ref, q_ref, k_ref, v_ref, o_ref, m_sc, a_sc, acc_sc, s_sc, p_sc, *, tq, tk):
    step = pl.program_id(2)
    i = qi_ref[step]
    j = kj_ref[step]
    sum_lane = (V_HEAD_DIM, 0)
    subs = tq // tk

    @pl.when(j == 0)
    def _():
        m_sc[...] = jnp.full(m_sc.shape, -jnp.inf, F32)
        acc_sc[...] = jnp.zeros(acc_sc.shape, F32)

    def sweep(diag_sub):
        lo = 0 if diag_sub is None else diag_sub * tk
        live = pl.ds(lo, tq - lo)
        lane = lax.broadcasted_iota(jnp.int32, (1, LANES), 1)
        for hh in range(ATTN_HEADS):
            s_sc[hh, live, :] = lax.dot_general(q_ref[hh, live, :], k_ref[hh], (((1,), (1,)), ((), ())),
                                                preferred_element_type=F32)
        for hh in range(ATTN_HEADS):
            v = v_ref[:, (hh // 2) * LANES:(hh // 2 + 1) * LANES]
            for r0 in range(lo, tq, ATTN_ROWS):
                rows = pl.ds(r0, ATTN_ROWS)
                s = s_sc[hh, rows, :]
                if diag_sub is not None and r0 < lo + tk:
                    row = (r0 - lo) + lax.broadcasted_iota(jnp.int32, (ATTN_ROWS, tk), 0)
                    col = lax.broadcasted_iota(jnp.int32, (ATTN_ROWS, tk), 1)
                    s = jnp.where(col <= row, s, -jnp.inf)
                m_prev = m_sc[hh, rows, :]
                m_new = jnp.maximum(m_prev, jnp.max(s, axis=-1, keepdims=True))
                a_sc[hh, rows, :] = jnp.exp2(m_prev - m_new)
                m_sc[hh, rows, :] = m_new
                shifted = s - jnp.concatenate([m_new] * (tk // LANES), axis=1)
                p_sc[hh, rows, :] = jnp.exp2(shifted.astype(BF16))
            own = (lane < V_HEAD_DIM) == (hh % 2 == 0)
            ones = jnp.where(lane == sum_lane[hh % 2], 1.0, 0.0).astype(v.dtype)
            vh = jnp.where(own, v, jnp.broadcast_to(ones, v.shape))
            acc_sc[hh, live, :] = (acc_sc[hh, live, :] * a_sc[hh, live, :]
                                   + jnp.dot(p_sc[hh, live, :], vh, preferred_element_type=F32))

    diag = j - subs * i

    @pl.when(diag < 0)
    def _():
        sweep(None)

    for ds_ in range(subs):
        @pl.when(diag == ds_)
        def _(ds_=ds_):
            sweep(ds_)

    @pl.when(diag == subs - 1)
    def _():
        lane = lax.broadcasted_iota(jnp.int32, (1, LANES), 1)
        for pp in range(ATTN_HEADS // 2):
            acc0 = acc_sc[2 * pp]
            acc1 = acc_sc[2 * pp + 1]
            l0 = acc0[:, sum_lane[0]:sum_lane[0] + 1]
            l1 = acc1[:, sum_lane[1]:sum_lane[1] + 1]
            o_ref[:, pp * LANES:(pp + 1) * LANES] = jnp.where(lane < V_HEAD_DIM, acc0 / l0,
                                                              acc1 / l1).astype(o_ref.dtype)


def _attn_call(q, k, v):
    bsz, nh, seq, _ = q.shape
    tk = min(ATTN_TILE, seq)
    tq = min(ATTN_Q_BLOCKS * tk, seq)
    subs = tq // tk
    nq = seq // tq
    pairs = [(i, j) for i in range(nq) for j in range(subs * (i + 1))]
    qi = jnp.asarray([p[0] for p in pairs], jnp.int32)
    kj = jnp.asarray([p[1] for p in pairs], jnp.int32)
    kern = functools.partial(_attn_kernel, tq=tq, tk=tk)
    hp = ATTN_HEADS
    assert nh % hp == 0
    return pl.pallas_call(
        kern,
        grid_spec=pltpu.PrefetchScalarGridSpec(
            num_scalar_prefetch=2,
            grid=(bsz, nh // hp, len(pairs)),
            in_specs=[
                pl.BlockSpec((None, hp, tq, HEAD_SLAB), lambda b, h, p, qi, kj: (b, h, qi[p], 0)),
                pl.BlockSpec((None, hp, tk, HEAD_SLAB), lambda b, h, p, qi, kj: (b, h, kj[p], 0)),
                pl.BlockSpec((None, tk, hp * V_HEAD_DIM), lambda b, h, p, qi, kj: (b, kj[p], h)),
            ],
            out_specs=pl.BlockSpec((None, tq, hp * V_HEAD_DIM), lambda b, h, p, qi, kj: (b, qi[p], h)),
            scratch_shapes=[pltpu.VMEM((hp, tq, LANES), F32), pltpu.VMEM((hp, tq, LANES), F32),
                            pltpu.VMEM((hp, tq, LANES), F32),
                            pltpu.VMEM((hp, tq, tk), F32), pltpu.VMEM((hp, tq, tk), BF16)],
        ),
        out_shape=jax.ShapeDtypeStruct((bsz, seq, nh * V_HEAD_DIM), BF16),
        compiler_params=pltpu.CompilerParams(
            dimension_semantics=("parallel", "parallel", "arbitrary"),
            vmem_limit_bytes=VMEM_LIMIT),
        name="mla_attention",
    )(qi, kj, q, k, v)


def _s5_disc_kernel(are_ref, aim_ref, ldt_ref, bre_ref, bim_ref, abre_ref, abim_ref, bbre_ref, bbim_ref):
    dt = jnp.exp(ldt_ref[...])
    lam_re = jnp.minimum(are_ref[...], -1e-4)
    lam_im = aim_ref[...]
    mag = jnp.exp(lam_re * dt)
    ab_re = mag * jnp.cos(lam_im * dt)
    ab_im = mag * jnp.sin(lam_im * dt)
    den = lam_re * lam_re + lam_im * lam_im
    num_re = ab_re - 1.0
    f_re = (num_re * lam_re + ab_im * lam_im) / den
    f_im = (ab_im * lam_re - num_re * lam_im) / den
    abre_ref[...] = ab_re
    abim_ref[...] = ab_im
    br = bre_ref[...]
    bi = bim_ref[...]
    bbre_ref[...] = f_re[:, None, :] * br - f_im[:, None, :] * bi
    bbim_ref[...] = f_re[:, None, :] * bi + f_im[:, None, :] * br


def _s5_disc_call(a_re, a_im, log_dt, b_re, b_im):
    g, p = a_re.shape
    bre_t = jnp.swapaxes(b_re, 1, 2)
    bim_t = jnp.swapaxes(b_im, 1, 2)
    return pl.pallas_call(
        _s5_disc_kernel,
        out_shape=[jax.ShapeDtypeStruct((g, p), F32), jax.ShapeDtypeStruct((g, p), F32),
                   jax.ShapeDtypeStruct(bre_t.shape, F32), jax.ShapeDtypeStruct(bre_t.shape, F32)],
        name="s5_discretize",
    )(a_re, a_im, log_dt.reshape(g, 1), bre_t, bim_t)


def _block_diag_halves(m):
    g, r, c = m.shape
    gh = g // 2
    eye = jnp.eye(gh, dtype=m.dtype)
    mh = m.reshape(2, gh, r, c)
    return (mh[:, :, :, None, :] * eye[None, :, None, :, None]).reshape(2, gh * r, gh * c)


def _s5_kernel(u_ref, bre_ref, bim_ref, are_ref, aim_ref, cre_ref, cim_ref, d_ref, gw_ref, gb_ref,
               o_ref, sre, sim, dre, dim, xbr, xbi, *, steps):
    @pl.when(pl.program_id(0) == 0)
    def _():
        sre[...] = jnp.zeros(sre.shape, F32)
        sim[...] = jnp.zeros(sim.shape, F32)

    rows = steps * SUBLANES
    w = u_ref.shape[-1]
    u = u_ref[...].reshape(rows, w)
    ub = u.astype(BF16)
    kh = w // 2
    nh = dre.shape[1] // 2
    for hf in range(2):
        dre[:, hf * nh:(hf + 1) * nh] = jnp.dot(ub[:, hf * kh:(hf + 1) * kh], bre_ref[hf], preferred_element_type=F32)
        dim[:, hf * nh:(hf + 1) * nh] = jnp.dot(ub[:, hf * kh:(hf + 1) * kh], bim_ref[hf], preferred_element_type=F32)

    xr = sre[...]
    xi = sim[...]
    for t in range(0, steps, 2):
        pair_r = []
        pair_i = []
        for r0 in (t * SUBLANES, (t + 1) * SUBLANES):
            nr = are_ref[...] * xr - aim_ref[...] * xi + dre[r0:r0 + SUBLANES, :]
            ni = are_ref[...] * xi + aim_ref[...] * xr + dim[r0:r0 + SUBLANES, :]
            xr, xi = nr, ni
            pair_r.append(nr)
            pair_i.append(ni)
        xbr[t * SUBLANES:(t + 2) * SUBLANES, :] = jnp.concatenate(pair_r, axis=0).astype(BF16)
        xbi[t * SUBLANES:(t + 2) * SUBLANES, :] = jnp.concatenate(pair_i, axis=0).astype(BF16)
    sre[...] = xr
    sim[...] = xi

    ys = []
    for hf in range(2):
        yr = jnp.dot(xbr[:, hf * nh:(hf + 1) * nh], cre_ref[hf], preferred_element_type=F32)
        yi = jnp.dot(xbi[:, hf * nh:(hf + 1) * nh], cim_ref[hf], preferred_element_type=F32)
        ys.append(yr - yi)
    y = jnp.concatenate(ys, axis=1) + d_ref[...] * u
    g = _gelu(y)
    out = g * _sigmoid(_bdot(g, gw_ref[...]) + gb_ref[...])
    o_ref[...] = out.reshape(steps, SUBLANES, w).astype(o_ref.dtype)


def _s5_call(u_t, disc, c_re, c_im, d_skip, glu_w, glu_b):
    seq, bsz, w = u_t.shape
    assert bsz == SUBLANES
    ab_re, ab_im, bb_re, bb_im = disc
    g, p = ab_re.shape
    n_state = g * p
    bre = _block_diag_halves(bb_re).astype(BF16)
    bim = _block_diag_halves(bb_im).astype(BF16)
    cre = _block_diag_halves(jnp.swapaxes(c_re, 1, 2)).astype(BF16)
    cim = _block_diag_halves(jnp.swapaxes(c_im, 1, 2)).astype(BF16)
    steps = min(S5_STEPS, seq)
    full = lambda a: pl.BlockSpec(a.shape, lambda s: (0,) * a.ndim)
    rep = lambda a: jnp.broadcast_to(a.reshape(1, n_state), (bsz, n_state))
    args = (bre, bim, rep(ab_re), rep(ab_im), cre, cim,
            d_skip.reshape(1, w), glu_w.astype(BF16), glu_b.reshape(1, w))
    return pl.pallas_call(
        functools.partial(_s5_kernel, steps=steps),
        grid=(seq // steps,),
        in_specs=[pl.BlockSpec((steps, bsz, w), lambda s: (s, 0, 0))] + [full(a) for a in args],
        out_specs=pl.BlockSpec((steps, bsz, w), lambda s: (s, 0, 0)),
        out_shape=jax.ShapeDtypeStruct((seq, bsz, w), BF16),
        scratch_shapes=[pltpu.VMEM((bsz, n_state), F32), pltpu.VMEM((bsz, n_state), F32),
                        pltpu.VMEM((steps * bsz, n_state), F32), pltpu.VMEM((steps * bsz, n_state), F32),
                        pltpu.VMEM((steps * bsz, n_state), BF16), pltpu.VMEM((steps * bsz, n_state), BF16)],
        compiler_params=pltpu.CompilerParams(dimension_semantics=("arbitrary",), vmem_limit_bytes=VMEM_LIMIT),
        name="s5_scan",
    )(u_t, *args)


def _router_tail(x_new, mod_ref, gf_ref, rw_ref, rb_ref, h2_ref, te_ref, tet_ref, cnt_ref):
    h2 = _mod_norm(x_new, gf_ref[...], mod_ref[4:5, :], mod_ref[3:4, :])
    h2_ref[...] = _pack_pairs(h2)
    h_hi = h2.astype(BF16)
    h_lo = (h2 - h_hi.astype(F32)).astype(BF16)
    r_hi = jnp.dot(h_hi, rw_ref[...], preferred_element_type=F32)
    r_lo = jnp.dot(h_lo, rw_ref[...], preferred_element_type=F32)
    logits = r_hi[:, :LANES] + r_hi[:, LANES:] + r_lo[:, :LANES] + rb_ref[...]
    lane = lax.broadcasted_iota(jnp.int32, logits.shape, 1).astype(F32)
    vals = []
    idxs = []
    work = logits
    for _ in range(TOP_K):
        m = jnp.max(work, axis=-1, keepdims=True)
        idx = jnp.min(jnp.where(work == m, lane, float(LANES)), axis=-1, keepdims=True)
        vals.append(m)
        idxs.append(idx)
        work = jnp.where(lane == idx, NEG_BIG * 2.0, work)
    exps = [jnp.exp(vv - vals[0]) for vv in vals]
    tot = exps[0] + exps[1] + exps[2] + exps[3]
    te = jnp.zeros(logits.shape, F32)
    picked = jnp.zeros(logits.shape, F32)
    for kk in range(TOP_K):
        te = jnp.where(lane == float(kk), idxs[kk], te)
        te = jnp.where(lane == float(TOP_K + kk), exps[kk] / tot, te)
        picked = picked + jnp.where(lane == idxs[kk], 1.0, 0.0)
    te_ref[...] = te[:, :2 * TOP_K]
    tet_ref[...] = te.T[:2 * TOP_K, :]
    cnt_ref[...] = jnp.sum(picked, axis=0, keepdims=True)


def _mix_out_kernel(x_ref, a_ref, s_ref, mod_ref, wo_ref, gf_ref, rw_ref, rb_ref, xo_ref, h2_ref, te_ref,
                    tet_ref, cnt_ref):
    ka = a_ref.shape[-1]
    mix = jnp.dot(a_ref[...], wo_ref[:ka, :], preferred_element_type=F32)
    mix = mix + jnp.dot(s_ref[...], wo_ref[ka:, :], preferred_element_type=F32)
    x_new = x_ref[...] + mod_ref[2:3, :] * mix
    xo_ref[...] = x_new
    _router_tail(x_new, mod_ref, gf_ref, rw_ref, rb_ref, h2_ref, te_ref, tet_ref, cnt_ref)


def _tail_out_specs(bsz, seq, tm, d):
    nt = seq // tm
    specs = [
        pl.BlockSpec((None, tm, d), lambda b, s: (b, s, 0)),
        pl.BlockSpec((tm, d // 2), lambda b, s: (b * nt + s, 0)),
        pl.BlockSpec((tm, 2 * TOP_K), lambda b, s: (b * nt + s, 0)),
        pl.BlockSpec((2 * TOP_K, tm), lambda b, s: (0, b * nt + s)),
        pl.BlockSpec((None, 1, LANES), lambda b, s: (b * nt + s, 0, 0)),
    ]
    shapes = [
        jax.ShapeDtypeStruct((bsz, seq, d), F32),
        jax.ShapeDtypeStruct((bsz * seq, d // 2), jnp.uint32),
        jax.ShapeDtypeStruct((bsz * seq, 2 * TOP_K), F32),
        jax.ShapeDtypeStruct((2 * TOP_K, bsz * seq), F32),
        jax.ShapeDtypeStruct((bsz * nt, 1, LANES), F32),
    ]
    return specs, shapes


def _router_pad(router_w, router_b):
    d, e = router_w.shape
    rw = jnp.concatenate([router_w, jnp.zeros((d, LANES - e), F32)], axis=1)
    rw_hi = rw.astype(BF16)
    rw_lo = (rw - rw_hi.astype(F32)).astype(BF16)
    rb = jnp.concatenate([router_b, jnp.full((LANES - e,), NEG_BIG, F32)]).reshape(1, LANES)
    return jnp.concatenate([rw_hi, rw_lo], axis=1), rb


def _mix_out_call(x, attn, ssm_t, mod, w_out, gf, rw, rb):
    bsz, seq, d = x.shape
    tm = min(ROW_TILE, seq)
    ka = attn.shape[-1]
    ks = ssm_t.shape[-1] // bsz
    full = lambda a: pl.BlockSpec(a.shape, lambda b, s: (0,) * a.ndim)
    out_specs, out_shape = _tail_out_specs(bsz, seq, tm, d)
    return pl.pallas_call(
        _mix_out_kernel,
        grid=(bsz, seq // tm),
        in_specs=[
            pl.BlockSpec((None, tm, d), lambda b, s: (b, s, 0)),
            pl.BlockSpec((None, tm, ka), lambda b, s: (b, s, 0)),
            pl.BlockSpec((tm, ks), lambda b, s: (s, b)),
            pl.BlockSpec((None, 6, d), lambda b, s: (b, 0, 0)),
            full(w_out), full(gf), full(rw), full(rb),
        ],
        out_specs=out_specs,
        out_shape=out_shape,
        compiler_params=pltpu.CompilerParams(dimension_semantics=("parallel", "parallel"),
                                             vmem_limit_bytes=VMEM_LIMIT),
        name="even_out",
    )(x, attn, ssm_t, mod, w_out, gf, rw, rb)


def _moe_sum(yg_ref, te_ref):
    te = te_ref[...]
    acc_lo = acc_hi = None
    for kk in range(TOP_K):
        lo, hi = _unpack_pairs(yg_ref[kk])
        gate = te[:, TOP_K + kk:TOP_K + kk + 1]
        acc_lo = gate * lo.astype(F32) if kk == 0 else acc_lo + gate * lo.astype(F32)
        acc_hi = gate * hi.astype(F32) if kk == 0 else acc_hi + gate * hi.astype(F32)
    return jnp.concatenate([acc_lo, acc_hi], axis=1)


def _odd_kernel(x_ref, yg_ref, tep_ref, modp_ref, mod_ref, g_ref, win_ref, icnt_ref, wp_ref, ps_ref, gv_ref,
                wsp_ref, bsp_ref, wo_ref, gf_ref, rw_ref, rb_ref, xo_ref, h2_ref, te_ref, tet_ref, cnt_ref, ext_sc):
    tm = x_ref.shape[0]
    pw = wp_ref.shape[-1]
    width = pw * len(POOL_WINDOWS)

    @pl.when(pl.program_id(1) == 0)
    def _():
        ext_sc[0:POOL_HALO, :] = jnp.zeros((POOL_HALO, width), F32)

    x = x_ref[...] + modp_ref[5:6, :] * _moe_sum(yg_ref, tep_ref)
    h = _mod_norm(x, g_ref[...], mod_ref[1:2, :], mod_ref[0:1, :])
    z = _bdot(h, win_ref[...])
    up = z[:, :width]
    ext_sc[POOL_HALO:POOL_HALO + tm, :] = up

    pooled = []
    for gi, win in enumerate(POOL_WINDOWS):
        cols = slice(gi * pw, (gi + 1) * pw)
        acc = up[:, cols]
        for lag in range(1, win):
            acc = acc + ext_sc[POOL_HALO - lag:POOL_HALO - lag + tm, cols]
        pg = acc * icnt_ref[:, gi:gi + 1] - up[:, cols]
        pooled.append(_bdot(pg, wp_ref[gi]) * ps_ref[:, cols])
    ext_sc[0:POOL_HALO, :] = ext_sc[tm:tm + POOL_HALO, :]
    pooled = jnp.concatenate(pooled, axis=1)

    ug = _gelu(z[:, width:2 * width])
    vg = _gelu(z[:, 2 * width:])
    vn = (vg * _rms(vg, width) * gv_ref[...]).astype(BF16)
    hd = width // SGU_HEADS
    chunks = []
    for ci in range(tm // SGU_CHUNK):
        heads = []
        for hh in range(SGU_HEADS):
            blk = vn[ci * SGU_CHUNK:(ci + 1) * SGU_CHUNK, hh * hd:(hh + 1) * hd]
            heads.append(jnp.dot(wsp_ref[hh], blk, preferred_element_type=F32) + bsp_ref[hh])
        chunks.append(jnp.concatenate(heads, axis=1))
    gated = ug * jnp.concatenate(chunks, axis=0)

    mix = _bdot(pooled, wo_ref[:width, :]) + _bdot(gated, wo_ref[width:, :])
    x_new = x + mod_ref[2:3, :] * mix
    xo_ref[...] = x_new
    _router_tail(x_new, mod_ref, gf_ref, rw_ref, rb_ref, h2_ref, te_ref, tet_ref, cnt_ref)


def _odd_call(grp, mod_prev, mod, nb, g, w_in, pool_w, pool_scale, sgu_norm_g, sgu_w, sgu_b, w_out, gf, rw, rb):
    x = grp["x"]
    _, seq, d = x.shape
    tm = min(ROW_TILE, seq)
    nt = seq // tm
    xb0, tok0, mb0 = grp["x_b0"], grp["tok0"], grp["mod_b0"]
    tile0 = tok0 // tm
    yg = grp["yg"]
    width = pool_scale.shape[0]
    hd = width // SGU_HEADS
    t = jnp.arange(seq, dtype=jnp.int32)
    icnt = jnp.stack([1.0 / jnp.minimum(t + 1, wn).astype(F32) for wn in POOL_WINDOWS], axis=1)
    wsp = jnp.tril(sgu_w).astype(BF16)
    bsp = jnp.broadcast_to(sgu_b[:, :, None], (SGU_HEADS, SGU_CHUNK, hd))
    args = (g, w_in.astype(BF16), icnt, pool_w.astype(BF16), pool_scale.reshape(1, width),
            sgu_norm_g.reshape(1, width), wsp, bsp, w_out.astype(BF16), gf, rw, rb)
    full = lambda a: pl.BlockSpec(a.shape, lambda b, s: (0,) * a.ndim)
    in_specs = [pl.BlockSpec((None, tm, d), lambda b, s: (xb0 + b, s, 0)),
                pl.BlockSpec((TOP_K, tm, yg.shape[-1]), lambda b, s: (0, b * nt + s, 0)),
                pl.BlockSpec((tm, 2 * TOP_K), lambda b, s: (tile0 + b * nt + s, 0)),
                pl.BlockSpec((None, 6, d), lambda b, s: (mb0 + b, 0, 0)),
                pl.BlockSpec((None, 6, d), lambda b, s: (mb0 + b, 0, 0))]
    for idx, a in enumerate(args):
        in_specs.append(pl.BlockSpec((tm, len(POOL_WINDOWS)), lambda b, s: (s, 0)) if idx == 2 else full(a))
    out_specs, out_shape = _tail_out_specs(nb, seq, tm, d)
    return pl.pallas_call(
        _odd_kernel,
        grid=(nb, nt),
        in_specs=in_specs,
        out_specs=out_specs,
        out_shape=out_shape,
        scratch_shapes=[pltpu.VMEM((tm + POOL_HALO, width), F32)],
        compiler_params=pltpu.CompilerParams(dimension_semantics=("parallel", "arbitrary"),
                                             vmem_limit_bytes=VMEM_LIMIT),
        name="odd_mixer",
    )(x, yg, grp["te"], mod_prev, mod, *args)


def _dest_kernel(tet_ref, base_ref, tri_ref, dst_ref):
    tr = tet_ref.shape[1]
    tet = tet_ref[...]
    expert = lax.broadcasted_iota(jnp.int32, (LANES, tr), 0).astype(F32)
    hots = [tet[kk:kk + 1, :] == expert for kk in range(TOP_K)]
    oh = jnp.zeros((LANES, tr), F32)
    for hot in hots:
        oh = oh + jnp.where(hot, 1.0, 0.0)
    before = jnp.dot(oh.astype(BF16), tri_ref[...], preferred_element_type=F32) + base_ref[...]
    row = lax.broadcasted_iota(jnp.int32, (2 * TOP_K, tr), 0)
    dst = jnp.zeros((2 * TOP_K, tr), F32)
    for kk, hot in enumerate(hots):
        dst = jnp.where(row == kk, jnp.sum(jnp.where(hot, before, 0.0), axis=0, keepdims=True), dst)
    dst_ref[...] = dst.astype(jnp.int32)


def _dest_call(tet, base, tok0, n_tok):
    tiles = base.shape[0]
    tr = n_tok // tiles
    tile0 = tok0 // tr
    tri = (jnp.arange(tr)[:, None] < jnp.arange(tr)[None, :]).astype(BF16)
    return pl.pallas_call(
        _dest_kernel,
        grid=(tiles,),
        in_specs=[pl.BlockSpec((2 * TOP_K, tr), lambda i: (0, tile0 + i)),
                  pl.BlockSpec((None, LANES, 1), lambda i: (i, 0, 0)),
                  pl.BlockSpec((tr, tr), lambda i: (0, 0))],
        out_specs=pl.BlockSpec((2 * TOP_K, tr), lambda i: (0, i)),
        out_shape=jax.ShapeDtypeStruct((2 * TOP_K, n_tok), jnp.int32),
        compiler_params=pltpu.CompilerParams(dimension_semantics=("parallel",)),
        name="route_dest",
    )(tet, base, tri)


def _sc_gather(table, idx):
    n = idx.shape[0]
    per_w = n // SC_WORKERS
    assert per_w * SC_WORKERS == n and per_w % SC_CHUNK == 0
    n_chunks = per_w // SC_CHUNK
    row_shape = table.shape[1:]
    mesh = plsc.VectorSubcoreMesh(core_axis_name="c", subcore_axis_name="s")

    @functools.partial(
        pl.kernel, mesh=mesh,
        out_type=jax.ShapeDtypeStruct((n,) + row_shape, table.dtype),
        scratch_types=[pltpu.VMEM((SC_CHUNK,), jnp.int32), pltpu.VMEM((SC_CHUNK,) + row_shape, table.dtype),
                       pltpu.SemaphoreType.DMA],
        name="sc_row_gather",
    )
    def gather(table_hbm, idx_hbm, out_hbm, idx_v, rows_v, sem):
        wid = lax.axis_index("s") * 2 + lax.axis_index("c")
        base = wid * per_w

        @pl.loop(0, n_chunks)
        def _(ci):
            off = pl.multiple_of(base + ci * SC_CHUNK, SC_CHUNK)
            pltpu.sync_copy(idx_hbm.at[pl.ds(off, SC_CHUNK)], idx_v)
            pltpu.async_copy(table_hbm.at[idx_v], rows_v, sem).wait()
            pltpu.sync_copy(rows_v, out_hbm.at[pl.ds(off, SC_CHUNK)])

    return gather(table, idx)


def _sc_scatter(rows, dests, n_out, tok0):
    t = dests[0].shape[0]
    per_w = t // SC_WORKERS
    assert per_w * SC_WORKERS == t and per_w % SC_CHUNK == 0
    n_chunks = per_w // SC_CHUNK
    row_shape = rows.shape[1:]
    nk = len(dests)
    mesh = plsc.VectorSubcoreMesh(core_axis_name="c", subcore_axis_name="s")

    @functools.partial(
        pl.kernel, mesh=mesh,
        out_type=jax.ShapeDtypeStruct((n_out,) + row_shape, rows.dtype),
        scratch_types=[pltpu.VMEM((SC_CHUNK,), jnp.int32)] * nk
        + [pltpu.VMEM((SC_CHUNK,) + row_shape, rows.dtype), pltpu.SemaphoreType.DMA],
        name="sc_row_scatter",
    )
    def scatter(rows_hbm, *rest):
        dest_hbm = rest[:nk]
        out_hbm = rest[nk]
        idx_v = rest[nk + 1:2 * nk + 1]
        rows_v, sem = rest[2 * nk + 1:]
        wid = lax.axis_index("s") * 2 + lax.axis_index("c")
        base = wid * per_w

        @pl.loop(0, n_chunks)
        def _(ci):
            off = pl.multiple_of(base + ci * SC_CHUNK, SC_CHUNK)
            src = pl.multiple_of(tok0 + off, SC_CHUNK)
            pltpu.sync_copy(rows_hbm.at[pl.ds(src, SC_CHUNK)], rows_v)
            for kk in range(nk):
                pltpu.sync_copy(dest_hbm[kk].at[pl.ds(off, SC_CHUNK)], idx_v[kk])
            copies = [pltpu.async_copy(rows_v, out_hbm.at[idx_v[kk]], sem) for kk in range(nk)]
            for cp in copies:
                cp.wait()

    return scatter(rows, *dests)


def _expert_kernel(be_ref, nv_ref, ord_ref, ue_ref, nu_ref, x_ref, wgu_hbm, bgu_ref, wdn_hbm, bdn_ref, y_ref,
                   wgu_f32, wdn_f32, wgu_bf, wdn_bf, sem, *, layer):
    i = pl.program_id(0)
    used = i < nu_ref[0]
    pos = ord_ref[i]
    fresh = jnp.logical_or(i == 0, ord_ref[jnp.maximum(i - 1, 0)] != pos)

    def weight_copies(expert):
        return (pltpu.make_async_copy(wgu_hbm.at[layer, expert], wgu_f32, sem.at[0]),
                pltpu.make_async_copy(wdn_hbm.at[layer, expert], wdn_f32, sem.at[1]))

    @pl.when(i == 0)
    def _():
        for cp in weight_copies(ue_ref[0]):
            cp.start()

    @pl.when(jnp.logical_and(used, fresh))
    def _():
        for cp in weight_copies(ue_ref[pos]):
            cp.wait()
        wgu_bf[...] = wgu_f32[...].astype(BF16)
        wdn_bf[...] = wdn_f32[...].astype(BF16)

        @pl.when(pos + 1 < nu_ref[1])
        def _():
            for cp in weight_copies(ue_ref[pos + 1]):
                cp.start()

    def ffn(rows):
        x = jnp.concatenate(_unpack_pairs(x_ref[0:rows, :]), axis=1)
        z = jnp.dot(x, wgu_bf[...], preferred_element_type=F32) + bgu_ref[...]
        ff = z.shape[-1] // 2
        gate = jnp.minimum(z[:, :ff], SWIGLU_LIMIT)
        lin = jnp.clip(z[:, ff:], -SWIGLU_LIMIT, SWIGLU_LIMIT)
        act = gate * _sigmoid(SWIGLU_ALPHA * gate) * (lin + 1.0)
        y = _bdot(act, wdn_bf[...]) + bdn_ref[...]
        y_ref[0:rows, :] = _pack_pairs(y)

    nv = nv_ref[i]
    below = 0
    for size in MOE_PATHS:
        @pl.when(jnp.logical_and(used, jnp.logical_and(nv > below, nv <= size)))
        def _(size=size):
            ffn(size)
        below = size


def _expert_call(layer, block_e, block_valid, block_pos, used_experts, n_used, xs, w_gu, b_gu, w_dn, b_dn):
    n_rows, dh = xs.shape
    depth, e, d, ff2 = w_gu.shape
    ff = ff2 // 2
    nb = n_rows // MOE_ROWS
    row_map = lambda i, be, nv, po, ue, nu: (jnp.minimum(i, nu[0] - 1), 0)
    b_map = lambda i, be, nv, po, ue, nu: (layer, be[i], 0, 0)
    return pl.pallas_call(
        functools.partial(_expert_kernel, layer=layer),
        grid_spec=pltpu.PrefetchScalarGridSpec(
            num_scalar_prefetch=5,
            grid=(nb,),
            in_specs=[
                pl.BlockSpec((MOE_ROWS, dh), row_map),
                pl.BlockSpec(memory_space=pl.ANY),
                pl.BlockSpec((None, None, 1, ff2), b_map),
                pl.BlockSpec(memory_space=pl.ANY),
                pl.BlockSpec((None, None, 1, d), b_map),
            ],
            out_specs=pl.BlockSpec((MOE_ROWS, dh), row_map),
            scratch_shapes=[pltpu.VMEM((d, ff2), F32), pltpu.VMEM((ff, d), F32),
                            pltpu.VMEM((d, ff2), BF16), pltpu.VMEM((ff, d), BF16),
                            pltpu.SemaphoreType.DMA((2,))],
        ),
        out_shape=jax.ShapeDtypeStruct((n_rows, dh), jnp.uint32),
        compiler_params=pltpu.CompilerParams(dimension_semantics=("arbitrary",), vmem_limit_bytes=VMEM_LIMIT),
        name="moe_experts",
    )(block_e, block_valid, block_pos, used_experts, n_used, xs, w_gu, b_gu.reshape(depth, e, 1, ff2), w_dn,
      b_dn.reshape(depth, e, 1, d))


def _combine_kernel(x_ref, yg_ref, te_ref, mod_ref, *rest):
    o_ref = rest[-1]
    o_ref[...] = x_ref[...] + mod_ref[5:6, :] * _moe_sum(yg_ref, te_ref)


def _combine_call(grp, mod, prev, bsz, nb):
    x = grp["x"]
    _, seq, d = x.shape
    tm = min(ROW_TILE, seq)
    nt = seq // tm
    xb0, mb0 = grp["x_b0"], grp["mod_b0"]
    tile0 = grp["tok0"] // tm
    yg = grp["yg"]
    in_specs = [
        pl.BlockSpec((None, tm, d), lambda b, s: (xb0 + b, s, 0)),
        pl.BlockSpec((TOP_K, tm, yg.shape[-1]), lambda b, s: (0, b * nt + s, 0)),
        pl.BlockSpec((tm, 2 * TOP_K), lambda b, s: (tile0 + b * nt + s, 0)),
        pl.BlockSpec((None, 6, d), lambda b, s: (mb0 + b, 0, 0)),
    ]
    args = [x, yg, grp["te"], mod]
    aliases = {}
    if prev is not None:
        in_specs.append(pl.BlockSpec(memory_space=pl.ANY))
        args.append(prev)
        aliases = {len(args) - 1: 0}
    return pl.pallas_call(
        _combine_kernel,
        grid=(nb, nt),
        in_specs=in_specs,
        out_specs=pl.BlockSpec((None, tm, d), lambda b, s: (mb0 + b, s, 0)),
        out_shape=jax.ShapeDtypeStruct((bsz, seq, d), F32),
        input_output_aliases=aliases,
        compiler_params=pltpu.CompilerParams(dimension_semantics=("parallel", "parallel"),
                                             vmem_limit_bytes=VMEM_LIMIT),
        name="moe_combine",
    )(*args)


def _moe_rows(layer, grp, gt, w_gu, b_gu, w_dn, b_dn):
    h2 = grp["h2"]
    dh = h2.shape[-1]
    n_rows = -(-(gt * TOP_K + N_EXPERTS * (MOE_ROWS - 1)) // MOE_ROWS) * MOE_ROWS
    nb = n_rows // MOE_ROWS
    first_row = jnp.arange(nb, dtype=jnp.int32) * MOE_ROWS
    upto = jnp.arange(LANES)[:, None] <= jnp.arange(LANES)[None, :]
    g_cnt = grp["cnt"][:, 0, :].astype(jnp.int32)
    counts = jnp.sum(g_cnt, axis=0)
    padded = (counts + MOE_ROWS - 1) // MOE_ROWS * MOE_ROWS
    pad_end = jnp.sum(jnp.where(upto, padded[:, None], 0), axis=0)
    pad_start = pad_end - padded
    tile_base = pad_start[None, :] + jnp.cumsum(g_cnt, axis=0) - g_cnt
    dest = _dest_call(grp["tet"], tile_base.astype(F32)[:, :, None], grp["tok0"], gt)
    dests = [dest[kk] for kk in range(TOP_K)]
    block_e = jnp.minimum(jnp.sum(pad_end[None, :N_EXPERTS] <= first_row[:, None], axis=1),
                          N_EXPERTS - 1).astype(jnp.int32)
    valid_end = (pad_start + counts)[block_e]
    block_valid = jnp.clip(valid_end - first_row, 0, MOE_ROWS).astype(jnp.int32)
    owns = counts[:N_EXPERTS] > 0
    expert_pos = jnp.cumsum(owns.astype(jnp.int32)) - 1
    slot = jnp.arange(N_EXPERTS, dtype=jnp.int32)
    used_experts = jnp.sum(jnp.where(owns[None, :] & (expert_pos[None, :] == slot[:, None]),
                                     slot[None, :], 0), axis=1).astype(jnp.int32)
    block_pos = expert_pos[block_e].astype(jnp.int32)
    n_used = jnp.stack([pad_end[N_EXPERTS - 1] // MOE_ROWS, jnp.sum(owns)]).astype(jnp.int32)
    xs = _sc_scatter(h2, dests, n_rows, grp["tok0"])
    y = _expert_call(layer, block_e, block_valid, block_pos, used_experts, n_used, xs, w_gu, b_gu, w_dn, b_dn)
    return _sc_gather(y, dest[:TOP_K].reshape(-1)).reshape(TOP_K, gt, dh)


def kernel(x, c, positions, ada_w, ada_b, norm_mix_g, norm_ffn_g, router_w, router_b, moe_w_gu, moe_b_gu,
           moe_w_dn, moe_b_dn, even_w_in, mla_q_norm_g, mla_w_uq, mla_kv_norm_g, mla_w_ukv, mla_q_head_g,
           mla_k_head_g, s5_a_re, s5_a_im, s5_log_dt, s5_b_re, s5_b_im, s5_c_re, s5_c_im, s5_d, s5_glu_w,
           s5_glu_b, even_w_out, odd_w_in, pool_w, pool_scale, sgu_norm_g, sgu_w, sgu_b, odd_w_out):
    bsz, seq, d = x.shape
    depth = ada_w.shape[0]
    mods = _ada_call(c, ada_w, ada_b).reshape(depth, bsz, 6, d)
    posf = positions.astype(F32).reshape(bsz, seq, 1)
    splits = MOE_SPLITS if bsz % MOE_SPLITS == 0 else 1
    gb = bsz // splits
    gt = gb * seq
    tiles_g = gt // min(ROW_TILE, seq)

    def settle(groups, mod):
        out = None
        for grp in groups:
            out = _combine_call(grp, mod, out, bsz, gb)
        return out

    pending = None
    for layer in range(depth):
        mod = mods[layer]
        i = layer // 2
        g_mix = norm_mix_g[layer].reshape(1, d)
        g_ffn = norm_ffn_g[layer].reshape(1, d)
        rw, rb = _router_pad(router_w[layer], router_b[layer])
        if layer % 2 == 0:
            if pending is not None:
                x = settle(pending, mods[layer - 1])
            prep = _prep_even(even_w_in[i], mla_q_norm_g[i], mla_w_uq[i], mla_kv_norm_g[i], mla_w_ukv[i],
                              mla_q_head_g[i], mla_k_head_g[i])
            q, k, v, u_t = _even_in_call(x, mod, posf, g_mix, prep)
            attn = _attn_call(q, k, v)
            disc = _s5_disc_call(s5_a_re[i], s5_a_im[i], s5_log_dt[i], s5_b_re[i], s5_b_im[i])
            ssm_t = _s5_call(u_t.reshape(seq, bsz, d // 2), disc, s5_c_re[i], s5_c_im[i], s5_d[i],
                             s5_glu_w[i], s5_glu_b[i])
            x_new, h2, te, tet, tile_cnt = _mix_out_call(x, attn, ssm_t.reshape(seq, bsz * (d // 2)), mod,
                                                         even_w_out[i].astype(BF16), g_ffn, rw, rb)
            groups = [dict(x=x_new, x_b0=gi * gb, h2=h2, te=te, tet=tet, tok0=gi * gt, mod_b0=gi * gb,
                           cnt=tile_cnt[gi * tiles_g:(gi + 1) * tiles_g]) for gi in range(splits)]
        else:
            groups = []
            for gi, grp in enumerate(pending):
                x_new, h2, te, tet, tile_cnt = _odd_call(grp, mods[layer - 1], mod, gb, g_mix, odd_w_in[i],
                                                         pool_w[i], pool_scale[i], sgu_norm_g[i], sgu_w[i],
                                                         sgu_b[i], odd_w_out[i], g_ffn, rw, rb)
                groups.append(dict(x=x_new, x_b0=0, h2=h2, te=te, tet=tet, tok0=0, mod_b0=gi * gb,
                                   cnt=tile_cnt))
        for grp in groups:
            grp["yg"] = _moe_rows(layer, grp, gt, moe_w_gu, moe_b_gu, moe_w_dn, moe_b_dn)
        pending = groups
    return settle(pending, mods[depth - 1])
```

```python
import functools
import math

import jax
import jax.numpy as jnp
from jax import lax
from jax.experimental import pallas as pl
from jax.experimental.pallas import tpu as pltpu
from jax.experimental.pallas import tpu_sc as plsc

F32 = jnp.float32
BF16 = jnp.bfloat16
HIGHEST = lax.Precision.HIGHEST

NORM_EPS = 1e-6
MLA_HEADS = 8
QK_NOPE_DIM = 64
QK_ROPE_DIM = 32
QK_HEAD_DIM = QK_NOPE_DIM + QK_ROPE_DIM
V_HEAD_DIM = 64
Q_LORA_RANK = 256
KV_LORA_RANK = 128
ROPE_THETA = 10000.0
S5_GROUP = 16
S5_STATE = 64
POOL_WINDOWS = (2, 4, 8, 16)
SGU_HEADS = 4
SGU_CHUNK = 128
N_EXPERTS = 32
TOP_K = 4
SWIGLU_ALPHA = 1.702
SWIGLU_LIMIT = 7.0

LANES = 128
SUBLANES = 8
HEAD_SLAB = LANES
POOL_HALO = 16
ROW_TILE = 1024
ODD_TILE = 512
ATTN_TILE = 512
ATTN_Q_BLOCKS = 2
ATTN_ROWS = 32
ATTN_HEADS = 4
S5_STEPS = 64
MOE_ROWS = 1024
MOE_PATHS = (128, 256, 512, 768, MOE_ROWS)
MOE_SPLITS = 2
SC_WORKERS = 32
SC_CHUNK = 64
VMEM_LIMIT = 56 * 1024 * 1024
NEG_BIG = -1e30


def _sigmoid(v):
    return 1.0 / (1.0 + jnp.exp(-v))


def _gelu(v):
    return 0.5 * v * (1.0 + jnp.tanh(math.sqrt(2.0 / math.pi) * (v + 0.044715 * (v * v * v))))


def _rms(v, width):
    return lax.rsqrt(jnp.sum(v * v, axis=-1, keepdims=True) * (1.0 / width) + NORM_EPS)


def _mod_norm(x, g, sc, sh):
    return x * _rms(x, x.shape[-1]) * (g * (1.0 + sc)) + sh


def _bdot(a, b):
    return jnp.dot(a.astype(BF16), b, preferred_element_type=F32)


def _pack_pairs(v):
    w = v.shape[-1] // 2
    bits = pltpu.bitcast(v.astype(BF16).astype(F32), jnp.uint32)
    return (bits[:, :w] >> 16) | bits[:, w:]


def _unpack_pairs_f32(p):
    lo = pltpu.bitcast(p << 16, F32)
    hi = pltpu.bitcast(p & jnp.uint32(0xFFFF0000), F32)
    return lo, hi


def _unpack_pairs(p):
    lo, hi = _unpack_pairs_f32(p)
    return lo.astype(BF16), hi.astype(BF16)


def _ada_kernel(c_ref, w_ref, b_ref, o_ref):
    c = c_ref[...]
    act = c * _sigmoid(c)
    o_ref[...] = jnp.dot(act, w_ref[...], precision=HIGHEST, preferred_element_type=F32) + b_ref[...]


def _ada_call(c, ada_w, ada_b):
    depth, d, n = ada_w.shape
    bsz = c.shape[0]
    tn = 1536
    return pl.pallas_call(
        _ada_kernel,
        grid=(depth, n // tn),
        in_specs=[
            pl.BlockSpec((bsz, d), lambda l, j: (0, 0)),
            pl.BlockSpec((None, d, tn), lambda l, j: (l, 0, j)),
            pl.BlockSpec((None, 1, tn), lambda l, j: (l, 0, j)),
        ],
        out_specs=pl.BlockSpec((None, bsz, tn), lambda l, j: (l, 0, j)),
        out_shape=jax.ShapeDtypeStruct((depth, bsz, n), F32),
        compiler_params=pltpu.CompilerParams(dimension_semantics=("parallel", "parallel"),
                                             vmem_limit_bytes=VMEM_LIMIT),
        name="ada_mod",
    )(c, ada_w, ada_b.reshape(depth, 1, n))


_C_Q = 0
_C_KV = Q_LORA_RANK
_C_PE = _C_KV + KV_LORA_RANK
_C_PESW = _C_PE + HEAD_SLAB
_C_U = _C_PESW + HEAD_SLAB


def _even_in_kernel(x_ref, mod_ref, pos_ref, g_ref, win_ref, gq_ref, wq_ref, gkv_ref, wk_ref, wv_ref,
                    tab_ref, q_ref, k_ref, v_ref, u_ref):
    x = x_ref[...]
    h = _mod_norm(x, g_ref[...], mod_ref[1:2, :], mod_ref[0:1, :])
    z = _bdot(h, win_ref[...])
    q_c = z[:, _C_Q:_C_KV]
    kv_c = z[:, _C_KV:_C_PE]
    kpe = z[:, _C_PE:_C_PESW]
    kpe_sw = z[:, _C_PESW:_C_U]
    u_ref[...] = z[:, _C_U:]

    tm = x.shape[0]
    packs = LANES // QK_ROPE_DIM
    qrows = tm // packs
    lane = lax.broadcasted_iota(jnp.int32, (1, LANES), 1)
    packed = jnp.zeros((qrows, LANES), F32)
    for part in range(packs):
        in_part = (lane >= part * QK_ROPE_DIM) & (lane < (part + 1) * QK_ROPE_DIM)
        packed = jnp.where(in_part, pos_ref[part * qrows:(part + 1) * qrows, :], packed)
    ang = packed * tab_ref[5:6, :]
    cs_p = jnp.cos(ang)
    sn_p = jnp.sin(ang)
    on_rope = (lane >= QK_NOPE_DIM) & (lane < QK_HEAD_DIM)
    cs_parts = []
    sn_parts = []
    for part in range(packs):
        shift = (QK_NOPE_DIM - part * QK_ROPE_DIM) % LANES
        cs_r = cs_p if shift == 0 else pltpu.roll(cs_p, shift, axis=1)
        sn_r = sn_p if shift == 0 else pltpu.roll(sn_p, shift, axis=1)
        cs_parts.append(jnp.where(on_rope, cs_r, 1.0))
        sn_parts.append(jnp.where(on_rope, sn_r, 0.0))
    cs = jnp.concatenate(cs_parts, axis=0)
    sn = jnp.concatenate(sn_parts, axis=0)
    gcq = cs * tab_ref[1:2, :]
    gsq = sn * tab_ref[2:3, :]
    gck = cs * tab_ref[3:4, :]
    gsk = sn * tab_ref[4:5, :]

    qn = q_c * _rms(q_c, Q_LORA_RANK) * gq_ref[...]
    qq = _bdot(qn, wq_ref[...])
    kvn = kv_c * _rms(kv_c, KV_LORA_RANK) * gkv_ref[...]
    kk = _bdot(kvn, wk_ref[...])
    v_ref[...] = _bdot(kvn, wv_ref[...]).astype(v_ref.dtype)

    pe_rot = kpe * gck + kpe_sw * gsk
    pe_ss = jnp.sum(kpe * kpe, axis=-1, keepdims=True)
    hw = MLA_HEADS * HEAD_SLAB
    for hd in range(MLA_HEADS):
        lo = hd * HEAD_SLAB
        qr = qq[:, lo:lo + HEAD_SLAB]
        qs = qq[:, hw + lo:hw + lo + HEAD_SLAB]
        rq = _rms(qr, QK_HEAD_DIM)
        q_ref[hd] = (rq * (qr * gcq + qs * gsq)).astype(q_ref.dtype)
        kr = kk[:, lo:lo + HEAD_SLAB]
        rk = lax.rsqrt((jnp.sum(kr * kr, axis=-1, keepdims=True) + pe_ss) * (1.0 / QK_HEAD_DIM) + NORM_EPS)
        k_ref[hd] = (rk * (kr * gck + pe_rot)).astype(k_ref.dtype)


def _even_in_call(x, mod, posf, g, prep):
    bsz, seq, d = x.shape
    tm = min(ROW_TILE, seq)
    hw = MLA_HEADS * HEAD_SLAB
    full = lambda a: pl.BlockSpec(a.shape, lambda b, s: (0,) * a.ndim)
    return pl.pallas_call(
        _even_in_kernel,
        grid=(bsz, seq // tm),
        in_specs=[
            pl.BlockSpec((None, tm, d), lambda b, s: (b, s, 0)),
            pl.BlockSpec((None, 6, d), lambda b, s: (b, 0, 0)),
            pl.BlockSpec((None, tm, 1), lambda b, s: (b, s, 0)),
            full(g), full(prep["w_in"]), full(prep["gq"]), full(prep["wq"]), full(prep["gkv"]),
            full(prep["wk"]), full(prep["wv"]), full(prep["tab"]),
        ],
        out_specs=[
            pl.BlockSpec((None, MLA_HEADS, tm, HEAD_SLAB), lambda b, s: (b, 0, s, 0)),
            pl.BlockSpec((None, MLA_HEADS, tm, HEAD_SLAB), lambda b, s: (b, 0, s, 0)),
            pl.BlockSpec((None, tm, MLA_HEADS * V_HEAD_DIM), lambda b, s: (b, s, 0)),
            pl.BlockSpec((tm, d // 2), lambda b, s: (s, b)),
        ],
        out_shape=[
            jax.ShapeDtypeStruct((bsz, MLA_HEADS, seq, HEAD_SLAB), BF16),
            jax.ShapeDtypeStruct((bsz, MLA_HEADS, seq, HEAD_SLAB), BF16),
            jax.ShapeDtypeStruct((bsz, seq, MLA_HEADS * V_HEAD_DIM), BF16),
            jax.ShapeDtypeStruct((seq, bsz * (d // 2)), F32),
        ],
        compiler_params=pltpu.CompilerParams(dimension_semantics=("parallel", "parallel"),
                                             vmem_limit_bytes=VMEM_LIMIT),
        name="even_in",
    )(x, mod, posf, g, prep["w_in"], prep["gq"], prep["wq"], prep["gkv"], prep["wk"], prep["wv"], prep["tab"])


def _prep_even(even_w_in, q_norm_g, w_uq, kv_norm_g, w_ukv, q_head_g, k_head_g):
    d = even_w_in.shape[0]
    half = QK_ROPE_DIM // 2
    nope = QK_NOPE_DIM
    c_pe = Q_LORA_RANK + KV_LORA_RANK
    w_pe = even_w_in[:, c_pe:c_pe + QK_ROPE_DIM]
    zeros = lambda n: jnp.zeros((d, n), F32)
    pe_slab = jnp.concatenate([zeros(nope), w_pe, zeros(HEAD_SLAB - QK_HEAD_DIM)], axis=1)
    pe_sw = jnp.concatenate([zeros(nope), -w_pe[:, half:], w_pe[:, :half], zeros(HEAD_SLAB - QK_HEAD_DIM)], axis=1)
    w_in = jnp.concatenate([even_w_in[:, :c_pe], pe_slab, pe_sw, even_w_in[:, c_pe + QK_ROPE_DIM:]], axis=1)

    r = w_uq.shape[0]
    padq = jnp.zeros((r, MLA_HEADS, HEAD_SLAB - QK_HEAD_DIM), F32)
    wq_plain = jnp.concatenate([w_uq, padq], axis=2).reshape(r, MLA_HEADS * HEAD_SLAB)
    wq_sw = jnp.concatenate([jnp.zeros((r, MLA_HEADS, nope), F32), -w_uq[:, :, nope + half:],
                             w_uq[:, :, nope:nope + half], padq], axis=2).reshape(r, MLA_HEADS * HEAD_SLAB)
    wq = jnp.concatenate([wq_plain, wq_sw], axis=1)

    rk = w_ukv.shape[0]
    wk = jnp.concatenate([w_ukv[:, :, :nope], jnp.zeros((rk, MLA_HEADS, HEAD_SLAB - nope), F32)],
                         axis=2).reshape(rk, MLA_HEADS * HEAD_SLAB)
    wv = w_ukv[:, :, nope:].reshape(rk, MLA_HEADS * V_HEAD_DIM)

    inv_freq = 1.0 / (ROPE_THETA ** (jnp.arange(half, dtype=F32) / half))
    pad_tail = jnp.zeros((HEAD_SLAB - QK_HEAD_DIM,), F32)
    freq_row = jnp.concatenate([jnp.zeros((nope,), F32), inv_freq, inv_freq, pad_tail])

    def gain_rows(gv, scale):
        plain = jnp.concatenate([gv, pad_tail]) * scale
        swapped = jnp.concatenate([jnp.zeros((nope,), F32), gv[nope + half:], gv[nope:nope + half], pad_tail]) * scale
        return plain, swapped

    gq_plain, gq_sw = gain_rows(q_head_g, QK_HEAD_DIM ** -0.5 * math.log2(math.e))
    gk_plain, gk_sw = gain_rows(k_head_g, 1.0)
    freq_packed = jnp.tile(jnp.concatenate([inv_freq, inv_freq]), LANES // QK_ROPE_DIM)
    tab = jnp.stack([freq_row, gq_plain, gq_sw, gk_plain, gk_sw, freq_packed, freq_row * 0, freq_row * 0])
    return {
        "w_in": w_in.astype(BF16), "gq": q_norm_g.reshape(1, -1), "wq": wq.astype(BF16),
        "gkv": kv_norm_g.reshape(1, -1), "wk": wk.astype(BF16), "wv": wv.astype(BF16), "tab": tab,
    }


def _attn_kernel(qi_ref, kj_ref, q_ref, k_ref, v_ref, o_ref, m_sc, a_sc, acc_sc, s_sc, p_sc, *, tq, tk):
    step = pl.program_id(2)
    i = qi_ref[step]
    j = kj_ref[step]
    sum_lane = (V_HEAD_DIM, 0)
    subs = tq // tk

    @pl.when(j == 0)
    def _():
        m_sc[...] = jnp.full(m_sc.shape, -jnp.inf, F32)
        acc_sc[...] = jnp.zeros(acc_sc.shape, F32)

    def sweep(diag_sub):
        lo = 0 if diag_sub is None else diag_sub * tk
        live = pl.ds(lo, tq - lo)
        lane = lax.broadcasted_iota(jnp.int32, (1, LANES), 1)
        for hh in range(ATTN_HEADS):
            s_sc[hh, live, :] = lax.dot_general(q_ref[hh, live, :], k_ref[hh], (((1,), (1,)), ((), ())),
                                                preferred_element_type=F32)
        for hh in range(ATTN_HEADS):
            v = v_ref[:, (hh // 2) * LANES:(hh // 2 + 1) * LANES]
            for r0 in range(lo, tq, ATTN_ROWS):
                rows = pl.ds(r0, ATTN_ROWS)
                s = s_sc[hh, rows, :]
                if diag_sub is not None and r0 < lo + tk:
                    row = (r0 - lo) + lax.broadcasted_iota(jnp.int32, (ATTN_ROWS, tk), 0)
                    col = lax.broadcasted_iota(jnp.int32, (ATTN_ROWS, tk), 1)
                    s = jnp.where(col <= row, s, -jnp.inf)
                m_prev = m_sc[hh, rows, :]
                m_new = jnp.maximum(m_prev, jnp.max(s, axis=-1, keepdims=True))
                a_sc[hh, rows, :] = jnp.exp2(m_prev - m_new)
                m_sc[hh, rows, :] = m_new
                shifted = s - jnp.concatenate([m_new] * (tk // LANES), axis=1)
                p_sc[hh, rows, :] = jnp.exp2(shifted.astype(BF16))
            own = (lane < V_HEAD_DIM) == (hh % 2 == 0)
            ones = jnp.where(lane == sum_lane[hh % 2], 1.0, 0.0).astype(v.dtype)
            vh = jnp.where(own, v, jnp.broadcast_to(ones, v.shape))
            acc_sc[hh, live, :] = (acc_sc[hh, live, :] * a_sc[hh, live, :]
                                   + jnp.dot(p_sc[hh, live, :], vh, preferred_element_type=F32))

    diag = j - subs * i

    @pl.when(diag < 0)
    def _():
        sweep(None)

    for ds_ in range(subs):
        @pl.when(diag == ds_)
        def _(ds_=ds_):
            sweep(ds_)

    @pl.when(diag == subs - 1)
    def _():
        lane = lax.broadcasted_iota(jnp.int32, (1, LANES), 1)
        for pp in range(ATTN_HEADS // 2):
            acc0 = acc_sc[2 * pp]
            acc1 = acc_sc[2 * pp + 1]
            l0 = acc0[:, sum_lane[0]:sum_lane[0] + 1]
            l1 = acc1[:, sum_lane[1]:sum_lane[1] + 1]
            o_ref[:, pp * LANES:(pp + 1) * LANES] = jnp.where(lane < V_HEAD_DIM, acc0 / l0,
                                                              acc1 / l1).astype(o_ref.dtype)


def _attn_call(q, k, v):
    bsz, nh, seq, _ = q.shape
    tk = min(ATTN_TILE, seq)
    tq = min(ATTN_Q_BLOCKS * tk, seq)
    subs = tq // tk
    nq = seq // tq
    pairs = [(i, j) for i in range(nq) for j in range(subs * (i + 1))]
    qi = jnp.asarray([p[0] for p in pairs], jnp.int32)
    kj = jnp.asarray([p[1] for p in pairs], jnp.int32)
    kern = functools.partial(_attn_kernel, tq=tq, tk=tk)
    hp = ATTN_HEADS
    assert nh % hp == 0
    return pl.pallas_call(
        kern,
        grid_spec=pltpu.PrefetchScalarGridSpec(
            num_scalar_prefetch=2,
            grid=(bsz, nh // hp, len(pairs)),
            in_specs=[
                pl.BlockSpec((None, hp, tq, HEAD_SLAB), lambda b, h, p, qi, kj: (b, h, qi[p], 0)),
                pl.BlockSpec((None, hp, tk, HEAD_SLAB), lambda b, h, p, qi, kj: (b, h, kj[p], 0)),
                pl.BlockSpec((None, tk, hp * V_HEAD_DIM), lambda b, h, p, qi, kj: (b, kj[p], h)),
            ],
            out_specs=pl.BlockSpec((None, tq, hp * V_HEAD_DIM), lambda b, h, p, qi, kj: (b, qi[p], h)),
            scratch_shapes=[pltpu.VMEM((hp, tq, LANES), F32), pltpu.VMEM((hp, tq, LANES), F32),
                            pltpu.VMEM((hp, tq, LANES), F32),
                            pltpu.VMEM((hp, tq, tk), F32), pltpu.VMEM((hp, tq, tk), BF16)],
        ),
        out_shape=jax.ShapeDtypeStruct((bsz, seq, nh * V_HEAD_DIM), BF16),
        compiler_params=pltpu.CompilerParams(
            dimension_semantics=("parallel", "parallel", "arbitrary"),
            vmem_limit_bytes=VMEM_LIMIT),
        name="mla_attention",
    )(qi, kj, q, k, v)


def _s5_disc_kernel(are_ref, aim_ref, ldt_ref, bre_ref, bim_ref, abre_ref, abim_ref, bbre_ref, bbim_ref):
    dt = jnp.exp(ldt_ref[...])
    lam_re = jnp.minimum(are_ref[...], -1e-4)
    lam_im = aim_ref[...]
    mag = jnp.exp(lam_re * dt)
    ab_re = mag * jnp.cos(lam_im * dt)
    ab_im = mag * jnp.sin(lam_im * dt)
    den = lam_re * lam_re + lam_im * lam_im
    num_re = ab_re - 1.0
    f_re = (num_re * lam_re + ab_im * lam_im) / den
    f_im = (ab_im * lam_re - num_re * lam_im) / den
    abre_ref[...] = ab_re
    abim_ref[...] = ab_im
    br = bre_ref[...]
    bi = bim_ref[...]
    bbre_ref[...] = f_re[:, None, :] * br - f_im[:, None, :] * bi
    bbim_ref[...] = f_re[:, None, :] * bi + f_im[:, None, :] * br


def _s5_disc_call(a_re, a_im, log_dt, b_re, b_im):
    g, p = a_re.shape
    bre_t = jnp.swapaxes(b_re, 1, 2)
    bim_t = jnp.swapaxes(b_im, 1, 2)
    return pl.pallas_call(
        _s5_disc_kernel,
        out_shape=[jax.ShapeDtypeStruct((g, p), F32), jax.ShapeDtypeStruct((g, p), F32),
                   jax.ShapeDtypeStruct(bre_t.shape, F32), jax.ShapeDtypeStruct(bre_t.shape, F32)],
        name="s5_discretize",
    )(a_re, a_im, log_dt.reshape(g, 1), bre_t, bim_t)


def _block_diag_halves(m):
    g, r, c = m.shape
    gh = g // 2
    eye = jnp.eye(gh, dtype=m.dtype)
    mh = m.reshape(2, gh, r, c)
    return (mh[:, :, :, None, :] * eye[None, :, None, :, None]).reshape(2, gh * r, gh * c)


def _s5_kernel(u_ref, bre_ref, bim_ref, are_ref, aim_ref, cre_ref, cim_ref, d_ref, gw_ref, gb_ref,
               o_ref, sre, sim, dre, dim, xbr, xbi, *, steps):
    @pl.when(pl.program_id(0) == 0)
    def _():
        sre[...] = jnp.zeros(sre.shape, F32)
        sim[...] = jnp.zeros(sim.shape, F32)

    rows = steps * SUBLANES
    w = u_ref.shape[-1]
    u = u_ref[...].reshape(rows, w)
    ub = u.astype(BF16)
    kh = w // 2
    nh = dre.shape[1] // 2
    for hf in range(2):
        dre[:, hf * nh:(hf + 1) * nh] = jnp.dot(ub[:, hf * kh:(hf + 1) * kh], bre_ref[hf], preferred_element_type=F32)
        dim[:, hf * nh:(hf + 1) * nh] = jnp.dot(ub[:, hf * kh:(hf + 1) * kh], bim_ref[hf], preferred_element_type=F32)

    xr = sre[...]
    xi = sim[...]
    for t in range(0, steps, 2):
        pair_r = []
        pair_i = []
        for r0 in (t * SUBLANES, (t + 1) * SUBLANES):
            nr = are_ref[...] * xr - aim_ref[...] * xi + dre[r0:r0 + SUBLANES, :]
            ni = are_ref[...] * xi + aim_ref[...] * xr + dim[r0:r0 + SUBLANES, :]
            xr, xi = nr, ni
            pair_r.append(nr)
            pair_i.append(ni)
        xbr[t * SUBLANES:(t + 2) * SUBLANES, :] = jnp.concatenate(pair_r, axis=0).astype(BF16)
        xbi[t * SUBLANES:(t + 2) * SUBLANES, :] = jnp.concatenate(pair_i, axis=0).astype(BF16)
    sre[...] = xr
    sim[...] = xi

    ys = []
    for hf in range(2):
        yr = jnp.dot(xbr[:, hf * nh:(hf + 1) * nh], cre_ref[hf], preferred_element_type=F32)
        yi = jnp.dot(xbi[:, hf * nh:(hf + 1) * nh], cim_ref[hf], preferred_element_type=F32)
        ys.append(yr - yi)
    y = jnp.concatenate(ys, axis=1) + d_ref[...] * u
    g = _gelu(y)
    out = g * _sigmoid(_bdot(g, gw_ref[...]) + gb_ref[...])
    o_ref[...] = out.reshape(steps, SUBLANES, w).astype(o_ref.dtype)


def _s5_call(u_t, disc, c_re, c_im, d_skip, glu_w, glu_b):
    seq, bsz, w = u_t.shape
    assert bsz == SUBLANES
    ab_re, ab_im, bb_re, bb_im = disc
    g, p = ab_re.shape
    n_state = g * p
    bre = _block_diag_halves(bb_re).astype(BF16)
    bim = _block_diag_halves(bb_im).astype(BF16)
    cre = _block_diag_halves(jnp.swapaxes(c_re, 1, 2)).astype(BF16)
    cim = _block_diag_halves(jnp.swapaxes(c_im, 1, 2)).astype(BF16)
    steps = min(S5_STEPS, seq)
    full = lambda a: pl.BlockSpec(a.shape, lambda s: (0,) * a.ndim)
    rep = lambda a: jnp.broadcast_to(a.reshape(1, n_state), (bsz, n_state))
    args = (bre, bim, rep(ab_re), rep(ab_im), cre, cim,
            d_skip.reshape(1, w), glu_w.astype(BF16), glu_b.reshape(1, w))
    return pl.pallas_call(
        functools.partial(_s5_kernel, steps=steps),
        grid=(seq // steps,),
        in_specs=[pl.BlockSpec((steps, bsz, w), lambda s: (s, 0, 0))] + [full(a) for a in args],
        out_specs=pl.BlockSpec((steps, bsz, w), lambda s: (s, 0, 0)),
        out_shape=jax.ShapeDtypeStruct((seq, bsz, w), BF16),
        scratch_shapes=[pltpu.VMEM((bsz, n_state), F32), pltpu.VMEM((bsz, n_state), F32),
                        pltpu.VMEM((steps * bsz, n_state), F32), pltpu.VMEM((steps * bsz, n_state), F32),
                        pltpu.VMEM((steps * bsz, n_state), BF16), pltpu.VMEM((steps * bsz, n_state), BF16)],
        compiler_params=pltpu.CompilerParams(dimension_semantics=("arbitrary",), vmem_limit_bytes=VMEM_LIMIT),
        name="s5_scan",
    )(u_t, *args)


def _router_tail(x_new, mod_ref, gf_ref, rw_ref, rb_ref, h2_ref, te_ref, tet_ref, cnt_ref):
    h2 = _mod_norm(x_new, gf_ref[...], mod_ref[4:5, :], mod_ref[3:4, :])
    h2_ref[...] = _pack_pairs(h2)
    h_hi = h2.astype(BF16)
    h_lo = (h2 - h_hi.astype(F32)).astype(BF16)
    r_hi = jnp.dot(h_hi, rw_ref[...], preferred_element_type=F32)
    r_lo = jnp.dot(h_lo, rw_ref[...], preferred_element_type=F32)
    logits = r_hi[:, :LANES] + r_hi[:, LANES:] + r_lo[:, :LANES] + rb_ref[...]
    lane = lax.broadcasted_iota(jnp.int32, logits.shape, 1).astype(F32)
    vals = []
    idxs = []
    work = logits
    for _ in range(TOP_K):
        m = jnp.max(work, axis=-1, keepdims=True)
        idx = jnp.min(jnp.where(work == m, lane, float(LANES)), axis=-1, keepdims=True)
        vals.append(m)
        idxs.append(idx)
        work = jnp.where(lane == idx, NEG_BIG * 2.0, work)
    exps = [jnp.exp(vv - vals[0]) for vv in vals]
    tot = exps[0] + exps[1] + exps[2] + exps[3]
    te = jnp.zeros(logits.shape, F32)
    picked = jnp.zeros(logits.shape, F32)
    for kk in range(TOP_K):
        te = jnp.where(lane == float(kk), idxs[kk], te)
        te = jnp.where(lane == float(TOP_K + kk), exps[kk] / tot, te)
        picked = picked + jnp.where(lane == idxs[kk], 1.0, 0.0)
    te_ref[...] = te[:, :2 * TOP_K]
    tet_ref[...] = te.T[:2 * TOP_K, :]
    cnt_ref[...] = jnp.sum(picked, axis=0, keepdims=True)


def _mix_out_kernel(x_ref, a_ref, s_ref, mod_ref, wo_ref, gf_ref, rw_ref, rb_ref, xo_ref, h2_ref, te_ref,
                    tet_ref, cnt_ref):
    ka = a_ref.shape[-1]
    mix = jnp.dot(a_ref[...], wo_ref[:ka, :], preferred_element_type=F32)
    mix = mix + jnp.dot(s_ref[...], wo_ref[ka:, :], preferred_element_type=F32)
    x_new = x_ref[...] + mod_ref[2:3, :] * mix
    xo_ref[...] = x_new
    _router_tail(x_new, mod_ref, gf_ref, rw_ref, rb_ref, h2_ref, te_ref, tet_ref, cnt_ref)


def _tail_out_specs(bsz, seq, tm, d):
    nt = seq // tm
    specs = [
        pl.BlockSpec((None, tm, d), lambda b, s: (b, s, 0)),
        pl.BlockSpec((tm, d // 2), lambda b, s: (b * nt + s, 0)),
        pl.BlockSpec((tm, 2 * TOP_K), lambda b, s: (b * nt + s, 0)),
        pl.BlockSpec((2 * TOP_K, tm), lambda b, s: (0, b * nt + s)),
        pl.BlockSpec((None, 1, LANES), lambda b, s: (b * nt + s, 0, 0)),
    ]
    shapes = [
        jax.ShapeDtypeStruct((bsz, seq, d), F32),
        jax.ShapeDtypeStruct((bsz * seq, d // 2), jnp.uint32),
        jax.ShapeDtypeStruct((bsz * seq, 2 * TOP_K), F32),
        jax.ShapeDtypeStruct((2 * TOP_K, bsz * seq), F32),
        jax.ShapeDtypeStruct((bsz * nt, 1, LANES), F32),
    ]
    return specs, shapes


def _router_pad(router_w, router_b):
    d, e = router_w.shape
    rw = jnp.concatenate([router_w, jnp.zeros((d, LANES - e), F32)], axis=1)
    rw_hi = rw.astype(BF16)
    rw_lo = (rw - rw_hi.astype(F32)).astype(BF16)
    rb = jnp.concatenate([router_b, jnp.full((LANES - e,), NEG_BIG, F32)]).reshape(1, LANES)
    return jnp.concatenate([rw_hi, rw_lo], axis=1), rb


def _mix_out_call(x, attn, ssm_t, mod, w_out, gf, rw, rb):
    bsz, seq, d = x.shape
    tm = min(ROW_TILE, seq)
    ka = attn.shape[-1]
    ks = ssm_t.shape[-1] // bsz
    full = lambda a: pl.BlockSpec(a.shape, lambda b, s: (0,) * a.ndim)
    out_specs, out_shape = _tail_out_specs(bsz, seq, tm, d)
    return pl.pallas_call(
        _mix_out_kernel,
        grid=(bsz, seq // tm),
        in_specs=[
            pl.BlockSpec((None, tm, d), lambda b, s: (b, s, 0)),
            pl.BlockSpec((None, tm, ka), lambda b, s: (b, s, 0)),
            pl.BlockSpec((tm, ks), lambda b, s: (s, b)),
            pl.BlockSpec((None, 6, d), lambda b, s: (b, 0, 0)),
            full(w_out), full(gf), full(rw), full(rb),
        ],
        out_specs=out_specs,
        out_shape=out_shape,
        compiler_params=pltpu.CompilerParams(dimension_semantics=("parallel", "parallel"),
                                             vmem_limit_bytes=VMEM_LIMIT),
        name="even_out",
    )(x, attn, ssm_t, mod, w_out, gf, rw, rb)


def _moe_sum(yg_ref, te_ref):
    te = te_ref[...]
    acc_lo = acc_hi = None
    for kk in range(TOP_K):
        lo, hi = _unpack_pairs_f32(yg_ref[kk])
        gate = te[:, TOP_K + kk:TOP_K + kk + 1]
        acc_lo = gate * lo if kk == 0 else acc_lo + gate * lo
        acc_hi = gate * hi if kk == 0 else acc_hi + gate * hi
    return jnp.concatenate([acc_lo, acc_hi], axis=1)


def _odd_kernel(x_ref, yg_ref, tep_ref, modp_ref, mod_ref, g_ref, win_ref, icnt_ref, wp_ref, ps_ref, gv_ref,
                wsp_ref, bsp_ref, wo_ref, gf_ref, rw_ref, rb_ref, xo_ref, h2_ref, te_ref, tet_ref, cnt_ref, ext_sc):
    tm = x_ref.shape[0]
    pw = wp_ref.shape[-1]
    width = pw * len(POOL_WINDOWS)

    @pl.when(pl.program_id(1) == 0)
    def _():
        ext_sc[0:POOL_HALO, :] = jnp.zeros((POOL_HALO, width), F32)

    x = x_ref[...] + modp_ref[5:6, :] * _moe_sum(yg_ref, tep_ref)
    h = _mod_norm(x, g_ref[...], mod_ref[1:2, :], mod_ref[0:1, :])
    z = _bdot(h, win_ref[...])
    up = z[:, :width]
    ext_sc[POOL_HALO:POOL_HALO + tm, :] = up

    pooled = []
    for gi, win in enumerate(POOL_WINDOWS):
        cols = slice(gi * pw, (gi + 1) * pw)
        acc = up[:, cols]
        for lag in range(1, win):
            acc = acc + ext_sc[POOL_HALO - lag:POOL_HALO - lag + tm, cols]
        pg = acc * icnt_ref[:, gi:gi + 1] - up[:, cols]
        pooled.append(_bdot(pg, wp_ref[gi]) * ps_ref[:, cols])
    ext_sc[0:POOL_HALO, :] = ext_sc[tm:tm + POOL_HALO, :]
    pooled = jnp.concatenate(pooled, axis=1)

    ug = _gelu(z[:, width:2 * width])
    vg = _gelu(z[:, 2 * width:])
    vn = (vg * _rms(vg, width) * gv_ref[...]).astype(BF16)
    hd = width // SGU_HEADS
    chunks = []
    for ci in range(tm // SGU_CHUNK):
        heads = []
        for hh in range(SGU_HEADS):
            blk = vn[ci * SGU_CHUNK:(ci + 1) * SGU_CHUNK, hh * hd:(hh + 1) * hd]
            heads.append(jnp.dot(wsp_ref[hh], blk, preferred_element_type=F32) + bsp_ref[hh])
        chunks.append(jnp.concatenate(heads, axis=1))
    gated = ug * jnp.concatenate(chunks, axis=0)

    mix = _bdot(pooled, wo_ref[:width, :]) + _bdot(gated, wo_ref[width:, :])
    x_new = x + mod_ref[2:3, :] * mix
    xo_ref[...] = x_new
    _router_tail(x_new, mod_ref, gf_ref, rw_ref, rb_ref, h2_ref, te_ref, tet_ref, cnt_ref)


def _odd_call(grp, mod_prev, mod, nb, g, w_in, pool_w, pool_scale, sgu_norm_g, sgu_w, sgu_b, w_out, gf, rw, rb):
    x = grp["x"]
    _, seq, d = x.shape
    tm = min(ODD_TILE, seq)
    nt = seq // tm
    xb0, tok0, mb0 = grp["x_b0"], grp["tok0"], grp["mod_b0"]
    tile0 = tok0 // tm
    yg = grp["yg"]
    width = pool_scale.shape[0]
    hd = width // SGU_HEADS
    t = jnp.arange(seq, dtype=jnp.int32)
    icnt = jnp.stack([1.0 / jnp.minimum(t + 1, wn).astype(F32) for wn in POOL_WINDOWS], axis=1)
    wsp = jnp.tril(sgu_w).astype(BF16)
    bsp = jnp.broadcast_to(sgu_b[:, :, None], (SGU_HEADS, SGU_CHUNK, hd))
    args = (g, w_in.astype(BF16), icnt, pool_w.astype(BF16), pool_scale.reshape(1, width),
            sgu_norm_g.reshape(1, width), wsp, bsp, w_out.astype(BF16), gf, rw, rb)
    full = lambda a: pl.BlockSpec(a.shape, lambda b, s: (0,) * a.ndim)
    in_specs = [pl.BlockSpec((None, tm, d), lambda b, s: (xb0 + b, s, 0)),
                pl.BlockSpec((TOP_K, tm, yg.shape[-1]), lambda b, s: (0, b * nt + s, 0)),
                pl.BlockSpec((tm, 2 * TOP_K), lambda b, s: (tile0 + b * nt + s, 0)),
                pl.BlockSpec((None, 6, d), lambda b, s: (mb0 + b, 0, 0)),
                pl.BlockSpec((None, 6, d), lambda b, s: (mb0 + b, 0, 0))]
    for idx, a in enumerate(args):
        in_specs.append(pl.BlockSpec((tm, len(POOL_WINDOWS)), lambda b, s: (s, 0)) if idx == 2 else full(a))
    out_specs, out_shape = _tail_out_specs(nb, seq, tm, d)
    return pl.pallas_call(
        _odd_kernel,
        grid=(nb, nt),
        in_specs=in_specs,
        out_specs=out_specs,
        out_shape=out_shape,
        scratch_shapes=[pltpu.VMEM((tm + POOL_HALO, width), F32)],
        compiler_params=pltpu.CompilerParams(dimension_semantics=("parallel", "arbitrary"),
                                             vmem_limit_bytes=VMEM_LIMIT),
        name="odd_mixer",
    )(x, yg, grp["te"], mod_prev, mod, *args)


def _dest_kernel(tet_ref, base_ref, tri_ref, dst_ref):
    tr = tet_ref.shape[1]
    tet = tet_ref[...]
    expert = lax.broadcasted_iota(jnp.int32, (LANES, tr), 0).astype(F32)
    hots = [tet[kk:kk + 1, :] == expert for kk in range(TOP_K)]
    oh = jnp.zeros((LANES, tr), F32)
    for hot in hots:
        oh = oh + jnp.where(hot, 1.0, 0.0)
    before = jnp.dot(oh.astype(BF16), tri_ref[...], preferred_element_type=F32) + base_ref[...]
    row = lax.broadcasted_iota(jnp.int32, (2 * TOP_K, tr), 0)
    dst = jnp.zeros((2 * TOP_K, tr), F32)
    for kk, hot in enumerate(hots):
        dst = jnp.where(row == kk, jnp.sum(jnp.where(hot, before, 0.0), axis=0, keepdims=True), dst)
    dst_ref[...] = dst.astype(jnp.int32)


def _dest_call(tet, base, tok0, n_tok):
    tiles = base.shape[0]
    tr = n_tok // tiles
    tile0 = tok0 // tr
    tri = (jnp.arange(tr)[:, None] < jnp.arange(tr)[None, :]).astype(BF16)
    return pl.pallas_call(
        _dest_kernel,
        grid=(tiles,),
        in_specs=[pl.BlockSpec((2 * TOP_K, tr), lambda i: (0, tile0 + i)),
                  pl.BlockSpec((None, LANES, 1), lambda i: (i, 0, 0)),
                  pl.BlockSpec((tr, tr), lambda i: (0, 0))],
        out_specs=pl.BlockSpec((2 * TOP_K, tr), lambda i: (0, i)),
        out_shape=jax.ShapeDtypeStruct((2 * TOP_K, n_tok), jnp.int32),
        compiler_params=pltpu.CompilerParams(dimension_semantics=("parallel",)),
        name="route_dest",
    )(tet, base, tri)


def _sc_gather(table, idx):
    n = idx.shape[0]
    per_w = n // SC_WORKERS
    assert per_w * SC_WORKERS == n and per_w % SC_CHUNK == 0
    n_chunks = per_w // SC_CHUNK
    row_shape = table.shape[1:]
    mesh = plsc.VectorSubcoreMesh(core_axis_name="c", subcore_axis_name="s")

    @functools.partial(
        pl.kernel, mesh=mesh,
        out_type=jax.ShapeDtypeStruct((n,) + row_shape, table.dtype),
        scratch_types=[pltpu.VMEM((SC_CHUNK,), jnp.int32), pltpu.VMEM((SC_CHUNK,) + row_shape, table.dtype),
                       pltpu.SemaphoreType.DMA],
        name="sc_row_gather",
    )
    def gather(table_hbm, idx_hbm, out_hbm, idx_v, rows_v, sem):
        wid = lax.axis_index("s") * 2 + lax.axis_index("c")
        base = wid * per_w

        @pl.loop(0, n_chunks)
        def _(ci):
            off = pl.multiple_of(base + ci * SC_CHUNK, SC_CHUNK)
            pltpu.sync_copy(idx_hbm.at[pl.ds(off, SC_CHUNK)], idx_v)
            pltpu.async_copy(table_hbm.at[idx_v], rows_v, sem).wait()
            pltpu.sync_copy(rows_v, out_hbm.at[pl.ds(off, SC_CHUNK)])

    return gather(table, idx)


def _sc_scatter(rows, dests, n_out, tok0):
    t = dests[0].shape[0]
    per_w = t // SC_WORKERS
    assert per_w * SC_WORKERS == t and per_w % SC_CHUNK == 0
    n_chunks = per_w // SC_CHUNK
    row_shape = rows.shape[1:]
    nk = len(dests)
    mesh = plsc.VectorSubcoreMesh(core_axis_name="c", subcore_axis_name="s")

    @functools.partial(
        pl.kernel, mesh=mesh,
        out_type=jax.ShapeDtypeStruct((n_out,) + row_shape, rows.dtype),
        scratch_types=[pltpu.VMEM((SC_CHUNK,), jnp.int32)] * nk
        + [pltpu.VMEM((SC_CHUNK,) + row_shape, rows.dtype), pltpu.SemaphoreType.DMA],
        name="sc_row_scatter",
    )
    def scatter(rows_hbm, *rest):
        dest_hbm = rest[:nk]
        out_hbm = rest[nk]
        idx_v = rest[nk + 1:2 * nk + 1]
        rows_v, sem = rest[2 * nk + 1:]
        wid = lax.axis_index("s") * 2 + lax.axis_index("c")
        base = wid * per_w

        @pl.loop(0, n_chunks)
        def _(ci):
            off = pl.multiple_of(base + ci * SC_CHUNK, SC_CHUNK)
            src = pl.multiple_of(tok0 + off, SC_CHUNK)
            pltpu.sync_copy(rows_hbm.at[pl.ds(src, SC_CHUNK)], rows_v)
            for kk in range(nk):
                pltpu.sync_copy(dest_hbm[kk].at[pl.ds(off, SC_CHUNK)], idx_v[kk])
            copies = [pltpu.async_copy(rows_v, out_hbm.at[idx_v[kk]], sem) for kk in range(nk)]
            for cp in copies:
                cp.wait()

    return scatter(rows, *dests)


def _expert_kernel(be_ref, nv_ref, ord_ref, ue_ref, nu_ref, x_ref, wgu_hbm, bgu_ref, wdn_hbm, bdn_ref, y_ref,
                   wgu_f32, wdn_f32, wgu_bf, wdn_bf, sem, *, layer):
    i = pl.program_id(0)
    used = i < nu_ref[0]
    pos = ord_ref[i]
    fresh = jnp.logical_or(i == 0, ord_ref[jnp.maximum(i - 1, 0)] != pos)

    def weight_copies(expert):
        return (pltpu.make_async_copy(wgu_hbm.at[layer, expert], wgu_f32, sem.at[0]),
                pltpu.make_async_copy(wdn_hbm.at[layer, expert], wdn_f32, sem.at[1]))

    @pl.when(i == 0)
    def _():
        for cp in weight_copies(ue_ref[0]):
            cp.start()

    @pl.when(jnp.logical_and(used, fresh))
    def _():
        for cp in weight_copies(ue_ref[pos]):
            cp.wait()
        wgu_bf[...] = wgu_f32[...].astype(BF16)
        wdn_bf[...] = wdn_f32[...].astype(BF16)

        @pl.when(pos + 1 < nu_ref[1])
        def _():
            for cp in weight_copies(ue_ref[pos + 1]):
                cp.start()

    def ffn(rows):
        x = jnp.concatenate(_unpack_pairs(x_ref[0:rows, :]), axis=1)
        z = jnp.dot(x, wgu_bf[...], preferred_element_type=F32) + bgu_ref[...]
        ff = z.shape[-1] // 2
        gate = jnp.minimum(z[:, :ff], SWIGLU_LIMIT)
        lin = jnp.clip(z[:, ff:], -SWIGLU_LIMIT, SWIGLU_LIMIT)
        act = gate * _sigmoid(SWIGLU_ALPHA * gate) * (lin + 1.0)
        y = _bdot(act, wdn_bf[...]) + bdn_ref[...]
        y_ref[0:rows, :] = _pack_pairs(y)

    nv = nv_ref[i]
    below = 0
    for size in MOE_PATHS:
        @pl.when(jnp.logical_and(used, jnp.logical_and(nv > below, nv <= size)))
        def _(size=size):
            ffn(size)
        below = size


def _expert_call(layer, block_e, block_valid, block_pos, used_experts, n_used, xs, w_gu, b_gu, w_dn, b_dn):
    n_rows, dh = xs.shape
    depth, e, d, ff2 = w_gu.shape
    ff = ff2 // 2
    nb = n_rows // MOE_ROWS
    row_map = lambda i, be, nv, po, ue, nu: (jnp.minimum(i, nu[0] - 1), 0)
    b_map = lambda i, be, nv, po, ue, nu: (layer, be[i], 0, 0)
    return pl.pallas_call(
        functools.partial(_expert_kernel, layer=layer),
        grid_spec=pltpu.PrefetchScalarGridSpec(
            num_scalar_prefetch=5,
            grid=(nb,),
            in_specs=[
                pl.BlockSpec((MOE_ROWS, dh), row_map),
                pl.BlockSpec(memory_space=pl.ANY),
                pl.BlockSpec((None, None, 1, ff2), b_map),
                pl.BlockSpec(memory_space=pl.ANY),
                pl.BlockSpec((None, None, 1, d), b_map),
            ],
            out_specs=pl.BlockSpec((MOE_ROWS, dh), row_map),
            scratch_shapes=[pltpu.VMEM((d, ff2), F32), pltpu.VMEM((ff, d), F32),
                            pltpu.VMEM((d, ff2), BF16), pltpu.VMEM((ff, d), BF16),
                            pltpu.SemaphoreType.DMA((2,))],
        ),
        out_shape=jax.ShapeDtypeStruct((n_rows, dh), jnp.uint32),
        compiler_params=pltpu.CompilerParams(dimension_semantics=("arbitrary",), vmem_limit_bytes=VMEM_LIMIT),
        name="moe_experts",
    )(block_e, block_valid, block_pos, used_experts, n_used, xs, w_gu, b_gu.reshape(depth, e, 1, ff2), w_dn,
      b_dn.reshape(depth, e, 1, d))


def _combine_kernel(x_ref, yg_ref, te_ref, mod_ref, *rest):
    o_ref = rest[-1]
    o_ref[...] = x_ref[...] + mod_ref[5:6, :] * _moe_sum(yg_ref, te_ref)


def _combine_call(grp, mod, prev, bsz, nb):
    x = grp["x"]
    _, seq, d = x.shape
    tm = min(ROW_TILE, seq)
    nt = seq // tm
    xb0, mb0 = grp["x_b0"], grp["mod_b0"]
    tile0 = grp["tok0"] // tm
    yg = grp["yg"]
    in_specs = [
        pl.BlockSpec((None, tm, d), lambda b, s: (xb0 + b, s, 0)),
        pl.BlockSpec((TOP_K, tm, yg.shape[-1]), lambda b, s: (0, b * nt + s, 0)),
        pl.BlockSpec((tm, 2 * TOP_K), lambda b, s: (tile0 + b * nt + s, 0)),
        pl.BlockSpec((None, 6, d), lambda b, s: (mb0 + b, 0, 0)),
    ]
    args = [x, yg, grp["te"], mod]
    aliases = {}
    if prev is not None:
        in_specs.append(pl.BlockSpec(memory_space=pl.ANY))
        args.append(prev)
        aliases = {len(args) - 1: 0}
    return pl.pallas_call(
        _combine_kernel,
        grid=(nb, nt),
        in_specs=in_specs,
        out_specs=pl.BlockSpec((None, tm, d), lambda b, s: (mb0 + b, s, 0)),
        out_shape=jax.ShapeDtypeStruct((bsz, seq, d), F32),
        input_output_aliases=aliases,
        compiler_params=pltpu.CompilerParams(dimension_semantics=("parallel", "parallel"),
                                             vmem_limit_bytes=VMEM_LIMIT),
        name="moe_combine",
    )(*args)


def _moe_rows(layer, grp, gt, w_gu, b_gu, w_dn, b_dn):
    h2 = grp["h2"]
    dh = h2.shape[-1]
    n_rows = -(-(gt * TOP_K + N_EXPERTS * (MOE_ROWS - 1)) // MOE_ROWS) * MOE_ROWS
    nb = n_rows // MOE_ROWS
    first_row = jnp.arange(nb, dtype=jnp.int32) * MOE_ROWS
    upto = jnp.arange(LANES)[:, None] <= jnp.arange(LANES)[None, :]
    g_cnt = grp["cnt"][:, 0, :].astype(jnp.int32)
    counts = jnp.sum(g_cnt, axis=0)
    padded = (counts + MOE_ROWS - 1) // MOE_ROWS * MOE_ROWS
    pad_end = jnp.sum(jnp.where(upto, padded[:, None], 0), axis=0)
    pad_start = pad_end - padded
    tile_base = pad_start[None, :] + jnp.cumsum(g_cnt, axis=0) - g_cnt
    dest = _dest_call(grp["tet"], tile_base.astype(F32)[:, :, None], grp["tok0"], gt)
    dests = [dest[kk] for kk in range(TOP_K)]
    block_e = jnp.minimum(jnp.sum(pad_end[None, :N_EXPERTS] <= first_row[:, None], axis=1),
                          N_EXPERTS - 1).astype(jnp.int32)
    valid_end = (pad_start + counts)[block_e]
    block_valid = jnp.clip(valid_end - first_row, 0, MOE_ROWS).astype(jnp.int32)
    owns = counts[:N_EXPERTS] > 0
    expert_pos = jnp.cumsum(owns.astype(jnp.int32)) - 1
    slot = jnp.arange(N_EXPERTS, dtype=jnp.int32)
    used_experts = jnp.sum(jnp.where(owns[None, :] & (expert_pos[None, :] == slot[:, None]),
                                     slot[None, :], 0), axis=1).astype(jnp.int32)
    block_pos = expert_pos[block_e].astype(jnp.int32)
    n_used = jnp.stack([pad_end[N_EXPERTS - 1] // MOE_ROWS, jnp.sum(owns)]).astype(jnp.int32)
    xs = _sc_scatter(h2, dests, n_rows, grp["tok0"])
    y = _expert_call(layer, block_e, block_valid, block_pos, used_experts, n_used, xs, w_gu, b_gu, w_dn, b_dn)
    return _sc_gather(y, dest[:TOP_K].reshape(-1)).reshape(TOP_K, gt, dh)


def kernel(x, c, positions, ada_w, ada_b, norm_mix_g, norm_ffn_g, router_w, router_b, moe_w_gu, moe_b_gu,
           moe_w_dn, moe_b_dn, even_w_in, mla_q_norm_g, mla_w_uq, mla_kv_norm_g, mla_w_ukv, mla_q_head_g,
           mla_k_head_g, s5_a_re, s5_a_im, s5_log_dt, s5_b_re, s5_b_im, s5_c_re, s5_c_im, s5_d, s5_glu_w,
           s5_glu_b, even_w_out, odd_w_in, pool_w, pool_scale, sgu_norm_g, sgu_w, sgu_b, odd_w_out):
    bsz, seq, d = x.shape
    depth = ada_w.shape[0]
    mods = _ada_call(c, ada_w, ada_b).reshape(depth, bsz, 6, d)
    posf = positions.astype(F32).reshape(bsz, seq, 1)
    splits = MOE_SPLITS if bsz % MOE_SPLITS == 0 else 1
    gb = bsz // splits
    gt = gb * seq
    tiles_g = gt // min(ROW_TILE, seq)

    def settle(groups, mod):
        out = None
        for grp in groups:
            out = _combine_call(grp, mod, out, bsz, gb)
        return out

    pending = None
    for layer in range(depth):
        mod = mods[layer]
        i = layer // 2
        g_mix = norm_mix_g[layer].reshape(1, d)
        g_ffn = norm_ffn_g[layer].reshape(1, d)
        rw, rb = _router_pad(router_w[layer], router_b[layer])
        if layer % 2 == 0:
            if pending is not None:
                x = settle(pending, mods[layer - 1])
            prep = _prep_even(even_w_in[i], mla_q_norm_g[i], mla_w_uq[i], mla_kv_norm_g[i], mla_w_ukv[i],
                              mla_q_head_g[i], mla_k_head_g[i])
            q, k, v, u_t = _even_in_call(x, mod, posf, g_mix, prep)
            attn = _attn_call(q, k, v)
            disc = _s5_disc_call(s5_a_re[i], s5_a_im[i], s5_log_dt[i], s5_b_re[i], s5_b_im[i])
            ssm_t = _s5_call(u_t.reshape(seq, bsz, d // 2), disc, s5_c_re[i], s5_c_im[i], s5_d[i],
                             s5_glu_w[i], s5_glu_b[i])
            x_new, h2, te, tet, tile_cnt = _mix_out_call(x, attn, ssm_t.reshape(seq, bsz * (d // 2)), mod,
                                                         even_w_out[i].astype(BF16), g_ffn, rw, rb)
            groups = [dict(x=x_new, x_b0=gi * gb, h2=h2, te=te, tet=tet, tok0=gi * gt, mod_b0=gi * gb,
                           cnt=tile_cnt[gi * tiles_g:(gi + 1) * tiles_g]) for gi in range(splits)]
        else:
            groups = []
            for gi, grp in enumerate(pending):
                x_new, h2, te, tet, tile_cnt = _odd_call(grp, mods[layer - 1], mod, gb, g_mix, odd_w_in[i],
                                                         pool_w[i], pool_scale[i], sgu_norm_g[i], sgu_w[i],
                                                         sgu_b[i], odd_w_out[i], g_ffn, rw, rb)
                groups.append(dict(x=x_new, x_b0=0, h2=h2, te=te, tet=tet, tok0=0, mod_b0=gi * gb,
                                   cnt=tile_cnt))
        for grp in groups:
            grp["yg"] = _moe_rows(layer, grp, gt, moe_w_gu, moe_b_gu, moe_w_dn, moe_b_dn)
        pending = groups
    return settle(pending, mods[depth - 1])
```

```python
import functools
import math

import jax
import jax.numpy as jnp
from jax import lax
from jax.experimental import pallas as pl
from jax.experimental.pallas import tpu as pltpu
from jax.experimental.pallas import tpu_sc as plsc

F32 = jnp.float32
BF16 = jnp.bfloat16
HIGHEST = lax.Precision.HIGHEST

NORM_EPS = 1e-6
MLA_HEADS = 8
QK_NOPE_DIM = 64
QK_ROPE_DIM = 32
QK_HEAD_DIM = QK_NOPE_DIM + QK_ROPE_DIM
V_HEAD_DIM = 64
Q_LORA_RANK = 256
KV_LORA_RANK = 128
ROPE_THETA = 10000.0
POOL_WINDOWS = (2, 4, 8, 16)
SGU_HEADS = 4
SGU_CHUNK = 128
N_EXPERTS = 32
TOP_K = 4
SWIGLU_ALPHA = 1.702
SWIGLU_LIMIT = 7.0

LANES = 128
SUBLANES = 8
HEAD_SLAB = LANES
POOL_HALO = 16
ROW_TILE = 1024
ODD_TILE = 512
ATTN_TILE = 512
ATTN_Q_BLOCKS = 2
ATTN_ROWS = 32
ATTN_HEADS = 4
S5_STEPS = 64
MOE_ROWS = 1024
MOE_PATHS = (128, 256, 512, 768, MOE_ROWS)
MOE_SPLITS = 2
SC_CORES = 2
SC_WORKERS = SC_CORES * 16
SC_CHUNK = 64
ADA_COLS = 1536
VMEM_LIMIT = 56 * 1024 * 1024
NEG_BIG = -1e30


def _sigmoid(v):
    return 1.0 / (1.0 + jnp.exp(-v))


def _gelu(v):
    return 0.5 * v * (1.0 + jnp.tanh(math.sqrt(2.0 / math.pi) * (v + 0.044715 * (v * v * v))))


def _rms(v, width):
    return lax.rsqrt(jnp.sum(v * v, axis=-1, keepdims=True) * (1.0 / width) + NORM_EPS)


def _mod_norm(x, g, sc, sh):
    return x * _rms(x, x.shape[-1]) * (g * (1.0 + sc)) + sh


def _bdot(a, b):
    return jnp.dot(a.astype(BF16), b, preferred_element_type=F32)


def _pack_pairs(v):
    w = v.shape[-1] // 2
    bits = pltpu.bitcast(v.astype(BF16).astype(F32), jnp.uint32)
    return (bits[:, :w] >> 16) | bits[:, w:]


def _unpack_pairs_f32(p):
    lo = pltpu.bitcast(p << 16, F32)
    hi = pltpu.bitcast(p & jnp.uint32(0xFFFF0000), F32)
    return lo, hi


def _unpack_pairs(p):
    lo, hi = _unpack_pairs_f32(p)
    return lo.astype(BF16), hi.astype(BF16)


def _ada_kernel(c_ref, w_ref, b_ref, o_ref):
    c = c_ref[...]
    act = c * _sigmoid(c)
    o_ref[...] = jnp.dot(act, w_ref[...], precision=HIGHEST, preferred_element_type=F32) + b_ref[...]


def _ada_call(c, ada_w, ada_b):
    depth, d, n = ada_w.shape
    bsz = c.shape[0]
    tn = ADA_COLS
    return pl.pallas_call(
        _ada_kernel,
        grid=(depth, n // tn),
        in_specs=[
            pl.BlockSpec((bsz, d), lambda l, j: (0, 0)),
            pl.BlockSpec((None, d, tn), lambda l, j: (l, 0, j)),
            pl.BlockSpec((None, 1, tn), lambda l, j: (l, 0, j)),
        ],
        out_specs=pl.BlockSpec((None, bsz, tn), lambda l, j: (l, 0, j)),
        out_shape=jax.ShapeDtypeStruct((depth, bsz, n), F32),
        compiler_params=pltpu.CompilerParams(dimension_semantics=("parallel", "parallel"),
                                             vmem_limit_bytes=VMEM_LIMIT),
        name="ada_mod",
    )(c, ada_w, ada_b.reshape(depth, 1, n))


_C_Q = 0
_C_KV = Q_LORA_RANK
_C_PE = _C_KV + KV_LORA_RANK
_C_PESW = _C_PE + HEAD_SLAB
_C_U = _C_PESW + HEAD_SLAB


def _even_in_kernel(x_ref, mod_ref, pos_ref, g_ref, win_ref, gq_ref, wq_ref, gkv_ref, wk_ref, wv_ref,
                    tab_ref, q_ref, k_ref, v_ref, u_ref):
    x = x_ref[...]
    h = _mod_norm(x, g_ref[...], mod_ref[1:2, :], mod_ref[0:1, :])
    z = _bdot(h, win_ref[...])
    q_c = z[:, _C_Q:_C_KV]
    kv_c = z[:, _C_KV:_C_PE]
    kpe = z[:, _C_PE:_C_PESW]
    kpe_sw = z[:, _C_PESW:_C_U]
    u_ref[...] = z[:, _C_U:]

    tm = x.shape[0]
    packs = LANES // QK_ROPE_DIM
    qrows = tm // packs
    lane = lax.broadcasted_iota(jnp.int32, (1, LANES), 1)
    packed = jnp.zeros((qrows, LANES), F32)
    for part in range(packs):
        in_part = (lane >= part * QK_ROPE_DIM) & (lane < (part + 1) * QK_ROPE_DIM)
        packed = jnp.where(in_part, pos_ref[part * qrows:(part + 1) * qrows, :], packed)
    ang = packed * tab_ref[5:6, :]
    cs_p = jnp.cos(ang)
    sn_p = jnp.sin(ang)
    on_rope = (lane >= QK_NOPE_DIM) & (lane < QK_HEAD_DIM)
    cs_parts = []
    sn_parts = []
    for part in range(packs):
        shift = (QK_NOPE_DIM - part * QK_ROPE_DIM) % LANES
        cs_r = cs_p if shift == 0 else pltpu.roll(cs_p, shift, axis=1)
        sn_r = sn_p if shift == 0 else pltpu.roll(sn_p, shift, axis=1)
        cs_parts.append(jnp.where(on_rope, cs_r, 1.0))
        sn_parts.append(jnp.where(on_rope, sn_r, 0.0))
    cs = jnp.concatenate(cs_parts, axis=0)
    sn = jnp.concatenate(sn_parts, axis=0)
    gcq = cs * tab_ref[1:2, :]
    gsq = sn * tab_ref[2:3, :]
    gck = cs * tab_ref[3:4, :]
    gsk = sn * tab_ref[4:5, :]

    qn = q_c * _rms(q_c, Q_LORA_RANK) * gq_ref[...]
    qq = _bdot(qn, wq_ref[...])
    kvn = kv_c * _rms(kv_c, KV_LORA_RANK) * gkv_ref[...]
    kk = _bdot(kvn, wk_ref[...])
    v_ref[...] = _bdot(kvn, wv_ref[...]).astype(v_ref.dtype)

    pe_rot = kpe * gck + kpe_sw * gsk
    pe_ss = jnp.sum(kpe * kpe, axis=-1, keepdims=True)
    hw = MLA_HEADS * HEAD_SLAB
    for hd in range(MLA_HEADS):
        lo = hd * HEAD_SLAB
        qr = qq[:, lo:lo + HEAD_SLAB]
        qs = qq[:, hw + lo:hw + lo + HEAD_SLAB]
        rq = _rms(qr, QK_HEAD_DIM)
        q_ref[hd] = (rq * (qr * gcq + qs * gsq)).astype(q_ref.dtype)
        kr = kk[:, lo:lo + HEAD_SLAB]
        rk = lax.rsqrt((jnp.sum(kr * kr, axis=-1, keepdims=True) + pe_ss) * (1.0 / QK_HEAD_DIM) + NORM_EPS)
        k_ref[hd] = (rk * (kr * gck + pe_rot)).astype(k_ref.dtype)


def _even_in_call(x, mod, posf, g, prep):
    bsz, seq, d = x.shape
    tm = min(ROW_TILE, seq)
    full = lambda a: pl.BlockSpec(a.shape, lambda b, s: (0,) * a.ndim)
    return pl.pallas_call(
        _even_in_kernel,
        grid=(bsz, seq // tm),
        in_specs=[
            pl.BlockSpec((None, tm, d), lambda b, s: (b, s, 0)),
            pl.BlockSpec((None, 6, d), lambda b, s: (b, 0, 0)),
            pl.BlockSpec((None, tm, 1), lambda b, s: (b, s, 0)),
            full(g), full(prep["w_in"]), full(prep["gq"]), full(prep["wq"]), full(prep["gkv"]),
            full(prep["wk"]), full(prep["wv"]), full(prep["tab"]),
        ],
        out_specs=[
            pl.BlockSpec((None, MLA_HEADS, tm, HEAD_SLAB), lambda b, s: (b, 0, s, 0)),
            pl.BlockSpec((None, MLA_HEADS, tm, HEAD_SLAB), lambda b, s: (b, 0, s, 0)),
            pl.BlockSpec((None, tm, MLA_HEADS * V_HEAD_DIM), lambda b, s: (b, s, 0)),
            pl.BlockSpec((tm, d // 2), lambda b, s: (s, b)),
        ],
        out_shape=[
            jax.ShapeDtypeStruct((bsz, MLA_HEADS, seq, HEAD_SLAB), BF16),
            jax.ShapeDtypeStruct((bsz, MLA_HEADS, seq, HEAD_SLAB), BF16),
            jax.ShapeDtypeStruct((bsz, seq, MLA_HEADS * V_HEAD_DIM), BF16),
            jax.ShapeDtypeStruct((seq, bsz * (d // 2)), F32),
        ],
        compiler_params=pltpu.CompilerParams(dimension_semantics=("parallel", "parallel"),
                                             vmem_limit_bytes=VMEM_LIMIT),
        name="even_in",
    )(x, mod, posf, g, prep["w_in"], prep["gq"], prep["wq"], prep["gkv"], prep["wk"], prep["wv"], prep["tab"])


def _prep_even(even_w_in, q_norm_g, w_uq, kv_norm_g, w_ukv, q_head_g, k_head_g):
    d = even_w_in.shape[0]
    half = QK_ROPE_DIM // 2
    nope = QK_NOPE_DIM
    c_pe = Q_LORA_RANK + KV_LORA_RANK
    w_pe = even_w_in[:, c_pe:c_pe + QK_ROPE_DIM]
    zeros = lambda n: jnp.zeros((d, n), F32)
    pe_slab = jnp.concatenate([zeros(nope), w_pe, zeros(HEAD_SLAB - QK_HEAD_DIM)], axis=1)
    pe_sw = jnp.concatenate([zeros(nope), -w_pe[:, half:], w_pe[:, :half], zeros(HEAD_SLAB - QK_HEAD_DIM)], axis=1)
    w_in = jnp.concatenate([even_w_in[:, :c_pe], pe_slab, pe_sw, even_w_in[:, c_pe + QK_ROPE_DIM:]], axis=1)

    r = w_uq.shape[0]
    padq = jnp.zeros((r, MLA_HEADS, HEAD_SLAB - QK_HEAD_DIM), F32)
    wq_plain = jnp.concatenate([w_uq, padq], axis=2).reshape(r, MLA_HEADS * HEAD_SLAB)
    wq_sw = jnp.concatenate([jnp.zeros((r, MLA_HEADS, nope), F32), -w_uq[:, :, nope + half:],
                             w_uq[:, :, nope:nope + half], padq], axis=2).reshape(r, MLA_HEADS * HEAD_SLAB)
    wq = jnp.concatenate([wq_plain, wq_sw], axis=1)

    rk = w_ukv.shape[0]
    wk = jnp.concatenate([w_ukv[:, :, :nope], jnp.zeros((rk, MLA_HEADS, HEAD_SLAB - nope), F32)],
                         axis=2).reshape(rk, MLA_HEADS * HEAD_SLAB)
    wv = w_ukv[:, :, nope:].reshape(rk, MLA_HEADS * V_HEAD_DIM)

    inv_freq = 1.0 / (ROPE_THETA ** (jnp.arange(half, dtype=F32) / half))
    pad_tail = jnp.zeros((HEAD_SLAB - QK_HEAD_DIM,), F32)
    freq_row = jnp.concatenate([jnp.zeros((nope,), F32), inv_freq, inv_freq, pad_tail])

    def gain_rows(gv, scale):
        plain = jnp.concatenate([gv, pad_tail]) * scale
        swapped = jnp.concatenate([jnp.zeros((nope,), F32), gv[nope + half:], gv[nope:nope + half], pad_tail]) * scale
        return plain, swapped

    gq_plain, gq_sw = gain_rows(q_head_g, QK_HEAD_DIM ** -0.5 * math.log2(math.e))
    gk_plain, gk_sw = gain_rows(k_head_g, 1.0)
    freq_packed = jnp.tile(jnp.concatenate([inv_freq, inv_freq]), LANES // QK_ROPE_DIM)
    tab = jnp.stack([freq_row, gq_plain, gq_sw, gk_plain, gk_sw, freq_packed, freq_row * 0, freq_row * 0])
    return {
        "w_in": w_in.astype(BF16), "gq": q_norm_g.reshape(1, -1), "wq": wq.astype(BF16),
        "gkv": kv_norm_g.reshape(1, -1), "wk": wk.astype(BF16), "wv": wv.astype(BF16), "tab": tab,
    }


def _attn_kernel(qi_ref, kj_ref, q_ref, k_ref, v_ref, o_ref, m_sc, a_sc, acc_sc, s_sc, p_sc, *, tq, tk):
    step = pl.program_id(2)
    i = qi_ref[step]
    j = kj_ref[step]
    sum_lane = (V_HEAD_DIM, 0)
    subs = tq // tk

    @pl.when(j == 0)
    def _():
        m_sc[...] = jnp.full(m_sc.shape, -jnp.inf, F32)
        acc_sc[...] = jnp.zeros(acc_sc.shape, F32)

    def sweep(diag_sub):
        lo = 0 if diag_sub is None else diag_sub * tk
        live = pl.ds(lo, tq - lo)
        lane = lax.broadcasted_iota(jnp.int32, (1, LANES), 1)
        for hh in range(ATTN_HEADS):
            s_sc[hh, live, :] = lax.dot_general(q_ref[hh, live, :], k_ref[hh], (((1,), (1,)), ((), ())),
                                                preferred_element_type=F32)
        for hh in range(ATTN_HEADS):
            v = v_ref[:, (hh // 2) * LANES:(hh // 2 + 1) * LANES]
            for r0 in range(lo, tq, ATTN_ROWS):
                rows = pl.ds(r0, ATTN_ROWS)
                s = s_sc[hh, rows, :]
                if diag_sub is not None and r0 < lo + tk:
                    row = (r0 - lo) + lax.broadcasted_iota(jnp.int32, (ATTN_ROWS, tk), 0)
                    col = lax.broadcasted_iota(jnp.int32, (ATTN_ROWS, tk), 1)
                    s = jnp.where(col <= row, s, -jnp.inf)
                m_prev = m_sc[hh, rows, :]
                m_new = jnp.maximum(m_prev, jnp.max(s, axis=-1, keepdims=True))
                a_sc[hh, rows, :] = jnp.exp2(m_prev - m_new)
                m_sc[hh, rows, :] = m_new
                shifted = s - jnp.concatenate([m_new] * (tk // LANES), axis=1)
                p_sc[hh, rows, :] = jnp.exp2(shifted.astype(BF16))
            own = (lane < V_HEAD_DIM) == (hh % 2 == 0)
            ones = jnp.where(lane == sum_lane[hh % 2], 1.0, 0.0).astype(v.dtype)
            vh = jnp.where(own, v, jnp.broadcast_to(ones, v.shape))
            acc_sc[hh, live, :] = (acc_sc[hh, live, :] * a_sc[hh, live, :]
                                   + jnp.dot(p_sc[hh, live, :], vh, preferred_element_type=F32))

    diag = j - subs * i

    @pl.when(diag < 0)
    def _():
        sweep(None)

    for ds_ in range(subs):
        @pl.when(diag == ds_)
        def _(ds_=ds_):
            sweep(ds_)

    @pl.when(diag == subs - 1)
    def _():
        lane = lax.broadcasted_iota(jnp.int32, (1, LANES), 1)
        for pp in range(ATTN_HEADS // 2):
            acc0 = acc_sc[2 * pp]
            acc1 = acc_sc[2 * pp + 1]
            l0 = acc0[:, sum_lane[0]:sum_lane[0] + 1]
            l1 = acc1[:, sum_lane[1]:sum_lane[1] + 1]
            o_ref[:, pp * LANES:(pp + 1) * LANES] = jnp.where(lane < V_HEAD_DIM, acc0 / l0,
                                                              acc1 / l1).astype(o_ref.dtype)


def _attn_call(q, k, v):
    bsz, nh, seq, _ = q.shape
    tk = min(ATTN_TILE, seq)
    tq = min(ATTN_Q_BLOCKS * tk, seq)
    subs = tq // tk
    nq = seq // tq
    pairs = [(i, j) for i in range(nq) for j in range(subs * (i + 1))]
    qi = jnp.asarray([p[0] for p in pairs], jnp.int32)
    kj = jnp.asarray([p[1] for p in pairs], jnp.int32)
    kern = functools.partial(_attn_kernel, tq=tq, tk=tk)
    hp = ATTN_HEADS
    assert nh % hp == 0
    return pl.pallas_call(
        kern,
        grid_spec=pltpu.PrefetchScalarGridSpec(
            num_scalar_prefetch=2,
            grid=(bsz, nh // hp, len(pairs)),
            in_specs=[
                pl.BlockSpec((None, hp, tq, HEAD_SLAB), lambda b, h, p, qi, kj: (b, h, qi[p], 0)),
                pl.BlockSpec((None, hp, tk, HEAD_SLAB), lambda b, h, p, qi, kj: (b, h, kj[p], 0)),
                pl.BlockSpec((None, tk, hp * V_HEAD_DIM), lambda b, h, p, qi, kj: (b, kj[p], h)),
            ],
            out_specs=pl.BlockSpec((None, tq, hp * V_HEAD_DIM), lambda b, h, p, qi, kj: (b, qi[p], h)),
            scratch_shapes=[pltpu.VMEM((hp, tq, LANES), F32), pltpu.VMEM((hp, tq, LANES), F32),
                            pltpu.VMEM((hp, tq, LANES), F32),
                            pltpu.VMEM((hp, tq, tk), F32), pltpu.VMEM((hp, tq, tk), BF16)],
        ),
        out_shape=jax.ShapeDtypeStruct((bsz, seq, nh * V_HEAD_DIM), BF16),
        compiler_params=pltpu.CompilerParams(
            dimension_semantics=("parallel", "parallel", "arbitrary"),
            vmem_limit_bytes=VMEM_LIMIT),
        name="mla_attention",
    )(qi, kj, q, k, v)


def _s5_disc_kernel(are_ref, aim_ref, ldt_ref, bre_ref, bim_ref, abre_ref, abim_ref, bbre_ref, bbim_ref):
    dt = jnp.exp(ldt_ref[...])
    lam_re = jnp.minimum(are_ref[...], -1e-4)
    lam_im = aim_ref[...]
    mag = jnp.exp(lam_re * dt)
    ab_re = mag * jnp.cos(lam_im * dt)
    ab_im = mag * jnp.sin(lam_im * dt)
    den = lam_re * lam_re + lam_im * lam_im
    num_re = ab_re - 1.0
    f_re = (num_re * lam_re + ab_im * lam_im) / den
    f_im = (ab_im * lam_re - num_re * lam_im) / den
    abre_ref[...] = ab_re
    abim_ref[...] = ab_im
    br = bre_ref[...]
    bi = bim_ref[...]
    bbre_ref[...] = f_re[:, None, :] * br - f_im[:, None, :] * bi
    bbim_ref[...] = f_re[:, None, :] * bi + f_im[:, None, :] * br


def _s5_disc_call(a_re, a_im, log_dt, b_re, b_im):
    g, p = a_re.shape
    bre_t = jnp.swapaxes(b_re, 1, 2)
    bim_t = jnp.swapaxes(b_im, 1, 2)
    return pl.pallas_call(
        _s5_disc_kernel,
        out_shape=[jax.ShapeDtypeStruct((g, p), F32), jax.ShapeDtypeStruct((g, p), F32),
                   jax.ShapeDtypeStruct(bre_t.shape, F32), jax.ShapeDtypeStruct(bre_t.shape, F32)],
        name="s5_discretize",
    )(a_re, a_im, log_dt.reshape(g, 1), bre_t, bim_t)


def _block_diag_halves(m):
    g, r, c = m.shape
    gh = g // 2
    eye = jnp.eye(gh, dtype=m.dtype)
    mh = m.reshape(2, gh, r, c)
    return (mh[:, :, :, None, :] * eye[None, :, None, :, None]).reshape(2, gh * r, gh * c)


def _s5_kernel(u_ref, bre_ref, bim_ref, are_ref, aim_ref, cre_ref, cim_ref, d_ref, gw_ref, gb_ref,
               o_ref, sre, sim, dre, dim, xbr, xbi, *, steps):
    @pl.when(pl.program_id(0) == 0)
    def _():
        sre[...] = jnp.zeros(sre.shape, F32)
        sim[...] = jnp.zeros(sim.shape, F32)

    rows = steps * SUBLANES
    w = u_ref.shape[-1]
    u = u_ref[...].reshape(rows, w)
    ub = u.astype(BF16)
    kh = w // 2
    nh = dre.shape[1] // 2
    for hf in range(2):
        dre[:, hf * nh:(hf + 1) * nh] = jnp.dot(ub[:, hf * kh:(hf + 1) * kh], bre_ref[hf], preferred_element_type=F32)
        dim[:, hf * nh:(hf + 1) * nh] = jnp.dot(ub[:, hf * kh:(hf + 1) * kh], bim_ref[hf], preferred_element_type=F32)

    xr = sre[...]
    xi = sim[...]
    for t in range(0, steps, 2):
        pair_r = []
        pair_i = []
        for r0 in (t * SUBLANES, (t + 1) * SUBLANES):
            nr = are_ref[...] * xr - aim_ref[...] * xi + dre[r0:r0 + SUBLANES, :]
            ni = are_ref[...] * xi + aim_ref[...] * xr + dim[r0:r0 + SUBLANES, :]
            xr, xi = nr, ni
            pair_r.append(nr)
            pair_i.append(ni)
        xbr[t * SUBLANES:(t + 2) * SUBLANES, :] = jnp.concatenate(pair_r, axis=0).astype(BF16)
        xbi[t * SUBLANES:(t + 2) * SUBLANES, :] = jnp.concatenate(pair_i, axis=0).astype(BF16)
    sre[...] = xr
    sim[...] = xi

    ys = []
    for hf in range(2):
        yr = jnp.dot(xbr[:, hf * nh:(hf + 1) * nh], cre_ref[hf], preferred_element_type=F32)
        yi = jnp.dot(xbi[:, hf * nh:(hf + 1) * nh], cim_ref[hf], preferred_element_type=F32)
        ys.append(yr - yi)
    y = jnp.concatenate(ys, axis=1) + d_ref[...] * u
    g = _gelu(y)
    out = g * _sigmoid(_bdot(g, gw_ref[...]) + gb_ref[...])
    o_ref[...] = out.reshape(steps, SUBLANES, w).astype(o_ref.dtype)


def _s5_call(u_t, disc, c_re, c_im, d_skip, glu_w, glu_b):
    seq, bsz, w = u_t.shape
    assert bsz == SUBLANES
    ab_re, ab_im, bb_re, bb_im = disc
    g, p = ab_re.shape
    n_state = g * p
    bre = _block_diag_halves(bb_re).astype(BF16)
    bim = _block_diag_halves(bb_im).astype(BF16)
    cre = _block_diag_halves(jnp.swapaxes(c_re, 1, 2)).astype(BF16)
    cim = _block_diag_halves(jnp.swapaxes(c_im, 1, 2)).astype(BF16)
    steps = min(S5_STEPS, seq)
    full = lambda a: pl.BlockSpec(a.shape, lambda s: (0,) * a.ndim)
    rep = lambda a: jnp.broadcast_to(a.reshape(1, n_state), (bsz, n_state))
    args = (bre, bim, rep(ab_re), rep(ab_im), cre, cim,
            d_skip.reshape(1, w), glu_w.astype(BF16), glu_b.reshape(1, w))
    return pl.pallas_call(
        functools.partial(_s5_kernel, steps=steps),
        grid=(seq // steps,),
        in_specs=[pl.BlockSpec((steps, bsz, w), lambda s: (s, 0, 0))] + [full(a) for a in args],
        out_specs=pl.BlockSpec((steps, bsz, w), lambda s: (s, 0, 0)),
        out_shape=jax.ShapeDtypeStruct((seq, bsz, w), BF16),
        scratch_shapes=[pltpu.VMEM((bsz, n_state), F32), pltpu.VMEM((bsz, n_state), F32),
                        pltpu.VMEM((steps * bsz, n_state), F32), pltpu.VMEM((steps * bsz, n_state), F32),
                        pltpu.VMEM((steps * bsz, n_state), BF16), pltpu.VMEM((steps * bsz, n_state), BF16)],
        compiler_params=pltpu.CompilerParams(dimension_semantics=("arbitrary",), vmem_limit_bytes=VMEM_LIMIT),
        name="s5_scan",
    )(u_t, *args)


def _router_tail(x_new, mod_ref, gf_ref, rw_ref, rb_ref, h2_ref, te_ref, tet_ref, cnt_ref):
    h2 = _mod_norm(x_new, gf_ref[...], mod_ref[4:5, :], mod_ref[3:4, :])
    h2_ref[...] = _pack_pairs(h2)
    h_hi = h2.astype(BF16)
    h_lo = (h2 - h_hi.astype(F32)).astype(BF16)
    r_hi = jnp.dot(h_hi, rw_ref[...], preferred_element_type=F32)
    r_lo = jnp.dot(h_lo, rw_ref[...], preferred_element_type=F32)
    logits = r_hi[:, :LANES] + r_hi[:, LANES:] + r_lo[:, :LANES] + rb_ref[...]
    lane = lax.broadcasted_iota(jnp.int32, logits.shape, 1).astype(F32)
    vals = []
    idxs = []
    work = logits
    for _ in range(TOP_K):
        m = jnp.max(work, axis=-1, keepdims=True)
        idx = jnp.min(jnp.where(work == m, lane, float(LANES)), axis=-1, keepdims=True)
        vals.append(m)
        idxs.append(idx)
        work = jnp.where(lane == idx, NEG_BIG * 2.0, work)
    exps = [jnp.exp(vv - vals[0]) for vv in vals]
    tot = exps[0] + exps[1] + exps[2] + exps[3]
    te = jnp.zeros(logits.shape, F32)
    picked = jnp.zeros(logits.shape, F32)
    for kk in range(TOP_K):
        te = jnp.where(lane == float(kk), idxs[kk], te)
        te = jnp.where(lane == float(TOP_K + kk), exps[kk] / tot, te)
        picked = picked + jnp.where(lane == idxs[kk], 1.0, 0.0)
    te_ref[...] = te[:, :2 * TOP_K]
    tet_ref[...] = te.T[:2 * TOP_K, :]
    cnt_ref[...] = jnp.sum(picked, axis=0, keepdims=True)


def _mix_out_kernel(x_ref, a_ref, s_ref, mod_ref, wo_ref, gf_ref, rw_ref, rb_ref, xo_ref, h2_ref, te_ref,
                    tet_ref, cnt_ref):
    ka = a_ref.shape[-1]
    mix = jnp.dot(a_ref[...], wo_ref[:ka, :], preferred_element_type=F32)
    mix = mix + jnp.dot(s_ref[...], wo_ref[ka:, :], preferred_element_type=F32)
    x_new = x_ref[...] + mod_ref[2:3, :] * mix
    xo_ref[...] = x_new
    _router_tail(x_new, mod_ref, gf_ref, rw_ref, rb_ref, h2_ref, te_ref, tet_ref, cnt_ref)


def _tail_out_specs(bsz, seq, tm, d):
    nt = seq // tm
    specs = [
        pl.BlockSpec((None, tm, d), lambda b, s: (b, s, 0)),
        pl.BlockSpec((tm, d // 2), lambda b, s: (b * nt + s, 0)),
        pl.BlockSpec((tm, 2 * TOP_K), lambda b, s: (b * nt + s, 0)),
        pl.BlockSpec((2 * TOP_K, tm), lambda b, s: (0, b * nt + s)),
        pl.BlockSpec((None, 1, LANES), lambda b, s: (b * nt + s, 0, 0)),
    ]
    shapes = [
        jax.ShapeDtypeStruct((bsz, seq, d), F32),
        jax.ShapeDtypeStruct((bsz * seq, d // 2), jnp.uint32),
        jax.ShapeDtypeStruct((bsz * seq, 2 * TOP_K), F32),
        jax.ShapeDtypeStruct((2 * TOP_K, bsz * seq), F32),
        jax.ShapeDtypeStruct((bsz * nt, 1, LANES), F32),
    ]
    return specs, shapes


def _router_pad(router_w, router_b):
    d, e = router_w.shape
    rw = jnp.concatenate([router_w, jnp.zeros((d, LANES - e), F32)], axis=1)
    rw_hi = rw.astype(BF16)
    rw_lo = (rw - rw_hi.astype(F32)).astype(BF16)
    rb = jnp.concatenate([router_b, jnp.full((LANES - e,), NEG_BIG, F32)]).reshape(1, LANES)
    return jnp.concatenate([rw_hi, rw_lo], axis=1), rb


def _mix_out_call(x, attn, ssm_t, mod, w_out, gf, rw, rb):
    bsz, seq, d = x.shape
    tm = min(ROW_TILE, seq)
    ka = attn.shape[-1]
    ks = ssm_t.shape[-1] // bsz
    full = lambda a: pl.BlockSpec(a.shape, lambda b, s: (0,) * a.ndim)
    out_specs, out_shape = _tail_out_specs(bsz, seq, tm, d)
    return pl.pallas_call(
        _mix_out_kernel,
        grid=(bsz, seq // tm),
        in_specs=[
            pl.BlockSpec((None, tm, d), lambda b, s: (b, s, 0)),
            pl.BlockSpec((None, tm, ka), lambda b, s: (b, s, 0)),
            pl.BlockSpec((tm, ks), lambda b, s: (s, b)),
            pl.BlockSpec((None, 6, d), lambda b, s: (b, 0, 0)),
            full(w_out), full(gf), full(rw), full(rb),
        ],
        out_specs=out_specs,
        out_shape=out_shape,
        compiler_params=pltpu.CompilerParams(dimension_semantics=("parallel", "parallel"),
                                             vmem_limit_bytes=VMEM_LIMIT),
        name="even_out",
    )(x, attn, ssm_t, mod, w_out, gf, rw, rb)


def _moe_sum(yg_ref, te_ref):
    te = te_ref[...]
    acc_lo = acc_hi = None
    for kk in range(TOP_K):
        lo, hi = _unpack_pairs_f32(yg_ref[kk])
        gate = te[:, TOP_K + kk:TOP_K + kk + 1]
        acc_lo = gate * lo if kk == 0 else acc_lo + gate * lo
        acc_hi = gate * hi if kk == 0 else acc_hi + gate * hi
    return jnp.concatenate([acc_lo, acc_hi], axis=1)


def _odd_kernel(x_ref, yg_ref, tep_ref, modp_ref, mod_ref, g_ref, win_ref, icnt_ref, wp_ref, ps_ref, gv_ref,
                wsp_ref, bsp_ref, wo_ref, gf_ref, rw_ref, rb_ref, xo_ref, h2_ref, te_ref, tet_ref, cnt_ref, ext_sc):
    tm = x_ref.shape[0]
    pw = wp_ref.shape[-1]
    width = pw * len(POOL_WINDOWS)

    @pl.when(pl.program_id(1) == 0)
    def _():
        ext_sc[0:POOL_HALO, :] = jnp.zeros((POOL_HALO, width), F32)

    x = x_ref[...] + modp_ref[5:6, :] * _moe_sum(yg_ref, tep_ref)
    h = _mod_norm(x, g_ref[...], mod_ref[1:2, :], mod_ref[0:1, :])
    z = _bdot(h, win_ref[...])
    up = z[:, :width]
    ext_sc[POOL_HALO:POOL_HALO + tm, :] = up

    pooled = []
    for gi, win in enumerate(POOL_WINDOWS):
        cols = slice(gi * pw, (gi + 1) * pw)
        acc = up[:, cols]
        for lag in range(1, win):
            acc = acc + ext_sc[POOL_HALO - lag:POOL_HALO - lag + tm, cols]
        pg = acc * icnt_ref[:, gi:gi + 1] - up[:, cols]
        pooled.append(_bdot(pg, wp_ref[gi]) * ps_ref[:, cols])
    ext_sc[0:POOL_HALO, :] = ext_sc[tm:tm + POOL_HALO, :]
    pooled = jnp.concatenate(pooled, axis=1)

    ug = _gelu(z[:, width:2 * width])
    vg = _gelu(z[:, 2 * width:])
    vn = (vg * _rms(vg, width) * gv_ref[...]).astype(BF16)
    hd = width // SGU_HEADS
    chunks = []
    for ci in range(tm // SGU_CHUNK):
        heads = []
        for hh in range(SGU_HEADS):
            blk = vn[ci * SGU_CHUNK:(ci + 1) * SGU_CHUNK, hh * hd:(hh + 1) * hd]
            heads.append(jnp.dot(wsp_ref[hh], blk, preferred_element_type=F32) + bsp_ref[hh])
        chunks.append(jnp.concatenate(heads, axis=1))
    gated = ug * jnp.concatenate(chunks, axis=0)

    mix = _bdot(pooled, wo_ref[:width, :]) + _bdot(gated, wo_ref[width:, :])
    x_new = x + mod_ref[2:3, :] * mix
    xo_ref[...] = x_new
    _router_tail(x_new, mod_ref, gf_ref, rw_ref, rb_ref, h2_ref, te_ref, tet_ref, cnt_ref)


def _odd_call(grp, mod_prev, mod, nb, g, w_in, pool_w, pool_scale, sgu_norm_g, sgu_w, sgu_b, w_out, gf, rw, rb):
    x = grp["x"]
    _, seq, d = x.shape
    tm = min(ODD_TILE, seq)
    nt = seq // tm
    xb0, tok0, mb0 = grp["x_b0"], grp["tok0"], grp["mod_b0"]
    tile0 = tok0 // tm
    yg = grp["yg"]
    width = pool_scale.shape[0]
    hd = width // SGU_HEADS
    t = jnp.arange(seq, dtype=jnp.int32)
    icnt = jnp.stack([1.0 / jnp.minimum(t + 1, wn).astype(F32) for wn in POOL_WINDOWS], axis=1)
    wsp = jnp.tril(sgu_w).astype(BF16)
    bsp = jnp.broadcast_to(sgu_b[:, :, None], (SGU_HEADS, SGU_CHUNK, hd))
    args = (g, w_in.astype(BF16), icnt, pool_w.astype(BF16), pool_scale.reshape(1, width),
            sgu_norm_g.reshape(1, width), wsp, bsp, w_out.astype(BF16), gf, rw, rb)
    full = lambda a: pl.BlockSpec(a.shape, lambda b, s: (0,) * a.ndim)
    in_specs = [pl.BlockSpec((None, tm, d), lambda b, s: (xb0 + b, s, 0)),
                pl.BlockSpec((TOP_K, tm, yg.shape[-1]), lambda b, s: (0, b * nt + s, 0)),
                pl.BlockSpec((tm, 2 * TOP_K), lambda b, s: (tile0 + b * nt + s, 0)),
                pl.BlockSpec((None, 6, d), lambda b, s: (mb0 + b, 0, 0)),
                pl.BlockSpec((None, 6, d), lambda b, s: (mb0 + b, 0, 0))]
    for idx, a in enumerate(args):
        in_specs.append(pl.BlockSpec((tm, len(POOL_WINDOWS)), lambda b, s: (s, 0)) if idx == 2 else full(a))
    out_specs, out_shape = _tail_out_specs(nb, seq, tm, d)
    return pl.pallas_call(
        _odd_kernel,
        grid=(nb, nt),
        in_specs=in_specs,
        out_specs=out_specs,
        out_shape=out_shape,
        scratch_shapes=[pltpu.VMEM((tm + POOL_HALO, width), F32)],
        compiler_params=pltpu.CompilerParams(dimension_semantics=("parallel", "arbitrary"),
                                             vmem_limit_bytes=VMEM_LIMIT),
        name="odd_mixer",
    )(x, yg, grp["te"], mod_prev, mod, *args)


def _dest_kernel(tet_ref, base_ref, tri_ref, dst_ref):
    tr = tet_ref.shape[1]
    tet = tet_ref[...]
    expert = lax.broadcasted_iota(jnp.int32, (LANES, tr), 0).astype(F32)
    hots = [tet[kk:kk + 1, :] == expert for kk in range(TOP_K)]
    oh = jnp.zeros((LANES, tr), F32)
    for hot in hots:
        oh = oh + jnp.where(hot, 1.0, 0.0)
    before = jnp.dot(oh.astype(BF16), tri_ref[...], preferred_element_type=F32) + base_ref[...]
    row = lax.broadcasted_iota(jnp.int32, (2 * TOP_K, tr), 0)
    dst = jnp.zeros((2 * TOP_K, tr), F32)
    for kk, hot in enumerate(hots):
        dst = jnp.where(row == kk, jnp.sum(jnp.where(hot, before, 0.0), axis=0, keepdims=True), dst)
    dst_ref[...] = dst.astype(jnp.int32)


def _dest_call(tet, base, tok0, n_tok):
    tiles = base.shape[0]
    tr = n_tok // tiles
    tile0 = tok0 // tr
    tri = (jnp.arange(tr)[:, None] < jnp.arange(tr)[None, :]).astype(BF16)
    return pl.pallas_call(
        _dest_kernel,
        grid=(tiles,),
        in_specs=[pl.BlockSpec((2 * TOP_K, tr), lambda i: (0, tile0 + i)),
                  pl.BlockSpec((None, LANES, 1), lambda i: (i, 0, 0)),
                  pl.BlockSpec((tr, tr), lambda i: (0, 0))],
        out_specs=pl.BlockSpec((2 * TOP_K, tr), lambda i: (0, i)),
        out_shape=jax.ShapeDtypeStruct((2 * TOP_K, n_tok), jnp.int32),
        compiler_params=pltpu.CompilerParams(dimension_semantics=("parallel",)),
        name="route_dest",
    )(tet, base, tri)


def _sc_gather(table, idx):
    n = idx.shape[0]
    per_w = n // SC_WORKERS
    assert per_w * SC_WORKERS == n and per_w % SC_CHUNK == 0
    n_chunks = per_w // SC_CHUNK
    row_shape = table.shape[1:]
    mesh = plsc.VectorSubcoreMesh(core_axis_name="c", subcore_axis_name="s")

    @functools.partial(
        pl.kernel, mesh=mesh,
        out_type=jax.ShapeDtypeStruct((n,) + row_shape, table.dtype),
        scratch_types=[pltpu.VMEM((SC_CHUNK,), jnp.int32), pltpu.VMEM((SC_CHUNK,) + row_shape, table.dtype),
                       pltpu.SemaphoreType.DMA],
        name="sc_row_gather",
    )
    def gather(table_hbm, idx_hbm, out_hbm, idx_v, rows_v, sem):
        wid = lax.axis_index("s") * SC_CORES + lax.axis_index("c")
        base = wid * per_w

        @pl.loop(0, n_chunks)
        def _(ci):
            off = pl.multiple_of(base + ci * SC_CHUNK, SC_CHUNK)
            pltpu.sync_copy(idx_hbm.at[pl.ds(off, SC_CHUNK)], idx_v)
            pltpu.async_copy(table_hbm.at[idx_v], rows_v, sem).wait()
            pltpu.sync_copy(rows_v, out_hbm.at[pl.ds(off, SC_CHUNK)])

    return gather(table, idx)


def _sc_scatter(rows, dests, n_out, tok0):
    t = dests[0].shape[0]
    per_w = t // SC_WORKERS
    assert per_w * SC_WORKERS == t and per_w % SC_CHUNK == 0
    n_chunks = per_w // SC_CHUNK
    row_shape = rows.shape[1:]
    nk = len(dests)
    mesh = plsc.VectorSubcoreMesh(core_axis_name="c", subcore_axis_name="s")

    @functools.partial(
        pl.kernel, mesh=mesh,
        out_type=jax.ShapeDtypeStruct((n_out,) + row_shape, rows.dtype),
        scratch_types=[pltpu.VMEM((SC_CHUNK,), jnp.int32)] * nk
        + [pltpu.VMEM((SC_CHUNK,) + row_shape, rows.dtype), pltpu.SemaphoreType.DMA],
        name="sc_row_scatter",
    )
    def scatter(rows_hbm, *rest):
        dest_hbm = rest[:nk]
        out_hbm = rest[nk]
        idx_v = rest[nk + 1:2 * nk + 1]
        rows_v, sem = rest[2 * nk + 1:]
        wid = lax.axis_index("s") * SC_CORES + lax.axis_index("c")
        base = wid * per_w

        @pl.loop(0, n_chunks)
        def _(ci):
            off = pl.multiple_of(base + ci * SC_CHUNK, SC_CHUNK)
            src = pl.multiple_of(tok0 + off, SC_CHUNK)
            pltpu.sync_copy(rows_hbm.at[pl.ds(src, SC_CHUNK)], rows_v)
            for kk in range(nk):
                pltpu.sync_copy(dest_hbm[kk].at[pl.ds(off, SC_CHUNK)], idx_v[kk])
            copies = [pltpu.async_copy(rows_v, out_hbm.at[idx_v[kk]], sem) for kk in range(nk)]
            for cp in copies:
                cp.wait()

    return scatter(rows, *dests)


def _expert_kernel(be_ref, nv_ref, ord_ref, ue_ref, nu_ref, x_ref, wgu_hbm, bgu_ref, wdn_hbm, bdn_ref, y_ref,
                   wgu_f32, wdn_f32, wgu_bf, wdn_bf, sem, *, layer):
    i = pl.program_id(0)
    used = i < nu_ref[0]
    pos = ord_ref[i]
    fresh = jnp.logical_or(i == 0, ord_ref[jnp.maximum(i - 1, 0)] != pos)

    def weight_copies(expert):
        return (pltpu.make_async_copy(wgu_hbm.at[layer, expert], wgu_f32, sem.at[0]),
                pltpu.make_async_copy(wdn_hbm.at[layer, expert], wdn_f32, sem.at[1]))

    @pl.when(i == 0)
    def _():
        for cp in weight_copies(ue_ref[0]):
            cp.start()

    @pl.when(jnp.logical_and(used, fresh))
    def _():
        for cp in weight_copies(ue_ref[pos]):
            cp.wait()
        wgu_bf[...] = wgu_f32[...].astype(BF16)
        wdn_bf[...] = wdn_f32[...].astype(BF16)

        @pl.when(pos + 1 < nu_ref[1])
        def _():
            for cp in weight_copies(ue_ref[pos + 1]):
                cp.start()

    def ffn(rows):
        x = jnp.concatenate(_unpack_pairs(x_ref[0:rows, :]), axis=1)
        z = jnp.dot(x, wgu_bf[...], preferred_element_type=F32) + bgu_ref[...]
        ff = z.shape[-1] // 2
        gate = jnp.minimum(z[:, :ff], SWIGLU_LIMIT)
        lin = jnp.clip(z[:, ff:], -SWIGLU_LIMIT, SWIGLU_LIMIT)
        act = gate * _sigmoid(SWIGLU_ALPHA * gate) * (lin + 1.0)
        y = _bdot(act, wdn_bf[...]) + bdn_ref[...]
        y_ref[0:rows, :] = _pack_pairs(y)

    nv = nv_ref[i]
    below = 0
    for size in MOE_PATHS:
        @pl.when(jnp.logical_and(used, jnp.logical_and(nv > below, nv <= size)))
        def _(size=size):
            ffn(size)
        below = size


def _expert_call(layer, block_e, block_valid, block_pos, used_experts, n_used, xs, w_gu, b_gu, w_dn, b_dn):
    n_rows, dh = xs.shape
    depth, e, d, ff2 = w_gu.shape
    ff = ff2 // 2
    nb = n_rows // MOE_ROWS
    row_map = lambda i, be, nv, po, ue, nu: (jnp.minimum(i, nu[0] - 1), 0)
    b_map = lambda i, be, nv, po, ue, nu: (layer, be[i], 0, 0)
    return pl.pallas_call(
        functools.partial(_expert_kernel, layer=layer),
        grid_spec=pltpu.PrefetchScalarGridSpec(
            num_scalar_prefetch=5,
            grid=(nb,),
            in_specs=[
                pl.BlockSpec((MOE_ROWS, dh), row_map),
                pl.BlockSpec(memory_space=pl.ANY),
                pl.BlockSpec((None, None, 1, ff2), b_map),
                pl.BlockSpec(memory_space=pl.ANY),
                pl.BlockSpec((None, None, 1, d), b_map),
            ],
            out_specs=pl.BlockSpec((MOE_ROWS, dh), row_map),
            scratch_shapes=[pltpu.VMEM((d, ff2), F32), pltpu.VMEM((ff, d), F32),
                            pltpu.VMEM((d, ff2), BF16), pltpu.VMEM((ff, d), BF16),
                            pltpu.SemaphoreType.DMA((2,))],
        ),
        out_shape=jax.ShapeDtypeStruct((n_rows, dh), jnp.uint32),
        compiler_params=pltpu.CompilerParams(dimension_semantics=("arbitrary",), vmem_limit_bytes=VMEM_LIMIT),
        name="moe_experts",
    )(block_e, block_valid, block_pos, used_experts, n_used, xs, w_gu, b_gu.reshape(depth, e, 1, ff2), w_dn,
      b_dn.reshape(depth, e, 1, d))


def _combine_kernel(x_ref, yg_ref, te_ref, mod_ref, *rest):
    o_ref = rest[-1]
    o_ref[...] = x_ref[...] + mod_ref[5:6, :] * _moe_sum(yg_ref, te_ref)


def _combine_call(grp, mod, prev, bsz, nb):
    x = grp["x"]
    _, seq, d = x.shape
    tm = min(ROW_TILE, seq)
    nt = seq // tm
    xb0, mb0 = grp["x_b0"], grp["mod_b0"]
    tile0 = grp["tok0"] // tm
    yg = grp["yg"]
    in_specs = [
        pl.BlockSpec((None, tm, d), lambda b, s: (xb0 + b, s, 0)),
        pl.BlockSpec((TOP_K, tm, yg.shape[-1]), lambda b, s: (0, b * nt + s, 0)),
        pl.BlockSpec((tm, 2 * TOP_K), lambda b, s: (tile0 + b * nt + s, 0)),
        pl.BlockSpec((None, 6, d), lambda b, s: (mb0 + b, 0, 0)),
    ]
    args = [x, yg, grp["te"], mod]
    aliases = {}
    if prev is not None:
        in_specs.append(pl.BlockSpec(memory_space=pl.ANY))
        args.append(prev)
        aliases = {len(args) - 1: 0}
    return pl.pallas_call(
        _combine_kernel,
        grid=(nb, nt),
        in_specs=in_specs,
        out_specs=pl.BlockSpec((None, tm, d), lambda b, s: (mb0 + b, s, 0)),
        out_shape=jax.ShapeDtypeStruct((bsz, seq, d), F32),
        input_output_aliases=aliases,
        compiler_params=pltpu.CompilerParams(dimension_semantics=("parallel", "parallel"),
                                             vmem_limit_bytes=VMEM_LIMIT),
        name="moe_combine",
    )(*args)


def _moe_rows(layer, grp, gt, w_gu, b_gu, w_dn, b_dn):
    h2 = grp["h2"]
    dh = h2.shape[-1]
    n_rows = -(-(gt * TOP_K + N_EXPERTS * (MOE_ROWS - 1)) // MOE_ROWS) * MOE_ROWS
    nb = n_rows // MOE_ROWS
    first_row = jnp.arange(nb, dtype=jnp.int32) * MOE_ROWS
    upto = jnp.arange(LANES)[:, None] <= jnp.arange(LANES)[None, :]
    g_cnt = grp["cnt"][:, 0, :].astype(jnp.int32)
    counts = jnp.sum(g_cnt, axis=0)
    padded = (counts + MOE_ROWS - 1) // MOE_ROWS * MOE_ROWS
    pad_end = jnp.sum(jnp.where(upto, padded[:, None], 0), axis=0)
    pad_start = pad_end - padded
    tile_base = pad_start[None, :] + jnp.cumsum(g_cnt, axis=0) - g_cnt
    dest = _dest_call(grp["tet"], tile_base.astype(F32)[:, :, None], grp["tok0"], gt)
    dests = [dest[kk] for kk in range(TOP_K)]
    block_e = jnp.minimum(jnp.sum(pad_end[None, :N_EXPERTS] <= first_row[:, None], axis=1),
                          N_EXPERTS - 1).astype(jnp.int32)
    valid_end = (pad_start + counts)[block_e]
    block_valid = jnp.clip(valid_end - first_row, 0, MOE_ROWS).astype(jnp.int32)
    owns = counts[:N_EXPERTS] > 0
    expert_pos = jnp.cumsum(owns.astype(jnp.int32)) - 1
    slot = jnp.arange(N_EXPERTS, dtype=jnp.int32)
    used_experts = jnp.sum(jnp.where(owns[None, :] & (expert_pos[None, :] == slot[:, None]),
                                     slot[None, :], 0), axis=1).astype(jnp.int32)
    block_pos = expert_pos[block_e].astype(jnp.int32)
    n_used = jnp.stack([pad_end[N_EXPERTS - 1] // MOE_ROWS, jnp.sum(owns)]).astype(jnp.int32)
    xs = _sc_scatter(h2, dests, n_rows, grp["tok0"])
    y = _expert_call(layer, block_e, block_valid, block_pos, used_experts, n_used, xs, w_gu, b_gu, w_dn, b_dn)
    return _sc_gather(y, dest[:TOP_K].reshape(-1)).reshape(TOP_K, gt, dh)


def kernel(x, c, positions, ada_w, ada_b, norm_mix_g, norm_ffn_g, router_w, router_b, moe_w_gu, moe_b_gu,
           moe_w_dn, moe_b_dn, even_w_in, mla_q_norm_g, mla_w_uq, mla_kv_norm_g, mla_w_ukv, mla_q_head_g,
           mla_k_head_g, s5_a_re, s5_a_im, s5_log_dt, s5_b_re, s5_b_im, s5_c_re, s5_c_im, s5_d, s5_glu_w,
           s5_glu_b, even_w_out, odd_w_in, pool_w, pool_scale, sgu_norm_g, sgu_w, sgu_b, odd_w_out):
    bsz, seq, d = x.shape
    depth = ada_w.shape[0]
    mods = _ada_call(c, ada_w, ada_b).reshape(depth, bsz, 6, d)
    posf = positions.astype(F32).reshape(bsz, seq, 1)
    splits = MOE_SPLITS if bsz % MOE_SPLITS == 0 else 1
    gb = bsz // splits
    gt = gb * seq
    tiles_g = gt // min(ROW_TILE, seq)

    def settle(groups, mod):
        out = None
        for grp in groups:
            out = _combine_call(grp, mod, out, bsz, gb)
        return out

    pending = None
    for layer in range(depth):
        mod = mods[layer]
        i = layer // 2
        g_mix = norm_mix_g[layer].reshape(1, d)
        g_ffn = norm_ffn_g[layer].reshape(1, d)
        rw, rb = _router_pad(router_w[layer], router_b[layer])
        if layer % 2 == 0:
            if pending is not None:
                x = settle(pending, mods[layer - 1])
            prep = _prep_even(even_w_in[i], mla_q_norm_g[i], mla_w_uq[i], mla_kv_norm_g[i], mla_w_ukv[i],
                              mla_q_head_g[i], mla_k_head_g[i])
            q, k, v, u_t = _even_in_call(x, mod, posf, g_mix, prep)
            attn = _attn_call(q, k, v)
            disc = _s5_disc_call(s5_a_re[i], s5_a_im[i], s5_log_dt[i], s5_b_re[i], s5_b_im[i])
            ssm_t = _s5_call(u_t.reshape(seq, bsz, d // 2), disc, s5_c_re[i], s5_c_im[i], s5_d[i],
                             s5_glu_w[i], s5_glu_b[i])
            x_new, h2, te, tet, tile_cnt = _mix_out_call(x, attn, ssm_t.reshape(seq, bsz * (d // 2)), mod,
                                                         even_w_out[i].astype(BF16), g_ffn, rw, rb)
            groups = [dict(x=x_new, x_b0=gi * gb, h2=h2, te=te, tet=tet, tok0=gi * gt, mod_b0=gi * gb,
                           cnt=tile_cnt[gi * tiles_g:(gi + 1) * tiles_g]) for gi in range(splits)]
        else:
            groups = []
            for gi, grp in enumerate(pending):
                x_new, h2, te, tet, tile_cnt = _odd_call(grp, mods[layer - 1], mod, gb, g_mix, odd_w_in[i],
                                                         pool_w[i], pool_scale[i], sgu_norm_g[i], sgu_w[i],
                                                         sgu_b[i], odd_w_out[i], g_ffn, rw, rb)
                groups.append(dict(x=x_new, x_b0=0, h2=h2, te=te, tet=tet, tok0=0, mod_b0=gi * gb,
                                   cnt=tile_cnt))
        for grp in groups:
            grp["yg"] = _moe_rows(layer, grp, gt, moe_w_gu, moe_b_gu, moe_w_dn, moe_b_dn)
        pending = groups
    return settle(pending, mods[depth - 1])
```

```python
import functools
import math

import jax
import jax.numpy as jnp
from jax import lax
from jax.experimental import pallas as pl
from jax.experimental.pallas import tpu as pltpu
from jax.experimental.pallas import tpu_sc as plsc

F32 = jnp.float32
BF16 = jnp.bfloat16
HIGHEST = lax.Precision.HIGHEST

NORM_EPS = 1e-6
MLA_HEADS = 8
QK_NOPE_DIM = 64
QK_ROPE_DIM = 32
QK_HEAD_DIM = QK_NOPE_DIM + QK_ROPE_DIM
V_HEAD_DIM = 64
Q_LORA_RANK = 256
KV_LORA_RANK = 128
ROPE_THETA = 10000.0
POOL_WINDOWS = (2, 4, 8, 16)
SGU_HEADS = 4
SGU_CHUNK = 128
N_EXPERTS = 32
TOP_K = 4
SWIGLU_ALPHA = 1.702
SWIGLU_LIMIT = 7.0

LANES = 128
SUBLANES = 8
HEAD_SLAB = LANES
POOL_HALO = 16
ROW_TILE = 1024
ODD_TILE = 512
ATTN_TILE = 1024
ATTN_Q_BLOCKS = 1
ATTN_HEADS = 4
ATTN_KEYS = 128
V_ROWS = V_HEAD_DIM + 16
S5_STEPS = 64
MOE_ROWS = 1024
MOE_PATHS = (128, 256, 512, 768, MOE_ROWS)
MOE_SPLITS = 2
SC_CORES = 2
SC_WORKERS = SC_CORES * 16
SC_CHUNK = 64
ADA_COLS = 1536
VMEM_LIMIT = 56 * 1024 * 1024
NEG_BIG = -1e30


def _sigmoid(v):
    return 1.0 / (1.0 + jnp.exp(-v))


def _gelu(v):
    return 0.5 * v * (1.0 + jnp.tanh(math.sqrt(2.0 / math.pi) * (v + 0.044715 * (v * v * v))))


def _rms(v, width):
    return lax.rsqrt(jnp.sum(v * v, axis=-1, keepdims=True) * (1.0 / width) + NORM_EPS)


def _mod_norm(x, g, sc, sh):
    return x * _rms(x, x.shape[-1]) * (g * (1.0 + sc)) + sh


def _bdot(a, b):
    return jnp.dot(a.astype(BF16), b, preferred_element_type=F32)


def _pack_pairs(v):
    w = v.shape[-1] // 2
    bits = pltpu.bitcast(v.astype(BF16).astype(F32), jnp.uint32)
    return (bits[:, :w] >> 16) | bits[:, w:]


def _unpack_pairs_f32(p):
    lo = pltpu.bitcast(p << 16, F32)
    hi = pltpu.bitcast(p & jnp.uint32(0xFFFF0000), F32)
    return lo, hi


def _unpack_pairs(p):
    lo, hi = _unpack_pairs_f32(p)
    return lo.astype(BF16), hi.astype(BF16)


def _ada_kernel(c_ref, w_ref, b_ref, o_ref):
    c = c_ref[...]
    act = c * _sigmoid(c)
    o_ref[...] = jnp.dot(act, w_ref[...], precision=HIGHEST, preferred_element_type=F32) + b_ref[...]


def _ada_call(c, ada_w, ada_b):
    depth, d, n = ada_w.shape
    bsz = c.shape[0]
    tn = ADA_COLS
    return pl.pallas_call(
        _ada_kernel,
        grid=(depth, n // tn),
        in_specs=[
            pl.BlockSpec((bsz, d), lambda l, j: (0, 0)),
            pl.BlockSpec((None, d, tn), lambda l, j: (l, 0, j)),
            pl.BlockSpec((None, 1, tn), lambda l, j: (l, 0, j)),
        ],
        out_specs=pl.BlockSpec((None, bsz, tn), lambda l, j: (l, 0, j)),
        out_shape=jax.ShapeDtypeStruct((depth, bsz, n), F32),
        compiler_params=pltpu.CompilerParams(dimension_semantics=("parallel", "parallel"),
                                             vmem_limit_bytes=VMEM_LIMIT),
        name="ada_mod",
    )(c, ada_w, ada_b.reshape(depth, 1, n))


_C_Q = 0
_C_KV = Q_LORA_RANK
_C_PE = _C_KV + KV_LORA_RANK
_C_PESW = _C_PE + HEAD_SLAB
_C_U = _C_PESW + HEAD_SLAB


def _even_in_kernel(x_ref, mod_ref, pos_ref, g_ref, win_ref, gq_ref, wq_ref, gkv_ref, wk_ref, wv_ref,
                    tab_ref, q_ref, k_ref, v_ref, u_ref):
    x = x_ref[...]
    h = _mod_norm(x, g_ref[...], mod_ref[1:2, :], mod_ref[0:1, :])
    z = _bdot(h, win_ref[...])
    q_c = z[:, _C_Q:_C_KV]
    kv_c = z[:, _C_KV:_C_PE]
    kpe = z[:, _C_PE:_C_PESW]
    kpe_sw = z[:, _C_PESW:_C_U]
    u_ref[...] = z[:, _C_U:]

    tm = x.shape[0]
    packs = LANES // QK_ROPE_DIM
    qrows = tm // packs
    lane = lax.broadcasted_iota(jnp.int32, (1, LANES), 1)
    packed = jnp.zeros((qrows, LANES), F32)
    for part in range(packs):
        in_part = (lane >= part * QK_ROPE_DIM) & (lane < (part + 1) * QK_ROPE_DIM)
        packed = jnp.where(in_part, pos_ref[part * qrows:(part + 1) * qrows, :], packed)
    ang = packed * tab_ref[5:6, :]
    cs_p = jnp.cos(ang)
    sn_p = jnp.sin(ang)
    on_rope = (lane >= QK_NOPE_DIM) & (lane < QK_HEAD_DIM)
    cs_parts = []
    sn_parts = []
    for part in range(packs):
        shift = (QK_NOPE_DIM - part * QK_ROPE_DIM) % LANES
        cs_r = cs_p if shift == 0 else pltpu.roll(cs_p, shift, axis=1)
        sn_r = sn_p if shift == 0 else pltpu.roll(sn_p, shift, axis=1)
        cs_parts.append(jnp.where(on_rope, cs_r, 1.0))
        sn_parts.append(jnp.where(on_rope, sn_r, 0.0))
    cs = jnp.concatenate(cs_parts, axis=0)
    sn = jnp.concatenate(sn_parts, axis=0)
    gcq = cs * tab_ref[1:2, :]
    gsq = sn * tab_ref[2:3, :]
    gck = cs * tab_ref[3:4, :]
    gsk = sn * tab_ref[4:5, :]

    qn = q_c * _rms(q_c, Q_LORA_RANK) * gq_ref[...]
    qq = _bdot(qn, wq_ref[...])
    kvn = kv_c * _rms(kv_c, KV_LORA_RANK) * gkv_ref[...]
    kk = _bdot(kvn, wk_ref[...])
    v_ref[...] = lax.dot_general(wv_ref[...], kvn.astype(BF16), (((1,), (1,)), ((), ())),
                                 preferred_element_type=F32).astype(v_ref.dtype)

    pe_rot = kpe * gck + kpe_sw * gsk
    pe_ss = jnp.sum(kpe * kpe, axis=-1, keepdims=True)
    hw = MLA_HEADS * HEAD_SLAB
    for hd in range(MLA_HEADS):
        lo = hd * HEAD_SLAB
        qr = qq[:, lo:lo + HEAD_SLAB]
        qs = qq[:, hw + lo:hw + lo + HEAD_SLAB]
        rq = _rms(qr, QK_HEAD_DIM)
        q_ref[hd] = (rq * (qr * gcq + qs * gsq)).astype(q_ref.dtype)
        kr = kk[:, lo:lo + HEAD_SLAB]
        rk = lax.rsqrt((jnp.sum(kr * kr, axis=-1, keepdims=True) + pe_ss) * (1.0 / QK_HEAD_DIM) + NORM_EPS)
        k_ref[hd] = (rk * (kr * gck + pe_rot)).astype(k_ref.dtype)


def _even_in_call(x, mod, posf, g, prep):
    bsz, seq, d = x.shape
    tm = min(ROW_TILE, seq)
    full = lambda a: pl.BlockSpec(a.shape, lambda b, s: (0,) * a.ndim)
    return pl.pallas_call(
        _even_in_kernel,
        grid=(bsz, seq // tm),
        in_specs=[
            pl.BlockSpec((None, tm, d), lambda b, s: (b, s, 0)),
            pl.BlockSpec((None, 6, d), lambda b, s: (b, 0, 0)),
            pl.BlockSpec((None, tm, 1), lambda b, s: (b, s, 0)),
            full(g), full(prep["w_in"]), full(prep["gq"]), full(prep["wq"]), full(prep["gkv"]),
            full(prep["wk"]), full(prep["wv"]), full(prep["tab"]),
        ],
        out_specs=[
            pl.BlockSpec((None, MLA_HEADS, tm, HEAD_SLAB), lambda b, s: (b, 0, s, 0)),
            pl.BlockSpec((None, MLA_HEADS, tm, HEAD_SLAB), lambda b, s: (b, 0, s, 0)),
            pl.BlockSpec((None, MLA_HEADS * V_HEAD_DIM, tm), lambda b, s: (b, 0, s)),
            pl.BlockSpec((tm, d // 2), lambda b, s: (s, b)),
        ],
        out_shape=[
            jax.ShapeDtypeStruct((bsz, MLA_HEADS, seq, HEAD_SLAB), BF16),
            jax.ShapeDtypeStruct((bsz, MLA_HEADS, seq, HEAD_SLAB), BF16),
            jax.ShapeDtypeStruct((bsz, MLA_HEADS * V_HEAD_DIM, seq), BF16),
            jax.ShapeDtypeStruct((seq, bsz * (d // 2)), F32),
        ],
        compiler_params=pltpu.CompilerParams(dimension_semantics=("parallel", "parallel"),
                                             vmem_limit_bytes=VMEM_LIMIT),
        name="even_in",
    )(x, mod, posf, g, prep["w_in"], prep["gq"], prep["wq"], prep["gkv"], prep["wk"], prep["wv"], prep["tab"])


def _prep_even(even_w_in, q_norm_g, w_uq, kv_norm_g, w_ukv, q_head_g, k_head_g):
    d = even_w_in.shape[0]
    half = QK_ROPE_DIM // 2
    nope = QK_NOPE_DIM
    c_pe = Q_LORA_RANK + KV_LORA_RANK
    w_pe = even_w_in[:, c_pe:c_pe + QK_ROPE_DIM]
    zeros = lambda n: jnp.zeros((d, n), F32)
    pe_slab = jnp.concatenate([zeros(nope), w_pe, zeros(HEAD_SLAB - QK_HEAD_DIM)], axis=1)
    pe_sw = jnp.concatenate([zeros(nope), -w_pe[:, half:], w_pe[:, :half], zeros(HEAD_SLAB - QK_HEAD_DIM)], axis=1)
    w_in = jnp.concatenate([even_w_in[:, :c_pe], pe_slab, pe_sw, even_w_in[:, c_pe + QK_ROPE_DIM:]], axis=1)

    r = w_uq.shape[0]
    padq = jnp.zeros((r, MLA_HEADS, HEAD_SLAB - QK_HEAD_DIM), F32)
    wq_plain = jnp.concatenate([w_uq, padq], axis=2).reshape(r, MLA_HEADS * HEAD_SLAB)
    wq_sw = jnp.concatenate([jnp.zeros((r, MLA_HEADS, nope), F32), -w_uq[:, :, nope + half:],
                             w_uq[:, :, nope:nope + half], padq], axis=2).reshape(r, MLA_HEADS * HEAD_SLAB)
    wq = jnp.concatenate([wq_plain, wq_sw], axis=1)

    rk = w_ukv.shape[0]
    wk = jnp.concatenate([w_ukv[:, :, :nope], jnp.zeros((rk, MLA_HEADS, HEAD_SLAB - nope), F32)],
                         axis=2).reshape(rk, MLA_HEADS * HEAD_SLAB)
    wv = w_ukv[:, :, nope:].reshape(rk, MLA_HEADS * V_HEAD_DIM).T

    inv_freq = 1.0 / (ROPE_THETA ** (jnp.arange(half, dtype=F32) / half))
    pad_tail = jnp.zeros((HEAD_SLAB - QK_HEAD_DIM,), F32)
    freq_row = jnp.concatenate([jnp.zeros((nope,), F32), inv_freq, inv_freq, pad_tail])

    def gain_rows(gv, scale):
        plain = jnp.concatenate([gv, pad_tail]) * scale
        swapped = jnp.concatenate([jnp.zeros((nope,), F32), gv[nope + half:], gv[nope:nope + half], pad_tail]) * scale
        return plain, swapped

    gq_plain, gq_sw = gain_rows(q_head_g, QK_HEAD_DIM ** -0.5 * math.log2(math.e))
    gk_plain, gk_sw = gain_rows(k_head_g, 1.0)
    freq_packed = jnp.tile(jnp.concatenate([inv_freq, inv_freq]), LANES // QK_ROPE_DIM)
    tab = jnp.stack([freq_row, gq_plain, gq_sw, gk_plain, gk_sw, freq_packed, freq_row * 0, freq_row * 0])
    return {
        "w_in": w_in.astype(BF16), "gq": q_norm_g.reshape(1, -1), "wq": wq.astype(BF16),
        "gkv": kv_norm_g.reshape(1, -1), "wk": wk.astype(BF16), "wv": wv.astype(BF16), "tab": tab,
    }


def _attn_kernel(qi_ref, kj_ref, q_ref, k_ref, vt_ref, o_ref, m_sc, a_sc, acc_sc, s_sc, p_sc, vt_sc, *, tq, tk):
    step = pl.program_id(2)
    i = qi_ref[step]
    j = kj_ref[step]
    subs = tq // tk

    @pl.when(j == 0)
    def _():
        m_sc[...] = jnp.full(m_sc.shape, -jnp.inf, F32)
        acc_sc[...] = jnp.zeros(acc_sc.shape, F32)

    def sweep(diag_sub):
        lo = 0 if diag_sub is None else diag_sub * tk
        key = lax.broadcasted_iota(jnp.int32, (ATTN_KEYS, LANES), 0)
        qry = lax.broadcasted_iota(jnp.int32, (ATTN_KEYS, LANES), 1)
        pad_row = lax.broadcasted_iota(jnp.int32, (V_ROWS - V_HEAD_DIM, tk), 0)
        for hh in range(ATTN_HEADS):
            vt_sc[hh, 0:V_HEAD_DIM, :] = vt_ref[hh * V_HEAD_DIM:(hh + 1) * V_HEAD_DIM, :]
            vt_sc[hh, V_HEAD_DIM:V_ROWS, :] = jnp.where(pad_row == 0, 1.0, 0.0).astype(BF16)
        for k0 in range(0, tk, ATTN_KEYS):
            keys = pl.ds(k0, ATTN_KEYS)
            first = lo if diag_sub is None else lo + k0
            live = pl.ds(first, tq - first)
            for hh in range(ATTN_HEADS):
                s_sc[hh, :, live] = lax.dot_general(k_ref[hh, keys, :], q_ref[hh, live, :],
                                                    (((1,), (1,)), ((), ())), preferred_element_type=F32)
            for hh in range(ATTN_HEADS):
                for c0 in range(first, tq, LANES):
                    cols = pl.ds(c0, LANES)
                    s = s_sc[hh, :, cols]
                    if diag_sub is not None and c0 < first + ATTN_KEYS:
                        s = jnp.where(key <= qry + (c0 - first), s, -jnp.inf)
                    m_prev = m_sc[hh, :, cols]
                    m_new = jnp.maximum(m_prev, jnp.max(s, axis=0, keepdims=True))
                    a_sc[hh, :, cols] = jnp.exp2(m_prev - m_new)
                    m_sc[hh, :, cols] = m_new
                    p_sc[hh, :, cols] = jnp.exp2((s - m_new).astype(BF16))
                acc_sc[hh, :, live] = (acc_sc[hh, :, live] * a_sc[hh, :, live]
                                       + jnp.dot(vt_sc[hh, :, keys], p_sc[hh, :, live],
                                                 preferred_element_type=F32))

    diag = j - subs * i

    @pl.when(diag < 0)
    def _():
        sweep(None)

    for ds_ in range(subs):
        @pl.when(diag == ds_)
        def _(ds_=ds_):
            sweep(ds_)

    @pl.when(diag == subs - 1)
    def _():
        for pp in range(ATTN_HEADS // 2):
            outs = []
            for hh in (2 * pp, 2 * pp + 1):
                acc = acc_sc[hh]
                outs.append(acc[:V_HEAD_DIM, :] / acc[V_HEAD_DIM:V_HEAD_DIM + 1, :])
            o_ref[:, pp * LANES:(pp + 1) * LANES] = jnp.concatenate(outs, axis=0).T.astype(o_ref.dtype)


def _attn_call(q, k, vt):
    bsz, nh, seq, _ = q.shape
    tk = min(ATTN_TILE, seq)
    tq = min(ATTN_Q_BLOCKS * tk, seq)
    subs = tq // tk
    nq = seq // tq
    pairs = [(i, j) for i in range(nq) for j in range(subs * (i + 1))]
    qi = jnp.asarray([p[0] for p in pairs], jnp.int32)
    kj = jnp.asarray([p[1] for p in pairs], jnp.int32)
    kern = functools.partial(_attn_kernel, tq=tq, tk=tk)
    hp = ATTN_HEADS
    assert nh % hp == 0
    return pl.pallas_call(
        kern,
        grid_spec=pltpu.PrefetchScalarGridSpec(
            num_scalar_prefetch=2,
            grid=(bsz, nh // hp, len(pairs)),
            in_specs=[
                pl.BlockSpec((None, hp, tq, HEAD_SLAB), lambda b, h, p, qi, kj: (b, h, qi[p], 0)),
                pl.BlockSpec((None, hp, tk, HEAD_SLAB), lambda b, h, p, qi, kj: (b, h, kj[p], 0)),
                pl.BlockSpec((None, hp * V_HEAD_DIM, tk), lambda b, h, p, qi, kj: (b, h, kj[p])),
            ],
            out_specs=pl.BlockSpec((None, tq, hp * V_HEAD_DIM), lambda b, h, p, qi, kj: (b, qi[p], h)),
            scratch_shapes=[pltpu.VMEM((hp, 1, tq), F32), pltpu.VMEM((hp, 1, tq), F32),
                            pltpu.VMEM((hp, V_ROWS, tq), F32),
                            pltpu.VMEM((hp, ATTN_KEYS, tq), F32), pltpu.VMEM((hp, ATTN_KEYS, tq), BF16),
                            pltpu.VMEM((hp, V_ROWS, tk), BF16)],
        ),
        out_shape=jax.ShapeDtypeStruct((bsz, seq, nh * V_HEAD_DIM), BF16),
        compiler_params=pltpu.CompilerParams(
            dimension_semantics=("parallel", "parallel", "arbitrary"),
            vmem_limit_bytes=VMEM_LIMIT),
        name="mla_attention",
    )(qi, kj, q, k, vt)


def _s5_disc_kernel(are_ref, aim_ref, ldt_ref, bre_ref, bim_ref, abre_ref, abim_ref, bbre_ref, bbim_ref):
    dt = jnp.exp(ldt_ref[...])
    lam_re = jnp.minimum(are_ref[...], -1e-4)
    lam_im = aim_ref[...]
    mag = jnp.exp(lam_re * dt)
    ab_re = mag * jnp.cos(lam_im * dt)
    ab_im = mag * jnp.sin(lam_im * dt)
    den = lam_re * lam_re + lam_im * lam_im
    num_re = ab_re - 1.0
    f_re = (num_re * lam_re + ab_im * lam_im) / den
    f_im = (ab_im * lam_re - num_re * lam_im) / den
    abre_ref[...] = ab_re
    abim_ref[...] = ab_im
    br = bre_ref[...]
    bi = bim_ref[...]
    bbre_ref[...] = f_re[:, None, :] * br - f_im[:, None, :] * bi
    bbim_ref[...] = f_re[:, None, :] * bi + f_im[:, None, :] * br


def _s5_disc_call(a_re, a_im, log_dt, b_re, b_im):
    g, p = a_re.shape
    bre_t = jnp.swapaxes(b_re, 1, 2)
    bim_t = jnp.swapaxes(b_im, 1, 2)
    return pl.pallas_call(
        _s5_disc_kernel,
        out_shape=[jax.ShapeDtypeStruct((g, p), F32), jax.ShapeDtypeStruct((g, p), F32),
                   jax.ShapeDtypeStruct(bre_t.shape, F32), jax.ShapeDtypeStruct(bre_t.shape, F32)],
        name="s5_discretize",
    )(a_re, a_im, log_dt.reshape(g, 1), bre_t, bim_t)


def _block_diag_halves(m):
    g, r, c = m.shape
    gh = g // 2
    eye = jnp.eye(gh, dtype=m.dtype)
    mh = m.reshape(2, gh, r, c)
    return (mh[:, :, :, None, :] * eye[None, :, None, :, None]).reshape(2, gh * r, gh * c)


def _s5_kernel(u_ref, bre_ref, bim_ref, are_ref, aim_ref, cre_ref, cim_ref, d_ref, gw_ref, gb_ref,
               o_ref, sre, sim, dre, dim, xbr, xbi, *, steps):
    @pl.when(pl.program_id(0) == 0)
    def _():
        sre[...] = jnp.zeros(sre.shape, F32)
        sim[...] = jnp.zeros(sim.shape, F32)

    rows = steps * SUBLANES
    w = u_ref.shape[-1]
    u = u_ref[...].reshape(rows, w)
    ub = u.astype(BF16)
    kh = w // 2
    nh = dre.shape[1] // 2
    for hf in range(2):
        dre[:, hf * nh:(hf + 1) * nh] = jnp.dot(ub[:, hf * kh:(hf + 1) * kh], bre_ref[hf], preferred_element_type=F32)
        dim[:, hf * nh:(hf + 1) * nh] = jnp.dot(ub[:, hf * kh:(hf + 1) * kh], bim_ref[hf], preferred_element_type=F32)

    xr = sre[...]
    xi = sim[...]
    for t in range(0, steps, 2):
        pair_r = []
        pair_i = []
        for r0 in (t * SUBLANES, (t + 1) * SUBLANES):
            nr = are_ref[...] * xr - aim_ref[...] * xi + dre[r0:r0 + SUBLANES, :]
            ni = are_ref[...] * xi + aim_ref[...] * xr + dim[r0:r0 + SUBLANES, :]
            xr, xi = nr, ni
            pair_r.append(nr)
            pair_i.append(ni)
        xbr[t * SUBLANES:(t + 2) * SUBLANES, :] = jnp.concatenate(pair_r, axis=0).astype(BF16)
        xbi[t * SUBLANES:(t + 2) * SUBLANES, :] = jnp.concatenate(pair_i, axis=0).astype(BF16)
    sre[...] = xr
    sim[...] = xi

    ys = []
    for hf in range(2):
        yr = jnp.dot(xbr[:, hf * nh:(hf + 1) * nh], cre_ref[hf], preferred_element_type=F32)
        yi = jnp.dot(xbi[:, hf * nh:(hf + 1) * nh], cim_ref[hf], preferred_element_type=F32)
        ys.append(yr - yi)
    y = jnp.concatenate(ys, axis=1) + d_ref[...] * u
    g = _gelu(y)
    out = g * _sigmoid(_bdot(g, gw_ref[...]) + gb_ref[...])
    o_ref[...] = out.reshape(steps, SUBLANES, w).astype(o_ref.dtype)


def _s5_call(u_t, disc, c_re, c_im, d_skip, glu_w, glu_b):
    seq, bsz, w = u_t.shape
    assert bsz == SUBLANES
    ab_re, ab_im, bb_re, bb_im = disc
    g, p = ab_re.shape
    n_state = g * p
    bre = _block_diag_halves(bb_re).astype(BF16)
    bim = _block_diag_halves(bb_im).astype(BF16)
    cre = _block_diag_halves(jnp.swapaxes(c_re, 1, 2)).astype(BF16)
    cim = _block_diag_halves(jnp.swapaxes(c_im, 1, 2)).astype(BF16)
    steps = min(S5_STEPS, seq)
    full = lambda a: pl.BlockSpec(a.shape, lambda s: (0,) * a.ndim)
    rep = lambda a: jnp.broadcast_to(a.reshape(1, n_state), (bsz, n_state))
    args = (bre, bim, rep(ab_re), rep(ab_im), cre, cim,
            d_skip.reshape(1, w), glu_w.astype(BF16), glu_b.reshape(1, w))
    return pl.pallas_call(
        functools.partial(_s5_kernel, steps=steps),
        grid=(seq // steps,),
        in_specs=[pl.BlockSpec((steps, bsz, w), lambda s: (s, 0, 0))] + [full(a) for a in args],
        out_specs=pl.BlockSpec((steps, bsz, w), lambda s: (s, 0, 0)),
        out_shape=jax.ShapeDtypeStruct((seq, bsz, w), BF16),
        scratch_shapes=[pltpu.VMEM((bsz, n_state), F32), pltpu.VMEM((bsz, n_state), F32),
                        pltpu.VMEM((steps * bsz, n_state), F32), pltpu.VMEM((steps * bsz, n_state), F32),
                        pltpu.VMEM((steps * bsz, n_state), BF16), pltpu.VMEM((steps * bsz, n_state), BF16)],
        compiler_params=pltpu.CompilerParams(dimension_semantics=("arbitrary",), vmem_limit_bytes=VMEM_LIMIT),
        name="s5_scan",
    )(u_t, *args)


def _router_tail(x_new, mod_ref, gf_ref, rw_ref, rb_ref, h2_ref, te_ref, tet_ref, cnt_ref):
    h2 = _mod_norm(x_new, gf_ref[...], mod_ref[4:5, :], mod_ref[3:4, :])
    h2_ref[...] = _pack_pairs(h2)
    h_hi = h2.astype(BF16)
    h_lo = (h2 - h_hi.astype(F32)).astype(BF16)
    r_hi = jnp.dot(h_hi, rw_ref[...], preferred_element_type=F32)
    r_lo = jnp.dot(h_lo, rw_ref[...], preferred_element_type=F32)
    logits = r_hi[:, :LANES] + r_hi[:, LANES:] + r_lo[:, :LANES] + rb_ref[...]
    lane = lax.broadcasted_iota(jnp.int32, logits.shape, 1).astype(F32)
    vals = []
    idxs = []
    work = logits
    for _ in range(TOP_K):
        m = jnp.max(work, axis=-1, keepdims=True)
        idx = jnp.min(jnp.where(work == m, lane, float(LANES)), axis=-1, keepdims=True)
        vals.append(m)
        idxs.append(idx)
        work = jnp.where(lane == idx, NEG_BIG * 2.0, work)
    exps = [jnp.exp(vv - vals[0]) for vv in vals]
    tot = exps[0] + exps[1] + exps[2] + exps[3]
    te = jnp.zeros(logits.shape, F32)
    picked = jnp.zeros(logits.shape, F32)
    for kk in range(TOP_K):
        te = jnp.where(lane == float(kk), idxs[kk], te)
        te = jnp.where(lane == float(TOP_K + kk), exps[kk] / tot, te)
        picked = picked + jnp.where(lane == idxs[kk], 1.0, 0.0)
    te_ref[...] = te[:, :2 * TOP_K]
    tet_ref[...] = te.T[:2 * TOP_K, :]
    cnt_ref[...] = jnp.sum(picked, axis=0, keepdims=True)


def _mix_out_kernel(x_ref, a_ref, s_ref, mod_ref, wo_ref, gf_ref, rw_ref, rb_ref, xo_ref, h2_ref, te_ref,
                    tet_ref, cnt_ref):
    ka = a_ref.shape[-1]
    mix = jnp.dot(a_ref[...], wo_ref[:ka, :], preferred_element_type=F32)
    mix = mix + jnp.dot(s_ref[...], wo_ref[ka:, :], preferred_element_type=F32)
    x_new = x_ref[...] + mod_ref[2:3, :] * mix
    xo_ref[...] = x_new
    _router_tail(x_new, mod_ref, gf_ref, rw_ref, rb_ref, h2_ref, te_ref, tet_ref, cnt_ref)


def _tail_out_specs(bsz, seq, tm, d):
    nt = seq // tm
    specs = [
        pl.BlockSpec((None, tm, d), lambda b, s: (b, s, 0)),
        pl.BlockSpec((tm, d // 2), lambda b, s: (b * nt + s, 0)),
        pl.BlockSpec((tm, 2 * TOP_K), lambda b, s: (b * nt + s, 0)),
        pl.BlockSpec((2 * TOP_K, tm), lambda b, s: (0, b * nt + s)),
        pl.BlockSpec((None, 1, LANES), lambda b, s: (b * nt + s, 0, 0)),
    ]
    shapes = [
        jax.ShapeDtypeStruct((bsz, seq, d), F32),
        jax.ShapeDtypeStruct((bsz * seq, d // 2), jnp.uint32),
        jax.ShapeDtypeStruct((bsz * seq, 2 * TOP_K), F32),
        jax.ShapeDtypeStruct((2 * TOP_K, bsz * seq), F32),
        jax.ShapeDtypeStruct((bsz * nt, 1, LANES), F32),
    ]
    return specs, shapes


def _router_pad(router_w, router_b):
    d, e = router_w.shape
    rw = jnp.concatenate([router_w, jnp.zeros((d, LANES - e), F32)], axis=1)
    rw_hi = rw.astype(BF16)
    rw_lo = (rw - rw_hi.astype(F32)).astype(BF16)
    rb = jnp.concatenate([router_b, jnp.full((LANES - e,), NEG_BIG, F32)]).reshape(1, LANES)
    return jnp.concatenate([rw_hi, rw_lo], axis=1), rb


def _mix_out_call(x, attn, ssm_t, mod, w_out, gf, rw, rb):
    bsz, seq, d = x.shape
    tm = min(ROW_TILE, seq)
    ka = attn.shape[-1]
    ks = ssm_t.shape[-1] // bsz
    full = lambda a: pl.BlockSpec(a.shape, lambda b, s: (0,) * a.ndim)
    out_specs, out_shape = _tail_out_specs(bsz, seq, tm, d)
    return pl.pallas_call(
        _mix_out_kernel,
        grid=(bsz, seq // tm),
        in_specs=[
            pl.BlockSpec((None, tm, d), lambda b, s: (b, s, 0)),
            pl.BlockSpec((None, tm, ka), lambda b, s: (b, s, 0)),
            pl.BlockSpec((tm, ks), lambda b, s: (s, b)),
            pl.BlockSpec((None, 6, d), lambda b, s: (b, 0, 0)),
            full(w_out), full(gf), full(rw), full(rb),
        ],
        out_specs=out_specs,
        out_shape=out_shape,
        compiler_params=pltpu.CompilerParams(dimension_semantics=("parallel", "parallel"),
                                             vmem_limit_bytes=VMEM_LIMIT),
        name="even_out",
    )(x, attn, ssm_t, mod, w_out, gf, rw, rb)


def _moe_sum(yg_ref, te_ref):
    te = te_ref[...]
    acc_lo = acc_hi = None
    for kk in range(TOP_K):
        lo, hi = _unpack_pairs_f32(yg_ref[kk])
        gate = te[:, TOP_K + kk:TOP_K + kk + 1]
        acc_lo = gate * lo if kk == 0 else acc_lo + gate * lo
        acc_hi = gate * hi if kk == 0 else acc_hi + gate * hi
    return jnp.concatenate([acc_lo, acc_hi], axis=1)


def _odd_kernel(x_ref, yg_ref, tep_ref, modp_ref, mod_ref, g_ref, win_ref, icnt_ref, wp_ref, ps_ref, gv_ref,
                wsp_ref, bsp_ref, wo_ref, gf_ref, rw_ref, rb_ref, xo_ref, h2_ref, te_ref, tet_ref, cnt_ref, ext_sc):
    tm = x_ref.shape[0]
    pw = wp_ref.shape[-1]
    width = pw * len(POOL_WINDOWS)

    @pl.when(pl.program_id(1) == 0)
    def _():
        ext_sc[0:POOL_HALO, :] = jnp.zeros((POOL_HALO, width), F32)

    x = x_ref[...] + modp_ref[5:6, :] * _moe_sum(yg_ref, tep_ref)
    h = _mod_norm(x, g_ref[...], mod_ref[1:2, :], mod_ref[0:1, :])
    z = _bdot(h, win_ref[...])
    up = z[:, :width]
    ext_sc[POOL_HALO:POOL_HALO + tm, :] = up

    pooled = []
    for gi, win in enumerate(POOL_WINDOWS):
        cols = slice(gi * pw, (gi + 1) * pw)
        acc = up[:, cols]
        for lag in range(1, win):
            acc = acc + ext_sc[POOL_HALO - lag:POOL_HALO - lag + tm, cols]
        pg = acc * icnt_ref[:, gi:gi + 1] - up[:, cols]
        pooled.append(_bdot(pg, wp_ref[gi]) * ps_ref[:, cols])
    ext_sc[0:POOL_HALO, :] = ext_sc[tm:tm + POOL_HALO, :]
    pooled = jnp.concatenate(pooled, axis=1)

    ug = _gelu(z[:, width:2 * width])
    vg = _gelu(z[:, 2 * width:])
    vn = (vg * _rms(vg, width) * gv_ref[...]).astype(BF16)
    hd = width // SGU_HEADS
    chunks = []
    for ci in range(tm // SGU_CHUNK):
        heads = []
        for hh in range(SGU_HEADS):
            blk = vn[ci * SGU_CHUNK:(ci + 1) * SGU_CHUNK, hh * hd:(hh + 1) * hd]
            heads.append(jnp.dot(wsp_ref[hh], blk, preferred_element_type=F32) + bsp_ref[hh])
        chunks.append(jnp.concatenate(heads, axis=1))
    gated = ug * jnp.concatenate(chunks, axis=0)

    mix = _bdot(pooled, wo_ref[:width, :]) + _bdot(gated, wo_ref[width:, :])
    x_new = x + mod_ref[2:3, :] * mix
    xo_ref[...] = x_new
    _router_tail(x_new, mod_ref, gf_ref, rw_ref, rb_ref, h2_ref, te_ref, tet_ref, cnt_ref)


def _odd_call(grp, mod_prev, mod, nb, g, w_in, pool_w, pool_scale, sgu_norm_g, sgu_w, sgu_b, w_out, gf, rw, rb):
    x = grp["x"]
    _, seq, d = x.shape
    tm = min(ODD_TILE, seq)
    nt = seq // tm
    xb0, tok0, mb0 = grp["x_b0"], grp["tok0"], grp["mod_b0"]
    tile0 = tok0 // tm
    yg = grp["yg"]
    width = pool_scale.shape[0]
    hd = width // SGU_HEADS
    t = jnp.arange(seq, dtype=jnp.int32)
    icnt = jnp.stack([1.0 / jnp.minimum(t + 1, wn).astype(F32) for wn in POOL_WINDOWS], axis=1)
    wsp = jnp.tril(sgu_w).astype(BF16)
    bsp = jnp.broadcast_to(sgu_b[:, :, None], (SGU_HEADS, SGU_CHUNK, hd))
    args = (g, w_in.astype(BF16), icnt, pool_w.astype(BF16), pool_scale.reshape(1, width),
            sgu_norm_g.reshape(1, width), wsp, bsp, w_out.astype(BF16), gf, rw, rb)
    full = lambda a: pl.BlockSpec(a.shape, lambda b, s: (0,) * a.ndim)
    in_specs = [pl.BlockSpec((None, tm, d), lambda b, s: (xb0 + b, s, 0)),
                pl.BlockSpec((TOP_K, tm, yg.shape[-1]), lambda b, s: (0, b * nt + s, 0)),
                pl.BlockSpec((tm, 2 * TOP_K), lambda b, s: (tile0 + b * nt + s, 0)),
                pl.BlockSpec((None, 6, d), lambda b, s: (mb0 + b, 0, 0)),
                pl.BlockSpec((None, 6, d), lambda b, s: (mb0 + b, 0, 0))]
    for idx, a in enumerate(args):
        in_specs.append(pl.BlockSpec((tm, len(POOL_WINDOWS)), lambda b, s: (s, 0)) if idx == 2 else full(a))
    out_specs, out_shape = _tail_out_specs(nb, seq, tm, d)
    return pl.pallas_call(
        _odd_kernel,
        grid=(nb, nt),
        in_specs=in_specs,
        out_specs=out_specs,
        out_shape=out_shape,
        scratch_shapes=[pltpu.VMEM((tm + POOL_HALO, width), F32)],
        compiler_params=pltpu.CompilerParams(dimension_semantics=("parallel", "arbitrary"),
                                             vmem_limit_bytes=VMEM_LIMIT),
        name="odd_mixer",
    )(x, yg, grp["te"], mod_prev, mod, *args)


def _dest_kernel(tet_ref, base_ref, tri_ref, dst_ref):
    tr = tet_ref.shape[1]
    tet = tet_ref[...]
    expert = lax.broadcasted_iota(jnp.int32, (LANES, tr), 0).astype(F32)
    hots = [tet[kk:kk + 1, :] == expert for kk in range(TOP_K)]
    oh = jnp.zeros((LANES, tr), F32)
    for hot in hots:
        oh = oh + jnp.where(hot, 1.0, 0.0)
    before = jnp.dot(oh.astype(BF16), tri_ref[...], preferred_element_type=F32) + base_ref[...]
    row = lax.broadcasted_iota(jnp.int32, (2 * TOP_K, tr), 0)
    dst = jnp.zeros((2 * TOP_K, tr), F32)
    for kk, hot in enumerate(hots):
        dst = jnp.where(row == kk, jnp.sum(jnp.where(hot, before, 0.0), axis=0, keepdims=True), dst)
    dst_ref[...] = dst.astype(jnp.int32)


def _dest_call(tet, base, tok0, n_tok):
    tiles = base.shape[0]
    tr = n_tok // tiles
    tile0 = tok0 // tr
    tri = (jnp.arange(tr)[:, None] < jnp.arange(tr)[None, :]).astype(BF16)
    return pl.pallas_call(
        _dest_kernel,
        grid=(tiles,),
        in_specs=[pl.BlockSpec((2 * TOP_K, tr), lambda i: (0, tile0 + i)),
                  pl.BlockSpec((None, LANES, 1), lambda i: (i, 0, 0)),
                  pl.BlockSpec((tr, tr), lambda i: (0, 0))],
        out_specs=pl.BlockSpec((2 * TOP_K, tr), lambda i: (0, i)),
        out_shape=jax.ShapeDtypeStruct((2 * TOP_K, n_tok), jnp.int32),
        compiler_params=pltpu.CompilerParams(dimension_semantics=("parallel",)),
        name="route_dest",
    )(tet, base, tri)


def _sc_gather(table, idx):
    n = idx.shape[0]
    per_w = n // SC_WORKERS
    assert per_w * SC_WORKERS == n and per_w % SC_CHUNK == 0
    n_chunks = per_w // SC_CHUNK
    row_shape = table.shape[1:]
    mesh = plsc.VectorSubcoreMesh(core_axis_name="c", subcore_axis_name="s")

    @functools.partial(
        pl.kernel, mesh=mesh,
        out_type=jax.ShapeDtypeStruct((n,) + row_shape, table.dtype),
        scratch_types=[pltpu.VMEM((SC_CHUNK,), jnp.int32), pltpu.VMEM((SC_CHUNK,) + row_shape, table.dtype),
                       pltpu.SemaphoreType.DMA],
        name="sc_row_gather",
    )
    def gather(table_hbm, idx_hbm, out_hbm, idx_v, rows_v, sem):
        wid = lax.axis_index("s") * SC_CORES + lax.axis_index("c")
        base = wid * per_w

        @pl.loop(0, n_chunks)
        def _(ci):
            off = pl.multiple_of(base + ci * SC_CHUNK, SC_CHUNK)
            pltpu.sync_copy(idx_hbm.at[pl.ds(off, SC_CHUNK)], idx_v)
            pltpu.async_copy(table_hbm.at[idx_v], rows_v, sem).wait()
            pltpu.sync_copy(rows_v, out_hbm.at[pl.ds(off, SC_CHUNK)])

    return gather(table, idx)


def _sc_scatter(rows, dests, n_out, tok0):
    t = dests[0].shape[0]
    per_w = t // SC_WORKERS
    assert per_w * SC_WORKERS == t and per_w % SC_CHUNK == 0
    n_chunks = per_w // SC_CHUNK
    row_shape = rows.shape[1:]
    nk = len(dests)
    mesh = plsc.VectorSubcoreMesh(core_axis_name="c", subcore_axis_name="s")

    @functools.partial(
        pl.kernel, mesh=mesh,
        out_type=jax.ShapeDtypeStruct((n_out,) + row_shape, rows.dtype),
        scratch_types=[pltpu.VMEM((SC_CHUNK,), jnp.int32)] * nk
        + [pltpu.VMEM((SC_CHUNK,) + row_shape, rows.dtype), pltpu.SemaphoreType.DMA],
        name="sc_row_scatter",
    )
    def scatter(rows_hbm, *rest):
        dest_hbm = rest[:nk]
        out_hbm = rest[nk]
        idx_v = rest[nk + 1:2 * nk + 1]
        rows_v, sem = rest[2 * nk + 1:]
        wid = lax.axis_index("s") * SC_CORES + lax.axis_index("c")
        base = wid * per_w

        @pl.loop(0, n_chunks)
        def _(ci):
            off = pl.multiple_of(base + ci * SC_CHUNK, SC_CHUNK)
            src = pl.multiple_of(tok0 + off, SC_CHUNK)
            pltpu.sync_copy(rows_hbm.at[pl.ds(src, SC_CHUNK)], rows_v)
            for kk in range(nk):
                pltpu.sync_copy(dest_hbm[kk].at[pl.ds(off, SC_CHUNK)], idx_v[kk])
            copies = [pltpu.async_copy(rows_v, out_hbm.at[idx_v[kk]], sem) for kk in range(nk)]
            for cp in copies:
                cp.wait()

    return scatter(rows, *dests)


def _expert_kernel(be_ref, nv_ref, ord_ref, ue_ref, nu_ref, x_ref, wgu_hbm, bgu_ref, wdn_hbm, bdn_ref, y_ref,
                   wgu_f32, wdn_f32, wgu_bf, wdn_bf, sem, *, layer):
    i = pl.program_id(0)
    used = i < nu_ref[0]
    pos = ord_ref[i]
    fresh = jnp.logical_or(i == 0, ord_ref[jnp.maximum(i - 1, 0)] != pos)

    def weight_copies(expert):
        return (pltpu.make_async_copy(wgu_hbm.at[layer, expert], wgu_f32, sem.at[0]),
                pltpu.make_async_copy(wdn_hbm.at[layer, expert], wdn_f32, sem.at[1]))

    @pl.when(i == 0)
    def _():
        for cp in weight_copies(ue_ref[0]):
            cp.start()

    @pl.when(jnp.logical_and(used, fresh))
    def _():
        for cp in weight_copies(ue_ref[pos]):
            cp.wait()
        wgu_bf[...] = wgu_f32[...].astype(BF16)
        wdn_bf[...] = wdn_f32[...].astype(BF16)

        @pl.when(pos + 1 < nu_ref[1])
        def _():
            for cp in weight_copies(ue_ref[pos + 1]):
                cp.start()

    def ffn(rows):
        x = jnp.concatenate(_unpack_pairs(x_ref[0:rows, :]), axis=1)
        z = jnp.dot(x, wgu_bf[...], preferred_element_type=F32) + bgu_ref[...]
        ff = z.shape[-1] // 2
        gate = jnp.minimum(z[:, :ff], SWIGLU_LIMIT)
        lin = jnp.clip(z[:, ff:], -SWIGLU_LIMIT, SWIGLU_LIMIT)
        act = gate * _sigmoid(SWIGLU_ALPHA * gate) * (lin + 1.0)
        y = _bdot(act, wdn_bf[...]) + bdn_ref[...]
        y_ref[0:rows, :] = _pack_pairs(y)

    nv = nv_ref[i]
    below = 0
    for size in MOE_PATHS:
        @pl.when(jnp.logical_and(used, jnp.logical_and(nv > below, nv <= size)))
        def _(size=size):
            ffn(size)
        below = size


def _expert_call(layer, block_e, block_valid, block_pos, used_experts, n_used, xs, w_gu, b_gu, w_dn, b_dn):
    n_rows, dh = xs.shape
    depth, e, d, ff2 = w_gu.shape
    ff = ff2 // 2
    nb = n_rows // MOE_ROWS
    row_map = lambda i, be, nv, po, ue, nu: (jnp.minimum(i, nu[0] - 1), 0)
    b_map = lambda i, be, nv, po, ue, nu: (layer, be[i], 0, 0)
    return pl.pallas_call(
        functools.partial(_expert_kernel, layer=layer),
        grid_spec=pltpu.PrefetchScalarGridSpec(
            num_scalar_prefetch=5,
            grid=(nb,),
            in_specs=[
                pl.BlockSpec((MOE_ROWS, dh), row_map),
                pl.BlockSpec(memory_space=pl.ANY),
                pl.BlockSpec((None, None, 1, ff2), b_map),
                pl.BlockSpec(memory_space=pl.ANY),
                pl.BlockSpec((None, None, 1, d), b_map),
            ],
            out_specs=pl.BlockSpec((MOE_ROWS, dh), row_map),
            scratch_shapes=[pltpu.VMEM((d, ff2), F32), pltpu.VMEM((ff, d), F32),
                            pltpu.VMEM((d, ff2), BF16), pltpu.VMEM((ff, d), BF16),
                            pltpu.SemaphoreType.DMA((2,))],
        ),
        out_shape=jax.ShapeDtypeStruct((n_rows, dh), jnp.uint32),
        compiler_params=pltpu.CompilerParams(dimension_semantics=("arbitrary",), vmem_limit_bytes=VMEM_LIMIT),
        name="moe_experts",
    )(block_e, block_valid, block_pos, used_experts, n_used, xs, w_gu, b_gu.reshape(depth, e, 1, ff2), w_dn,
      b_dn.reshape(depth, e, 1, d))


def _combine_kernel(x_ref, yg_ref, te_ref, mod_ref, *rest):
    o_ref = rest[-1]
    o_ref[...] = x_ref[...] + mod_ref[5:6, :] * _moe_sum(yg_ref, te_ref)


def _combine_call(grp, mod, prev, bsz, nb):
    x = grp["x"]
    _, seq, d = x.shape
    tm = min(ROW_TILE, seq)
    nt = seq // tm
    xb0, mb0 = grp["x_b0"], grp["mod_b0"]
    tile0 = grp["tok0"] // tm
    yg = grp["yg"]
    in_specs = [
        pl.BlockSpec((None, tm, d), lambda b, s: (xb0 + b, s, 0)),
        pl.BlockSpec((TOP_K, tm, yg.shape[-1]), lambda b, s: (0, b * nt + s, 0)),
        pl.BlockSpec((tm, 2 * TOP_K), lambda b, s: (tile0 + b * nt + s, 0)),
        pl.BlockSpec((None, 6, d), lambda b, s: (mb0 + b, 0, 0)),
    ]
    args = [x, yg, grp["te"], mod]
    aliases = {}
    if prev is not None:
        in_specs.append(pl.BlockSpec(memory_space=pl.ANY))
        args.append(prev)
        aliases = {len(args) - 1: 0}
    return pl.pallas_call(
        _combine_kernel,
        grid=(nb, nt),
        in_specs=in_specs,
        out_specs=pl.BlockSpec((None, tm, d), lambda b, s: (mb0 + b, s, 0)),
        out_shape=jax.ShapeDtypeStruct((bsz, seq, d), F32),
        input_output_aliases=aliases,
        compiler_params=pltpu.CompilerParams(dimension_semantics=("parallel", "parallel"),
                                             vmem_limit_bytes=VMEM_LIMIT),
        name="moe_combine",
    )(*args)


def _moe_rows(layer, grp, gt, w_gu, b_gu, w_dn, b_dn):
    h2 = grp["h2"]
    dh = h2.shape[-1]
    n_rows = -(-(gt * TOP_K + N_EXPERTS * (MOE_ROWS - 1)) // MOE_ROWS) * MOE_ROWS
    nb = n_rows // MOE_ROWS
    first_row = jnp.arange(nb, dtype=jnp.int32) * MOE_ROWS
    upto = jnp.arange(LANES)[:, None] <= jnp.arange(LANES)[None, :]
    g_cnt = grp["cnt"][:, 0, :].astype(jnp.int32)
    counts = jnp.sum(g_cnt, axis=0)
    padded = (counts + MOE_ROWS - 1) // MOE_ROWS * MOE_ROWS
    pad_end = jnp.sum(jnp.where(upto, padded[:, None], 0), axis=0)
    pad_start = pad_end - padded
    tile_base = pad_start[None, :] + jnp.cumsum(g_cnt, axis=0) - g_cnt
    dest = _dest_call(grp["tet"], tile_base.astype(F32)[:, :, None], grp["tok0"], gt)
    dests = [dest[kk] for kk in range(TOP_K)]
    block_e = jnp.minimum(jnp.sum(pad_end[None, :N_EXPERTS] <= first_row[:, None], axis=1),
                          N_EXPERTS - 1).astype(jnp.int32)
    valid_end = (pad_start + counts)[block_e]
    block_valid = jnp.clip(valid_end - first_row, 0, MOE_ROWS).astype(jnp.int32)
    owns = counts[:N_EXPERTS] > 0
    expert_pos = jnp.cumsum(owns.astype(jnp.int32)) - 1
    slot = jnp.arange(N_EXPERTS, dtype=jnp.int32)
    used_experts = jnp.sum(jnp.where(owns[None, :] & (expert_pos[None, :] == slot[:, None]),
                                     slot[None, :], 0), axis=1).astype(jnp.int32)
    block_pos = expert_pos[block_e].astype(jnp.int32)
    n_used = jnp.stack([pad_end[N_EXPERTS - 1] // MOE_ROWS, jnp.sum(owns)]).astype(jnp.int32)
    xs = _sc_scatter(h2, dests, n_rows, grp["tok0"])
    y = _expert_call(layer, block_e, block_valid, block_pos, used_experts, n_used, xs, w_gu, b_gu, w_dn, b_dn)
    return _sc_gather(y, dest[:TOP_K].reshape(-1)).reshape(TOP_K, gt, dh)


def kernel(x, c, positions, ada_w, ada_b, norm_mix_g, norm_ffn_g, router_w, router_b, moe_w_gu, moe_b_gu,
           moe_w_dn, moe_b_dn, even_w_in, mla_q_norm_g, mla_w_uq, mla_kv_norm_g, mla_w_ukv, mla_q_head_g,
           mla_k_head_g, s5_a_re, s5_a_im, s5_log_dt, s5_b_re, s5_b_im, s5_c_re, s5_c_im, s5_d, s5_glu_w,
           s5_glu_b, even_w_out, odd_w_in, pool_w, pool_scale, sgu_norm_g, sgu_w, sgu_b, odd_w_out):
    bsz, seq, d = x.shape
    depth = ada_w.shape[0]
    mods = _ada_call(c, ada_w, ada_b).reshape(depth, bsz, 6, d)
    posf = positions.astype(F32).reshape(bsz, seq, 1)
    splits = MOE_SPLITS if bsz % MOE_SPLITS == 0 else 1
    gb = bsz // splits
    gt = gb * seq
    tiles_g = gt // min(ROW_TILE, seq)

    def settle(groups, mod):
        out = None
        for grp in groups:
            out = _combine_call(grp, mod, out, bsz, gb)
        return out

    pending = None
    for layer in range(depth):
        mod = mods[layer]
        i = layer // 2
        g_mix = norm_mix_g[layer].reshape(1, d)
        g_ffn = norm_ffn_g[layer].reshape(1, d)
        rw, rb = _router_pad(router_w[layer], router_b[layer])
        if layer % 2 == 0:
            if pending is not None:
                x = settle(pending, mods[layer - 1])
            prep = _prep_even(even_w_in[i], mla_q_norm_g[i], mla_w_uq[i], mla_kv_norm_g[i], mla_w_ukv[i],
                              mla_q_head_g[i], mla_k_head_g[i])
            q, k, v, u_t = _even_in_call(x, mod, posf, g_mix, prep)
            attn = _attn_call(q, k, v)
            disc = _s5_disc_call(s5_a_re[i], s5_a_im[i], s5_log_dt[i], s5_b_re[i], s5_b_im[i])
            ssm_t = _s5_call(u_t.reshape(seq, bsz, d // 2), disc, s5_c_re[i], s5_c_im[i], s5_d[i],
                             s5_glu_w[i], s5_glu_b[i])
            x_new, h2, te, tet, tile_cnt = _mix_out_call(x, attn, ssm_t.reshape(seq, bsz * (d // 2)), mod,
                                                         even_w_out[i].astype(BF16), g_ffn, rw, rb)
            groups = [dict(x=x_new, x_b0=gi * gb, h2=h2, te=te, tet=tet, tok0=gi * gt, mod_b0=gi * gb,
                           cnt=tile_cnt[gi * tiles_g:(gi + 1) * tiles_g]) for gi in range(splits)]
        else:
            groups = []
            for gi, grp in enumerate(pending):
                x_new, h2, te, tet, tile_cnt = _odd_call(grp, mods[layer - 1], mod, gb, g_mix, odd_w_in[i],
                                                         pool_w[i], pool_scale[i], sgu_norm_g[i], sgu_w[i],
                                                         sgu_b[i], odd_w_out[i], g_ffn, rw, rb)
                groups.append(dict(x=x_new, x_b0=0, h2=h2, te=te, tet=tet, tok0=0, mod_b0=gi * gb,
                                   cnt=tile_cnt))
        for grp in groups:
            grp["yg"] = _moe_rows(layer, grp, gt, moe_w_gu, moe_b_gu, moe_w_dn, moe_b_dn)
        pending = groups
    return settle(pending, mods[depth - 1])
```

```python
import functools
import math

import jax
import jax.numpy as jnp
from jax import lax
from jax.experimental import pallas as pl
from jax.experimental.pallas import tpu as pltpu
from jax.experimental.pallas import tpu_sc as plsc

F32 = jnp.float32
BF16 = jnp.bfloat16
HIGHEST = lax.Precision.HIGHEST

NORM_EPS = 1e-6
MLA_HEADS = 8
QK_NOPE_DIM = 64
QK_ROPE_DIM = 32
QK_HEAD_DIM = QK_NOPE_DIM + QK_ROPE_DIM
V_HEAD_DIM = 64
Q_LORA_RANK = 256
KV_LORA_RANK = 128
ROPE_THETA = 10000.0
POOL_WINDOWS = (2, 4, 8, 16)
SGU_HEADS = 4
SGU_CHUNK = 128
N_EXPERTS = 32
TOP_K = 4
SWIGLU_ALPHA = 1.702
SWIGLU_LIMIT = 7.0

LANES = 128
SUBLANES = 8
HEAD_SLAB = LANES
POOL_HALO = 16
ROW_TILE = 1024
ODD_TILE = 512
TILE_PARTS = 2
ATTN_TILE = 512
ATTN_Q_BLOCKS = 2
ATTN_ROWS = 32
ATTN_HEADS = 4
S5_STEPS = 64
MOE_ROWS = 1024
MOE_PATHS = (128, 256, 512, 768, MOE_ROWS)
MOE_SPLITS = 2
SC_CORES = 2
SC_WORKERS = SC_CORES * 16
SC_CHUNK = 64
ADA_COLS = 1536
VMEM_LIMIT = 56 * 1024 * 1024
NEG_BIG = -1e30


def _sigmoid(v):
    return 1.0 / (1.0 + jnp.exp(-v))


def _gelu(v):
    return 0.5 * v * (1.0 + jnp.tanh(math.sqrt(2.0 / math.pi) * (v + 0.044715 * (v * v * v))))


def _rms(v, width):
    return lax.rsqrt(jnp.sum(v * v, axis=-1, keepdims=True) * (1.0 / width) + NORM_EPS)


def _mod_norm(x, g, sc, sh):
    return x * _rms(x, x.shape[-1]) * (g * (1.0 + sc)) + sh


def _bdot(a, b):
    return jnp.dot(a.astype(BF16), b, preferred_element_type=F32)


def _pack_pairs(v):
    w = v.shape[-1] // 2
    bits = pltpu.bitcast(v.astype(BF16).astype(F32), jnp.uint32)
    return (bits[:, :w] >> 16) | bits[:, w:]


def _unpack_pairs_f32(p):
    lo = pltpu.bitcast(p << 16, F32)
    hi = pltpu.bitcast(p & jnp.uint32(0xFFFF0000), F32)
    return lo, hi


def _unpack_pairs(p):
    lo, hi = _unpack_pairs_f32(p)
    return lo.astype(BF16), hi.astype(BF16)


def _ada_kernel(c_ref, w_ref, b_ref, o_ref):
    c = c_ref[...]
    act = c * _sigmoid(c)
    o_ref[...] = jnp.dot(act, w_ref[...], precision=HIGHEST, preferred_element_type=F32) + b_ref[...]


def _ada_call(c, ada_w, ada_b):
    depth, d, n = ada_w.shape
    bsz = c.shape[0]
    tn = ADA_COLS
    return pl.pallas_call(
        _ada_kernel,
        grid=(depth, n // tn),
        in_specs=[
            pl.BlockSpec((bsz, d), lambda l, j: (0, 0)),
            pl.BlockSpec((None, d, tn), lambda l, j: (l, 0, j)),
            pl.BlockSpec((None, 1, tn), lambda l, j: (l, 0, j)),
        ],
        out_specs=pl.BlockSpec((None, bsz, tn), lambda l, j: (l, 0, j)),
        out_shape=jax.ShapeDtypeStruct((depth, bsz, n), F32),
        compiler_params=pltpu.CompilerParams(dimension_semantics=("parallel", "parallel"),
                                             vmem_limit_bytes=VMEM_LIMIT),
        name="ada_mod",
    )(c, ada_w, ada_b.reshape(depth, 1, n))


_C_Q = 0
_C_KV = Q_LORA_RANK
_C_PE = _C_KV + KV_LORA_RANK
_C_PESW = _C_PE + HEAD_SLAB
_C_U = _C_PESW + HEAD_SLAB


def _even_in_kernel(x_ref, mod_ref, pos_ref, g_ref, win_ref, gq_ref, wq_ref, gkv_ref, wk_ref, wv_ref,
                    tab_ref, q_ref, k_ref, v_ref, u_ref):
    x = x_ref[...]
    h = _mod_norm(x, g_ref[...], mod_ref[1:2, :], mod_ref[0:1, :])
    z = _bdot(h, win_ref[...])
    q_c = z[:, _C_Q:_C_KV]
    kv_c = z[:, _C_KV:_C_PE]
    kpe = z[:, _C_PE:_C_PESW]
    kpe_sw = z[:, _C_PESW:_C_U]
    u_ref[...] = z[:, _C_U:]

    tm = x.shape[0]
    packs = LANES // QK_ROPE_DIM
    qrows = tm // packs
    lane = lax.broadcasted_iota(jnp.int32, (1, LANES), 1)
    packed = jnp.zeros((qrows, LANES), F32)
    for part in range(packs):
        in_part = (lane >= part * QK_ROPE_DIM) & (lane < (part + 1) * QK_ROPE_DIM)
        packed = jnp.where(in_part, pos_ref[part * qrows:(part + 1) * qrows, :], packed)
    ang = packed * tab_ref[5:6, :]
    cs_p = jnp.cos(ang)
    sn_p = jnp.sin(ang)
    on_rope = (lane >= QK_NOPE_DIM) & (lane < QK_HEAD_DIM)
    cs_parts = []
    sn_parts = []
    for part in range(packs):
        shift = (QK_NOPE_DIM - part * QK_ROPE_DIM) % LANES
        cs_r = cs_p if shift == 0 else pltpu.roll(cs_p, shift, axis=1)
        sn_r = sn_p if shift == 0 else pltpu.roll(sn_p, shift, axis=1)
        cs_parts.append(jnp.where(on_rope, cs_r, 1.0))
        sn_parts.append(jnp.where(on_rope, sn_r, 0.0))
    cs = jnp.concatenate(cs_parts, axis=0)
    sn = jnp.concatenate(sn_parts, axis=0)
    gcq = cs * tab_ref[1:2, :]
    gsq = sn * tab_ref[2:3, :]
    gck = cs * tab_ref[3:4, :]
    gsk = sn * tab_ref[4:5, :]

    qn = q_c * _rms(q_c, Q_LORA_RANK) * gq_ref[...]
    qq = _bdot(qn, wq_ref[...])
    kvn = kv_c * _rms(kv_c, KV_LORA_RANK) * gkv_ref[...]
    kk = _bdot(kvn, wk_ref[...])
    v_ref[...] = _bdot(kvn, wv_ref[...]).astype(v_ref.dtype)

    pe_rot = kpe * gck + kpe_sw * gsk
    pe_ss = jnp.sum(kpe * kpe, axis=-1, keepdims=True)
    hw = MLA_HEADS * HEAD_SLAB
    for hd in range(MLA_HEADS):
        lo = hd * HEAD_SLAB
        qr = qq[:, lo:lo + HEAD_SLAB]
        qs = qq[:, hw + lo:hw + lo + HEAD_SLAB]
        rq = _rms(qr, QK_HEAD_DIM)
        q_ref[hd] = (rq * (qr * gcq + qs * gsq)).astype(q_ref.dtype)
        kr = kk[:, lo:lo + HEAD_SLAB]
        rk = lax.rsqrt((jnp.sum(kr * kr, axis=-1, keepdims=True) + pe_ss) * (1.0 / QK_HEAD_DIM) + NORM_EPS)
        k_ref[hd] = (rk * (kr * gck + pe_rot)).astype(k_ref.dtype)


def _even_in_call(x, mod, posf, g, prep):
    bsz, seq, d = x.shape
    tm = min(ROW_TILE, seq)
    full = lambda a: pl.BlockSpec(a.shape, lambda b, s: (0,) * a.ndim)
    return pl.pallas_call(
        _even_in_kernel,
        grid=(bsz, seq // tm),
        in_specs=[
            pl.BlockSpec((None, tm, d), lambda b, s: (b, s, 0)),
            pl.BlockSpec((None, 6, d), lambda b, s: (b, 0, 0)),
            pl.BlockSpec((None, tm, 1), lambda b, s: (b, s, 0)),
            full(g), full(prep["w_in"]), full(prep["gq"]), full(prep["wq"]), full(prep["gkv"]),
            full(prep["wk"]), full(prep["wv"]), full(prep["tab"]),
        ],
        out_specs=[
            pl.BlockSpec((None, MLA_HEADS, tm, HEAD_SLAB), lambda b, s: (b, 0, s, 0)),
            pl.BlockSpec((None, MLA_HEADS, tm, HEAD_SLAB), lambda b, s: (b, 0, s, 0)),
            pl.BlockSpec((None, tm, MLA_HEADS * V_HEAD_DIM), lambda b, s: (b, s, 0)),
            pl.BlockSpec((tm, d // 2), lambda b, s: (s, b)),
        ],
        out_shape=[
            jax.ShapeDtypeStruct((bsz, MLA_HEADS, seq, HEAD_SLAB), BF16),
            jax.ShapeDtypeStruct((bsz, MLA_HEADS, seq, HEAD_SLAB), BF16),
            jax.ShapeDtypeStruct((bsz, seq, MLA_HEADS * V_HEAD_DIM), BF16),
            jax.ShapeDtypeStruct((seq, bsz * (d // 2)), F32),
        ],
        compiler_params=pltpu.CompilerParams(dimension_semantics=("parallel", "parallel"),
                                             vmem_limit_bytes=VMEM_LIMIT),
        name="even_in",
    )(x, mod, posf, g, prep["w_in"], prep["gq"], prep["wq"], prep["gkv"], prep["wk"], prep["wv"], prep["tab"])


def _prep_even(even_w_in, q_norm_g, w_uq, kv_norm_g, w_ukv, q_head_g, k_head_g):
    d = even_w_in.shape[0]
    half = QK_ROPE_DIM // 2
    nope = QK_NOPE_DIM
    c_pe = Q_LORA_RANK + KV_LORA_RANK
    w_pe = even_w_in[:, c_pe:c_pe + QK_ROPE_DIM]
    zeros = lambda n: jnp.zeros((d, n), F32)
    pe_slab = jnp.concatenate([zeros(nope), w_pe, zeros(HEAD_SLAB - QK_HEAD_DIM)], axis=1)
    pe_sw = jnp.concatenate([zeros(nope), -w_pe[:, half:], w_pe[:, :half], zeros(HEAD_SLAB - QK_HEAD_DIM)], axis=1)
    w_in = jnp.concatenate([even_w_in[:, :c_pe], pe_slab, pe_sw, even_w_in[:, c_pe + QK_ROPE_DIM:]], axis=1)

    r = w_uq.shape[0]
    padq = jnp.zeros((r, MLA_HEADS, HEAD_SLAB - QK_HEAD_DIM), F32)
    wq_plain = jnp.concatenate([w_uq, padq], axis=2).reshape(r, MLA_HEADS * HEAD_SLAB)
    wq_sw = jnp.concatenate([jnp.zeros((r, MLA_HEADS, nope), F32), -w_uq[:, :, nope + half:],
                             w_uq[:, :, nope:nope + half], padq], axis=2).reshape(r, MLA_HEADS * HEAD_SLAB)
    wq = jnp.concatenate([wq_plain, wq_sw], axis=1)

    rk = w_ukv.shape[0]
    wk = jnp.concatenate([w_ukv[:, :, :nope], jnp.zeros((rk, MLA_HEADS, HEAD_SLAB - nope), F32)],
                         axis=2).reshape(rk, MLA_HEADS * HEAD_SLAB)
    wv = w_ukv[:, :, nope:].reshape(rk, MLA_HEADS * V_HEAD_DIM)

    inv_freq = 1.0 / (ROPE_THETA ** (jnp.arange(half, dtype=F32) / half))
    pad_tail = jnp.zeros((HEAD_SLAB - QK_HEAD_DIM,), F32)
    freq_row = jnp.concatenate([jnp.zeros((nope,), F32), inv_freq, inv_freq, pad_tail])

    def gain_rows(gv, scale):
        plain = jnp.concatenate([gv, pad_tail]) * scale
        swapped = jnp.concatenate([jnp.zeros((nope,), F32), gv[nope + half:], gv[nope:nope + half], pad_tail]) * scale
        return plain, swapped

    gq_plain, gq_sw = gain_rows(q_head_g, QK_HEAD_DIM ** -0.5 * math.log2(math.e))
    gk_plain, gk_sw = gain_rows(k_head_g, 1.0)
    freq_packed = jnp.tile(jnp.concatenate([inv_freq, inv_freq]), LANES // QK_ROPE_DIM)
    tab = jnp.stack([freq_row, gq_plain, gq_sw, gk_plain, gk_sw, freq_packed, freq_row * 0, freq_row * 0])
    return {
        "w_in": w_in.astype(BF16), "gq": q_norm_g.reshape(1, -1), "wq": wq.astype(BF16),
        "gkv": kv_norm_g.reshape(1, -1), "wk": wk.astype(BF16), "wv": wv.astype(BF16), "tab": tab,
    }


def _attn_kernel(qi_ref, kj_ref, q_ref, k_ref, v_ref, o_ref, m_sc, a_sc, acc_sc, s_sc, p_sc, *, tq, tk):
    step = pl.program_id(2)
    i = qi_ref[step]
    j = kj_ref[step]
    sum_lane = (V_HEAD_DIM, 0)
    subs = tq // tk

    @pl.when(j == 0)
    def _():
        m_sc[...] = jnp.full(m_sc.shape, -jnp.inf, F32)
        acc_sc[...] = jnp.zeros(acc_sc.shape, F32)

    def sweep(diag_sub):
        lo = 0 if diag_sub is None else diag_sub * tk
        live = pl.ds(lo, tq - lo)
        lane = lax.broadcasted_iota(jnp.int32, (1, LANES), 1)
        for hh in range(ATTN_HEADS):
            s_sc[hh, live, :] = lax.dot_general(q_ref[hh, live, :], k_ref[hh], (((1,), (1,)), ((), ())),
                                                preferred_element_type=F32)
        for hh in range(ATTN_HEADS):
            v = v_ref[:, (hh // 2) * LANES:(hh // 2 + 1) * LANES]
            for r0 in range(lo, tq, ATTN_ROWS):
                rows = pl.ds(r0, ATTN_ROWS)
                s = s_sc[hh, rows, :]
                if diag_sub is not None and r0 < lo + tk:
                    row = (r0 - lo) + lax.broadcasted_iota(jnp.int32, (ATTN_ROWS, tk), 0)
                    col = lax.broadcasted_iota(jnp.int32, (ATTN_ROWS, tk), 1)
                    s = jnp.where(col <= row, s, -jnp.inf)
                m_prev = m_sc[hh, rows, :]
                m_new = jnp.maximum(m_prev, jnp.max(s, axis=-1, keepdims=True))
                a_sc[hh, rows, :] = jnp.exp2(m_prev - m_new)
                m_sc[hh, rows, :] = m_new
                shifted = s - jnp.concatenate([m_new] * (tk // LANES), axis=1)
                p_sc[hh, rows, :] = jnp.exp2(shifted.astype(BF16))
            own = (lane < V_HEAD_DIM) == (hh % 2 == 0)
            ones = jnp.where(lane == sum_lane[hh % 2], 1.0, 0.0).astype(v.dtype)
            vh = jnp.where(own, v, jnp.broadcast_to(ones, v.shape))
            acc_sc[hh, live, :] = (acc_sc[hh, live, :] * a_sc[hh, live, :]
                                   + jnp.dot(p_sc[hh, live, :], vh, preferred_element_type=F32))

    diag = j - subs * i

    @pl.when(diag < 0)
    def _():
        sweep(None)

    for ds_ in range(subs):
        @pl.when(diag == ds_)
        def _(ds_=ds_):
            sweep(ds_)

    @pl.when(diag == subs - 1)
    def _():
        lane = lax.broadcasted_iota(jnp.int32, (1, LANES), 1)
        for pp in range(ATTN_HEADS // 2):
            acc0 = acc_sc[2 * pp]
            acc1 = acc_sc[2 * pp + 1]
            l0 = acc0[:, sum_lane[0]:sum_lane[0] + 1]
            l1 = acc1[:, sum_lane[1]:sum_lane[1] + 1]
            o_ref[:, pp * LANES:(pp + 1) * LANES] = jnp.where(lane < V_HEAD_DIM, acc0 / l0,
                                                              acc1 / l1).astype(o_ref.dtype)


def _attn_call(q, k, v):
    bsz, nh, seq, _ = q.shape
    tk = min(ATTN_TILE, seq)
    tq = min(ATTN_Q_BLOCKS * tk, seq)
    subs = tq // tk
    nq = seq // tq
    pairs = [(i, j) for i in range(nq) for j in range(subs * (i + 1))]
    qi = jnp.asarray([p[0] for p in pairs], jnp.int32)
    kj = jnp.asarray([p[1] for p in pairs], jnp.int32)
    kern = functools.partial(_attn_kernel, tq=tq, tk=tk)
    hp = ATTN_HEADS
    assert nh % hp == 0
    return pl.pallas_call(
        kern,
        grid_spec=pltpu.PrefetchScalarGridSpec(
            num_scalar_prefetch=2,
            grid=(bsz, nh // hp, len(pairs)),
            in_specs=[
                pl.BlockSpec((None, hp, tq, HEAD_SLAB), lambda b, h, p, qi, kj: (b, h, qi[p], 0)),
                pl.BlockSpec((None, hp, tk, HEAD_SLAB), lambda b, h, p, qi, kj: (b, h, kj[p], 0)),
                pl.BlockSpec((None, tk, hp * V_HEAD_DIM), lambda b, h, p, qi, kj: (b, kj[p], h)),
            ],
            out_specs=pl.BlockSpec((None, tq, hp * V_HEAD_DIM), lambda b, h, p, qi, kj: (b, qi[p], h)),
            scratch_shapes=[pltpu.VMEM((hp, tq, LANES), F32), pltpu.VMEM((hp, tq, LANES), F32),
                            pltpu.VMEM((hp, tq, LANES), F32),
                            pltpu.VMEM((hp, tq, tk), F32), pltpu.VMEM((hp, tq, tk), BF16)],
        ),
        out_shape=jax.ShapeDtypeStruct((bsz, seq, nh * V_HEAD_DIM), BF16),
        compiler_params=pltpu.CompilerParams(
            dimension_semantics=("parallel", "parallel", "arbitrary"),
            vmem_limit_bytes=VMEM_LIMIT),
        name="mla_attention",
    )(qi, kj, q, k, v)


def _s5_disc_kernel(are_ref, aim_ref, ldt_ref, bre_ref, bim_ref, abre_ref, abim_ref, bbre_ref, bbim_ref):
    dt = jnp.exp(ldt_ref[...])
    lam_re = jnp.minimum(are_ref[...], -1e-4)
    lam_im = aim_ref[...]
    mag = jnp.exp(lam_re * dt)
    ab_re = mag * jnp.cos(lam_im * dt)
    ab_im = mag * jnp.sin(lam_im * dt)
    den = lam_re * lam_re + lam_im * lam_im
    num_re = ab_re - 1.0
    f_re = (num_re * lam_re + ab_im * lam_im) / den
    f_im = (ab_im * lam_re - num_re * lam_im) / den
    abre_ref[...] = ab_re
    abim_ref[...] = ab_im
    br = bre_ref[...]
    bi = bim_ref[...]
    bbre_ref[...] = f_re[:, None, :] * br - f_im[:, None, :] * bi
    bbim_ref[...] = f_re[:, None, :] * bi + f_im[:, None, :] * br


def _s5_disc_call(a_re, a_im, log_dt, b_re, b_im):
    g, p = a_re.shape
    bre_t = jnp.swapaxes(b_re, 1, 2)
    bim_t = jnp.swapaxes(b_im, 1, 2)
    return pl.pallas_call(
        _s5_disc_kernel,
        out_shape=[jax.ShapeDtypeStruct((g, p), F32), jax.ShapeDtypeStruct((g, p), F32),
                   jax.ShapeDtypeStruct(bre_t.shape, F32), jax.ShapeDtypeStruct(bre_t.shape, F32)],
        name="s5_discretize",
    )(a_re, a_im, log_dt.reshape(g, 1), bre_t, bim_t)


def _block_diag_halves(m):
    g, r, c = m.shape
    gh = g // 2
    eye = jnp.eye(gh, dtype=m.dtype)
    mh = m.reshape(2, gh, r, c)
    return (mh[:, :, :, None, :] * eye[None, :, None, :, None]).reshape(2, gh * r, gh * c)


def _s5_kernel(u_ref, bre_ref, bim_ref, are_ref, aim_ref, cre_ref, cim_ref, d_ref, gw_ref, gb_ref,
               o_ref, sre, sim, dre, dim, xbr, xbi, *, steps):
    @pl.when(pl.program_id(0) == 0)
    def _():
        sre[...] = jnp.zeros(sre.shape, F32)
        sim[...] = jnp.zeros(sim.shape, F32)

    rows = steps * SUBLANES
    w = u_ref.shape[-1]
    u = u_ref[...].reshape(rows, w)
    ub = u.astype(BF16)
    kh = w // 2
    nh = dre.shape[1] // 2
    for hf in range(2):
        dre[:, hf * nh:(hf + 1) * nh] = jnp.dot(ub[:, hf * kh:(hf + 1) * kh], bre_ref[hf], preferred_element_type=F32)
        dim[:, hf * nh:(hf + 1) * nh] = jnp.dot(ub[:, hf * kh:(hf + 1) * kh], bim_ref[hf], preferred_element_type=F32)

    xr = sre[...]
    xi = sim[...]
    for t in range(0, steps, 2):
        pair_r = []
        pair_i = []
        for r0 in (t * SUBLANES, (t + 1) * SUBLANES):
            nr = are_ref[...] * xr - aim_ref[...] * xi + dre[r0:r0 + SUBLANES, :]
            ni = are_ref[...] * xi + aim_ref[...] * xr + dim[r0:r0 + SUBLANES, :]
            xr, xi = nr, ni
            pair_r.append(nr)
            pair_i.append(ni)
        xbr[t * SUBLANES:(t + 2) * SUBLANES, :] = jnp.concatenate(pair_r, axis=0).astype(BF16)
        xbi[t * SUBLANES:(t + 2) * SUBLANES, :] = jnp.concatenate(pair_i, axis=0).astype(BF16)
    sre[...] = xr
    sim[...] = xi

    ys = []
    for hf in range(2):
        yr = jnp.dot(xbr[:, hf * nh:(hf + 1) * nh], cre_ref[hf], preferred_element_type=F32)
        yi = jnp.dot(xbi[:, hf * nh:(hf + 1) * nh], cim_ref[hf], preferred_element_type=F32)
        ys.append(yr - yi)
    y = jnp.concatenate(ys, axis=1) + d_ref[...] * u
    g = _gelu(y)
    out = g * _sigmoid(_bdot(g, gw_ref[...]) + gb_ref[...])
    o_ref[...] = out.reshape(steps, SUBLANES, w).astype(o_ref.dtype)


def _s5_call(u_t, disc, c_re, c_im, d_skip, glu_w, glu_b):
    seq, bsz, w = u_t.shape
    assert bsz == SUBLANES
    ab_re, ab_im, bb_re, bb_im = disc
    g, p = ab_re.shape
    n_state = g * p
    bre = _block_diag_halves(bb_re).astype(BF16)
    bim = _block_diag_halves(bb_im).astype(BF16)
    cre = _block_diag_halves(jnp.swapaxes(c_re, 1, 2)).astype(BF16)
    cim = _block_diag_halves(jnp.swapaxes(c_im, 1, 2)).astype(BF16)
    steps = min(S5_STEPS, seq)
    full = lambda a: pl.BlockSpec(a.shape, lambda s: (0,) * a.ndim)
    rep = lambda a: jnp.broadcast_to(a.reshape(1, n_state), (bsz, n_state))
    args = (bre, bim, rep(ab_re), rep(ab_im), cre, cim,
            d_skip.reshape(1, w), glu_w.astype(BF16), glu_b.reshape(1, w))
    return pl.pallas_call(
        functools.partial(_s5_kernel, steps=steps),
        grid=(seq // steps,),
        in_specs=[pl.BlockSpec((steps, bsz, w), lambda s: (s, 0, 0))] + [full(a) for a in args],
        out_specs=pl.BlockSpec((steps, bsz, w), lambda s: (s, 0, 0)),
        out_shape=jax.ShapeDtypeStruct((seq, bsz, w), BF16),
        scratch_shapes=[pltpu.VMEM((bsz, n_state), F32), pltpu.VMEM((bsz, n_state), F32),
                        pltpu.VMEM((steps * bsz, n_state), F32), pltpu.VMEM((steps * bsz, n_state), F32),
                        pltpu.VMEM((steps * bsz, n_state), BF16), pltpu.VMEM((steps * bsz, n_state), BF16)],
        compiler_params=pltpu.CompilerParams(dimension_semantics=("arbitrary",), vmem_limit_bytes=VMEM_LIMIT),
        name="s5_scan",
    )(u_t, *args)


def _router_logits(x_new, mod_ref, gf_ref, rw_ref, rb_ref, h2_ref, rows):
    h2 = _mod_norm(x_new, gf_ref[...], mod_ref[4:5, :], mod_ref[3:4, :])
    h2_ref[rows, :] = _pack_pairs(h2)
    h_hi = h2.astype(BF16)
    h_lo = (h2 - h_hi.astype(F32)).astype(BF16)
    r_hi = jnp.dot(h_hi, rw_ref[...], preferred_element_type=F32)
    r_lo = jnp.dot(h_lo, rw_ref[...], preferred_element_type=F32)
    return r_hi[:, :LANES] + r_hi[:, LANES:] + r_lo[:, :LANES] + rb_ref[...]


def _router_pick(logits, te_ref, tet_ref, rows):
    lane = lax.broadcasted_iota(jnp.int32, logits.shape, 1).astype(F32)
    vals = []
    idxs = []
    work = logits
    for _ in range(TOP_K):
        m = jnp.max(work, axis=-1, keepdims=True)
        idx = jnp.min(jnp.where(work == m, lane, float(LANES)), axis=-1, keepdims=True)
        vals.append(m)
        idxs.append(idx)
        work = jnp.where(lane == idx, NEG_BIG * 2.0, work)
    exps = [jnp.exp(vv - vals[0]) for vv in vals]
    tot = exps[0] + exps[1] + exps[2] + exps[3]
    te = jnp.zeros(logits.shape, F32)
    picked = jnp.zeros(logits.shape, F32)
    for kk in range(TOP_K):
        te = jnp.where(lane == float(kk), idxs[kk], te)
        te = jnp.where(lane == float(TOP_K + kk), exps[kk] / tot, te)
        picked = picked + jnp.where(lane == idxs[kk], 1.0, 0.0)
    te_ref[rows, :] = te[:, :2 * TOP_K]
    tet_ref[:, rows] = te.T[:2 * TOP_K, :]
    return jnp.sum(picked, axis=0, keepdims=True)


def _row_parts(tm):
    size = tm // TILE_PARTS
    return [pl.ds(p * size, size) for p in range(TILE_PARTS)]


def _mix_out_kernel(x_ref, a_ref, s_ref, mod_ref, wo_ref, gf_ref, rw_ref, rb_ref, xo_ref, h2_ref, te_ref,
                    tet_ref, cnt_ref):
    ka = a_ref.shape[-1]
    parts = _row_parts(x_ref.shape[0])
    logits = []
    for rows in parts:
        mix = jnp.dot(a_ref[rows, :], wo_ref[:ka, :], preferred_element_type=F32)
        mix = mix + jnp.dot(s_ref[rows, :], wo_ref[ka:, :], preferred_element_type=F32)
        x_new = x_ref[rows, :] + mod_ref[2:3, :] * mix
        xo_ref[rows, :] = x_new
        logits.append(_router_logits(x_new, mod_ref, gf_ref, rw_ref, rb_ref, h2_ref, rows))
    counts = [_router_pick(lg, te_ref, tet_ref, rows) for lg, rows in zip(logits, parts)]
    cnt_ref[...] = sum(counts[1:], counts[0])


def _tail_out_specs(bsz, seq, tm, d):
    nt = seq // tm
    specs = [
        pl.BlockSpec((None, tm, d), lambda b, s: (b, s, 0)),
        pl.BlockSpec((tm, d // 2), lambda b, s: (b * nt + s, 0)),
        pl.BlockSpec((tm, 2 * TOP_K), lambda b, s: (b * nt + s, 0)),
        pl.BlockSpec((2 * TOP_K, tm), lambda b, s: (0, b * nt + s)),
        pl.BlockSpec((None, 1, LANES), lambda b, s: (b * nt + s, 0, 0)),
    ]
    shapes = [
        jax.ShapeDtypeStruct((bsz, seq, d), F32),
        jax.ShapeDtypeStruct((bsz * seq, d // 2), jnp.uint32),
        jax.ShapeDtypeStruct((bsz * seq, 2 * TOP_K), F32),
        jax.ShapeDtypeStruct((2 * TOP_K, bsz * seq), F32),
        jax.ShapeDtypeStruct((bsz * nt, 1, LANES), F32),
    ]
    return specs, shapes


def _router_pad(router_w, router_b):
    d, e = router_w.shape
    rw = jnp.concatenate([router_w, jnp.zeros((d, LANES - e), F32)], axis=1)
    rw_hi = rw.astype(BF16)
    rw_lo = (rw - rw_hi.astype(F32)).astype(BF16)
    rb = jnp.concatenate([router_b, jnp.full((LANES - e,), NEG_BIG, F32)]).reshape(1, LANES)
    return jnp.concatenate([rw_hi, rw_lo], axis=1), rb


def _mix_out_call(x, attn, ssm_t, mod, w_out, gf, rw, rb):
    bsz, seq, d = x.shape
    tm = min(ROW_TILE, seq)
    ka = attn.shape[-1]
    ks = ssm_t.shape[-1] // bsz
    full = lambda a: pl.BlockSpec(a.shape, lambda b, s: (0,) * a.ndim)
    out_specs, out_shape = _tail_out_specs(bsz, seq, tm, d)
    return pl.pallas_call(
        _mix_out_kernel,
        grid=(bsz, seq // tm),
        in_specs=[
            pl.BlockSpec((None, tm, d), lambda b, s: (b, s, 0)),
            pl.BlockSpec((None, tm, ka), lambda b, s: (b, s, 0)),
            pl.BlockSpec((tm, ks), lambda b, s: (s, b)),
            pl.BlockSpec((None, 6, d), lambda b, s: (b, 0, 0)),
            full(w_out), full(gf), full(rw), full(rb),
        ],
        out_specs=out_specs,
        out_shape=out_shape,
        compiler_params=pltpu.CompilerParams(dimension_semantics=("parallel", "parallel"),
                                             vmem_limit_bytes=VMEM_LIMIT),
        name="even_out",
    )(x, attn, ssm_t, mod, w_out, gf, rw, rb)


def _moe_sum(yg_ref, te_ref, rows=slice(None)):
    te = te_ref[rows, :]
    acc_lo = acc_hi = None
    for kk in range(TOP_K):
        lo, hi = _unpack_pairs_f32(yg_ref[kk, rows, :])
        gate = te[:, TOP_K + kk:TOP_K + kk + 1]
        acc_lo = gate * lo if kk == 0 else acc_lo + gate * lo
        acc_hi = gate * hi if kk == 0 else acc_hi + gate * hi
    return jnp.concatenate([acc_lo, acc_hi], axis=1)


def _odd_kernel(x_ref, yg_ref, tep_ref, modp_ref, mod_ref, g_ref, win_ref, icnt_ref, wp_ref, ps_ref, gv_ref,
                wsp_ref, bsp_ref, wo_ref, gf_ref, rw_ref, rb_ref, xo_ref, h2_ref, te_ref, tet_ref, cnt_ref, ext_sc):
    tm = x_ref.shape[0]
    pw = wp_ref.shape[-1]
    width = pw * len(POOL_WINDOWS)

    @pl.when(pl.program_id(1) == 0)
    def _():
        ext_sc[0:POOL_HALO, :] = jnp.zeros((POOL_HALO, width), F32)

    parts = _row_parts(tm)
    pr = tm // TILE_PARTS
    hd = width // SGU_HEADS

    xs = []
    zs = []
    for p, rows in enumerate(parts):
        x = x_ref[rows, :] + modp_ref[5:6, :] * _moe_sum(yg_ref, tep_ref, rows)
        h = _mod_norm(x, g_ref[...], mod_ref[1:2, :], mod_ref[0:1, :])
        z = _bdot(h, win_ref[...])
        ext_sc[POOL_HALO + p * pr:POOL_HALO + (p + 1) * pr, :] = z[:, :width]
        xs.append(x)
        zs.append(z)

    mixes = []
    for p, rows in enumerate(parts):
        z = zs[p]
        up = z[:, :width]
        base = POOL_HALO + p * pr
        pooled = []
        for gi, win in enumerate(POOL_WINDOWS):
            cols = slice(gi * pw, (gi + 1) * pw)
            acc = up[:, cols]
            for lag in range(1, win):
                acc = acc + ext_sc[base - lag:base - lag + pr, cols]
            pg = acc * icnt_ref[rows, gi:gi + 1] - up[:, cols]
            pooled.append(_bdot(pg, wp_ref[gi]) * ps_ref[:, cols])
        pooled = jnp.concatenate(pooled, axis=1)

        ug = _gelu(z[:, width:2 * width])
        vg = _gelu(z[:, 2 * width:])
        vn = (vg * _rms(vg, width) * gv_ref[...]).astype(BF16)
        chunks = []
        for ci in range(pr // SGU_CHUNK):
            heads = []
            for hh in range(SGU_HEADS):
                blk = vn[ci * SGU_CHUNK:(ci + 1) * SGU_CHUNK, hh * hd:(hh + 1) * hd]
                heads.append(jnp.dot(wsp_ref[hh], blk, preferred_element_type=F32) + bsp_ref[hh])
            chunks.append(jnp.concatenate(heads, axis=1))
        gated = ug * jnp.concatenate(chunks, axis=0)
        mixes.append(_bdot(pooled, wo_ref[:width, :]) + _bdot(gated, wo_ref[width:, :]))
    ext_sc[0:POOL_HALO, :] = ext_sc[tm:tm + POOL_HALO, :]

    logits = []
    for p, rows in enumerate(parts):
        x_new = xs[p] + mod_ref[2:3, :] * mixes[p]
        xo_ref[rows, :] = x_new
        logits.append(_router_logits(x_new, mod_ref, gf_ref, rw_ref, rb_ref, h2_ref, rows))
    counts = [_router_pick(lg, te_ref, tet_ref, rows) for lg, rows in zip(logits, parts)]
    cnt_ref[...] = sum(counts[1:], counts[0])


def _odd_call(grp, mod_prev, mod, nb, g, w_in, pool_w, pool_scale, sgu_norm_g, sgu_w, sgu_b, w_out, gf, rw, rb):
    x = grp["x"]
    _, seq, d = x.shape
    tm = min(ODD_TILE, seq)
    nt = seq // tm
    xb0, tok0, mb0 = grp["x_b0"], grp["tok0"], grp["mod_b0"]
    tile0 = tok0 // tm
    yg = grp["yg"]
    width = pool_scale.shape[0]
    hd = width // SGU_HEADS
    t = jnp.arange(seq, dtype=jnp.int32)
    icnt = jnp.stack([1.0 / jnp.minimum(t + 1, wn).astype(F32) for wn in POOL_WINDOWS], axis=1)
    wsp = jnp.tril(sgu_w).astype(BF16)
    bsp = jnp.broadcast_to(sgu_b[:, :, None], (SGU_HEADS, SGU_CHUNK, hd))
    args = (g, w_in.astype(BF16), icnt, pool_w.astype(BF16), pool_scale.reshape(1, width),
            sgu_norm_g.reshape(1, width), wsp, bsp, w_out.astype(BF16), gf, rw, rb)
    full = lambda a: pl.BlockSpec(a.shape, lambda b, s: (0,) * a.ndim)
    in_specs = [pl.BlockSpec((None, tm, d), lambda b, s: (xb0 + b, s, 0)),
                pl.BlockSpec((TOP_K, tm, yg.shape[-1]), lambda b, s: (0, b * nt + s, 0)),
                pl.BlockSpec((tm, 2 * TOP_K), lambda b, s: (tile0 + b * nt + s, 0)),
                pl.BlockSpec((None, 6, d), lambda b, s: (mb0 + b, 0, 0)),
                pl.BlockSpec((None, 6, d), lambda b, s: (mb0 + b, 0, 0))]
    for idx, a in enumerate(args):
        in_specs.append(pl.BlockSpec((tm, len(POOL_WINDOWS)), lambda b, s: (s, 0)) if idx == 2 else full(a))
    out_specs, out_shape = _tail_out_specs(nb, seq, tm, d)
    return pl.pallas_call(
        _odd_kernel,
        grid=(nb, nt),
        in_specs=in_specs,
        out_specs=out_specs,
        out_shape=out_shape,
        scratch_shapes=[pltpu.VMEM((tm + POOL_HALO, width), F32)],
        compiler_params=pltpu.CompilerParams(dimension_semantics=("parallel", "arbitrary"),
                                             vmem_limit_bytes=VMEM_LIMIT),
        name="odd_mixer",
    )(x, yg, grp["te"], mod_prev, mod, *args)


def _dest_kernel(tet_ref, base_ref, tri_ref, dst_ref):
    tr = tet_ref.shape[1]
    tet = tet_ref[...]
    expert = lax.broadcasted_iota(jnp.int32, (LANES, tr), 0).astype(F32)
    hots = [tet[kk:kk + 1, :] == expert for kk in range(TOP_K)]
    oh = jnp.zeros((LANES, tr), F32)
    for hot in hots:
        oh = oh + jnp.where(hot, 1.0, 0.0)
    before = jnp.dot(oh.astype(BF16), tri_ref[...], preferred_element_type=F32) + base_ref[...]
    row = lax.broadcasted_iota(jnp.int32, (2 * TOP_K, tr), 0)
    dst = jnp.zeros((2 * TOP_K, tr), F32)
    for kk, hot in enumerate(hots):
        dst = jnp.where(row == kk, jnp.sum(jnp.where(hot, before, 0.0), axis=0, keepdims=True), dst)
    dst_ref[...] = dst.astype(jnp.int32)


def _dest_call(tet, base, tok0, n_tok):
    tiles = base.shape[0]
    tr = n_tok // tiles
    tile0 = tok0 // tr
    tri = (jnp.arange(tr)[:, None] < jnp.arange(tr)[None, :]).astype(BF16)
    return pl.pallas_call(
        _dest_kernel,
        grid=(tiles,),
        in_specs=[pl.BlockSpec((2 * TOP_K, tr), lambda i: (0, tile0 + i)),
                  pl.BlockSpec((None, LANES, 1), lambda i: (i, 0, 0)),
                  pl.BlockSpec((tr, tr), lambda i: (0, 0))],
        out_specs=pl.BlockSpec((2 * TOP_K, tr), lambda i: (0, i)),
        out_shape=jax.ShapeDtypeStruct((2 * TOP_K, n_tok), jnp.int32),
        compiler_params=pltpu.CompilerParams(dimension_semantics=("parallel",)),
        name="route_dest",
    )(tet, base, tri)


def _sc_gather(table, idx):
    n = idx.shape[0]
    per_w = n // SC_WORKERS
    assert per_w * SC_WORKERS == n and per_w % SC_CHUNK == 0
    n_chunks = per_w // SC_CHUNK
    row_shape = table.shape[1:]
    mesh = plsc.VectorSubcoreMesh(core_axis_name="c", subcore_axis_name="s")

    @functools.partial(
        pl.kernel, mesh=mesh,
        out_type=jax.ShapeDtypeStruct((n,) + row_shape, table.dtype),
        scratch_types=[pltpu.VMEM((SC_CHUNK,), jnp.int32), pltpu.VMEM((SC_CHUNK,) + row_shape, table.dtype),
                       pltpu.SemaphoreType.DMA],
        name="sc_row_gather",
    )
    def gather(table_hbm, idx_hbm, out_hbm, idx_v, rows_v, sem):
        wid = lax.axis_index("s") * SC_CORES + lax.axis_index("c")
        base = wid * per_w

        @pl.loop(0, n_chunks)
        def _(ci):
            off = pl.multiple_of(base + ci * SC_CHUNK, SC_CHUNK)
            pltpu.sync_copy(idx_hbm.at[pl.ds(off, SC_CHUNK)], idx_v)
            pltpu.async_copy(table_hbm.at[idx_v], rows_v, sem).wait()
            pltpu.sync_copy(rows_v, out_hbm.at[pl.ds(off, SC_CHUNK)])

    return gather(table, idx)


def _sc_scatter(rows, dests, n_out, tok0):
    t = dests[0].shape[0]
    per_w = t // SC_WORKERS
    assert per_w * SC_WORKERS == t and per_w % SC_CHUNK == 0
    n_chunks = per_w // SC_CHUNK
    row_shape = rows.shape[1:]
    nk = len(dests)
    mesh = plsc.VectorSubcoreMesh(core_axis_name="c", subcore_axis_name="s")

    @functools.partial(
        pl.kernel, mesh=mesh,
        out_type=jax.ShapeDtypeStruct((n_out,) + row_shape, rows.dtype),
        scratch_types=[pltpu.VMEM((SC_CHUNK,), jnp.int32)] * nk
        + [pltpu.VMEM((SC_CHUNK,) + row_shape, rows.dtype), pltpu.SemaphoreType.DMA],
        name="sc_row_scatter",
    )
    def scatter(rows_hbm, *rest):
        dest_hbm = rest[:nk]
        out_hbm = rest[nk]
        idx_v = rest[nk + 1:2 * nk + 1]
        rows_v, sem = rest[2 * nk + 1:]
        wid = lax.axis_index("s") * SC_CORES + lax.axis_index("c")
        base = wid * per_w

        @pl.loop(0, n_chunks)
        def _(ci):
            off = pl.multiple_of(base + ci * SC_CHUNK, SC_CHUNK)
            src = pl.multiple_of(tok0 + off, SC_CHUNK)
            pltpu.sync_copy(rows_hbm.at[pl.ds(src, SC_CHUNK)], rows_v)
            for kk in range(nk):
                pltpu.sync_copy(dest_hbm[kk].at[pl.ds(off, SC_CHUNK)], idx_v[kk])
            copies = [pltpu.async_copy(rows_v, out_hbm.at[idx_v[kk]], sem) for kk in range(nk)]
            for cp in copies:
                cp.wait()

    return scatter(rows, *dests)


def _expert_kernel(be_ref, nv_ref, ord_ref, ue_ref, nu_ref, x_ref, wgu_hbm, bgu_ref, wdn_hbm, bdn_ref, y_ref,
                   wgu_f32, wdn_f32, wgu_bf, wdn_bf, sem, *, layer):
    i = pl.program_id(0)
    used = i < nu_ref[0]
    pos = ord_ref[i]
    fresh = jnp.logical_or(i == 0, ord_ref[jnp.maximum(i - 1, 0)] != pos)

    def weight_copies(expert):
        return (pltpu.make_async_copy(wgu_hbm.at[layer, expert], wgu_f32, sem.at[0]),
                pltpu.make_async_copy(wdn_hbm.at[layer, expert], wdn_f32, sem.at[1]))

    @pl.when(i == 0)
    def _():
        for cp in weight_copies(ue_ref[0]):
            cp.start()

    @pl.when(jnp.logical_and(used, fresh))
    def _():
        for cp in weight_copies(ue_ref[pos]):
            cp.wait()
        wgu_bf[...] = wgu_f32[...].astype(BF16)
        wdn_bf[...] = wdn_f32[...].astype(BF16)

        @pl.when(pos + 1 < nu_ref[1])
        def _():
            for cp in weight_copies(ue_ref[pos + 1]):
                cp.start()

    def ffn(rows):
        x = jnp.concatenate(_unpack_pairs(x_ref[0:rows, :]), axis=1)
        z = jnp.dot(x, wgu_bf[...], preferred_element_type=F32) + bgu_ref[...]
        ff = z.shape[-1] // 2
        gate = jnp.minimum(z[:, :ff], SWIGLU_LIMIT)
        lin = jnp.clip(z[:, ff:], -SWIGLU_LIMIT, SWIGLU_LIMIT)
        act = gate * _sigmoid(SWIGLU_ALPHA * gate) * (lin + 1.0)
        y = _bdot(act, wdn_bf[...]) + bdn_ref[...]
        y_ref[0:rows, :] = _pack_pairs(y)

    nv = nv_ref[i]
    below = 0
    for size in MOE_PATHS:
        @pl.when(jnp.logical_and(used, jnp.logical_and(nv > below, nv <= size)))
        def _(size=size):
            ffn(size)
        below = size


def _expert_call(layer, block_e, block_valid, block_pos, used_experts, n_used, xs, w_gu, b_gu, w_dn, b_dn):
    n_rows, dh = xs.shape
    depth, e, d, ff2 = w_gu.shape
    ff = ff2 // 2
    nb = n_rows // MOE_ROWS
    row_map = lambda i, be, nv, po, ue, nu: (jnp.minimum(i, nu[0] - 1), 0)
    b_map = lambda i, be, nv, po, ue, nu: (layer, be[i], 0, 0)
    return pl.pallas_call(
        functools.partial(_expert_kernel, layer=layer),
        grid_spec=pltpu.PrefetchScalarGridSpec(
            num_scalar_prefetch=5,
            grid=(nb,),
            in_specs=[
                pl.BlockSpec((MOE_ROWS, dh), row_map),
                pl.BlockSpec(memory_space=pl.ANY),
                pl.BlockSpec((None, None, 1, ff2), b_map),
                pl.BlockSpec(memory_space=pl.ANY),
                pl.BlockSpec((None, None, 1, d), b_map),
            ],
            out_specs=pl.BlockSpec((MOE_ROWS, dh), row_map),
            scratch_shapes=[pltpu.VMEM((d, ff2), F32), pltpu.VMEM((ff, d), F32),
                            pltpu.VMEM((d, ff2), BF16), pltpu.VMEM((ff, d), BF16),
                            pltpu.SemaphoreType.DMA((2,))],
        ),
        out_shape=jax.ShapeDtypeStruct((n_rows, dh), jnp.uint32),
        compiler_params=pltpu.CompilerParams(dimension_semantics=("arbitrary",), vmem_limit_bytes=VMEM_LIMIT),
        name="moe_experts",
    )(block_e, block_valid, block_pos, used_experts, n_used, xs, w_gu, b_gu.reshape(depth, e, 1, ff2), w_dn,
      b_dn.reshape(depth, e, 1, d))


def _combine_kernel(x_ref, yg_ref, te_ref, mod_ref, *rest):
    o_ref = rest[-1]
    o_ref[...] = x_ref[...] + mod_ref[5:6, :] * _moe_sum(yg_ref, te_ref)


def _combine_call(grp, mod, prev, bsz, nb):
    x = grp["x"]
    _, seq, d = x.shape
    tm = min(ROW_TILE, seq)
    nt = seq // tm
    xb0, mb0 = grp["x_b0"], grp["mod_b0"]
    tile0 = grp["tok0"] // tm
    yg = grp["yg"]
    in_specs = [
        pl.BlockSpec((None, tm, d), lambda b, s: (xb0 + b, s, 0)),
        pl.BlockSpec((TOP_K, tm, yg.shape[-1]), lambda b, s: (0, b * nt + s, 0)),
        pl.BlockSpec((tm, 2 * TOP_K), lambda b, s: (tile0 + b * nt + s, 0)),
        pl.BlockSpec((None, 6, d), lambda b, s: (mb0 + b, 0, 0)),
    ]
    args = [x, yg, grp["te"], mod]
    aliases = {}
    if prev is not None:
        in_specs.append(pl.BlockSpec(memory_space=pl.ANY))
        args.append(prev)
        aliases = {len(args) - 1: 0}
    return pl.pallas_call(
        _combine_kernel,
        grid=(nb, nt),
        in_specs=in_specs,
        out_specs=pl.BlockSpec((None, tm, d), lambda b, s: (mb0 + b, s, 0)),
        out_shape=jax.ShapeDtypeStruct((bsz, seq, d), F32),
        input_output_aliases=aliases,
        compiler_params=pltpu.CompilerParams(dimension_semantics=("parallel", "parallel"),
                                             vmem_limit_bytes=VMEM_LIMIT),
        name="moe_combine",
    )(*args)


def _moe_rows(layer, grp, gt, w_gu, b_gu, w_dn, b_dn):
    h2 = grp["h2"]
    dh = h2.shape[-1]
    n_rows = -(-(gt * TOP_K + N_EXPERTS * (MOE_ROWS - 1)) // MOE_ROWS) * MOE_ROWS
    nb = n_rows // MOE_ROWS
    first_row = jnp.arange(nb, dtype=jnp.int32) * MOE_ROWS
    upto = jnp.arange(LANES)[:, None] <= jnp.arange(LANES)[None, :]
    g_cnt = grp["cnt"][:, 0, :].astype(jnp.int32)
    counts = jnp.sum(g_cnt, axis=0)
    padded = (counts + MOE_ROWS - 1) // MOE_ROWS * MOE_ROWS
    pad_end = jnp.sum(jnp.where(upto, padded[:, None], 0), axis=0)
    pad_start = pad_end - padded
    tile_base = pad_start[None, :] + jnp.cumsum(g_cnt, axis=0) - g_cnt
    dest = _dest_call(grp["tet"], tile_base.astype(F32)[:, :, None], grp["tok0"], gt)
    dests = [dest[kk] for kk in range(TOP_K)]
    block_e = jnp.minimum(jnp.sum(pad_end[None, :N_EXPERTS] <= first_row[:, None], axis=1),
                          N_EXPERTS - 1).astype(jnp.int32)
    valid_end = (pad_start + counts)[block_e]
    block_valid = jnp.clip(valid_end - first_row, 0, MOE_ROWS).astype(jnp.int32)
    owns = counts[:N_EXPERTS] > 0
    expert_pos = jnp.cumsum(owns.astype(jnp.int32)) - 1
    slot = jnp.arange(N_EXPERTS, dtype=jnp.int32)
    used_experts = jnp.sum(jnp.where(owns[None, :] & (expert_pos[None, :] == slot[:, None]),
                                     slot[None, :], 0), axis=1).astype(jnp.int32)
    block_pos = expert_pos[block_e].astype(jnp.int32)
    n_used = jnp.stack([pad_end[N_EXPERTS - 1] // MOE_ROWS, jnp.sum(owns)]).astype(jnp.int32)
    xs = _sc_scatter(h2, dests, n_rows, grp["tok0"])
    y = _expert_call(layer, block_e, block_valid, block_pos, used_experts, n_used, xs, w_gu, b_gu, w_dn, b_dn)
    return _sc_gather(y, dest[:TOP_K].reshape(-1)).reshape(TOP_K, gt, dh)


def kernel(x, c, positions, ada_w, ada_b, norm_mix_g, norm_ffn_g, router_w, router_b, moe_w_gu, moe_b_gu,
           moe_w_dn, moe_b_dn, even_w_in, mla_q_norm_g, mla_w_uq, mla_kv_norm_g, mla_w_ukv, mla_q_head_g,
           mla_k_head_g, s5_a_re, s5_a_im, s5_log_dt, s5_b_re, s5_b_im, s5_c_re, s5_c_im, s5_d, s5_glu_w,
           s5_glu_b, even_w_out, odd_w_in, pool_w, pool_scale, sgu_norm_g, sgu_w, sgu_b, odd_w_out):
    bsz, seq, d = x.shape
    depth = ada_w.shape[0]
    mods = _ada_call(c, ada_w, ada_b).reshape(depth, bsz, 6, d)
    posf = positions.astype(F32).reshape(bsz, seq, 1)
    splits = MOE_SPLITS if bsz % MOE_SPLITS == 0 else 1
    gb = bsz // splits
    gt = gb * seq
    tiles_g = gt // min(ROW_TILE, seq)

    def settle(groups, mod):
        out = None
        for grp in groups:
            out = _combine_call(grp, mod, out, bsz, gb)
        return out

    pending = None
    for layer in range(depth):
        mod = mods[layer]
        i = layer // 2
        g_mix = norm_mix_g[layer].reshape(1, d)
        g_ffn = norm_ffn_g[layer].reshape(1, d)
        rw, rb = _router_pad(router_w[layer], router_b[layer])
        if layer % 2 == 0:
            if pending is not None:
                x = settle(pending, mods[layer - 1])
            prep = _prep_even(even_w_in[i], mla_q_norm_g[i], mla_w_uq[i], mla_kv_norm_g[i], mla_w_ukv[i],
                              mla_q_head_g[i], mla_k_head_g[i])
            q, k, v, u_t = _even_in_call(x, mod, posf, g_mix, prep)
            attn = _attn_call(q, k, v)
            disc = _s5_disc_call(s5_a_re[i], s5_a_im[i], s5_log_dt[i], s5_b_re[i], s5_b_im[i])
            ssm_t = _s5_call(u_t.reshape(seq, bsz, d // 2), disc, s5_c_re[i], s5_c_im[i], s5_d[i],
                             s5_glu_w[i], s5_glu_b[i])
            x_new, h2, te, tet, tile_cnt = _mix_out_call(x, attn, ssm_t.reshape(seq, bsz * (d // 2)), mod,
                                                         even_w_out[i].astype(BF16), g_ffn, rw, rb)
            groups = [dict(x=x_new, x_b0=gi * gb, h2=h2, te=te, tet=tet, tok0=gi * gt, mod_b0=gi * gb,
                           cnt=tile_cnt[gi * tiles_g:(gi + 1) * tiles_g]) for gi in range(splits)]
        else:
            groups = []
            for gi, grp in enumerate(pending):
                x_new, h2, te, tet, tile_cnt = _odd_call(grp, mods[layer - 1], mod, gb, g_mix, odd_w_in[i],
                                                         pool_w[i], pool_scale[i], sgu_norm_g[i], sgu_w[i],
                                                         sgu_b[i], odd_w_out[i], g_ffn, rw, rb)
                groups.append(dict(x=x_new, x_b0=0, h2=h2, te=te, tet=tet, tok0=0, mod_b0=gi * gb,
                                   cnt=tile_cnt))
        for grp in groups:
            grp["yg"] = _moe_rows(layer, grp, gt, moe_w_gu, moe_b_gu, moe_w_dn, moe_b_dn)
        pending = groups
    return settle(pending, mods[depth - 1])
```

```python
import functools
import math

import jax
import jax.numpy as jnp
from jax import lax
from jax.experimental import pallas as pl
from jax.experimental.pallas import tpu as pltpu
from jax.experimental.pallas import tpu_sc as plsc

F32 = jnp.float32
BF16 = jnp.bfloat16
HIGHEST = lax.Precision.HIGHEST

NORM_EPS = 1e-6
MLA_HEADS = 8
QK_NOPE_DIM = 64
QK_ROPE_DIM = 32
QK_HEAD_DIM = QK_NOPE_DIM + QK_ROPE_DIM
V_HEAD_DIM = 64
Q_LORA_RANK = 256
KV_LORA_RANK = 128
ROPE_THETA = 10000.0
POOL_WINDOWS = (2, 4, 8, 16)
SGU_HEADS = 4
SGU_CHUNK = 128
N_EXPERTS = 32
TOP_K = 4
SWIGLU_ALPHA = 1.702
SWIGLU_LIMIT = 7.0

LANES = 128
SUBLANES = 8
HEAD_SLAB = LANES
POOL_HALO = 16
ROW_TILE = 1024
ODD_TILE = 512
ATTN_TILE = 512
ATTN_Q_BLOCKS = 2
ATTN_ROWS = 32
ATTN_DIAG = 128
ATTN_HEADS = 4
S5_STEPS = 64
MOE_ROWS = 1024
MOE_PATHS = (128, 256, 512, 768, MOE_ROWS)
MOE_SPLITS = 2
SC_CORES = 2
SC_WORKERS = SC_CORES * 16
SC_CHUNK = 64
ADA_COLS = 1536
VMEM_LIMIT = 56 * 1024 * 1024
NEG_BIG = -1e30


def _sigmoid(v):
    return 1.0 / (1.0 + jnp.exp(-v))


def _gelu(v):
    return 0.5 * v * (1.0 + jnp.tanh(math.sqrt(2.0 / math.pi) * (v + 0.044715 * (v * v * v))))


def _rms(v, width):
    return lax.rsqrt(jnp.sum(v * v, axis=-1, keepdims=True) * (1.0 / width) + NORM_EPS)


def _mod_norm(x, g, sc, sh):
    return x * _rms(x, x.shape[-1]) * (g * (1.0 + sc)) + sh


def _bdot(a, b):
    return jnp.dot(a.astype(BF16), b, preferred_element_type=F32)


def _pack_pairs(v):
    w = v.shape[-1] // 2
    bits = pltpu.bitcast(v.astype(BF16).astype(F32), jnp.uint32)
    return (bits[:, :w] >> 16) | bits[:, w:]


def _unpack_pairs_f32(p):
    lo = pltpu.bitcast(p << 16, F32)
    hi = pltpu.bitcast(p & jnp.uint32(0xFFFF0000), F32)
    return lo, hi


def _unpack_pairs(p):
    lo, hi = _unpack_pairs_f32(p)
    return lo.astype(BF16), hi.astype(BF16)


def _ada_kernel(c_ref, w_ref, b_ref, o_ref):
    c = c_ref[...]
    act = c * _sigmoid(c)
    o_ref[...] = jnp.dot(act, w_ref[...], precision=HIGHEST, preferred_element_type=F32) + b_ref[...]


def _ada_call(c, ada_w, ada_b):
    depth, d, n = ada_w.shape
    bsz = c.shape[0]
    tn = ADA_COLS
    return pl.pallas_call(
        _ada_kernel,
        grid=(depth, n // tn),
        in_specs=[
            pl.BlockSpec((bsz, d), lambda l, j: (0, 0)),
            pl.BlockSpec((None, d, tn), lambda l, j: (l, 0, j)),
            pl.BlockSpec((None, 1, tn), lambda l, j: (l, 0, j)),
        ],
        out_specs=pl.BlockSpec((None, bsz, tn), lambda l, j: (l, 0, j)),
        out_shape=jax.ShapeDtypeStruct((depth, bsz, n), F32),
        compiler_params=pltpu.CompilerParams(dimension_semantics=("parallel", "parallel"),
                                             vmem_limit_bytes=VMEM_LIMIT),
        name="ada_mod",
    )(c, ada_w, ada_b.reshape(depth, 1, n))


_C_Q = 0
_C_KV = Q_LORA_RANK
_C_PE = _C_KV + KV_LORA_RANK
_C_PESW = _C_PE + HEAD_SLAB
_C_U = _C_PESW + HEAD_SLAB


def _even_in_kernel(x_ref, mod_ref, pos_ref, g_ref, win_ref, gq_ref, wq_ref, gkv_ref, wk_ref, wv_ref,
                    tab_ref, q_ref, k_ref, v_ref, u_ref):
    x = x_ref[...]
    h = _mod_norm(x, g_ref[...], mod_ref[1:2, :], mod_ref[0:1, :])
    z = _bdot(h, win_ref[...])
    q_c = z[:, _C_Q:_C_KV]
    kv_c = z[:, _C_KV:_C_PE]
    kpe = z[:, _C_PE:_C_PESW]
    kpe_sw = z[:, _C_PESW:_C_U]
    u_ref[...] = z[:, _C_U:]

    tm = x.shape[0]
    packs = LANES // QK_ROPE_DIM
    qrows = tm // packs
    lane = lax.broadcasted_iota(jnp.int32, (1, LANES), 1)
    packed = jnp.zeros((qrows, LANES), F32)
    for part in range(packs):
        in_part = (lane >= part * QK_ROPE_DIM) & (lane < (part + 1) * QK_ROPE_DIM)
        packed = jnp.where(in_part, pos_ref[part * qrows:(part + 1) * qrows, :], packed)
    ang = packed * tab_ref[5:6, :]
    cs_p = jnp.cos(ang)
    sn_p = jnp.sin(ang)
    on_rope = (lane >= QK_NOPE_DIM) & (lane < QK_HEAD_DIM)
    cs_parts = []
    sn_parts = []
    for part in range(packs):
        shift = (QK_NOPE_DIM - part * QK_ROPE_DIM) % LANES
        cs_r = cs_p if shift == 0 else pltpu.roll(cs_p, shift, axis=1)
        sn_r = sn_p if shift == 0 else pltpu.roll(sn_p, shift, axis=1)
        cs_parts.append(jnp.where(on_rope, cs_r, 1.0))
        sn_parts.append(jnp.where(on_rope, sn_r, 0.0))
    cs = jnp.concatenate(cs_parts, axis=0)
    sn = jnp.concatenate(sn_parts, axis=0)
    gcq = cs * tab_ref[1:2, :]
    gsq = sn * tab_ref[2:3, :]
    gck = cs * tab_ref[3:4, :]
    gsk = sn * tab_ref[4:5, :]

    qn = q_c * _rms(q_c, Q_LORA_RANK) * gq_ref[...]
    qq = _bdot(qn, wq_ref[...])
    kvn = kv_c * _rms(kv_c, KV_LORA_RANK) * gkv_ref[...]
    kk = _bdot(kvn, wk_ref[...])
    v_ref[...] = _bdot(kvn, wv_ref[...]).astype(v_ref.dtype)

    pe_rot = kpe * gck + kpe_sw * gsk
    pe_ss = jnp.sum(kpe * kpe, axis=-1, keepdims=True)
    hw = MLA_HEADS * HEAD_SLAB
    for hd in range(MLA_HEADS):
        lo = hd * HEAD_SLAB
        qr = qq[:, lo:lo + HEAD_SLAB]
        qs = qq[:, hw + lo:hw + lo + HEAD_SLAB]
        rq = _rms(qr, QK_HEAD_DIM)
        q_ref[hd] = (rq * (qr * gcq + qs * gsq)).astype(q_ref.dtype)
        kr = kk[:, lo:lo + HEAD_SLAB]
        rk = lax.rsqrt((jnp.sum(kr * kr, axis=-1, keepdims=True) + pe_ss) * (1.0 / QK_HEAD_DIM) + NORM_EPS)
        k_ref[hd] = (rk * (kr * gck + pe_rot)).astype(k_ref.dtype)


def _even_in_call(x, mod, posf, g, prep):
    bsz, seq, d = x.shape
    tm = min(ROW_TILE, seq)
    full = lambda a: pl.BlockSpec(a.shape, lambda b, s: (0,) * a.ndim)
    return pl.pallas_call(
        _even_in_kernel,
        grid=(bsz, seq // tm),
        in_specs=[
            pl.BlockSpec((None, tm, d), lambda b, s: (b, s, 0)),
            pl.BlockSpec((None, 6, d), lambda b, s: (b, 0, 0)),
            pl.BlockSpec((None, tm, 1), lambda b, s: (b, s, 0)),
            full(g), full(prep["w_in"]), full(prep["gq"]), full(prep["wq"]), full(prep["gkv"]),
            full(prep["wk"]), full(prep["wv"]), full(prep["tab"]),
        ],
        out_specs=[
            pl.BlockSpec((None, MLA_HEADS, tm, HEAD_SLAB), lambda b, s: (b, 0, s, 0)),
            pl.BlockSpec((None, MLA_HEADS, tm, HEAD_SLAB), lambda b, s: (b, 0, s, 0)),
            pl.BlockSpec((None, tm, MLA_HEADS * V_HEAD_DIM), lambda b, s: (b, s, 0)),
            pl.BlockSpec((tm, d // 2), lambda b, s: (s, b)),
        ],
        out_shape=[
            jax.ShapeDtypeStruct((bsz, MLA_HEADS, seq, HEAD_SLAB), BF16),
            jax.ShapeDtypeStruct((bsz, MLA_HEADS, seq, HEAD_SLAB), BF16),
            jax.ShapeDtypeStruct((bsz, seq, MLA_HEADS * V_HEAD_DIM), BF16),
            jax.ShapeDtypeStruct((seq, bsz * (d // 2)), F32),
        ],
        compiler_params=pltpu.CompilerParams(dimension_semantics=("parallel", "parallel"),
                                             vmem_limit_bytes=VMEM_LIMIT),
        name="even_in",
    )(x, mod, posf, g, prep["w_in"], prep["gq"], prep["wq"], prep["gkv"], prep["wk"], prep["wv"], prep["tab"])


def _prep_even(even_w_in, q_norm_g, w_uq, kv_norm_g, w_ukv, q_head_g, k_head_g):
    d = even_w_in.shape[0]
    half = QK_ROPE_DIM // 2
    nope = QK_NOPE_DIM
    c_pe = Q_LORA_RANK + KV_LORA_RANK
    w_pe = even_w_in[:, c_pe:c_pe + QK_ROPE_DIM]
    zeros = lambda n: jnp.zeros((d, n), F32)
    pe_slab = jnp.concatenate([zeros(nope), w_pe, zeros(HEAD_SLAB - QK_HEAD_DIM)], axis=1)
    pe_sw = jnp.concatenate([zeros(nope), -w_pe[:, half:], w_pe[:, :half], zeros(HEAD_SLAB - QK_HEAD_DIM)], axis=1)
    w_in = jnp.concatenate([even_w_in[:, :c_pe], pe_slab, pe_sw, even_w_in[:, c_pe + QK_ROPE_DIM:]], axis=1)

    r = w_uq.shape[0]
    padq = jnp.zeros((r, MLA_HEADS, HEAD_SLAB - QK_HEAD_DIM), F32)
    wq_plain = jnp.concatenate([w_uq, padq], axis=2).reshape(r, MLA_HEADS * HEAD_SLAB)
    wq_sw = jnp.concatenate([jnp.zeros((r, MLA_HEADS, nope), F32), -w_uq[:, :, nope + half:],
                             w_uq[:, :, nope:nope + half], padq], axis=2).reshape(r, MLA_HEADS * HEAD_SLAB)
    wq = jnp.concatenate([wq_plain, wq_sw], axis=1)

    rk = w_ukv.shape[0]
    wk = jnp.concatenate([w_ukv[:, :, :nope], jnp.zeros((rk, MLA_HEADS, HEAD_SLAB - nope), F32)],
                         axis=2).reshape(rk, MLA_HEADS * HEAD_SLAB)
    wv = w_ukv[:, :, nope:].reshape(rk, MLA_HEADS * V_HEAD_DIM)

    inv_freq = 1.0 / (ROPE_THETA ** (jnp.arange(half, dtype=F32) / half))
    pad_tail = jnp.zeros((HEAD_SLAB - QK_HEAD_DIM,), F32)
    freq_row = jnp.concatenate([jnp.zeros((nope,), F32), inv_freq, inv_freq, pad_tail])

    def gain_rows(gv, scale):
        plain = jnp.concatenate([gv, pad_tail]) * scale
        swapped = jnp.concatenate([jnp.zeros((nope,), F32), gv[nope + half:], gv[nope:nope + half], pad_tail]) * scale
        return plain, swapped

    gq_plain, gq_sw = gain_rows(q_head_g, QK_HEAD_DIM ** -0.5 * math.log2(math.e))
    gk_plain, gk_sw = gain_rows(k_head_g, 1.0)
    freq_packed = jnp.tile(jnp.concatenate([inv_freq, inv_freq]), LANES // QK_ROPE_DIM)
    tab = jnp.stack([freq_row, gq_plain, gq_sw, gk_plain, gk_sw, freq_packed, freq_row * 0, freq_row * 0])
    return {
        "w_in": w_in.astype(BF16), "gq": q_norm_g.reshape(1, -1), "wq": wq.astype(BF16),
        "gkv": kv_norm_g.reshape(1, -1), "wk": wk.astype(BF16), "wv": wv.astype(BF16), "tab": tab,
    }


def _attn_kernel(qi_ref, kj_ref, q_ref, k_ref, v_ref, o_ref, m_sc, a_sc, acc_sc, s_sc, p_sc, *, tq, tk):
    step = pl.program_id(2)
    i = qi_ref[step]
    j = kj_ref[step]
    sum_lane = (V_HEAD_DIM, 0)
    subs = tq // tk

    @pl.when(j == 0)
    def _():
        m_sc[...] = jnp.full(m_sc.shape, -jnp.inf, F32)
        acc_sc[...] = jnp.zeros(acc_sc.shape, F32)

    def sweep(diag_sub):
        lo = 0 if diag_sub is None else diag_sub * tk
        lane = lax.broadcasted_iota(jnp.int32, (1, LANES), 1)
        if diag_sub is None:
            segs = [(0, tq, tk)]
        else:
            segs = [(lo + a * ATTN_DIAG, ATTN_DIAG, (a + 1) * ATTN_DIAG) for a in range(tk // ATTN_DIAG)]
            if lo + tk < tq:
                segs.append((lo + tk, tq - lo - tk, tk))
        for hh in range(ATTN_HEADS):
            for r_lo, r_n, nk in segs:
                s_sc[hh, r_lo:r_lo + r_n, 0:nk] = lax.dot_general(
                    q_ref[hh, r_lo:r_lo + r_n, :], k_ref[hh, 0:nk, :], (((1,), (1,)), ((), ())),
                    preferred_element_type=F32)
        for hh in range(ATTN_HEADS):
            v = v_ref[:, (hh // 2) * LANES:(hh // 2 + 1) * LANES]
            own = (lane < V_HEAD_DIM) == (hh % 2 == 0)
            ones = jnp.where(lane == sum_lane[hh % 2], 1.0, 0.0).astype(v.dtype)
            vh = jnp.where(own, v, jnp.broadcast_to(ones, v.shape))
            for r_lo, r_n, nk in segs:
                for r0 in range(r_lo, r_lo + r_n, ATTN_ROWS):
                    rows = pl.ds(r0, ATTN_ROWS)
                    s = s_sc[hh, rows, 0:nk]
                    if diag_sub is not None and r0 < lo + tk:
                        row = (r0 - lo) + lax.broadcasted_iota(jnp.int32, (ATTN_ROWS, nk), 0)
                        col = lax.broadcasted_iota(jnp.int32, (ATTN_ROWS, nk), 1)
                        s = jnp.where(col <= row, s, -jnp.inf)
                    m_prev = m_sc[hh, rows, :]
                    m_new = jnp.maximum(m_prev, jnp.max(s, axis=-1, keepdims=True))
                    a_sc[hh, rows, :] = jnp.exp2(m_prev - m_new)
                    m_sc[hh, rows, :] = m_new
                    shifted = s - jnp.concatenate([m_new] * (nk // LANES), axis=1)
                    p_sc[hh, rows, 0:nk] = jnp.exp2(shifted.astype(BF16))
                seg = pl.ds(r_lo, r_n)
                acc_sc[hh, seg, :] = (acc_sc[hh, seg, :] * a_sc[hh, seg, :]
                                      + jnp.dot(p_sc[hh, seg, 0:nk], vh[0:nk, :], preferred_element_type=F32))

    diag = j - subs * i

    @pl.when(diag < 0)
    def _():
        sweep(None)

    for ds_ in range(subs):
        @pl.when(diag == ds_)
        def _(ds_=ds_):
            sweep(ds_)

    @pl.when(diag == subs - 1)
    def _():
        lane = lax.broadcasted_iota(jnp.int32, (1, LANES), 1)
        for pp in range(ATTN_HEADS // 2):
            acc0 = acc_sc[2 * pp]
            acc1 = acc_sc[2 * pp + 1]
            l0 = acc0[:, sum_lane[0]:sum_lane[0] + 1]
            l1 = acc1[:, sum_lane[1]:sum_lane[1] + 1]
            o_ref[:, pp * LANES:(pp + 1) * LANES] = jnp.where(lane < V_HEAD_DIM, acc0 / l0,
                                                              acc1 / l1).astype(o_ref.dtype)


def _attn_call(q, k, v):
    bsz, nh, seq, _ = q.shape
    tk = min(ATTN_TILE, seq)
    tq = min(ATTN_Q_BLOCKS * tk, seq)
    subs = tq // tk
    nq = seq // tq
    pairs = [(i, j) for i in range(nq) for j in range(subs * (i + 1))]
    qi = jnp.asarray([p[0] for p in pairs], jnp.int32)
    kj = jnp.asarray([p[1] for p in pairs], jnp.int32)
    kern = functools.partial(_attn_kernel, tq=tq, tk=tk)
    hp = ATTN_HEADS
    assert nh % hp == 0
    return pl.pallas_call(
        kern,
        grid_spec=pltpu.PrefetchScalarGridSpec(
            num_scalar_prefetch=2,
            grid=(bsz, nh // hp, len(pairs)),
            in_specs=[
                pl.BlockSpec((None, hp, tq, HEAD_SLAB), lambda b, h, p, qi, kj: (b, h, qi[p], 0)),
                pl.BlockSpec((None, hp, tk, HEAD_SLAB), lambda b, h, p, qi, kj: (b, h, kj[p], 0)),
                pl.BlockSpec((None, tk, hp * V_HEAD_DIM), lambda b, h, p, qi, kj: (b, kj[p], h)),
            ],
            out_specs=pl.BlockSpec((None, tq, hp * V_HEAD_DIM), lambda b, h, p, qi, kj: (b, qi[p], h)),
            scratch_shapes=[pltpu.VMEM((hp, tq, LANES), F32), pltpu.VMEM((hp, tq, LANES), F32),
                            pltpu.VMEM((hp, tq, LANES), F32),
                            pltpu.VMEM((hp, tq, tk), F32), pltpu.VMEM((hp, tq, tk), BF16)],
        ),
        out_shape=jax.ShapeDtypeStruct((bsz, seq, nh * V_HEAD_DIM), BF16),
        compiler_params=pltpu.CompilerParams(
            dimension_semantics=("parallel", "parallel", "arbitrary"),
            vmem_limit_bytes=VMEM_LIMIT),
        name="mla_attention",
    )(qi, kj, q, k, v)


def _s5_disc_kernel(are_ref, aim_ref, ldt_ref, bre_ref, bim_ref, abre_ref, abim_ref, bbre_ref, bbim_ref):
    dt = jnp.exp(ldt_ref[...])
    lam_re = jnp.minimum(are_ref[...], -1e-4)
    lam_im = aim_ref[...]
    mag = jnp.exp(lam_re * dt)
    ab_re = mag * jnp.cos(lam_im * dt)
    ab_im = mag * jnp.sin(lam_im * dt)
    den = lam_re * lam_re + lam_im * lam_im
    num_re = ab_re - 1.0
    f_re = (num_re * lam_re + ab_im * lam_im) / den
    f_im = (ab_im * lam_re - num_re * lam_im) / den
    abre_ref[...] = ab_re
    abim_ref[...] = ab_im
    br = bre_ref[...]
    bi = bim_ref[...]
    bbre_ref[...] = f_re[:, None, :] * br - f_im[:, None, :] * bi
    bbim_ref[...] = f_re[:, None, :] * bi + f_im[:, None, :] * br


def _s5_disc_call(a_re, a_im, log_dt, b_re, b_im):
    g, p = a_re.shape
    bre_t = jnp.swapaxes(b_re, 1, 2)
    bim_t = jnp.swapaxes(b_im, 1, 2)
    return pl.pallas_call(
        _s5_disc_kernel,
        out_shape=[jax.ShapeDtypeStruct((g, p), F32), jax.ShapeDtypeStruct((g, p), F32),
                   jax.ShapeDtypeStruct(bre_t.shape, F32), jax.ShapeDtypeStruct(bre_t.shape, F32)],
        name="s5_discretize",
    )(a_re, a_im, log_dt.reshape(g, 1), bre_t, bim_t)


def _block_diag_halves(m):
    g, r, c = m.shape
    gh = g // 2
    eye = jnp.eye(gh, dtype=m.dtype)
    mh = m.reshape(2, gh, r, c)
    return (mh[:, :, :, None, :] * eye[None, :, None, :, None]).reshape(2, gh * r, gh * c)


def _s5_kernel(u_ref, bre_ref, bim_ref, are_ref, aim_ref, cre_ref, cim_ref, d_ref, gw_ref, gb_ref,
               o_ref, sre, sim, dre, dim, xbr, xbi, *, steps):
    @pl.when(pl.program_id(0) == 0)
    def _():
        sre[...] = jnp.zeros(sre.shape, F32)
        sim[...] = jnp.zeros(sim.shape, F32)

    rows = steps * SUBLANES
    w = u_ref.shape[-1]
    u = u_ref[...].reshape(rows, w)
    ub = u.astype(BF16)
    kh = w // 2
    nh = dre.shape[1] // 2
    for hf in range(2):
        dre[:, hf * nh:(hf + 1) * nh] = jnp.dot(ub[:, hf * kh:(hf + 1) * kh], bre_ref[hf], preferred_element_type=F32)
        dim[:, hf * nh:(hf + 1) * nh] = jnp.dot(ub[:, hf * kh:(hf + 1) * kh], bim_ref[hf], preferred_element_type=F32)

    xr = sre[...]
    xi = sim[...]
    for t in range(0, steps, 2):
        pair_r = []
        pair_i = []
        for r0 in (t * SUBLANES, (t + 1) * SUBLANES):
            nr = are_ref[...] * xr - aim_ref[...] * xi + dre[r0:r0 + SUBLANES, :]
            ni = are_ref[...] * xi + aim_ref[...] * xr + dim[r0:r0 + SUBLANES, :]
            xr, xi = nr, ni
            pair_r.append(nr)
            pair_i.append(ni)
        xbr[t * SUBLANES:(t + 2) * SUBLANES, :] = jnp.concatenate(pair_r, axis=0).astype(BF16)
        xbi[t * SUBLANES:(t + 2) * SUBLANES, :] = jnp.concatenate(pair_i, axis=0).astype(BF16)
    sre[...] = xr
    sim[...] = xi

    ys = []
    for hf in range(2):
        yr = jnp.dot(xbr[:, hf * nh:(hf + 1) * nh], cre_ref[hf], preferred_element_type=F32)
        yi = jnp.dot(xbi[:, hf * nh:(hf + 1) * nh], cim_ref[hf], preferred_element_type=F32)
        ys.append(yr - yi)
    y = jnp.concatenate(ys, axis=1) + d_ref[...] * u
    g = _gelu(y)
    out = g * _sigmoid(_bdot(g, gw_ref[...]) + gb_ref[...])
    o_ref[...] = out.reshape(steps, SUBLANES, w).astype(o_ref.dtype)


def _s5_call(u_t, disc, c_re, c_im, d_skip, glu_w, glu_b):
    seq, bsz, w = u_t.shape
    assert bsz == SUBLANES
    ab_re, ab_im, bb_re, bb_im = disc
    g, p = ab_re.shape
    n_state = g * p
    bre = _block_diag_halves(bb_re).astype(BF16)
    bim = _block_diag_halves(bb_im).astype(BF16)
    cre = _block_diag_halves(jnp.swapaxes(c_re, 1, 2)).astype(BF16)
    cim = _block_diag_halves(jnp.swapaxes(c_im, 1, 2)).astype(BF16)
    steps = min(S5_STEPS, seq)
    full = lambda a: pl.BlockSpec(a.shape, lambda s: (0,) * a.ndim)
    rep = lambda a: jnp.broadcast_to(a.reshape(1, n_state), (bsz, n_state))
    args = (bre, bim, rep(ab_re), rep(ab_im), cre, cim,
            d_skip.reshape(1, w), glu_w.astype(BF16), glu_b.reshape(1, w))
    return pl.pallas_call(
        functools.partial(_s5_kernel, steps=steps),
        grid=(seq // steps,),
        in_specs=[pl.BlockSpec((steps, bsz, w), lambda s: (s, 0, 0))] + [full(a) for a in args],
        out_specs=pl.BlockSpec((steps, bsz, w), lambda s: (s, 0, 0)),
        out_shape=jax.ShapeDtypeStruct((seq, bsz, w), BF16),
        scratch_shapes=[pltpu.VMEM((bsz, n_state), F32), pltpu.VMEM((bsz, n_state), F32),
                        pltpu.VMEM((steps * bsz, n_state), F32), pltpu.VMEM((steps * bsz, n_state), F32),
                        pltpu.VMEM((steps * bsz, n_state), BF16), pltpu.VMEM((steps * bsz, n_state), BF16)],
        compiler_params=pltpu.CompilerParams(dimension_semantics=("arbitrary",), vmem_limit_bytes=VMEM_LIMIT),
        name="s5_scan",
    )(u_t, *args)


def _router_tail(x_new, mod_ref, gf_ref, rw_ref, rb_ref, h2_ref, te_ref, tet_ref, cnt_ref):
    h2 = _mod_norm(x_new, gf_ref[...], mod_ref[4:5, :], mod_ref[3:4, :])
    h2_ref[...] = _pack_pairs(h2)
    h_hi = h2.astype(BF16)
    h_lo = (h2 - h_hi.astype(F32)).astype(BF16)
    r_hi = jnp.dot(h_hi, rw_ref[...], preferred_element_type=F32)
    r_lo = jnp.dot(h_lo, rw_ref[...], preferred_element_type=F32)
    logits = r_hi[:, :LANES] + r_hi[:, LANES:] + r_lo[:, :LANES] + rb_ref[...]
    lane = lax.broadcasted_iota(jnp.int32, logits.shape, 1).astype(F32)
    vals = []
    idxs = []
    work = logits
    for _ in range(TOP_K):
        m = jnp.max(work, axis=-1, keepdims=True)
        idx = jnp.min(jnp.where(work == m, lane, float(LANES)), axis=-1, keepdims=True)
        vals.append(m)
        idxs.append(idx)
        work = jnp.where(lane == idx, NEG_BIG * 2.0, work)
    exps = [jnp.exp(vv - vals[0]) for vv in vals]
    tot = exps[0] + exps[1] + exps[2] + exps[3]
    te = jnp.zeros(logits.shape, F32)
    picked = jnp.zeros(logits.shape, F32)
    for kk in range(TOP_K):
        te = jnp.where(lane == float(kk), idxs[kk], te)
        te = jnp.where(lane == float(TOP_K + kk), exps[kk] / tot, te)
        picked = picked + jnp.where(lane == idxs[kk], 1.0, 0.0)
    te_ref[...] = te[:, :2 * TOP_K]
    tet_ref[...] = te.T[:2 * TOP_K, :]
    cnt_ref[...] = jnp.sum(picked, axis=0, keepdims=True)


def _mix_out_kernel(x_ref, a_ref, s_ref, mod_ref, wo_ref, gf_ref, rw_ref, rb_ref, xo_ref, h2_ref, te_ref,
                    tet_ref, cnt_ref):
    ka = a_ref.shape[-1]
    mix = jnp.dot(a_ref[...], wo_ref[:ka, :], preferred_element_type=F32)
    mix = mix + jnp.dot(s_ref[...], wo_ref[ka:, :], preferred_element_type=F32)
    x_new = x_ref[...] + mod_ref[2:3, :] * mix
    xo_ref[...] = x_new
    _router_tail(x_new, mod_ref, gf_ref, rw_ref, rb_ref, h2_ref, te_ref, tet_ref, cnt_ref)


def _tail_out_specs(bsz, seq, tm, d):
    nt = seq // tm
    specs = [
        pl.BlockSpec((None, tm, d), lambda b, s: (b, s, 0)),
        pl.BlockSpec((tm, d // 2), lambda b, s: (b * nt + s, 0)),
        pl.BlockSpec((tm, 2 * TOP_K), lambda b, s: (b * nt + s, 0)),
        pl.BlockSpec((2 * TOP_K, tm), lambda b, s: (0, b * nt + s)),
        pl.BlockSpec((None, 1, LANES), lambda b, s: (b * nt + s, 0, 0)),
    ]
    shapes = [
        jax.ShapeDtypeStruct((bsz, seq, d), F32),
        jax.ShapeDtypeStruct((bsz * seq, d // 2), jnp.uint32),
        jax.ShapeDtypeStruct((bsz * seq, 2 * TOP_K), F32),
        jax.ShapeDtypeStruct((2 * TOP_K, bsz * seq), F32),
        jax.ShapeDtypeStruct((bsz * nt, 1, LANES), F32),
    ]
    return specs, shapes


def _router_pad(router_w, router_b):
    d, e = router_w.shape
    rw = jnp.concatenate([router_w, jnp.zeros((d, LANES - e), F32)], axis=1)
    rw_hi = rw.astype(BF16)
    rw_lo = (rw - rw_hi.astype(F32)).astype(BF16)
    rb = jnp.concatenate([router_b, jnp.full((LANES - e,), NEG_BIG, F32)]).reshape(1, LANES)
    return jnp.concatenate([rw_hi, rw_lo], axis=1), rb


def _mix_out_call(x, attn, ssm_t, mod, w_out, gf, rw, rb):
    bsz, seq, d = x.shape
    tm = min(ROW_TILE, seq)
    ka = attn.shape[-1]
    ks = ssm_t.shape[-1] // bsz
    full = lambda a: pl.BlockSpec(a.shape, lambda b, s: (0,) * a.ndim)
    out_specs, out_shape = _tail_out_specs(bsz, seq, tm, d)
    return pl.pallas_call(
        _mix_out_kernel,
        grid=(bsz, seq // tm),
        in_specs=[
            pl.BlockSpec((None, tm, d), lambda b, s: (b, s, 0)),
            pl.BlockSpec((None, tm, ka), lambda b, s: (b, s, 0)),
            pl.BlockSpec((tm, ks), lambda b, s: (s, b)),
            pl.BlockSpec((None, 6, d), lambda b, s: (b, 0, 0)),
            full(w_out), full(gf), full(rw), full(rb),
        ],
        out_specs=out_specs,
        out_shape=out_shape,
        compiler_params=pltpu.CompilerParams(dimension_semantics=("parallel", "parallel"),
                                             vmem_limit_bytes=VMEM_LIMIT),
        name="even_out",
    )(x, attn, ssm_t, mod, w_out, gf, rw, rb)


def _moe_sum(yg_ref, te_ref):
    te = te_ref[...]
    acc_lo = acc_hi = None
    for kk in range(TOP_K):
        lo, hi = _unpack_pairs_f32(yg_ref[kk])
        gate = te[:, TOP_K + kk:TOP_K + kk + 1]
        acc_lo = gate * lo if kk == 0 else acc_lo + gate * lo
        acc_hi = gate * hi if kk == 0 else acc_hi + gate * hi
    return jnp.concatenate([acc_lo, acc_hi], axis=1)


def _odd_kernel(x_ref, yg_ref, tep_ref, modp_ref, mod_ref, g_ref, win_ref, icnt_ref, wp_ref, ps_ref, gv_ref,
                wsp_ref, bsp_ref, wo_ref, gf_ref, rw_ref, rb_ref, xo_ref, h2_ref, te_ref, tet_ref, cnt_ref, ext_sc):
    tm = x_ref.shape[0]
    pw = wp_ref.shape[-1]
    width = pw * len(POOL_WINDOWS)

    @pl.when(pl.program_id(1) == 0)
    def _():
        ext_sc[0:POOL_HALO, :] = jnp.zeros((POOL_HALO, width), F32)

    x = x_ref[...] + modp_ref[5:6, :] * _moe_sum(yg_ref, tep_ref)
    h = _mod_norm(x, g_ref[...], mod_ref[1:2, :], mod_ref[0:1, :])
    z = _bdot(h, win_ref[...])
    up = z[:, :width]
    ext_sc[POOL_HALO:POOL_HALO + tm, :] = up

    pooled = []
    for gi, win in enumerate(POOL_WINDOWS):
        cols = slice(gi * pw, (gi + 1) * pw)
        acc = up[:, cols]
        for lag in range(1, win):
            acc = acc + ext_sc[POOL_HALO - lag:POOL_HALO - lag + tm, cols]
        pg = acc * icnt_ref[:, gi:gi + 1] - up[:, cols]
        pooled.append(_bdot(pg, wp_ref[gi]) * ps_ref[:, cols])
    ext_sc[0:POOL_HALO, :] = ext_sc[tm:tm + POOL_HALO, :]
    pooled = jnp.concatenate(pooled, axis=1)

    ug = _gelu(z[:, width:2 * width])
    vg = _gelu(z[:, 2 * width:])
    vn = (vg * _rms(vg, width) * gv_ref[...]).astype(BF16)
    hd = width // SGU_HEADS
    chunks = []
    for ci in range(tm // SGU_CHUNK):
        heads = []
        for hh in range(SGU_HEADS):
            blk = vn[ci * SGU_CHUNK:(ci + 1) * SGU_CHUNK, hh * hd:(hh + 1) * hd]
            heads.append(jnp.dot(wsp_ref[hh], blk, preferred_element_type=F32) + bsp_ref[hh])
        chunks.append(jnp.concatenate(heads, axis=1))
    gated = ug * jnp.concatenate(chunks, axis=0)

    mix = _bdot(pooled, wo_ref[:width, :]) + _bdot(gated, wo_ref[width:, :])
    x_new = x + mod_ref[2:3, :] * mix
    xo_ref[...] = x_new
    _router_tail(x_new, mod_ref, gf_ref, rw_ref, rb_ref, h2_ref, te_ref, tet_ref, cnt_ref)


def _odd_call(grp, mod_prev, mod, nb, g, w_in, pool_w, pool_scale, sgu_norm_g, sgu_w, sgu_b, w_out, gf, rw, rb):
    x = grp["x"]
    _, seq, d = x.shape
    tm = min(ODD_TILE, seq)
    nt = seq // tm
    xb0, tok0, mb0 = grp["x_b0"], grp["tok0"], grp["mod_b0"]
    tile0 = tok0 // tm
    yg = grp["yg"]
    width = pool_scale.shape[0]
    hd = width // SGU_HEADS
    t = jnp.arange(seq, dtype=jnp.int32)
    icnt = jnp.stack([1.0 / jnp.minimum(t + 1, wn).astype(F32) for wn in POOL_WINDOWS], axis=1)
    wsp = jnp.tril(sgu_w).astype(BF16)
    bsp = jnp.broadcast_to(sgu_b[:, :, None], (SGU_HEADS, SGU_CHUNK, hd))
    args = (g, w_in.astype(BF16), icnt, pool_w.astype(BF16), pool_scale.reshape(1, width),
            sgu_norm_g.reshape(1, width), wsp, bsp, w_out.astype(BF16), gf, rw, rb)
    full = lambda a: pl.BlockSpec(a.shape, lambda b, s: (0,) * a.ndim)
    in_specs = [pl.BlockSpec((None, tm, d), lambda b, s: (xb0 + b, s, 0)),
                pl.BlockSpec((TOP_K, tm, yg.shape[-1]), lambda b, s: (0, b * nt + s, 0)),
                pl.BlockSpec((tm, 2 * TOP_K), lambda b, s: (tile0 + b * nt + s, 0)),
                pl.BlockSpec((None, 6, d), lambda b, s: (mb0 + b, 0, 0)),
                pl.BlockSpec((None, 6, d), lambda b, s: (mb0 + b, 0, 0))]
    for idx, a in enumerate(args):
        in_specs.append(pl.BlockSpec((tm, len(POOL_WINDOWS)), lambda b, s: (s, 0)) if idx == 2 else full(a))
    out_specs, out_shape = _tail_out_specs(nb, seq, tm, d)
    return pl.pallas_call(
        _odd_kernel,
        grid=(nb, nt),
        in_specs=in_specs,
        out_specs=out_specs,
        out_shape=out_shape,
        scratch_shapes=[pltpu.VMEM((tm + POOL_HALO, width), F32)],
        compiler_params=pltpu.CompilerParams(dimension_semantics=("parallel", "arbitrary"),
                                             vmem_limit_bytes=VMEM_LIMIT),
        name="odd_mixer",
    )(x, yg, grp["te"], mod_prev, mod, *args)


def _dest_kernel(tet_ref, base_ref, tri_ref, dst_ref):
    tr = tet_ref.shape[1]
    tet = tet_ref[...]
    expert = lax.broadcasted_iota(jnp.int32, (LANES, tr), 0).astype(F32)
    hots = [tet[kk:kk + 1, :] == expert for kk in range(TOP_K)]
    oh = jnp.zeros((LANES, tr), F32)
    for hot in hots:
        oh = oh + jnp.where(hot, 1.0, 0.0)
    before = jnp.dot(oh.astype(BF16), tri_ref[...], preferred_element_type=F32) + base_ref[...]
    row = lax.broadcasted_iota(jnp.int32, (2 * TOP_K, tr), 0)
    dst = jnp.zeros((2 * TOP_K, tr), F32)
    for kk, hot in enumerate(hots):
        dst = jnp.where(row == kk, jnp.sum(jnp.where(hot, before, 0.0), axis=0, keepdims=True), dst)
    dst_ref[...] = dst.astype(jnp.int32)


def _dest_call(tet, base, tok0, n_tok):
    tiles = base.shape[0]
    tr = n_tok // tiles
    tile0 = tok0 // tr
    tri = (jnp.arange(tr)[:, None] < jnp.arange(tr)[None, :]).astype(BF16)
    return pl.pallas_call(
        _dest_kernel,
        grid=(tiles,),
        in_specs=[pl.BlockSpec((2 * TOP_K, tr), lambda i: (0, tile0 + i)),
                  pl.BlockSpec((None, LANES, 1), lambda i: (i, 0, 0)),
                  pl.BlockSpec((tr, tr), lambda i: (0, 0))],
        out_specs=pl.BlockSpec((2 * TOP_K, tr), lambda i: (0, i)),
        out_shape=jax.ShapeDtypeStruct((2 * TOP_K, n_tok), jnp.int32),
        compiler_params=pltpu.CompilerParams(dimension_semantics=("parallel",)),
        name="route_dest",
    )(tet, base, tri)


def _sc_gather(table, idx):
    n = idx.shape[0]
    per_w = n // SC_WORKERS
    assert per_w * SC_WORKERS == n and per_w % SC_CHUNK == 0
    n_chunks = per_w // SC_CHUNK
    row_shape = table.shape[1:]
    mesh = plsc.VectorSubcoreMesh(core_axis_name="c", subcore_axis_name="s")

    @functools.partial(
        pl.kernel, mesh=mesh,
        out_type=jax.ShapeDtypeStruct((n,) + row_shape, table.dtype),
        scratch_types=[pltpu.VMEM((SC_CHUNK,), jnp.int32), pltpu.VMEM((SC_CHUNK,) + row_shape, table.dtype),
                       pltpu.SemaphoreType.DMA],
        name="sc_row_gather",
    )
    def gather(table_hbm, idx_hbm, out_hbm, idx_v, rows_v, sem):
        wid = lax.axis_index("s") * SC_CORES + lax.axis_index("c")
        base = wid * per_w

        @pl.loop(0, n_chunks)
        def _(ci):
            off = pl.multiple_of(base + ci * SC_CHUNK, SC_CHUNK)
            pltpu.sync_copy(idx_hbm.at[pl.ds(off, SC_CHUNK)], idx_v)
            pltpu.async_copy(table_hbm.at[idx_v], rows_v, sem).wait()
            pltpu.sync_copy(rows_v, out_hbm.at[pl.ds(off, SC_CHUNK)])

    return gather(table, idx)


def _sc_scatter(rows, dests, n_out, tok0):
    t = dests[0].shape[0]
    per_w = t // SC_WORKERS
    assert per_w * SC_WORKERS == t and per_w % SC_CHUNK == 0
    n_chunks = per_w // SC_CHUNK
    row_shape = rows.shape[1:]
    nk = len(dests)
    mesh = plsc.VectorSubcoreMesh(core_axis_name="c", subcore_axis_name="s")

    @functools.partial(
        pl.kernel, mesh=mesh,
        out_type=jax.ShapeDtypeStruct((n_out,) + row_shape, rows.dtype),
        scratch_types=[pltpu.VMEM((SC_CHUNK,), jnp.int32)] * nk
        + [pltpu.VMEM((SC_CHUNK,) + row_shape, rows.dtype), pltpu.SemaphoreType.DMA],
        name="sc_row_scatter",
    )
    def scatter(rows_hbm, *rest):
        dest_hbm = rest[:nk]
        out_hbm = rest[nk]
        idx_v = rest[nk + 1:2 * nk + 1]
        rows_v, sem = rest[2 * nk + 1:]
        wid = lax.axis_index("s") * SC_CORES + lax.axis_index("c")
        base = wid * per_w

        @pl.loop(0, n_chunks)
        def _(ci):
            off = pl.multiple_of(base + ci * SC_CHUNK, SC_CHUNK)
            src = pl.multiple_of(tok0 + off, SC_CHUNK)
            pltpu.sync_copy(rows_hbm.at[pl.ds(src, SC_CHUNK)], rows_v)
            for kk in range(nk):
                pltpu.sync_copy(dest_hbm[kk].at[pl.ds(off, SC_CHUNK)], idx_v[kk])
            copies = [pltpu.async_copy(rows_v, out_hbm.at[idx_v[kk]], sem) for kk in range(nk)]
            for cp in copies:
                cp.wait()

    return scatter(rows, *dests)


def _expert_kernel(be_ref, nv_ref, ord_ref, ue_ref, nu_ref, x_ref, wgu_hbm, bgu_ref, wdn_hbm, bdn_ref, y_ref,
                   wgu_f32, wdn_f32, wgu_bf, wdn_bf, sem, *, layer):
    i = pl.program_id(0)
    used = i < nu_ref[0]
    pos = ord_ref[i]
    fresh = jnp.logical_or(i == 0, ord_ref[jnp.maximum(i - 1, 0)] != pos)

    def weight_copies(expert):
        return (pltpu.make_async_copy(wgu_hbm.at[layer, expert], wgu_f32, sem.at[0]),
                pltpu.make_async_copy(wdn_hbm.at[layer, expert], wdn_f32, sem.at[1]))

    @pl.when(i == 0)
    def _():
        for cp in weight_copies(ue_ref[0]):
            cp.start()

    @pl.when(jnp.logical_and(used, fresh))
    def _():
        for cp in weight_copies(ue_ref[pos]):
            cp.wait()
        wgu_bf[...] = wgu_f32[...].astype(BF16)
        wdn_bf[...] = wdn_f32[...].astype(BF16)

        @pl.when(pos + 1 < nu_ref[1])
        def _():
            for cp in weight_copies(ue_ref[pos + 1]):
                cp.start()

    def ffn(rows):
        x = jnp.concatenate(_unpack_pairs(x_ref[0:rows, :]), axis=1)
        z = jnp.dot(x, wgu_bf[...], preferred_element_type=F32) + bgu_ref[...]
        ff = z.shape[-1] // 2
        gate = jnp.minimum(z[:, :ff], SWIGLU_LIMIT)
        lin = jnp.clip(z[:, ff:], -SWIGLU_LIMIT, SWIGLU_LIMIT)
        act = gate * _sigmoid(SWIGLU_ALPHA * gate) * (lin + 1.0)
        y = _bdot(act, wdn_bf[...]) + bdn_ref[...]
        y_ref[0:rows, :] = _pack_pairs(y)

    nv = nv_ref[i]
    below = 0
    for size in MOE_PATHS:
        @pl.when(jnp.logical_and(used, jnp.logical_and(nv > below, nv <= size)))
        def _(size=size):
            ffn(size)
        below = size


def _expert_call(layer, block_e, block_valid, block_pos, used_experts, n_used, xs, w_gu, b_gu, w_dn, b_dn):
    n_rows, dh = xs.shape
    depth, e, d, ff2 = w_gu.shape
    ff = ff2 // 2
    nb = n_rows // MOE_ROWS
    row_map = lambda i, be, nv, po, ue, nu: (jnp.minimum(i, nu[0] - 1), 0)
    b_map = lambda i, be, nv, po, ue, nu: (layer, be[i], 0, 0)
    return pl.pallas_call(
        functools.partial(_expert_kernel, layer=layer),
        grid_spec=pltpu.PrefetchScalarGridSpec(
            num_scalar_prefetch=5,
            grid=(nb,),
            in_specs=[
                pl.BlockSpec((MOE_ROWS, dh), row_map),
                pl.BlockSpec(memory_space=pl.ANY),
                pl.BlockSpec((None, None, 1, ff2), b_map),
                pl.BlockSpec(memory_space=pl.ANY),
                pl.BlockSpec((None, None, 1, d), b_map),
            ],
            out_specs=pl.BlockSpec((MOE_ROWS, dh), row_map),
            scratch_shapes=[pltpu.VMEM((d, ff2), F32), pltpu.VMEM((ff, d), F32),
                            pltpu.VMEM((d, ff2), BF16), pltpu.VMEM((ff, d), BF16),
                            pltpu.SemaphoreType.DMA((2,))],
        ),
        out_shape=jax.ShapeDtypeStruct((n_rows, dh), jnp.uint32),
        compiler_params=pltpu.CompilerParams(dimension_semantics=("arbitrary",), vmem_limit_bytes=VMEM_LIMIT),
        name="moe_experts",
    )(block_e, block_valid, block_pos, used_experts, n_used, xs, w_gu, b_gu.reshape(depth, e, 1, ff2), w_dn,
      b_dn.reshape(depth, e, 1, d))


def _combine_kernel(x_ref, yg_ref, te_ref, mod_ref, *rest):
    o_ref = rest[-1]
    o_ref[...] = x_ref[...] + mod_ref[5:6, :] * _moe_sum(yg_ref, te_ref)


def _combine_call(grp, mod, prev, bsz, nb):
    x = grp["x"]
    _, seq, d = x.shape
    tm = min(ROW_TILE, seq)
    nt = seq // tm
    xb0, mb0 = grp["x_b0"], grp["mod_b0"]
    tile0 = grp["tok0"] // tm
    yg = grp["yg"]
    in_specs = [
        pl.BlockSpec((None, tm, d), lambda b, s: (xb0 + b, s, 0)),
        pl.BlockSpec((TOP_K, tm, yg.shape[-1]), lambda b, s: (0, b * nt + s, 0)),
        pl.BlockSpec((tm, 2 * TOP_K), lambda b, s: (tile0 + b * nt + s, 0)),
        pl.BlockSpec((None, 6, d), lambda b, s: (mb0 + b, 0, 0)),
    ]
    args = [x, yg, grp["te"], mod]
    aliases = {}
    if prev is not None:
        in_specs.append(pl.BlockSpec(memory_space=pl.ANY))
        args.append(prev)
        aliases = {len(args) - 1: 0}
    return pl.pallas_call(
        _combine_kernel,
        grid=(nb, nt),
        in_specs=in_specs,
        out_specs=pl.BlockSpec((None, tm, d), lambda b, s: (mb0 + b, s, 0)),
        out_shape=jax.ShapeDtypeStruct((bsz, seq, d), F32),
        input_output_aliases=aliases,
        compiler_params=pltpu.CompilerParams(dimension_semantics=("parallel", "parallel"),
                                             vmem_limit_bytes=VMEM_LIMIT),
        name="moe_combine",
    )(*args)


def _moe_rows(layer, grp, gt, w_gu, b_gu, w_dn, b_dn):
    h2 = grp["h2"]
    dh = h2.shape[-1]
    n_rows = -(-(gt * TOP_K + N_EXPERTS * (MOE_ROWS - 1)) // MOE_ROWS) * MOE_ROWS
    nb = n_rows // MOE_ROWS
    first_row = jnp.arange(nb, dtype=jnp.int32) * MOE_ROWS
    upto = jnp.arange(LANES)[:, None] <= jnp.arange(LANES)[None, :]
    g_cnt = grp["cnt"][:, 0, :].astype(jnp.int32)
    counts = jnp.sum(g_cnt, axis=0)
    padded = (counts + MOE_ROWS - 1) // MOE_ROWS * MOE_ROWS
    pad_end = jnp.sum(jnp.where(upto, padded[:, None], 0), axis=0)
    pad_start = pad_end - padded
    tile_base = pad_start[None, :] + jnp.cumsum(g_cnt, axis=0) - g_cnt
    dest = _dest_call(grp["tet"], tile_base.astype(F32)[:, :, None], grp["tok0"], gt)
    dests = [dest[kk] for kk in range(TOP_K)]
    block_e = jnp.minimum(jnp.sum(pad_end[None, :N_EXPERTS] <= first_row[:, None], axis=1),
                          N_EXPERTS - 1).astype(jnp.int32)
    valid_end = (pad_start + counts)[block_e]
    block_valid = jnp.clip(valid_end - first_row, 0, MOE_ROWS).astype(jnp.int32)
    owns = counts[:N_EXPERTS] > 0
    expert_pos = jnp.cumsum(owns.astype(jnp.int32)) - 1
    slot = jnp.arange(N_EXPERTS, dtype=jnp.int32)
    used_experts = jnp.sum(jnp.where(owns[None, :] & (expert_pos[None, :] == slot[:, None]),
                                     slot[None, :], 0), axis=1).astype(jnp.int32)
    block_pos = expert_pos[block_e].astype(jnp.int32)
    n_used = jnp.stack([pad_end[N_EXPERTS - 1] // MOE_ROWS, jnp.sum(owns)]).astype(jnp.int32)
    xs = _sc_scatter(h2, dests, n_rows, grp["tok0"])
    y = _expert_call(layer, block_e, block_valid, block_pos, used_experts, n_used, xs, w_gu, b_gu, w_dn, b_dn)
    return _sc_gather(y, dest[:TOP_K].reshape(-1)).reshape(TOP_K, gt, dh)


def kernel(x, c, positions, ada_w, ada_b, norm_mix_g, norm_ffn_g, router_w, router_b, moe_w_gu, moe_b_gu,
           moe_w_dn, moe_b_dn, even_w_in, mla_q_norm_g, mla_w_uq, mla_kv_norm_g, mla_w_ukv, mla_q_head_g,
           mla_k_head_g, s5_a_re, s5_a_im, s5_log_dt, s5_b_re, s5_b_im, s5_c_re, s5_c_im, s5_d, s5_glu_w,
           s5_glu_b, even_w_out, odd_w_in, pool_w, pool_scale, sgu_norm_g, sgu_w, sgu_b, odd_w_out):
    bsz, seq, d = x.shape
    depth = ada_w.shape[0]
    mods = _ada_call(c, ada_w, ada_b).reshape(depth, bsz, 6, d)
    posf = positions.astype(F32).reshape(bsz, seq, 1)
    splits = MOE_SPLITS if bsz % MOE_SPLITS == 0 else 1
    gb = bsz // splits
    gt = gb * seq
    tiles_g = gt // min(ROW_TILE, seq)

    def settle(groups, mod):
        out = None
        for grp in groups:
            out = _combine_call(grp, mod, out, bsz, gb)
        return out

    pending = None
    for layer in range(depth):
        mod = mods[layer]
        i = layer // 2
        g_mix = norm_mix_g[layer].reshape(1, d)
        g_ffn = norm_ffn_g[layer].reshape(1, d)
        rw, rb = _router_pad(router_w[layer], router_b[layer])
        if layer % 2 == 0:
            if pending is not None:
                x = settle(pending, mods[layer - 1])
            prep = _prep_even(even_w_in[i], mla_q_norm_g[i], mla_w_uq[i], mla_kv_norm_g[i], mla_w_ukv[i],
                              mla_q_head_g[i], mla_k_head_g[i])
            q, k, v, u_t = _even_in_call(x, mod, posf, g_mix, prep)
            attn = _attn_call(q, k, v)
            disc = _s5_disc_call(s5_a_re[i], s5_a_im[i], s5_log_dt[i], s5_b_re[i], s5_b_im[i])
            ssm_t = _s5_call(u_t.reshape(seq, bsz, d // 2), disc, s5_c_re[i], s5_c_im[i], s5_d[i],
                             s5_glu_w[i], s5_glu_b[i])
            x_new, h2, te, tet, tile_cnt = _mix_out_call(x, attn, ssm_t.reshape(seq, bsz * (d // 2)), mod,
                                                         even_w_out[i].astype(BF16), g_ffn, rw, rb)
            groups = [dict(x=x_new, x_b0=gi * gb, h2=h2, te=te, tet=tet, tok0=gi * gt, mod_b0=gi * gb,
                           cnt=tile_cnt[gi * tiles_g:(gi + 1) * tiles_g]) for gi in range(splits)]
        else:
            groups = []
            for gi, grp in enumerate(pending):
                x_new, h2, te, tet, tile_cnt = _odd_call(grp, mods[layer - 1], mod, gb, g_mix, odd_w_in[i],
                                                         pool_w[i], pool_scale[i], sgu_norm_g[i], sgu_w[i],
                                                         sgu_b[i], odd_w_out[i], g_ffn, rw, rb)
                groups.append(dict(x=x_new, x_b0=0, h2=h2, te=te, tet=tet, tok0=0, mod_b0=gi * gb,
                                   cnt=tile_cnt))
        for grp in groups:
            grp["yg"] = _moe_rows(layer, grp, gt, moe_w_gu, moe_b_gu, moe_w_dn, moe_b_dn)
        pending = groups
    return settle(pending, mods[depth - 1])
```

```python
import functools
import math

import jax
import jax.numpy as jnp
from jax import lax
from jax.experimental import pallas as pl
from jax.experimental.pallas import tpu as pltpu
from jax.experimental.pallas import tpu_sc as plsc

F32 = jnp.float32
BF16 = jnp.bfloat16
HIGHEST = lax.Precision.HIGHEST

NORM_EPS = 1e-6
MLA_HEADS = 8
QK_NOPE_DIM = 64
QK_ROPE_DIM = 32
QK_HEAD_DIM = QK_NOPE_DIM + QK_ROPE_DIM
V_HEAD_DIM = 64
Q_LORA_RANK = 256
KV_LORA_RANK = 128
ROPE_THETA = 10000.0
POOL_WINDOWS = (2, 4, 8, 16)
SGU_HEADS = 4
SGU_CHUNK = 128
N_EXPERTS = 32
TOP_K = 4
SWIGLU_ALPHA = 1.702
SWIGLU_LIMIT = 7.0

LANES = 128
SUBLANES = 8
HEAD_SLAB = LANES
POOL_HALO = 16
ROW_TILE = 1024
ODD_TILE = 512
ATTN_TILE = 512
ATTN_Q_BLOCKS = 2
ATTN_ROWS = 32
ATTN_HEADS = 4
S5_STEPS = 64
MOE_ROWS = 1024
MOE_PATHS = (128, 256, 512, 768, MOE_ROWS)
MOE_SPLITS = 2
SC_CORES = 2
SC_WORKERS = SC_CORES * 16
SC_CHUNK = 128
ADA_COLS = 1536
VMEM_LIMIT = 56 * 1024 * 1024
NEG_BIG = -1e30


def _sigmoid(v):
    return 1.0 / (1.0 + jnp.exp(-v))


def _gelu(v):
    return 0.5 * v * (1.0 + jnp.tanh(math.sqrt(2.0 / math.pi) * (v + 0.044715 * (v * v * v))))


def _rms(v, width):
    return lax.rsqrt(jnp.sum(v * v, axis=-1, keepdims=True) * (1.0 / width) + NORM_EPS)


def _mod_norm(x, g, sc, sh):
    return x * _rms(x, x.shape[-1]) * (g * (1.0 + sc)) + sh


def _bdot(a, b):
    return jnp.dot(a.astype(BF16), b, preferred_element_type=F32)


def _pack_pairs(v):
    w = v.shape[-1] // 2
    bits = pltpu.bitcast(v.astype(BF16).astype(F32), jnp.uint32)
    return (bits[:, :w] >> 16) | bits[:, w:]


def _unpack_pairs_f32(p):
    lo = pltpu.bitcast(p << 16, F32)
    hi = pltpu.bitcast(p & jnp.uint32(0xFFFF0000), F32)
    return lo, hi


def _unpack_pairs(p):
    lo, hi = _unpack_pairs_f32(p)
    return lo.astype(BF16), hi.astype(BF16)


def _ada_kernel(c_ref, w_ref, b_ref, o_ref):
    c = c_ref[...]
    act = c * _sigmoid(c)
    o_ref[...] = jnp.dot(act, w_ref[...], precision=HIGHEST, preferred_element_type=F32) + b_ref[...]


def _ada_call(c, ada_w, ada_b):
    depth, d, n = ada_w.shape
    bsz = c.shape[0]
    tn = ADA_COLS
    return pl.pallas_call(
        _ada_kernel,
        grid=(depth, n // tn),
        in_specs=[
            pl.BlockSpec((bsz, d), lambda l, j: (0, 0)),
            pl.BlockSpec((None, d, tn), lambda l, j: (l, 0, j)),
            pl.BlockSpec((None, 1, tn), lambda l, j: (l, 0, j)),
        ],
        out_specs=pl.BlockSpec((None, bsz, tn), lambda l, j: (l, 0, j)),
        out_shape=jax.ShapeDtypeStruct((depth, bsz, n), F32),
        compiler_params=pltpu.CompilerParams(dimension_semantics=("parallel", "parallel"),
                                             vmem_limit_bytes=VMEM_LIMIT),
        name="ada_mod",
    )(c, ada_w, ada_b.reshape(depth, 1, n))


_C_Q = 0
_C_KV = Q_LORA_RANK
_C_PE = _C_KV + KV_LORA_RANK
_C_PESW = _C_PE + HEAD_SLAB
_C_U = _C_PESW + HEAD_SLAB


def _even_in_kernel(x_ref, mod_ref, pos_ref, g_ref, win_ref, gq_ref, wq_ref, gkv_ref, wk_ref, wv_ref,
                    tab_ref, q_ref, k_ref, v_ref, u_ref):
    x = x_ref[...]
    h = _mod_norm(x, g_ref[...], mod_ref[1:2, :], mod_ref[0:1, :])
    z = _bdot(h, win_ref[...])
    q_c = z[:, _C_Q:_C_KV]
    kv_c = z[:, _C_KV:_C_PE]
    kpe = z[:, _C_PE:_C_PESW]
    kpe_sw = z[:, _C_PESW:_C_U]
    u_ref[...] = z[:, _C_U:]

    tm = x.shape[0]
    packs = LANES // QK_ROPE_DIM
    qrows = tm // packs
    lane = lax.broadcasted_iota(jnp.int32, (1, LANES), 1)
    packed = jnp.zeros((qrows, LANES), F32)
    for part in range(packs):
        in_part = (lane >= part * QK_ROPE_DIM) & (lane < (part + 1) * QK_ROPE_DIM)
        packed = jnp.where(in_part, pos_ref[part * qrows:(part + 1) * qrows, :], packed)
    ang = packed * tab_ref[5:6, :]
    cs_p = jnp.cos(ang)
    sn_p = jnp.sin(ang)
    on_rope = (lane >= QK_NOPE_DIM) & (lane < QK_HEAD_DIM)
    cs_parts = []
    sn_parts = []
    for part in range(packs):
        shift = (QK_NOPE_DIM - part * QK_ROPE_DIM) % LANES
        cs_r = cs_p if shift == 0 else pltpu.roll(cs_p, shift, axis=1)
        sn_r = sn_p if shift == 0 else pltpu.roll(sn_p, shift, axis=1)
        cs_parts.append(jnp.where(on_rope, cs_r, 1.0))
        sn_parts.append(jnp.where(on_rope, sn_r, 0.0))
    cs = jnp.concatenate(cs_parts, axis=0)
    sn = jnp.concatenate(sn_parts, axis=0)
    gcq = cs * tab_ref[1:2, :]
    gsq = sn * tab_ref[2:3, :]
    gck = cs * tab_ref[3:4, :]
    gsk = sn * tab_ref[4:5, :]

    qn = q_c * _rms(q_c, Q_LORA_RANK) * gq_ref[...]
    qq = _bdot(qn, wq_ref[...])
    kvn = kv_c * _rms(kv_c, KV_LORA_RANK) * gkv_ref[...]
    kk = _bdot(kvn, wk_ref[...])
    v_ref[...] = _bdot(kvn, wv_ref[...]).astype(v_ref.dtype)

    pe_rot = kpe * gck + kpe_sw * gsk
    pe_ss = jnp.sum(kpe * kpe, axis=-1, keepdims=True)
    hw = MLA_HEADS * HEAD_SLAB
    for hd in range(MLA_HEADS):
        lo = hd * HEAD_SLAB
        qr = qq[:, lo:lo + HEAD_SLAB]
        qs = qq[:, hw + lo:hw + lo + HEAD_SLAB]
        rq = _rms(qr, QK_HEAD_DIM)
        q_ref[hd] = (rq * (qr * gcq + qs * gsq)).astype(q_ref.dtype)
        kr = kk[:, lo:lo + HEAD_SLAB]
        rk = lax.rsqrt((jnp.sum(kr * kr, axis=-1, keepdims=True) + pe_ss) * (1.0 / QK_HEAD_DIM) + NORM_EPS)
        k_ref[hd] = (rk * (kr * gck + pe_rot)).astype(k_ref.dtype)


def _even_in_call(x, mod, posf, g, prep):
    bsz, seq, d = x.shape
    tm = min(ROW_TILE, seq)
    full = lambda a: pl.BlockSpec(a.shape, lambda b, s: (0,) * a.ndim)
    return pl.pallas_call(
        _even_in_kernel,
        grid=(bsz, seq // tm),
        in_specs=[
            pl.BlockSpec((None, tm, d), lambda b, s: (b, s, 0)),
            pl.BlockSpec((None, 6, d), lambda b, s: (b, 0, 0)),
            pl.BlockSpec((None, tm, 1), lambda b, s: (b, s, 0)),
            full(g), full(prep["w_in"]), full(prep["gq"]), full(prep["wq"]), full(prep["gkv"]),
            full(prep["wk"]), full(prep["wv"]), full(prep["tab"]),
        ],
        out_specs=[
            pl.BlockSpec((None, MLA_HEADS, tm, HEAD_SLAB), lambda b, s: (b, 0, s, 0)),
            pl.BlockSpec((None, MLA_HEADS, tm, HEAD_SLAB), lambda b, s: (b, 0, s, 0)),
            pl.BlockSpec((None, tm, MLA_HEADS * V_HEAD_DIM), lambda b, s: (b, s, 0)),
            pl.BlockSpec((tm, d // 2), lambda b, s: (s, b)),
        ],
        out_shape=[
            jax.ShapeDtypeStruct((bsz, MLA_HEADS, seq, HEAD_SLAB), BF16),
            jax.ShapeDtypeStruct((bsz, MLA_HEADS, seq, HEAD_SLAB), BF16),
            jax.ShapeDtypeStruct((bsz, seq, MLA_HEADS * V_HEAD_DIM), BF16),
            jax.ShapeDtypeStruct((seq, bsz * (d // 2)), F32),
        ],
        compiler_params=pltpu.CompilerParams(dimension_semantics=("parallel", "parallel"),
                                             vmem_limit_bytes=VMEM_LIMIT),
        name="even_in",
    )(x, mod, posf, g, prep["w_in"], prep["gq"], prep["wq"], prep["gkv"], prep["wk"], prep["wv"], prep["tab"])


def _prep_even(even_w_in, q_norm_g, w_uq, kv_norm_g, w_ukv, q_head_g, k_head_g):
    d = even_w_in.shape[0]
    half = QK_ROPE_DIM // 2
    nope = QK_NOPE_DIM
    c_pe = Q_LORA_RANK + KV_LORA_RANK
    w_pe = even_w_in[:, c_pe:c_pe + QK_ROPE_DIM]
    zeros = lambda n: jnp.zeros((d, n), F32)
    pe_slab = jnp.concatenate([zeros(nope), w_pe, zeros(HEAD_SLAB - QK_HEAD_DIM)], axis=1)
    pe_sw = jnp.concatenate([zeros(nope), -w_pe[:, half:], w_pe[:, :half], zeros(HEAD_SLAB - QK_HEAD_DIM)], axis=1)
    w_in = jnp.concatenate([even_w_in[:, :c_pe], pe_slab, pe_sw, even_w_in[:, c_pe + QK_ROPE_DIM:]], axis=1)

    r = w_uq.shape[0]
    padq = jnp.zeros((r, MLA_HEADS, HEAD_SLAB - QK_HEAD_DIM), F32)
    wq_plain = jnp.concatenate([w_uq, padq], axis=2).reshape(r, MLA_HEADS * HEAD_SLAB)
    wq_sw = jnp.concatenate([jnp.zeros((r, MLA_HEADS, nope), F32), -w_uq[:, :, nope + half:],
                             w_uq[:, :, nope:nope + half], padq], axis=2).reshape(r, MLA_HEADS * HEAD_SLAB)
    wq = jnp.concatenate([wq_plain, wq_sw], axis=1)

    rk = w_ukv.shape[0]
    wk = jnp.concatenate([w_ukv[:, :, :nope], jnp.zeros((rk, MLA_HEADS, HEAD_SLAB - nope), F32)],
                         axis=2).reshape(rk, MLA_HEADS * HEAD_SLAB)
    wv = w_ukv[:, :, nope:].reshape(rk, MLA_HEADS * V_HEAD_DIM)

    inv_freq = 1.0 / (ROPE_THETA ** (jnp.arange(half, dtype=F32) / half))
    pad_tail = jnp.zeros((HEAD_SLAB - QK_HEAD_DIM,), F32)
    freq_row = jnp.concatenate([jnp.zeros((nope,), F32), inv_freq, inv_freq, pad_tail])

    def gain_rows(gv, scale):
        plain = jnp.concatenate([gv, pad_tail]) * scale
        swapped = jnp.concatenate([jnp.zeros((nope,), F32), gv[nope + half:], gv[nope:nope + half], pad_tail]) * scale
        return plain, swapped

    gq_plain, gq_sw = gain_rows(q_head_g, QK_HEAD_DIM ** -0.5 * math.log2(math.e))
    gk_plain, gk_sw = gain_rows(k_head_g, 1.0)
    freq_packed = jnp.tile(jnp.concatenate([inv_freq, inv_freq]), LANES // QK_ROPE_DIM)
    tab = jnp.stack([freq_row, gq_plain, gq_sw, gk_plain, gk_sw, freq_packed, freq_row * 0, freq_row * 0])
    return {
        "w_in": w_in.astype(BF16), "gq": q_norm_g.reshape(1, -1), "wq": wq.astype(BF16),
        "gkv": kv_norm_g.reshape(1, -1), "wk": wk.astype(BF16), "wv": wv.astype(BF16), "tab": tab,
    }


def _attn_kernel(qi_ref, kj_ref, q_ref, k_ref, v_ref, o_ref, m_sc, a_sc, acc_sc, s_sc, p_sc, *, tq, tk):
    step = pl.program_id(2)
    i = qi_ref[step]
    j = kj_ref[step]
    sum_lane = (V_HEAD_DIM, 0)
    subs = tq // tk

    @pl.when(j == 0)
    def _():
        m_sc[...] = jnp.full(m_sc.shape, -jnp.inf, F32)
        acc_sc[...] = jnp.zeros(acc_sc.shape, F32)

    def sweep(diag_sub):
        lo = 0 if diag_sub is None else diag_sub * tk
        live = pl.ds(lo, tq - lo)
        lane = lax.broadcasted_iota(jnp.int32, (1, LANES), 1)
        for hh in range(ATTN_HEADS):
            s_sc[hh, live, :] = lax.dot_general(q_ref[hh, live, :], k_ref[hh], (((1,), (1,)), ((), ())),
                                                preferred_element_type=F32)
        for hh in range(ATTN_HEADS):
            v = v_ref[:, (hh // 2) * LANES:(hh // 2 + 1) * LANES]
            for r0 in range(lo, tq, ATTN_ROWS):
                rows = pl.ds(r0, ATTN_ROWS)
                s = s_sc[hh, rows, :]
                if diag_sub is not None and r0 < lo + tk:
                    row = (r0 - lo) + lax.broadcasted_iota(jnp.int32, (ATTN_ROWS, tk), 0)
                    col = lax.broadcasted_iota(jnp.int32, (ATTN_ROWS, tk), 1)
                    s = jnp.where(col <= row, s, -jnp.inf)
                m_prev = m_sc[hh, rows, :]
                m_new = jnp.maximum(m_prev, jnp.max(s, axis=-1, keepdims=True))
                a_sc[hh, rows, :] = jnp.exp2(m_prev - m_new)
                m_sc[hh, rows, :] = m_new
                shifted = s - jnp.concatenate([m_new] * (tk // LANES), axis=1)
                p_sc[hh, rows, :] = jnp.exp2(shifted.astype(BF16))
            own = (lane < V_HEAD_DIM) == (hh % 2 == 0)
            ones = jnp.where(lane == sum_lane[hh % 2], 1.0, 0.0).astype(v.dtype)
            vh = jnp.where(own, v, jnp.broadcast_to(ones, v.shape))
            acc_sc[hh, live, :] = (acc_sc[hh, live, :] * a_sc[hh, live, :]
                                   + jnp.dot(p_sc[hh, live, :], vh, preferred_element_type=F32))

    diag = j - subs * i

    @pl.when(diag < 0)
    def _():
        sweep(None)

    for ds_ in range(subs):
        @pl.when(diag == ds_)
        def _(ds_=ds_):
            sweep(ds_)

    @pl.when(diag == subs - 1)
    def _():
        lane = lax.broadcasted_iota(jnp.int32, (1, LANES), 1)
        for pp in range(ATTN_HEADS // 2):
            acc0 = acc_sc[2 * pp]
            acc1 = acc_sc[2 * pp + 1]
            l0 = acc0[:, sum_lane[0]:sum_lane[0] + 1]
            l1 = acc1[:, sum_lane[1]:sum_lane[1] + 1]
            o_ref[:, pp * LANES:(pp + 1) * LANES] = jnp.where(lane < V_HEAD_DIM, acc0 / l0,
                                                              acc1 / l1).astype(o_ref.dtype)


def _attn_call(q, k, v):
    bsz, nh, seq, _ = q.shape
    tk = min(ATTN_TILE, seq)
    tq = min(ATTN_Q_BLOCKS * tk, seq)
    subs = tq // tk
    nq = seq // tq
    pairs = [(i, j) for i in range(nq) for j in range(subs * (i + 1))]
    qi = jnp.asarray([p[0] for p in pairs], jnp.int32)
    kj = jnp.asarray([p[1] for p in pairs], jnp.int32)
    kern = functools.partial(_attn_kernel, tq=tq, tk=tk)
    hp = ATTN_HEADS
    assert nh % hp == 0
    return pl.pallas_call(
        kern,
        grid_spec=pltpu.PrefetchScalarGridSpec(
            num_scalar_prefetch=2,
            grid=(bsz, nh // hp, len(pairs)),
            in_specs=[
                pl.BlockSpec((None, hp, tq, HEAD_SLAB), lambda b, h, p, qi, kj: (b, h, qi[p], 0)),
                pl.BlockSpec((None, hp, tk, HEAD_SLAB), lambda b, h, p, qi, kj: (b, h, kj[p], 0)),
                pl.BlockSpec((None, tk, hp * V_HEAD_DIM), lambda b, h, p, qi, kj: (b, kj[p], h)),
            ],
            out_specs=pl.BlockSpec((None, tq, hp * V_HEAD_DIM), lambda b, h, p, qi, kj: (b, qi[p], h)),
            scratch_shapes=[pltpu.VMEM((hp, tq, LANES), F32), pltpu.VMEM((hp, tq, LANES), F32),
                            pltpu.VMEM((hp, tq, LANES), F32),
                            pltpu.VMEM((hp, tq, tk), F32), pltpu.VMEM((hp, tq, tk), BF16)],
        ),
        out_shape=jax.ShapeDtypeStruct((bsz, seq, nh * V_HEAD_DIM), BF16),
        compiler_params=pltpu.CompilerParams(
            dimension_semantics=("parallel", "parallel", "arbitrary"),
            vmem_limit_bytes=VMEM_LIMIT),
        name="mla_attention",
    )(qi, kj, q, k, v)


def _s5_disc_kernel(are_ref, aim_ref, ldt_ref, bre_ref, bim_ref, abre_ref, abim_ref, bbre_ref, bbim_ref):
    dt = jnp.exp(ldt_ref[...])
    lam_re = jnp.minimum(are_ref[...], -1e-4)
    lam_im = aim_ref[...]
    mag = jnp.exp(lam_re * dt)
    ab_re = mag * jnp.cos(lam_im * dt)
    ab_im = mag * jnp.sin(lam_im * dt)
    den = lam_re * lam_re + lam_im * lam_im
    num_re = ab_re - 1.0
    f_re = (num_re * lam_re + ab_im * lam_im) / den
    f_im = (ab_im * lam_re - num_re * lam_im) / den
    abre_ref[...] = ab_re
    abim_ref[...] = ab_im
    br = bre_ref[...]
    bi = bim_ref[...]
    bbre_ref[...] = f_re[:, None, :] * br - f_im[:, None, :] * bi
    bbim_ref[...] = f_re[:, None, :] * bi + f_im[:, None, :] * br


def _s5_disc_call(a_re, a_im, log_dt, b_re, b_im):
    g, p = a_re.shape
    bre_t = jnp.swapaxes(b_re, 1, 2)
    bim_t = jnp.swapaxes(b_im, 1, 2)
    return pl.pallas_call(
        _s5_disc_kernel,
        out_shape=[jax.ShapeDtypeStruct((g, p), F32), jax.ShapeDtypeStruct((g, p), F32),
                   jax.ShapeDtypeStruct(bre_t.shape, F32), jax.ShapeDtypeStruct(bre_t.shape, F32)],
        name="s5_discretize",
    )(a_re, a_im, log_dt.reshape(g, 1), bre_t, bim_t)


def _block_diag_halves(m):
    g, r, c = m.shape
    gh = g // 2
    eye = jnp.eye(gh, dtype=m.dtype)
    mh = m.reshape(2, gh, r, c)
    return (mh[:, :, :, None, :] * eye[None, :, None, :, None]).reshape(2, gh * r, gh * c)


def _s5_kernel(u_ref, bre_ref, bim_ref, are_ref, aim_ref, cre_ref, cim_ref, d_ref, gw_ref, gb_ref,
               o_ref, sre, sim, dre, dim, xbr, xbi, *, steps):
    @pl.when(pl.program_id(0) == 0)
    def _():
        sre[...] = jnp.zeros(sre.shape, F32)
        sim[...] = jnp.zeros(sim.shape, F32)

    rows = steps * SUBLANES
    w = u_ref.shape[-1]
    u = u_ref[...].reshape(rows, w)
    ub = u.astype(BF16)
    kh = w // 2
    nh = dre.shape[1] // 2
    for hf in range(2):
        dre[:, hf * nh:(hf + 1) * nh] = jnp.dot(ub[:, hf * kh:(hf + 1) * kh], bre_ref[hf], preferred_element_type=F32)
        dim[:, hf * nh:(hf + 1) * nh] = jnp.dot(ub[:, hf * kh:(hf + 1) * kh], bim_ref[hf], preferred_element_type=F32)

    xr = sre[...]
    xi = sim[...]
    for t in range(0, steps, 2):
        pair_r = []
        pair_i = []
        for r0 in (t * SUBLANES, (t + 1) * SUBLANES):
            nr = are_ref[...] * xr - aim_ref[...] * xi + dre[r0:r0 + SUBLANES, :]
            ni = are_ref[...] * xi + aim_ref[...] * xr + dim[r0:r0 + SUBLANES, :]
            xr, xi = nr, ni
            pair_r.append(nr)
            pair_i.append(ni)
        xbr[t * SUBLANES:(t + 2) * SUBLANES, :] = jnp.concatenate(pair_r, axis=0).astype(BF16)
        xbi[t * SUBLANES:(t + 2) * SUBLANES, :] = jnp.concatenate(pair_i, axis=0).astype(BF16)
    sre[...] = xr
    sim[...] = xi

    ys = []
    for hf in range(2):
        yr = jnp.dot(xbr[:, hf * nh:(hf + 1) * nh], cre_ref[hf], preferred_element_type=F32)
        yi = jnp.dot(xbi[:, hf * nh:(hf + 1) * nh], cim_ref[hf], preferred_element_type=F32)
        ys.append(yr - yi)
    y = jnp.concatenate(ys, axis=1) + d_ref[...] * u
    g = _gelu(y)
    out = g * _sigmoid(_bdot(g, gw_ref[...]) + gb_ref[...])
    o_ref[...] = out.reshape(steps, SUBLANES, w).astype(o_ref.dtype)


def _s5_call(u_t, disc, c_re, c_im, d_skip, glu_w, glu_b):
    seq, bsz, w = u_t.shape
    assert bsz == SUBLANES
    ab_re, ab_im, bb_re, bb_im = disc
    g, p = ab_re.shape
    n_state = g * p
    bre = _block_diag_halves(bb_re).astype(BF16)
    bim = _block_diag_halves(bb_im).astype(BF16)
    cre = _block_diag_halves(jnp.swapaxes(c_re, 1, 2)).astype(BF16)
    cim = _block_diag_halves(jnp.swapaxes(c_im, 1, 2)).astype(BF16)
    steps = min(S5_STEPS, seq)
    full = lambda a: pl.BlockSpec(a.shape, lambda s: (0,) * a.ndim)
    rep = lambda a: jnp.broadcast_to(a.reshape(1, n_state), (bsz, n_state))
    args = (bre, bim, rep(ab_re), rep(ab_im), cre, cim,
            d_skip.reshape(1, w), glu_w.astype(BF16), glu_b.reshape(1, w))
    return pl.pallas_call(
        functools.partial(_s5_kernel, steps=steps),
        grid=(seq // steps,),
        in_specs=[pl.BlockSpec((steps, bsz, w), lambda s: (s, 0, 0))] + [full(a) for a in args],
        out_specs=pl.BlockSpec((steps, bsz, w), lambda s: (s, 0, 0)),
        out_shape=jax.ShapeDtypeStruct((seq, bsz, w), BF16),
        scratch_shapes=[pltpu.VMEM((bsz, n_state), F32), pltpu.VMEM((bsz, n_state), F32),
                        pltpu.VMEM((steps * bsz, n_state), F32), pltpu.VMEM((steps * bsz, n_state), F32),
                        pltpu.VMEM((steps * bsz, n_state), BF16), pltpu.VMEM((steps * bsz, n_state), BF16)],
        compiler_params=pltpu.CompilerParams(dimension_semantics=("arbitrary",), vmem_limit_bytes=VMEM_LIMIT),
        name="s5_scan",
    )(u_t, *args)


def _router_tail(x_new, mod_ref, gf_ref, rw_ref, rb_ref, h2_ref, te_ref, tet_ref, cnt_ref):
    h2 = _mod_norm(x_new, gf_ref[...], mod_ref[4:5, :], mod_ref[3:4, :])
    h2_ref[...] = _pack_pairs(h2)
    h_hi = h2.astype(BF16)
    h_lo = (h2 - h_hi.astype(F32)).astype(BF16)
    r_hi = jnp.dot(h_hi, rw_ref[...], preferred_element_type=F32)
    r_lo = jnp.dot(h_lo, rw_ref[...], preferred_element_type=F32)
    logits = r_hi[:, :LANES] + r_hi[:, LANES:] + r_lo[:, :LANES] + rb_ref[...]
    lane = lax.broadcasted_iota(jnp.int32, logits.shape, 1).astype(F32)
    vals = []
    idxs = []
    work = logits
    for _ in range(TOP_K):
        m = jnp.max(work, axis=-1, keepdims=True)
        idx = jnp.min(jnp.where(work == m, lane, float(LANES)), axis=-1, keepdims=True)
        vals.append(m)
        idxs.append(idx)
        work = jnp.where(lane == idx, NEG_BIG * 2.0, work)
    exps = [jnp.exp(vv - vals[0]) for vv in vals]
    tot = exps[0] + exps[1] + exps[2] + exps[3]
    te = jnp.zeros(logits.shape, F32)
    picked = jnp.zeros(logits.shape, F32)
    for kk in range(TOP_K):
        te = jnp.where(lane == float(kk), idxs[kk], te)
        te = jnp.where(lane == float(TOP_K + kk), exps[kk] / tot, te)
        picked = picked + jnp.where(lane == idxs[kk], 1.0, 0.0)
    te_ref[...] = te[:, :2 * TOP_K]
    tet_ref[...] = te.T[:2 * TOP_K, :]
    cnt_ref[...] = jnp.sum(picked, axis=0, keepdims=True)


def _mix_out_kernel(x_ref, a_ref, s_ref, mod_ref, wo_ref, gf_ref, rw_ref, rb_ref, xo_ref, h2_ref, te_ref,
                    tet_ref, cnt_ref):
    ka = a_ref.shape[-1]
    mix = jnp.dot(a_ref[...], wo_ref[:ka, :], preferred_element_type=F32)
    mix = mix + jnp.dot(s_ref[...], wo_ref[ka:, :], preferred_element_type=F32)
    x_new = x_ref[...] + mod_ref[2:3, :] * mix
    xo_ref[...] = x_new
    _router_tail(x_new, mod_ref, gf_ref, rw_ref, rb_ref, h2_ref, te_ref, tet_ref, cnt_ref)


def _tail_out_specs(bsz, seq, tm, d):
    nt = seq // tm
    specs = [
        pl.BlockSpec((None, tm, d), lambda b, s: (b, s, 0)),
        pl.BlockSpec((tm, d // 2), lambda b, s: (b * nt + s, 0)),
        pl.BlockSpec((tm, 2 * TOP_K), lambda b, s: (b * nt + s, 0)),
        pl.BlockSpec((2 * TOP_K, tm), lambda b, s: (0, b * nt + s)),
        pl.BlockSpec((None, 1, LANES), lambda b, s: (b * nt + s, 0, 0)),
    ]
    shapes = [
        jax.ShapeDtypeStruct((bsz, seq, d), F32),
        jax.ShapeDtypeStruct((bsz * seq, d // 2), jnp.uint32),
        jax.ShapeDtypeStruct((bsz * seq, 2 * TOP_K), F32),
        jax.ShapeDtypeStruct((2 * TOP_K, bsz * seq), F32),
        jax.ShapeDtypeStruct((bsz * nt, 1, LANES), F32),
    ]
    return specs, shapes


def _router_pad(router_w, router_b):
    d, e = router_w.shape
    rw = jnp.concatenate([router_w, jnp.zeros((d, LANES - e), F32)], axis=1)
    rw_hi = rw.astype(BF16)
    rw_lo = (rw - rw_hi.astype(F32)).astype(BF16)
    rb = jnp.concatenate([router_b, jnp.full((LANES - e,), NEG_BIG, F32)]).reshape(1, LANES)
    return jnp.concatenate([rw_hi, rw_lo], axis=1), rb


def _mix_out_call(x, attn, ssm_t, mod, w_out, gf, rw, rb):
    bsz, seq, d = x.shape
    tm = min(ROW_TILE, seq)
    ka = attn.shape[-1]
    ks = ssm_t.shape[-1] // bsz
    full = lambda a: pl.BlockSpec(a.shape, lambda b, s: (0,) * a.ndim)
    out_specs, out_shape = _tail_out_specs(bsz, seq, tm, d)
    return pl.pallas_call(
        _mix_out_kernel,
        grid=(bsz, seq // tm),
        in_specs=[
            pl.BlockSpec((None, tm, d), lambda b, s: (b, s, 0)),
            pl.BlockSpec((None, tm, ka), lambda b, s: (b, s, 0)),
            pl.BlockSpec((tm, ks), lambda b, s: (s, b)),
            pl.BlockSpec((None, 6, d), lambda b, s: (b, 0, 0)),
            full(w_out), full(gf), full(rw), full(rb),
        ],
        out_specs=out_specs,
        out_shape=out_shape,
        compiler_params=pltpu.CompilerParams(dimension_semantics=("parallel", "parallel"),
                                             vmem_limit_bytes=VMEM_LIMIT),
        name="even_out",
    )(x, attn, ssm_t, mod, w_out, gf, rw, rb)


def _moe_sum(yg_ref, te_ref):
    te = te_ref[...]
    acc_lo = acc_hi = None
    for kk in range(TOP_K):
        lo, hi = _unpack_pairs_f32(yg_ref[kk])
        gate = te[:, TOP_K + kk:TOP_K + kk + 1]
        acc_lo = gate * lo if kk == 0 else acc_lo + gate * lo
        acc_hi = gate * hi if kk == 0 else acc_hi + gate * hi
    return jnp.concatenate([acc_lo, acc_hi], axis=1)


def _odd_kernel(x_ref, yg_ref, tep_ref, modp_ref, mod_ref, g_ref, win_ref, icnt_ref, wp_ref, ps_ref, gv_ref,
                wsp_ref, bsp_ref, wo_ref, gf_ref, rw_ref, rb_ref, xo_ref, h2_ref, te_ref, tet_ref, cnt_ref, ext_sc):
    tm = x_ref.shape[0]
    pw = wp_ref.shape[-1]
    width = pw * len(POOL_WINDOWS)

    @pl.when(pl.program_id(1) == 0)
    def _():
        ext_sc[0:POOL_HALO, :] = jnp.zeros((POOL_HALO, width), F32)

    x = x_ref[...] + modp_ref[5:6, :] * _moe_sum(yg_ref, tep_ref)
    h = _mod_norm(x, g_ref[...], mod_ref[1:2, :], mod_ref[0:1, :])
    z = _bdot(h, win_ref[...])
    up = z[:, :width]
    ext_sc[POOL_HALO:POOL_HALO + tm, :] = up

    pooled = []
    for gi, win in enumerate(POOL_WINDOWS):
        cols = slice(gi * pw, (gi + 1) * pw)
        acc = up[:, cols]
        for lag in range(1, win):
            acc = acc + ext_sc[POOL_HALO - lag:POOL_HALO - lag + tm, cols]
        pg = acc * icnt_ref[:, gi:gi + 1] - up[:, cols]
        pooled.append(_bdot(pg, wp_ref[gi]) * ps_ref[:, cols])
    ext_sc[0:POOL_HALO, :] = ext_sc[tm:tm + POOL_HALO, :]
    pooled = jnp.concatenate(pooled, axis=1)

    ug = _gelu(z[:, width:2 * width])
    vg = _gelu(z[:, 2 * width:])
    vn = (vg * _rms(vg, width) * gv_ref[...]).astype(BF16)
    hd = width // SGU_HEADS
    chunks = []
    for ci in range(tm // SGU_CHUNK):
        heads = []
        for hh in range(SGU_HEADS):
            blk = vn[ci * SGU_CHUNK:(ci + 1) * SGU_CHUNK, hh * hd:(hh + 1) * hd]
            heads.append(jnp.dot(wsp_ref[hh], blk, preferred_element_type=F32) + bsp_ref[hh])
        chunks.append(jnp.concatenate(heads, axis=1))
    gated = ug * jnp.concatenate(chunks, axis=0)

    mix = _bdot(pooled, wo_ref[:width, :]) + _bdot(gated, wo_ref[width:, :])
    x_new = x + mod_ref[2:3, :] * mix
    xo_ref[...] = x_new
    _router_tail(x_new, mod_ref, gf_ref, rw_ref, rb_ref, h2_ref, te_ref, tet_ref, cnt_ref)


def _odd_call(grp, mod_prev, mod, nb, g, w_in, pool_w, pool_scale, sgu_norm_g, sgu_w, sgu_b, w_out, gf, rw, rb):
    x = grp["x"]
    _, seq, d = x.shape
    tm = min(ODD_TILE, seq)
    nt = seq // tm
    xb0, tok0, mb0 = grp["x_b0"], grp["tok0"], grp["mod_b0"]
    tile0 = tok0 // tm
    yg = grp["yg"]
    width = pool_scale.shape[0]
    hd = width // SGU_HEADS
    t = jnp.arange(seq, dtype=jnp.int32)
    icnt = jnp.stack([1.0 / jnp.minimum(t + 1, wn).astype(F32) for wn in POOL_WINDOWS], axis=1)
    wsp = jnp.tril(sgu_w).astype(BF16)
    bsp = jnp.broadcast_to(sgu_b[:, :, None], (SGU_HEADS, SGU_CHUNK, hd))
    args = (g, w_in.astype(BF16), icnt, pool_w.astype(BF16), pool_scale.reshape(1, width),
            sgu_norm_g.reshape(1, width), wsp, bsp, w_out.astype(BF16), gf, rw, rb)
    full = lambda a: pl.BlockSpec(a.shape, lambda b, s: (0,) * a.ndim)
    in_specs = [pl.BlockSpec((None, tm, d), lambda b, s: (xb0 + b, s, 0)),
                pl.BlockSpec((TOP_K, tm, yg.shape[-1]), lambda b, s: (0, b * nt + s, 0)),
                pl.BlockSpec((tm, 2 * TOP_K), lambda b, s: (tile0 + b * nt + s, 0)),
                pl.BlockSpec((None, 6, d), lambda b, s: (mb0 + b, 0, 0)),
                pl.BlockSpec((None, 6, d), lambda b, s: (mb0 + b, 0, 0))]
    for idx, a in enumerate(args):
        in_specs.append(pl.BlockSpec((tm, len(POOL_WINDOWS)), lambda b, s: (s, 0)) if idx == 2 else full(a))
    out_specs, out_shape = _tail_out_specs(nb, seq, tm, d)
    return pl.pallas_call(
        _odd_kernel,
        grid=(nb, nt),
        in_specs=in_specs,
        out_specs=out_specs,
        out_shape=out_shape,
        scratch_shapes=[pltpu.VMEM((tm + POOL_HALO, width), F32)],
        compiler_params=pltpu.CompilerParams(dimension_semantics=("parallel", "arbitrary"),
                                             vmem_limit_bytes=VMEM_LIMIT),
        name="odd_mixer",
    )(x, yg, grp["te"], mod_prev, mod, *args)


def _dest_kernel(tet_ref, base_ref, tri_ref, dst_ref):
    tr = tet_ref.shape[1]
    tet = tet_ref[...]
    expert = lax.broadcasted_iota(jnp.int32, (LANES, tr), 0).astype(F32)
    hots = [tet[kk:kk + 1, :] == expert for kk in range(TOP_K)]
    oh = jnp.zeros((LANES, tr), F32)
    for hot in hots:
        oh = oh + jnp.where(hot, 1.0, 0.0)
    before = jnp.dot(oh.astype(BF16), tri_ref[...], preferred_element_type=F32) + base_ref[...]
    row = lax.broadcasted_iota(jnp.int32, (2 * TOP_K, tr), 0)
    dst = jnp.zeros((2 * TOP_K, tr), F32)
    for kk, hot in enumerate(hots):
        dst = jnp.where(row == kk, jnp.sum(jnp.where(hot, before, 0.0), axis=0, keepdims=True), dst)
    dst_ref[...] = dst.astype(jnp.int32)


def _dest_call(tet, base, tok0, n_tok):
    tiles = base.shape[0]
    tr = n_tok // tiles
    tile0 = tok0 // tr
    tri = (jnp.arange(tr)[:, None] < jnp.arange(tr)[None, :]).astype(BF16)
    return pl.pallas_call(
        _dest_kernel,
        grid=(tiles,),
        in_specs=[pl.BlockSpec((2 * TOP_K, tr), lambda i: (0, tile0 + i)),
                  pl.BlockSpec((None, LANES, 1), lambda i: (i, 0, 0)),
                  pl.BlockSpec((tr, tr), lambda i: (0, 0))],
        out_specs=pl.BlockSpec((2 * TOP_K, tr), lambda i: (0, i)),
        out_shape=jax.ShapeDtypeStruct((2 * TOP_K, n_tok), jnp.int32),
        compiler_params=pltpu.CompilerParams(dimension_semantics=("parallel",)),
        name="route_dest",
    )(tet, base, tri)


def _sc_gather(table, idx):
    n = idx.shape[0]
    per_w = n // SC_WORKERS
    assert per_w * SC_WORKERS == n and per_w % SC_CHUNK == 0
    n_chunks = per_w // SC_CHUNK
    row_shape = table.shape[1:]
    mesh = plsc.VectorSubcoreMesh(core_axis_name="c", subcore_axis_name="s")

    @functools.partial(
        pl.kernel, mesh=mesh,
        out_type=jax.ShapeDtypeStruct((n,) + row_shape, table.dtype),
        scratch_types=[pltpu.VMEM((SC_CHUNK,), jnp.int32), pltpu.VMEM((SC_CHUNK,) + row_shape, table.dtype),
                       pltpu.SemaphoreType.DMA],
        name="sc_row_gather",
    )
    def gather(table_hbm, idx_hbm, out_hbm, idx_v, rows_v, sem):
        wid = lax.axis_index("s") * SC_CORES + lax.axis_index("c")
        base = wid * per_w

        @pl.loop(0, n_chunks)
        def _(ci):
            off = pl.multiple_of(base + ci * SC_CHUNK, SC_CHUNK)
            pltpu.sync_copy(idx_hbm.at[pl.ds(off, SC_CHUNK)], idx_v)
            pltpu.async_copy(table_hbm.at[idx_v], rows_v, sem).wait()
            pltpu.sync_copy(rows_v, out_hbm.at[pl.ds(off, SC_CHUNK)])

    return gather(table, idx)


def _sc_scatter(rows, dests, n_out, tok0):
    t = dests[0].shape[0]
    per_w = t // SC_WORKERS
    assert per_w * SC_WORKERS == t and per_w % SC_CHUNK == 0
    n_chunks = per_w // SC_CHUNK
    row_shape = rows.shape[1:]
    nk = len(dests)
    mesh = plsc.VectorSubcoreMesh(core_axis_name="c", subcore_axis_name="s")

    @functools.partial(
        pl.kernel, mesh=mesh,
        out_type=jax.ShapeDtypeStruct((n_out,) + row_shape, rows.dtype),
        scratch_types=[pltpu.VMEM((SC_CHUNK,), jnp.int32)] * nk
        + [pltpu.VMEM((SC_CHUNK,) + row_shape, rows.dtype), pltpu.SemaphoreType.DMA],
        name="sc_row_scatter",
    )
    def scatter(rows_hbm, *rest):
        dest_hbm = rest[:nk]
        out_hbm = rest[nk]
        idx_v = rest[nk + 1:2 * nk + 1]
        rows_v, sem = rest[2 * nk + 1:]
        wid = lax.axis_index("s") * SC_CORES + lax.axis_index("c")
        base = wid * per_w

        @pl.loop(0, n_chunks)
        def _(ci):
            off = pl.multiple_of(base + ci * SC_CHUNK, SC_CHUNK)
            src = pl.multiple_of(tok0 + off, SC_CHUNK)
            pltpu.sync_copy(rows_hbm.at[pl.ds(src, SC_CHUNK)], rows_v)
            for kk in range(nk):
                pltpu.sync_copy(dest_hbm[kk].at[pl.ds(off, SC_CHUNK)], idx_v[kk])
            copies = [pltpu.async_copy(rows_v, out_hbm.at[idx_v[kk]], sem) for kk in range(nk)]
            for cp in copies:
                cp.wait()

    return scatter(rows, *dests)


def _expert_kernel(be_ref, nv_ref, ord_ref, ue_ref, nu_ref, x_ref, wgu_hbm, bgu_ref, wdn_hbm, bdn_ref, y_ref,
                   wgu_f32, wdn_f32, wgu_bf, wdn_bf, sem, *, layer):
    i = pl.program_id(0)
    used = i < nu_ref[0]
    pos = ord_ref[i]
    fresh = jnp.logical_or(i == 0, ord_ref[jnp.maximum(i - 1, 0)] != pos)

    def weight_copies(expert):
        return (pltpu.make_async_copy(wgu_hbm.at[layer, expert], wgu_f32, sem.at[0]),
                pltpu.make_async_copy(wdn_hbm.at[layer, expert], wdn_f32, sem.at[1]))

    @pl.when(i == 0)
    def _():
        for cp in weight_copies(ue_ref[0]):
            cp.start()

    @pl.when(jnp.logical_and(used, fresh))
    def _():
        for cp in weight_copies(ue_ref[pos]):
            cp.wait()
        wgu_bf[...] = wgu_f32[...].astype(BF16)
        wdn_bf[...] = wdn_f32[...].astype(BF16)

        @pl.when(pos + 1 < nu_ref[1])
        def _():
            for cp in weight_copies(ue_ref[pos + 1]):
                cp.start()

    def ffn(rows):
        x = jnp.concatenate(_unpack_pairs(x_ref[0:rows, :]), axis=1)
        z = jnp.dot(x, wgu_bf[...], preferred_element_type=F32) + bgu_ref[...]
        ff = z.shape[-1] // 2
        gate = jnp.minimum(z[:, :ff], SWIGLU_LIMIT)
        lin = jnp.clip(z[:, ff:], -SWIGLU_LIMIT, SWIGLU_LIMIT)
        act = gate * _sigmoid(SWIGLU_ALPHA * gate) * (lin + 1.0)
        y = _bdot(act, wdn_bf[...]) + bdn_ref[...]
        y_ref[0:rows, :] = _pack_pairs(y)

    nv = nv_ref[i]
    below = 0
    for size in MOE_PATHS:
        @pl.when(jnp.logical_and(used, jnp.logical_and(nv > below, nv <= size)))
        def _(size=size):
            ffn(size)
        below = size


def _expert_call(layer, block_e, block_valid, block_pos, used_experts, n_used, xs, w_gu, b_gu, w_dn, b_dn):
    n_rows, dh = xs.shape
    depth, e, d, ff2 = w_gu.shape
    ff = ff2 // 2
    nb = n_rows // MOE_ROWS
    row_map = lambda i, be, nv, po, ue, nu: (jnp.minimum(i, nu[0] - 1), 0)
    b_map = lambda i, be, nv, po, ue, nu: (layer, be[i], 0, 0)
    return pl.pallas_call(
        functools.partial(_expert_kernel, layer=layer),
        grid_spec=pltpu.PrefetchScalarGridSpec(
            num_scalar_prefetch=5,
            grid=(nb,),
            in_specs=[
                pl.BlockSpec((MOE_ROWS, dh), row_map),
                pl.BlockSpec(memory_space=pl.ANY),
                pl.BlockSpec((None, None, 1, ff2), b_map),
                pl.BlockSpec(memory_space=pl.ANY),
                pl.BlockSpec((None, None, 1, d), b_map),
            ],
            out_specs=pl.BlockSpec((MOE_ROWS, dh), row_map),
            scratch_shapes=[pltpu.VMEM((d, ff2), F32), pltpu.VMEM((ff, d), F32),
                            pltpu.VMEM((d, ff2), BF16), pltpu.VMEM((ff, d), BF16),
                            pltpu.SemaphoreType.DMA((2,))],
        ),
        out_shape=jax.ShapeDtypeStruct((n_rows, dh), jnp.uint32),
        compiler_params=pltpu.CompilerParams(dimension_semantics=("arbitrary",), vmem_limit_bytes=VMEM_LIMIT),
        name="moe_experts",
    )(block_e, block_valid, block_pos, used_experts, n_used, xs, w_gu, b_gu.reshape(depth, e, 1, ff2), w_dn,
      b_dn.reshape(depth, e, 1, d))


def _combine_kernel(x_ref, yg_ref, te_ref, mod_ref, *rest):
    o_ref = rest[-1]
    o_ref[...] = x_ref[...] + mod_ref[5:6, :] * _moe_sum(yg_ref, te_ref)


def _combine_call(grp, mod, prev, bsz, nb):
    x = grp["x"]
    _, seq, d = x.shape
    tm = min(ROW_TILE, seq)
    nt = seq // tm
    xb0, mb0 = grp["x_b0"], grp["mod_b0"]
    tile0 = grp["tok0"] // tm
    yg = grp["yg"]
    in_specs = [
        pl.BlockSpec((None, tm, d), lambda b, s: (xb0 + b, s, 0)),
        pl.BlockSpec((TOP_K, tm, yg.shape[-1]), lambda b, s: (0, b * nt + s, 0)),
        pl.BlockSpec((tm, 2 * TOP_K), lambda b, s: (tile0 + b * nt + s, 0)),
        pl.BlockSpec((None, 6, d), lambda b, s: (mb0 + b, 0, 0)),
    ]
    args = [x, yg, grp["te"], mod]
    aliases = {}
    if prev is not None:
        in_specs.append(pl.BlockSpec(memory_space=pl.ANY))
        args.append(prev)
        aliases = {len(args) - 1: 0}
    return pl.pallas_call(
        _combine_kernel,
        grid=(nb, nt),
        in_specs=in_specs,
        out_specs=pl.BlockSpec((None, tm, d), lambda b, s: (mb0 + b, s, 0)),
        out_shape=jax.ShapeDtypeStruct((bsz, seq, d), F32),
        input_output_aliases=aliases,
        compiler_params=pltpu.CompilerParams(dimension_semantics=("parallel", "parallel"),
                                             vmem_limit_bytes=VMEM_LIMIT),
        name="moe_combine",
    )(*args)


def _moe_rows(layer, grp, gt, w_gu, b_gu, w_dn, b_dn):
    h2 = grp["h2"]
    dh = h2.shape[-1]
    n_rows = -(-(gt * TOP_K + N_EXPERTS * (MOE_ROWS - 1)) // MOE_ROWS) * MOE_ROWS
    nb = n_rows // MOE_ROWS
    first_row = jnp.arange(nb, dtype=jnp.int32) * MOE_ROWS
    upto = jnp.arange(LANES)[:, None] <= jnp.arange(LANES)[None, :]
    g_cnt = grp["cnt"][:, 0, :].astype(jnp.int32)
    counts = jnp.sum(g_cnt, axis=0)
    padded = (counts + MOE_ROWS - 1) // MOE_ROWS * MOE_ROWS
    pad_end = jnp.sum(jnp.where(upto, padded[:, None], 0), axis=0)
    pad_start = pad_end - padded
    tile_base = pad_start[None, :] + jnp.cumsum(g_cnt, axis=0) - g_cnt
    dest = _dest_call(grp["tet"], tile_base.astype(F32)[:, :, None], grp["tok0"], gt)
    dests = [dest[kk] for kk in range(TOP_K)]
    block_e = jnp.minimum(jnp.sum(pad_end[None, :N_EXPERTS] <= first_row[:, None], axis=1),
                          N_EXPERTS - 1).astype(jnp.int32)
    valid_end = (pad_start + counts)[block_e]
    block_valid = jnp.clip(valid_end - first_row, 0, MOE_ROWS).astype(jnp.int32)
    owns = counts[:N_EXPERTS] > 0
    expert_pos = jnp.cumsum(owns.astype(jnp.int32)) - 1
    slot = jnp.arange(N_EXPERTS, dtype=jnp.int32)
    used_experts = jnp.sum(jnp.where(owns[None, :] & (expert_pos[None, :] == slot[:, None]),
                                     slot[None, :], 0), axis=1).astype(jnp.int32)
    block_pos = expert_pos[block_e].astype(jnp.int32)
    n_used = jnp.stack([pad_end[N_EXPERTS - 1] // MOE_ROWS, jnp.sum(owns)]).astype(jnp.int32)
    xs = _sc_scatter(h2, dests, n_rows, grp["tok0"])
    y = _expert_call(layer, block_e, block_valid, block_pos, used_experts, n_used, xs, w_gu, b_gu, w_dn, b_dn)
    return _sc_gather(y, dest[:TOP_K].reshape(-1)).reshape(TOP_K, gt, dh)


def kernel(x, c, positions, ada_w, ada_b, norm_mix_g, norm_ffn_g, router_w, router_b, moe_w_gu, moe_b_gu,
           moe_w_dn, moe_b_dn, even_w_in, mla_q_norm_g, mla_w_uq, mla_kv_norm_g, mla_w_ukv, mla_q_head_g,
           mla_k_head_g, s5_a_re, s5_a_im, s5_log_dt, s5_b_re, s5_b_im, s5_c_re, s5_c_im, s5_d, s5_glu_w,
           s5_glu_b, even_w_out, odd_w_in, pool_w, pool_scale, sgu_norm_g, sgu_w, sgu_b, odd_w_out):
    bsz, seq, d = x.shape
    depth = ada_w.shape[0]
    mods = _ada_call(c, ada_w, ada_b).reshape(depth, bsz, 6, d)
    posf = positions.astype(F32).reshape(bsz, seq, 1)
    splits = MOE_SPLITS if bsz % MOE_SPLITS == 0 else 1
    gb = bsz // splits
    gt = gb * seq
    tiles_g = gt // min(ROW_TILE, seq)

    def settle(groups, mod):
        out = None
        for grp in groups:
            out = _combine_call(grp, mod, out, bsz, gb)
        return out

    pending = None
    for layer in range(depth):
        mod = mods[layer]
        i = layer // 2
        g_mix = norm_mix_g[layer].reshape(1, d)
        g_ffn = norm_ffn_g[layer].reshape(1, d)
        rw, rb = _router_pad(router_w[layer], router_b[layer])
        if layer % 2 == 0:
            if pending is not None:
                x = settle(pending, mods[layer - 1])
            prep = _prep_even(even_w_in[i], mla_q_norm_g[i], mla_w_uq[i], mla_kv_norm_g[i], mla_w_ukv[i],
                              mla_q_head_g[i], mla_k_head_g[i])
            q, k, v, u_t = _even_in_call(x, mod, posf, g_mix, prep)
            attn = _attn_call(q, k, v)
            disc = _s5_disc_call(s5_a_re[i], s5_a_im[i], s5_log_dt[i], s5_b_re[i], s5_b_im[i])
            ssm_t = _s5_call(u_t.reshape(seq, bsz, d // 2), disc, s5_c_re[i], s5_c_im[i], s5_d[i],
                             s5_glu_w[i], s5_glu_b[i])
            x_new, h2, te, tet, tile_cnt = _mix_out_call(x, attn, ssm_t.reshape(seq, bsz * (d // 2)), mod,
                                                         even_w_out[i].astype(BF16), g_ffn, rw, rb)
            groups = [dict(x=x_new, x_b0=gi * gb, h2=h2, te=te, tet=tet, tok0=gi * gt, mod_b0=gi * gb,
                           cnt=tile_cnt[gi * tiles_g:(gi + 1) * tiles_g]) for gi in range(splits)]
        else:
            groups = []
            for gi, grp in enumerate(pending):
                x_new, h2, te, tet, tile_cnt = _odd_call(grp, mods[layer - 1], mod, gb, g_mix, odd_w_in[i],
                                                         pool_w[i], pool_scale[i], sgu_norm_g[i], sgu_w[i],
                                                         sgu_b[i], odd_w_out[i], g_ffn, rw, rb)
                groups.append(dict(x=x_new, x_b0=0, h2=h2, te=te, tet=tet, tok0=0, mod_b0=gi * gb,
                                   cnt=tile_cnt))
        for grp in groups:
            grp["yg"] = _moe_rows(layer, grp, gt, moe_w_gu, moe_b_gu, moe_w_dn, moe_b_dn)
        pending = groups
    return settle(pending, mods[depth - 1])
```

```python
import functools
import math

import jax
import jax.numpy as jnp
from jax import lax
from jax.experimental import pallas as pl
from jax.experimental.pallas import tpu as pltpu
from jax.experimental.pallas import tpu_sc as plsc

F32 = jnp.float32
BF16 = jnp.bfloat16
HIGHEST = lax.Precision.HIGHEST

NORM_EPS = 1e-6
MLA_HEADS = 8
QK_NOPE_DIM = 64
QK_ROPE_DIM = 32
QK_HEAD_DIM = QK_NOPE_DIM + QK_ROPE_DIM
V_HEAD_DIM = 64
Q_LORA_RANK = 256
KV_LORA_RANK = 128
ROPE_THETA = 10000.0
POOL_WINDOWS = (2, 4, 8, 16)
SGU_HEADS = 4
SGU_CHUNK = 128
N_EXPERTS = 32
TOP_K = 4
SWIGLU_ALPHA = 1.702
SWIGLU_LIMIT = 7.0

LANES = 128
SUBLANES = 8
HEAD_SLAB = LANES
POOL_HALO = 16
ROW_TILE = 1024
ODD_TILE = 512
ATTN_TILE = 512
ATTN_Q_BLOCKS = 2
ATTN_ROWS = 32
ATTN_HEADS = 4
S5_STEPS = 64
MOE_ROWS = 1024
MOE_PATHS = (128, 256, 512, 768, MOE_ROWS)
MOE_SPLITS = 2
SC_CORES = 2
SC_WORKERS = SC_CORES * 16
SC_CHUNK = 64
ADA_COLS = 1536
VMEM_LIMIT = 56 * 1024 * 1024
NEG_BIG = -1e30


def _sigmoid(v):
    return 1.0 / (1.0 + jnp.exp(-v))


def _gelu(v):
    return 0.5 * v * (1.0 + jnp.tanh(math.sqrt(2.0 / math.pi) * (v + 0.044715 * (v * v * v))))


def _rms(v, width):
    return lax.rsqrt(jnp.sum(v * v, axis=-1, keepdims=True) * (1.0 / width) + NORM_EPS)


def _mod_norm(x, g, sc, sh):
    return x * _rms(x, x.shape[-1]) * (g * (1.0 + sc)) + sh


def _bdot(a, b):
    return jnp.dot(a.astype(BF16), b, preferred_element_type=F32)


def _pack_pairs(v):
    w = v.shape[-1] // 2
    bits = pltpu.bitcast(v.astype(BF16).astype(F32), jnp.uint32)
    return (bits[:, :w] >> 16) | bits[:, w:]


def _unpack_pairs_f32(p):
    lo = pltpu.bitcast(p << 16, F32)
    hi = pltpu.bitcast(p & jnp.uint32(0xFFFF0000), F32)
    return lo, hi


def _unpack_pairs(p):
    lo, hi = _unpack_pairs_f32(p)
    return lo.astype(BF16), hi.astype(BF16)


def _ada_kernel(c_ref, w_ref, b_ref, o_ref):
    c = c_ref[...]
    act = c * _sigmoid(c)
    w = w_ref[...]
    a_hi = act.astype(BF16)
    a_lo = (act - a_hi.astype(F32)).astype(BF16)
    w_hi = w.astype(BF16)
    w_lo = (w - w_hi.astype(F32)).astype(BF16)
    o_ref[...] = (jnp.dot(a_hi, w_hi, preferred_element_type=F32) + jnp.dot(a_lo, w_hi, preferred_element_type=F32)
                  + jnp.dot(a_hi, w_lo, preferred_element_type=F32) + b_ref[...])


def _ada_call(c, ada_w, ada_b):
    depth, d, n = ada_w.shape
    bsz = c.shape[0]
    tn = ADA_COLS
    return pl.pallas_call(
        _ada_kernel,
        grid=(depth, n // tn),
        in_specs=[
            pl.BlockSpec((bsz, d), lambda l, j: (0, 0)),
            pl.BlockSpec((None, d, tn), lambda l, j: (l, 0, j)),
            pl.BlockSpec((None, 1, tn), lambda l, j: (l, 0, j)),
        ],
        out_specs=pl.BlockSpec((None, bsz, tn), lambda l, j: (l, 0, j)),
        out_shape=jax.ShapeDtypeStruct((depth, bsz, n), F32),
        compiler_params=pltpu.CompilerParams(dimension_semantics=("parallel", "parallel"),
                                             vmem_limit_bytes=VMEM_LIMIT),
        name="ada_mod",
    )(c, ada_w, ada_b.reshape(depth, 1, n))


_C_Q = 0
_C_KV = Q_LORA_RANK
_C_PE = _C_KV + KV_LORA_RANK
_C_PESW = _C_PE + HEAD_SLAB
_C_U = _C_PESW + HEAD_SLAB


def _even_in_kernel(x_ref, mod_ref, pos_ref, g_ref, win_ref, gq_ref, wq_ref, gkv_ref, wk_ref, wv_ref,
                    tab_ref, q_ref, k_ref, v_ref, u_ref):
    x = x_ref[...]
    h = _mod_norm(x, g_ref[...], mod_ref[1:2, :], mod_ref[0:1, :])
    z = _bdot(h, win_ref[...])
    q_c = z[:, _C_Q:_C_KV]
    kv_c = z[:, _C_KV:_C_PE]
    kpe = z[:, _C_PE:_C_PESW]
    kpe_sw = z[:, _C_PESW:_C_U]
    u_ref[...] = z[:, _C_U:]

    tm = x.shape[0]
    packs = LANES // QK_ROPE_DIM
    qrows = tm // packs
    lane = lax.broadcasted_iota(jnp.int32, (1, LANES), 1)
    packed = jnp.zeros((qrows, LANES), F32)
    for part in range(packs):
        in_part = (lane >= part * QK_ROPE_DIM) & (lane < (part + 1) * QK_ROPE_DIM)
        packed = jnp.where(in_part, pos_ref[part * qrows:(part + 1) * qrows, :], packed)
    ang = packed * tab_ref[5:6, :]
    cs_p = jnp.cos(ang)
    sn_p = jnp.sin(ang)
    on_rope = (lane >= QK_NOPE_DIM) & (lane < QK_HEAD_DIM)
    cs_parts = []
    sn_parts = []
    for part in range(packs):
        shift = (QK_NOPE_DIM - part * QK_ROPE_DIM) % LANES
        cs_r = cs_p if shift == 0 else pltpu.roll(cs_p, shift, axis=1)
        sn_r = sn_p if shift == 0 else pltpu.roll(sn_p, shift, axis=1)
        cs_parts.append(jnp.where(on_rope, cs_r, 1.0))
        sn_parts.append(jnp.where(on_rope, sn_r, 0.0))
    cs = jnp.concatenate(cs_parts, axis=0)
    sn = jnp.concatenate(sn_parts, axis=0)
    gcq = cs * tab_ref[1:2, :]
    gsq = sn * tab_ref[2:3, :]
    gck = cs * tab_ref[3:4, :]
    gsk = sn * tab_ref[4:5, :]

    qn = q_c * _rms(q_c, Q_LORA_RANK) * gq_ref[...]
    qq = _bdot(qn, wq_ref[...])
    kvn = kv_c * _rms(kv_c, KV_LORA_RANK) * gkv_ref[...]
    kk = _bdot(kvn, wk_ref[...])
    v_ref[...] = _bdot(kvn, wv_ref[...]).astype(v_ref.dtype)

    pe_rot = kpe * gck + kpe_sw * gsk
    pe_ss = jnp.sum(kpe * kpe, axis=-1, keepdims=True)
    hw = MLA_HEADS * HEAD_SLAB
    for hd in range(MLA_HEADS):
        lo = hd * HEAD_SLAB
        qr = qq[:, lo:lo + HEAD_SLAB]
        qs = qq[:, hw + lo:hw + lo + HEAD_SLAB]
        rq = _rms(qr, QK_HEAD_DIM)
        q_ref[hd] = (rq * (qr * gcq + qs * gsq)).astype(q_ref.dtype)
        kr = kk[:, lo:lo + HEAD_SLAB]
        rk = lax.rsqrt((jnp.sum(kr * kr, axis=-1, keepdims=True) + pe_ss) * (1.0 / QK_HEAD_DIM) + NORM_EPS)
        k_ref[hd] = (rk * (kr * gck + pe_rot)).astype(k_ref.dtype)


def _even_in_call(x, mod, posf, g, prep):
    bsz, seq, d = x.shape
    tm = min(ROW_TILE, seq)
    full = lambda a: pl.BlockSpec(a.shape, lambda b, s: (0,) * a.ndim)
    return pl.pallas_call(
        _even_in_kernel,
        grid=(bsz, seq // tm),
        in_specs=[
            pl.BlockSpec((None, tm, d), lambda b, s: (b, s, 0)),
            pl.BlockSpec((None, 6, d), lambda b, s: (b, 0, 0)),
            pl.BlockSpec((None, tm, 1), lambda b, s: (b, s, 0)),
            full(g), full(prep["w_in"]), full(prep["gq"]), full(prep["wq"]), full(prep["gkv"]),
            full(prep["wk"]), full(prep["wv"]), full(prep["tab"]),
        ],
        out_specs=[
            pl.BlockSpec((None, MLA_HEADS, tm, HEAD_SLAB), lambda b, s: (b, 0, s, 0)),
            pl.BlockSpec((None, MLA_HEADS, tm, HEAD_SLAB), lambda b, s: (b, 0, s, 0)),
            pl.BlockSpec((None, tm, MLA_HEADS * V_HEAD_DIM), lambda b, s: (b, s, 0)),
            pl.BlockSpec((tm, d // 2), lambda b, s: (s, b)),
        ],
        out_shape=[
            jax.ShapeDtypeStruct((bsz, MLA_HEADS, seq, HEAD_SLAB), BF16),
            jax.ShapeDtypeStruct((bsz, MLA_HEADS, seq, HEAD_SLAB), BF16),
            jax.ShapeDtypeStruct((bsz, seq, MLA_HEADS * V_HEAD_DIM), BF16),
            jax.ShapeDtypeStruct((seq, bsz * (d // 2)), F32),
        ],
        compiler_params=pltpu.CompilerParams(dimension_semantics=("parallel", "parallel"),
                                             vmem_limit_bytes=VMEM_LIMIT),
        name="even_in",
    )(x, mod, posf, g, prep["w_in"], prep["gq"], prep["wq"], prep["gkv"], prep["wk"], prep["wv"], prep["tab"])


def _prep_even(even_w_in, q_norm_g, w_uq, kv_norm_g, w_ukv, q_head_g, k_head_g):
    d = even_w_in.shape[0]
    half = QK_ROPE_DIM // 2
    nope = QK_NOPE_DIM
    c_pe = Q_LORA_RANK + KV_LORA_RANK
    w_pe = even_w_in[:, c_pe:c_pe + QK_ROPE_DIM]
    zeros = lambda n: jnp.zeros((d, n), F32)
    pe_slab = jnp.concatenate([zeros(nope), w_pe, zeros(HEAD_SLAB - QK_HEAD_DIM)], axis=1)
    pe_sw = jnp.concatenate([zeros(nope), -w_pe[:, half:], w_pe[:, :half], zeros(HEAD_SLAB - QK_HEAD_DIM)], axis=1)
    w_in = jnp.concatenate([even_w_in[:, :c_pe], pe_slab, pe_sw, even_w_in[:, c_pe + QK_ROPE_DIM:]], axis=1)

    r = w_uq.shape[0]
    padq = jnp.zeros((r, MLA_HEADS, HEAD_SLAB - QK_HEAD_DIM), F32)
    wq_plain = jnp.concatenate([w_uq, padq], axis=2).reshape(r, MLA_HEADS * HEAD_SLAB)
    wq_sw = jnp.concatenate([jnp.zeros((r, MLA_HEADS, nope), F32), -w_uq[:, :, nope + half:],
                             w_uq[:, :, nope:nope + half], padq], axis=2).reshape(r, MLA_HEADS * HEAD_SLAB)
    wq = jnp.concatenate([wq_plain, wq_sw], axis=1)

    rk = w_ukv.shape[0]
    wk = jnp.concatenate([w_ukv[:, :, :nope], jnp.zeros((rk, MLA_HEADS, HEAD_SLAB - nope), F32)],
                         axis=2).reshape(rk, MLA_HEADS * HEAD_SLAB)
    wv = w_ukv[:, :, nope:].reshape(rk, MLA_HEADS * V_HEAD_DIM)

    inv_freq = 1.0 / (ROPE_THETA ** (jnp.arange(half, dtype=F32) / half))
    pad_tail = jnp.zeros((HEAD_SLAB - QK_HEAD_DIM,), F32)
    freq_row = jnp.concatenate([jnp.zeros((nope,), F32), inv_freq, inv_freq, pad_tail])

    def gain_rows(gv, scale):
        plain = jnp.concatenate([gv, pad_tail]) * scale
        swapped = jnp.concatenate([jnp.zeros((nope,), F32), gv[nope + half:], gv[nope:nope + half], pad_tail]) * scale
        return plain, swapped

    gq_plain, gq_sw = gain_rows(q_head_g, QK_HEAD_DIM ** -0.5 * math.log2(math.e))
    gk_plain, gk_sw = gain_rows(k_head_g, 1.0)
    freq_packed = jnp.tile(jnp.concatenate([inv_freq, inv_freq]), LANES // QK_ROPE_DIM)
    tab = jnp.stack([freq_row, gq_plain, gq_sw, gk_plain, gk_sw, freq_packed, freq_row * 0, freq_row * 0])
    return {
        "w_in": w_in.astype(BF16), "gq": q_norm_g.reshape(1, -1), "wq": wq.astype(BF16),
        "gkv": kv_norm_g.reshape(1, -1), "wk": wk.astype(BF16), "wv": wv.astype(BF16), "tab": tab,
    }


def _attn_kernel(qi_ref, kj_ref, q_ref, k_ref, v_ref, o_ref, m_sc, a_sc, acc_sc, s_sc, p_sc, *, tq, tk):
    step = pl.program_id(2)
    i = qi_ref[step]
    j = kj_ref[step]
    sum_lane = (V_HEAD_DIM, 0)
    subs = tq // tk

    @pl.when(j == 0)
    def _():
        m_sc[...] = jnp.full(m_sc.shape, -jnp.inf, F32)
        acc_sc[...] = jnp.zeros(acc_sc.shape, F32)

    def sweep(diag_sub):
        lo = 0 if diag_sub is None else diag_sub * tk
        live = pl.ds(lo, tq - lo)
        lane = lax.broadcasted_iota(jnp.int32, (1, LANES), 1)
        for hh in range(ATTN_HEADS):
            s_sc[hh, live, :] = lax.dot_general(q_ref[hh, live, :], k_ref[hh], (((1,), (1,)), ((), ())),
                                                preferred_element_type=F32)
        for hh in range(ATTN_HEADS):
            v = v_ref[:, (hh // 2) * LANES:(hh // 2 + 1) * LANES]
            for r0 in range(lo, tq, ATTN_ROWS):
                rows = pl.ds(r0, ATTN_ROWS)
                s = s_sc[hh, rows, :]
                if diag_sub is not None and r0 < lo + tk:
                    row = (r0 - lo) + lax.broadcasted_iota(jnp.int32, (ATTN_ROWS, tk), 0)
                    col = lax.broadcasted_iota(jnp.int32, (ATTN_ROWS, tk), 1)
                    s = jnp.where(col <= row, s, -jnp.inf)
                m_prev = m_sc[hh, rows, :]
                m_new = jnp.maximum(m_prev, jnp.max(s, axis=-1, keepdims=True))
                a_sc[hh, rows, :] = jnp.exp2(m_prev - m_new)
                m_sc[hh, rows, :] = m_new
                shifted = s - jnp.concatenate([m_new] * (tk // LANES), axis=1)
                p_sc[hh, rows, :] = jnp.exp2(shifted.astype(BF16))
            own = (lane < V_HEAD_DIM) == (hh % 2 == 0)
            ones = jnp.where(lane == sum_lane[hh % 2], 1.0, 0.0).astype(v.dtype)
            vh = jnp.where(own, v, jnp.broadcast_to(ones, v.shape))
            acc_sc[hh, live, :] = (acc_sc[hh, live, :] * a_sc[hh, live, :]
                                   + jnp.dot(p_sc[hh, live, :], vh, preferred_element_type=F32))

    diag = j - subs * i

    @pl.when(diag < 0)
    def _():
        sweep(None)

    for ds_ in range(subs):
        @pl.when(diag == ds_)
        def _(ds_=ds_):
            sweep(ds_)

    @pl.when(diag == subs - 1)
    def _():
        lane = lax.broadcasted_iota(jnp.int32, (1, LANES), 1)
        for pp in range(ATTN_HEADS // 2):
            acc0 = acc_sc[2 * pp]
            acc1 = acc_sc[2 * pp + 1]
            l0 = acc0[:, sum_lane[0]:sum_lane[0] + 1]
            l1 = acc1[:, sum_lane[1]:sum_lane[1] + 1]
            o_ref[:, pp * LANES:(pp + 1) * LANES] = jnp.where(lane < V_HEAD_DIM, acc0 / l0,
                                                              acc1 / l1).astype(o_ref.dtype)


def _attn_call(q, k, v):
    bsz, nh, seq, _ = q.shape
    tk = min(ATTN_TILE, seq)
    tq = min(ATTN_Q_BLOCKS * tk, seq)
    subs = tq // tk
    nq = seq // tq
    pairs = [(i, j) for i in range(nq) for j in range(subs * (i + 1))]
    qi = jnp.asarray([p[0] for p in pairs], jnp.int32)
    kj = jnp.asarray([p[1] for p in pairs], jnp.int32)
    kern = functools.partial(_attn_kernel, tq=tq, tk=tk)
    hp = ATTN_HEADS
    assert nh % hp == 0
    return pl.pallas_call(
        kern,
        grid_spec=pltpu.PrefetchScalarGridSpec(
            num_scalar_prefetch=2,
            grid=(bsz, nh // hp, len(pairs)),
            in_specs=[
                pl.BlockSpec((None, hp, tq, HEAD_SLAB), lambda b, h, p, qi, kj: (b, h, qi[p], 0)),
                pl.BlockSpec((None, hp, tk, HEAD_SLAB), lambda b, h, p, qi, kj: (b, h, kj[p], 0)),
                pl.BlockSpec((None, tk, hp * V_HEAD_DIM), lambda b, h, p, qi, kj: (b, kj[p], h)),
            ],
            out_specs=pl.BlockSpec((None, tq, hp * V_HEAD_DIM), lambda b, h, p, qi, kj: (b, qi[p], h)),
            scratch_shapes=[pltpu.VMEM((hp, tq, LANES), F32), pltpu.VMEM((hp, tq, LANES), F32),
                            pltpu.VMEM((hp, tq, LANES), F32),
                            pltpu.VMEM((hp, tq, tk), F32), pltpu.VMEM((hp, tq, tk), BF16)],
        ),
        out_shape=jax.ShapeDtypeStruct((bsz, seq, nh * V_HEAD_DIM), BF16),
        compiler_params=pltpu.CompilerParams(
            dimension_semantics=("parallel", "parallel", "arbitrary"),
            vmem_limit_bytes=VMEM_LIMIT),
        name="mla_attention",
    )(qi, kj, q, k, v)


def _s5_disc_kernel(are_ref, aim_ref, ldt_ref, bre_ref, bim_ref, abre_ref, abim_ref, bbre_ref, bbim_ref):
    dt = jnp.exp(ldt_ref[...])
    lam_re = jnp.minimum(are_ref[...], -1e-4)
    lam_im = aim_ref[...]
    mag = jnp.exp(lam_re * dt)
    ab_re = mag * jnp.cos(lam_im * dt)
    ab_im = mag * jnp.sin(lam_im * dt)
    den = lam_re * lam_re + lam_im * lam_im
    num_re = ab_re - 1.0
    f_re = (num_re * lam_re + ab_im * lam_im) / den
    f_im = (ab_im * lam_re - num_re * lam_im) / den
    abre_ref[...] = ab_re
    abim_ref[...] = ab_im
    br = bre_ref[...]
    bi = bim_ref[...]
    bbre_ref[...] = f_re[:, None, :] * br - f_im[:, None, :] * bi
    bbim_ref[...] = f_re[:, None, :] * bi + f_im[:, None, :] * br


def _s5_disc_call(a_re, a_im, log_dt, b_re, b_im):
    g, p = a_re.shape
    bre_t = jnp.swapaxes(b_re, 1, 2)
    bim_t = jnp.swapaxes(b_im, 1, 2)
    return pl.pallas_call(
        _s5_disc_kernel,
        out_shape=[jax.ShapeDtypeStruct((g, p), F32), jax.ShapeDtypeStruct((g, p), F32),
                   jax.ShapeDtypeStruct(bre_t.shape, F32), jax.ShapeDtypeStruct(bre_t.shape, F32)],
        name="s5_discretize",
    )(a_re, a_im, log_dt.reshape(g, 1), bre_t, bim_t)


def _block_diag_halves(m):
    g, r, c = m.shape
    gh = g // 2
    eye = jnp.eye(gh, dtype=m.dtype)
    mh = m.reshape(2, gh, r, c)
    return (mh[:, :, :, None, :] * eye[None, :, None, :, None]).reshape(2, gh * r, gh * c)


def _s5_kernel(u_ref, bre_ref, bim_ref, are_ref, aim_ref, cre_ref, cim_ref, d_ref, gw_ref, gb_ref,
               o_ref, sre, sim, dre, dim, xbr, xbi, *, steps):
    @pl.when(pl.program_id(0) == 0)
    def _():
        sre[...] = jnp.zeros(sre.shape, F32)
        sim[...] = jnp.zeros(sim.shape, F32)

    rows = steps * SUBLANES
    w = u_ref.shape[-1]
    u = u_ref[...].reshape(rows, w)
    ub = u.astype(BF16)
    kh = w // 2
    nh = dre.shape[1] // 2
    for hf in range(2):
        dre[:, hf * nh:(hf + 1) * nh] = jnp.dot(ub[:, hf * kh:(hf + 1) * kh], bre_ref[hf], preferred_element_type=F32)
        dim[:, hf * nh:(hf + 1) * nh] = jnp.dot(ub[:, hf * kh:(hf + 1) * kh], bim_ref[hf], preferred_element_type=F32)

    xr = sre[...]
    xi = sim[...]
    for t in range(0, steps, 2):
        pair_r = []
        pair_i = []
        for r0 in (t * SUBLANES, (t + 1) * SUBLANES):
            nr = are_ref[...] * xr - aim_ref[...] * xi + dre[r0:r0 + SUBLANES, :]
            ni = are_ref[...] * xi + aim_ref[...] * xr + dim[r0:r0 + SUBLANES, :]
            xr, xi = nr, ni
            pair_r.append(nr)
            pair_i.append(ni)
        xbr[t * SUBLANES:(t + 2) * SUBLANES, :] = jnp.concatenate(pair_r, axis=0).astype(BF16)
        xbi[t * SUBLANES:(t + 2) * SUBLANES, :] = jnp.concatenate(pair_i, axis=0).astype(BF16)
    sre[...] = xr
    sim[...] = xi

    ys = []
    for hf in range(2):
        yr = jnp.dot(xbr[:, hf * nh:(hf + 1) * nh], cre_ref[hf], preferred_element_type=F32)
        yi = jnp.dot(xbi[:, hf * nh:(hf + 1) * nh], cim_ref[hf], preferred_element_type=F32)
        ys.append(yr - yi)
    y = jnp.concatenate(ys, axis=1) + d_ref[...] * u
    g = _gelu(y)
    out = g * _sigmoid(_bdot(g, gw_ref[...]) + gb_ref[...])
    o_ref[...] = out.reshape(steps, SUBLANES, w).astype(o_ref.dtype)


def _s5_call(u_t, disc, c_re, c_im, d_skip, glu_w, glu_b):
    seq, bsz, w = u_t.shape
    assert bsz == SUBLANES
    ab_re, ab_im, bb_re, bb_im = disc
    g, p = ab_re.shape
    n_state = g * p
    bre = _block_diag_halves(bb_re).astype(BF16)
    bim = _block_diag_halves(bb_im).astype(BF16)
    cre = _block_diag_halves(jnp.swapaxes(c_re, 1, 2)).astype(BF16)
    cim = _block_diag_halves(jnp.swapaxes(c_im, 1, 2)).astype(BF16)
    steps = min(S5_STEPS, seq)
    full = lambda a: pl.BlockSpec(a.shape, lambda s: (0,) * a.ndim)
    rep = lambda a: jnp.broadcast_to(a.reshape(1, n_state), (bsz, n_state))
    args = (bre, bim, rep(ab_re), rep(ab_im), cre, cim,
            d_skip.reshape(1, w), glu_w.astype(BF16), glu_b.reshape(1, w))
    return pl.pallas_call(
        functools.partial(_s5_kernel, steps=steps),
        grid=(seq // steps,),
        in_specs=[pl.BlockSpec((steps, bsz, w), lambda s: (s, 0, 0))] + [full(a) for a in args],
        out_specs=pl.BlockSpec((steps, bsz, w), lambda s: (s, 0, 0)),
        out_shape=jax.ShapeDtypeStruct((seq, bsz, w), BF16),
        scratch_shapes=[pltpu.VMEM((bsz, n_state), F32), pltpu.VMEM((bsz, n_state), F32),
                        pltpu.VMEM((steps * bsz, n_state), F32), pltpu.VMEM((steps * bsz, n_state), F32),
                        pltpu.VMEM((steps * bsz, n_state), BF16), pltpu.VMEM((steps * bsz, n_state), BF16)],
        compiler_params=pltpu.CompilerParams(dimension_semantics=("arbitrary",), vmem_limit_bytes=VMEM_LIMIT),
        name="s5_scan",
    )(u_t, *args)


def _router_tail(x_new, mod_ref, gf_ref, rw_ref, rb_ref, h2_ref, te_ref, tet_ref, cnt_ref):
    h2 = _mod_norm(x_new, gf_ref[...], mod_ref[4:5, :], mod_ref[3:4, :])
    h2_ref[...] = _pack_pairs(h2)
    h_hi = h2.astype(BF16)
    h_lo = (h2 - h_hi.astype(F32)).astype(BF16)
    r_hi = jnp.dot(h_hi, rw_ref[...], preferred_element_type=F32)
    r_lo = jnp.dot(h_lo, rw_ref[...], preferred_element_type=F32)
    logits = r_hi[:, :LANES] + r_hi[:, LANES:] + r_lo[:, :LANES] + rb_ref[...]
    lane = lax.broadcasted_iota(jnp.int32, logits.shape, 1).astype(F32)
    vals = []
    idxs = []
    work = logits
    for _ in range(TOP_K):
        m = jnp.max(work, axis=-1, keepdims=True)
        idx = jnp.min(jnp.where(work == m, lane, float(LANES)), axis=-1, keepdims=True)
        vals.append(m)
        idxs.append(idx)
        work = jnp.where(lane == idx, NEG_BIG * 2.0, work)
    exps = [jnp.exp(vv - vals[0]) for vv in vals]
    tot = exps[0] + exps[1] + exps[2] + exps[3]
    te = jnp.zeros(logits.shape, F32)
    picked = jnp.zeros(logits.shape, F32)
    for kk in range(TOP_K):
        te = jnp.where(lane == float(kk), idxs[kk], te)
        te = jnp.where(lane == float(TOP_K + kk), exps[kk] / tot, te)
        picked = picked + jnp.where(lane == idxs[kk], 1.0, 0.0)
    te_ref[...] = te[:, :2 * TOP_K]
    tet_ref[...] = te.T[:2 * TOP_K, :]
    cnt_ref[...] = jnp.sum(picked, axis=0, keepdims=True)


def _mix_out_kernel(x_ref, a_ref, s_ref, mod_ref, wo_ref, gf_ref, rw_ref, rb_ref, xo_ref, h2_ref, te_ref,
                    tet_ref, cnt_ref):
    ka = a_ref.shape[-1]
    mix = jnp.dot(a_ref[...], wo_ref[:ka, :], preferred_element_type=F32)
    mix = mix + jnp.dot(s_ref[...], wo_ref[ka:, :], preferred_element_type=F32)
    x_new = x_ref[...] + mod_ref[2:3, :] * mix
    xo_ref[...] = x_new
    _router_tail(x_new, mod_ref, gf_ref, rw_ref, rb_ref, h2_ref, te_ref, tet_ref, cnt_ref)


def _tail_out_specs(bsz, seq, tm, d):
    nt = seq // tm
    specs = [
        pl.BlockSpec((None, tm, d), lambda b, s: (b, s, 0)),
        pl.BlockSpec((tm, d // 2), lambda b, s: (b * nt + s, 0)),
        pl.BlockSpec((tm, 2 * TOP_K), lambda b, s: (b * nt + s, 0)),
        pl.BlockSpec((2 * TOP_K, tm), lambda b, s: (0, b * nt + s)),
        pl.BlockSpec((None, 1, LANES), lambda b, s: (b * nt + s, 0, 0)),
    ]
    shapes = [
        jax.ShapeDtypeStruct((bsz, seq, d), F32),
        jax.ShapeDtypeStruct((bsz * seq, d // 2), jnp.uint32),
        jax.ShapeDtypeStruct((bsz * seq, 2 * TOP_K), F32),
        jax.ShapeDtypeStruct((2 * TOP_K, bsz * seq), F32),
        jax.ShapeDtypeStruct((bsz * nt, 1, LANES), F32),
    ]
    return specs, shapes


def _router_pad(router_w, router_b):
    d, e = router_w.shape
    rw = jnp.concatenate([router_w, jnp.zeros((d, LANES - e), F32)], axis=1)
    rw_hi = rw.astype(BF16)
    rw_lo = (rw - rw_hi.astype(F32)).astype(BF16)
    rb = jnp.concatenate([router_b, jnp.full((LANES - e,), NEG_BIG, F32)]).reshape(1, LANES)
    return jnp.concatenate([rw_hi, rw_lo], axis=1), rb


def _mix_out_call(x, attn, ssm_t, mod, w_out, gf, rw, rb):
    bsz, seq, d = x.shape
    tm = min(ROW_TILE, seq)
    ka = attn.shape[-1]
    ks = ssm_t.shape[-1] // bsz
    full = lambda a: pl.BlockSpec(a.shape, lambda b, s: (0,) * a.ndim)
    out_specs, out_shape = _tail_out_specs(bsz, seq, tm, d)
    return pl.pallas_call(
        _mix_out_kernel,
        grid=(bsz, seq // tm),
        in_specs=[
            pl.BlockSpec((None, tm, d), lambda b, s: (b, s, 0)),
            pl.BlockSpec((None, tm, ka), lambda b, s: (b, s, 0)),
            pl.BlockSpec((tm, ks), lambda b, s: (s, b)),
            pl.BlockSpec((None, 6, d), lambda b, s: (b, 0, 0)),
            full(w_out), full(gf), full(rw), full(rb),
        ],
        out_specs=out_specs,
        out_shape=out_shape,
        compiler_params=pltpu.CompilerParams(dimension_semantics=("parallel", "parallel"),
                                             vmem_limit_bytes=VMEM_LIMIT),
        name="even_out",
    )(x, attn, ssm_t, mod, w_out, gf, rw, rb)


def _moe_sum(yg_ref, te_ref):
    te = te_ref[...]
    acc_lo = acc_hi = None
    for kk in range(TOP_K):
        lo, hi = _unpack_pairs_f32(yg_ref[kk])
        gate = te[:, TOP_K + kk:TOP_K + kk + 1]
        acc_lo = gate * lo if kk == 0 else acc_lo + gate * lo
        acc_hi = gate * hi if kk == 0 else acc_hi + gate * hi
    return jnp.concatenate([acc_lo, acc_hi], axis=1)


def _odd_kernel(x_ref, yg_ref, tep_ref, modp_ref, mod_ref, g_ref, win_ref, icnt_ref, wp_ref, ps_ref, gv_ref,
                wsp_ref, bsp_ref, wo_ref, gf_ref, rw_ref, rb_ref, xo_ref, h2_ref, te_ref, tet_ref, cnt_ref, ext_sc):
    tm = x_ref.shape[0]
    pw = wp_ref.shape[-1]
    width = pw * len(POOL_WINDOWS)

    @pl.when(pl.program_id(1) == 0)
    def _():
        ext_sc[0:POOL_HALO, :] = jnp.zeros((POOL_HALO, width), F32)

    x = x_ref[...] + modp_ref[5:6, :] * _moe_sum(yg_ref, tep_ref)
    h = _mod_norm(x, g_ref[...], mod_ref[1:2, :], mod_ref[0:1, :])
    z = _bdot(h, win_ref[...])
    up = z[:, :width]
    ext_sc[POOL_HALO:POOL_HALO + tm, :] = up

    pooled = []
    for gi, win in enumerate(POOL_WINDOWS):
        cols = slice(gi * pw, (gi + 1) * pw)
        acc = up[:, cols]
        for lag in range(1, win):
            acc = acc + ext_sc[POOL_HALO - lag:POOL_HALO - lag + tm, cols]
        pg = acc * icnt_ref[:, gi:gi + 1] - up[:, cols]
        pooled.append(_bdot(pg, wp_ref[gi]) * ps_ref[:, cols])
    ext_sc[0:POOL_HALO, :] = ext_sc[tm:tm + POOL_HALO, :]
    pooled = jnp.concatenate(pooled, axis=1)

    ug = _gelu(z[:, width:2 * width])
    vg = _gelu(z[:, 2 * width:])
    vn = (vg * _rms(vg, width) * gv_ref[...]).astype(BF16)
    hd = width // SGU_HEADS
    chunks = []
    for ci in range(tm // SGU_CHUNK):
        heads = []
        for hh in range(SGU_HEADS):
            blk = vn[ci * SGU_CHUNK:(ci + 1) * SGU_CHUNK, hh * hd:(hh + 1) * hd]
            heads.append(jnp.dot(wsp_ref[hh], blk, preferred_element_type=F32) + bsp_ref[hh])
        chunks.append(jnp.concatenate(heads, axis=1))
    gated = ug * jnp.concatenate(chunks, axis=0)

    mix = _bdot(pooled, wo_ref[:width, :]) + _bdot(gated, wo_ref[width:, :])
    x_new = x + mod_ref[2:3, :] * mix
    xo_ref[...] = x_new
    _router_tail(x_new, mod_ref, gf_ref, rw_ref, rb_ref, h2_ref, te_ref, tet_ref, cnt_ref)


def _odd_call(grp, mod_prev, mod, nb, g, w_in, pool_w, pool_scale, sgu_norm_g, sgu_w, sgu_b, w_out, gf, rw, rb):
    x = grp["x"]
    _, seq, d = x.shape
    tm = min(ODD_TILE, seq)
    nt = seq // tm
    xb0, tok0, mb0 = grp["x_b0"], grp["tok0"], grp["mod_b0"]
    tile0 = tok0 // tm
    yg = grp["yg"]
    width = pool_scale.shape[0]
    hd = width // SGU_HEADS
    t = jnp.arange(seq, dtype=jnp.int32)
    icnt = jnp.stack([1.0 / jnp.minimum(t + 1, wn).astype(F32) for wn in POOL_WINDOWS], axis=1)
    wsp = jnp.tril(sgu_w).astype(BF16)
    bsp = jnp.broadcast_to(sgu_b[:, :, None], (SGU_HEADS, SGU_CHUNK, hd))
    args = (g, w_in.astype(BF16), icnt, pool_w.astype(BF16), pool_scale.reshape(1, width),
            sgu_norm_g.reshape(1, width), wsp, bsp, w_out.astype(BF16), gf, rw, rb)
    full = lambda a: pl.BlockSpec(a.shape, lambda b, s: (0,) * a.ndim)
    in_specs = [pl.BlockSpec((None, tm, d), lambda b, s: (xb0 + b, s, 0)),
                pl.BlockSpec((TOP_K, tm, yg.shape[-1]), lambda b, s: (0, b * nt + s, 0)),
                pl.BlockSpec((tm, 2 * TOP_K), lambda b, s: (tile0 + b * nt + s, 0)),
                pl.BlockSpec((None, 6, d), lambda b, s: (mb0 + b, 0, 0)),
                pl.BlockSpec((None, 6, d), lambda b, s: (mb0 + b, 0, 0))]
    for idx, a in enumerate(args):
        in_specs.append(pl.BlockSpec((tm, len(POOL_WINDOWS)), lambda b, s: (s, 0)) if idx == 2 else full(a))
    out_specs, out_shape = _tail_out_specs(nb, seq, tm, d)
    return pl.pallas_call(
        _odd_kernel,
        grid=(nb, nt),
        in_specs=in_specs,
        out_specs=out_specs,
        out_shape=out_shape,
        scratch_shapes=[pltpu.VMEM((tm + POOL_HALO, width), F32)],
        compiler_params=pltpu.CompilerParams(dimension_semantics=("parallel", "arbitrary"),
                                             vmem_limit_bytes=VMEM_LIMIT),
        name="odd_mixer",
    )(x, yg, grp["te"], mod_prev, mod, *args)


def _dest_kernel(tet_ref, base_ref, tri_ref, dst_ref):
    tr = tet_ref.shape[1]
    tet = tet_ref[...]
    expert = lax.broadcasted_iota(jnp.int32, (LANES, tr), 0).astype(F32)
    hots = [tet[kk:kk + 1, :] == expert for kk in range(TOP_K)]
    oh = jnp.zeros((LANES, tr), F32)
    for hot in hots:
        oh = oh + jnp.where(hot, 1.0, 0.0)
    before = jnp.dot(oh.astype(BF16), tri_ref[...], preferred_element_type=F32) + base_ref[...]
    row = lax.broadcasted_iota(jnp.int32, (2 * TOP_K, tr), 0)
    dst = jnp.zeros((2 * TOP_K, tr), F32)
    for kk, hot in enumerate(hots):
        dst = jnp.where(row == kk, jnp.sum(jnp.where(hot, before, 0.0), axis=0, keepdims=True), dst)
    dst_ref[...] = dst.astype(jnp.int32)


def _dest_call(tet, base, tok0, n_tok):
    tiles = base.shape[0]
    tr = n_tok // tiles
    tile0 = tok0 // tr
    tri = (jnp.arange(tr)[:, None] < jnp.arange(tr)[None, :]).astype(BF16)
    return pl.pallas_call(
        _dest_kernel,
        grid=(tiles,),
        in_specs=[pl.BlockSpec((2 * TOP_K, tr), lambda i: (0, tile0 + i)),
                  pl.BlockSpec((None, LANES, 1), lambda i: (i, 0, 0)),
                  pl.BlockSpec((tr, tr), lambda i: (0, 0))],
        out_specs=pl.BlockSpec((2 * TOP_K, tr), lambda i: (0, i)),
        out_shape=jax.ShapeDtypeStruct((2 * TOP_K, n_tok), jnp.int32),
        compiler_params=pltpu.CompilerParams(dimension_semantics=("parallel",)),
        name="route_dest",
    )(tet, base, tri)


def _sc_gather(table, idx):
    n = idx.shape[0]
    per_w = n // SC_WORKERS
    assert per_w * SC_WORKERS == n and per_w % SC_CHUNK == 0
    n_chunks = per_w // SC_CHUNK
    row_shape = table.shape[1:]
    mesh = plsc.VectorSubcoreMesh(core_axis_name="c", subcore_axis_name="s")

    @functools.partial(
        pl.kernel, mesh=mesh,
        out_type=jax.ShapeDtypeStruct((n,) + row_shape, table.dtype),
        scratch_types=[pltpu.VMEM((SC_CHUNK,), jnp.int32), pltpu.VMEM((SC_CHUNK,) + row_shape, table.dtype),
                       pltpu.SemaphoreType.DMA],
        name="sc_row_gather",
    )
    def gather(table_hbm, idx_hbm, out_hbm, idx_v, rows_v, sem):
        wid = lax.axis_index("s") * SC_CORES + lax.axis_index("c")
        base = wid * per_w

        @pl.loop(0, n_chunks)
        def _(ci):
            off = pl.multiple_of(base + ci * SC_CHUNK, SC_CHUNK)
            pltpu.sync_copy(idx_hbm.at[pl.ds(off, SC_CHUNK)], idx_v)
            pltpu.async_copy(table_hbm.at[idx_v], rows_v, sem).wait()
            pltpu.sync_copy(rows_v, out_hbm.at[pl.ds(off, SC_CHUNK)])

    return gather(table, idx)


def _sc_scatter(rows, dests, n_out, tok0):
    t = dests[0].shape[0]
    per_w = t // SC_WORKERS
    assert per_w * SC_WORKERS == t and per_w % SC_CHUNK == 0
    n_chunks = per_w // SC_CHUNK
    row_shape = rows.shape[1:]
    nk = len(dests)
    mesh = plsc.VectorSubcoreMesh(core_axis_name="c", subcore_axis_name="s")

    @functools.partial(
        pl.kernel, mesh=mesh,
        out_type=jax.ShapeDtypeStruct((n_out,) + row_shape, rows.dtype),
        scratch_types=[pltpu.VMEM((SC_CHUNK,), jnp.int32)] * nk
        + [pltpu.VMEM((SC_CHUNK,) + row_shape, rows.dtype), pltpu.SemaphoreType.DMA],
        name="sc_row_scatter",
    )
    def scatter(rows_hbm, *rest):
        dest_hbm = rest[:nk]
        out_hbm = rest[nk]
        idx_v = rest[nk + 1:2 * nk + 1]
        rows_v, sem = rest[2 * nk + 1:]
        wid = lax.axis_index("s") * SC_CORES + lax.axis_index("c")
        base = wid * per_w

        @pl.loop(0, n_chunks)
        def _(ci):
            off = pl.multiple_of(base + ci * SC_CHUNK, SC_CHUNK)
            src = pl.multiple_of(tok0 + off, SC_CHUNK)
            pltpu.sync_copy(rows_hbm.at[pl.ds(src, SC_CHUNK)], rows_v)
            for kk in range(nk):
                pltpu.sync_copy(dest_hbm[kk].at[pl.ds(off, SC_CHUNK)], idx_v[kk])
            copies = [pltpu.async_copy(rows_v, out_hbm.at[idx_v[kk]], sem) for kk in range(nk)]
            for cp in copies:
                cp.wait()

    return scatter(rows, *dests)


def _expert_kernel(be_ref, nv_ref, ord_ref, ue_ref, nu_ref, x_ref, wgu_hbm, bgu_ref, wdn_hbm, bdn_ref, y_ref,
                   wgu_f32, wdn_f32, wgu_bf, wdn_bf, sem, *, layer):
    i = pl.program_id(0)
    used = i < nu_ref[0]
    pos = ord_ref[i]
    fresh = jnp.logical_or(i == 0, ord_ref[jnp.maximum(i - 1, 0)] != pos)

    def weight_copies(expert):
        return (pltpu.make_async_copy(wgu_hbm.at[layer, expert], wgu_f32, sem.at[0]),
                pltpu.make_async_copy(wdn_hbm.at[layer, expert], wdn_f32, sem.at[1]))

    @pl.when(i == 0)
    def _():
        for cp in weight_copies(ue_ref[0]):
            cp.start()

    @pl.when(jnp.logical_and(used, fresh))
    def _():
        for cp in weight_copies(ue_ref[pos]):
            cp.wait()
        wgu_bf[...] = wgu_f32[...].astype(BF16)
        wdn_bf[...] = wdn_f32[...].astype(BF16)

        @pl.when(pos + 1 < nu_ref[1])
        def _():
            for cp in weight_copies(ue_ref[pos + 1]):
                cp.start()

    def ffn(rows):
        x = jnp.concatenate(_unpack_pairs(x_ref[0:rows, :]), axis=1)
        z = jnp.dot(x, wgu_bf[...], preferred_element_type=F32) + bgu_ref[...]
        ff = z.shape[-1] // 2
        gate = jnp.minimum(z[:, :ff], SWIGLU_LIMIT)
        lin = jnp.clip(z[:, ff:], -SWIGLU_LIMIT, SWIGLU_LIMIT)
        act = gate * _sigmoid(SWIGLU_ALPHA * gate) * (lin + 1.0)
        y = _bdot(act, wdn_bf[...]) + bdn_ref[...]
        y_ref[0:rows, :] = _pack_pairs(y)

    nv = nv_ref[i]
    below = 0
    for size in MOE_PATHS:
        @pl.when(jnp.logical_and(used, jnp.logical_and(nv > below, nv <= size)))
        def _(size=size):
            ffn(size)
        below = size


def _expert_call(layer, block_e, block_valid, block_pos, used_experts, n_used, xs, w_gu, b_gu, w_dn, b_dn):
    n_rows, dh = xs.shape
    depth, e, d, ff2 = w_gu.shape
    ff = ff2 // 2
    nb = n_rows // MOE_ROWS
    row_map = lambda i, be, nv, po, ue, nu: (jnp.minimum(i, nu[0] - 1), 0)
    b_map = lambda i, be, nv, po, ue, nu: (layer, be[i], 0, 0)
    return pl.pallas_call(
        functools.partial(_expert_kernel, layer=layer),
        grid_spec=pltpu.PrefetchScalarGridSpec(
            num_scalar_prefetch=5,
            grid=(nb,),
            in_specs=[
                pl.BlockSpec((MOE_ROWS, dh), row_map),
                pl.BlockSpec(memory_space=pl.ANY),
                pl.BlockSpec((None, None, 1, ff2), b_map),
                pl.BlockSpec(memory_space=pl.ANY),
                pl.BlockSpec((None, None, 1, d), b_map),
            ],
            out_specs=pl.BlockSpec((MOE_ROWS, dh), row_map),
            scratch_shapes=[pltpu.VMEM((d, ff2), F32), pltpu.VMEM((ff, d), F32),
                            pltpu.VMEM((d, ff2), BF16), pltpu.VMEM((ff, d), BF16),
                            pltpu.SemaphoreType.DMA((2,))],
        ),
        out_shape=jax.ShapeDtypeStruct((n_rows, dh), jnp.uint32),
        compiler_params=pltpu.CompilerParams(dimension_semantics=("arbitrary",), vmem_limit_bytes=VMEM_LIMIT),
        name="moe_experts",
    )(block_e, block_valid, block_pos, used_experts, n_used, xs, w_gu, b_gu.reshape(depth, e, 1, ff2), w_dn,
      b_dn.reshape(depth, e, 1, d))


def _combine_kernel(x_ref, yg_ref, te_ref, mod_ref, *rest):
    o_ref = rest[-1]
    o_ref[...] = x_ref[...] + mod_ref[5:6, :] * _moe_sum(yg_ref, te_ref)


def _combine_call(grp, mod, prev, bsz, nb):
    x = grp["x"]
    _, seq, d = x.shape
    tm = min(ROW_TILE, seq)
    nt = seq // tm
    xb0, mb0 = grp["x_b0"], grp["mod_b0"]
    tile0 = grp["tok0"] // tm
    yg = grp["yg"]
    in_specs = [
        pl.BlockSpec((None, tm, d), lambda b, s: (xb0 + b, s, 0)),
        pl.BlockSpec((TOP_K, tm, yg.shape[-1]), lambda b, s: (0, b * nt + s, 0)),
        pl.BlockSpec((tm, 2 * TOP_K), lambda b, s: (tile0 + b * nt + s, 0)),
        pl.BlockSpec((None, 6, d), lambda b, s: (mb0 + b, 0, 0)),
    ]
    args = [x, yg, grp["te"], mod]
    aliases = {}
    if prev is not None:
        in_specs.append(pl.BlockSpec(memory_space=pl.ANY))
        args.append(prev)
        aliases = {len(args) - 1: 0}
    return pl.pallas_call(
        _combine_kernel,
        grid=(nb, nt),
        in_specs=in_specs,
        out_specs=pl.BlockSpec((None, tm, d), lambda b, s: (mb0 + b, s, 0)),
        out_shape=jax.ShapeDtypeStruct((bsz, seq, d), F32),
        input_output_aliases=aliases,
        compiler_params=pltpu.CompilerParams(dimension_semantics=("parallel", "parallel"),
                                             vmem_limit_bytes=VMEM_LIMIT),
        name="moe_combine",
    )(*args)


def _moe_rows(layer, grp, gt, w_gu, b_gu, w_dn, b_dn):
    h2 = grp["h2"]
    dh = h2.shape[-1]
    n_rows = -(-(gt * TOP_K + N_EXPERTS * (MOE_ROWS - 1)) // MOE_ROWS) * MOE_ROWS
    nb = n_rows // MOE_ROWS
    first_row = jnp.arange(nb, dtype=jnp.int32) * MOE_ROWS
    upto = jnp.arange(LANES)[:, None] <= jnp.arange(LANES)[None, :]
    g_cnt = grp["cnt"][:, 0, :].astype(jnp.int32)
    counts = jnp.sum(g_cnt, axis=0)
    padded = (counts + MOE_ROWS - 1) // MOE_ROWS * MOE_ROWS
    pad_end = jnp.sum(jnp.where(upto, padded[:, None], 0), axis=0)
    pad_start = pad_end - padded
    tile_base = pad_start[None, :] + jnp.cumsum(g_cnt, axis=0) - g_cnt
    dest = _dest_call(grp["tet"], tile_base.astype(F32)[:, :, None], grp["tok0"], gt)
    dests = [dest[kk] for kk in range(TOP_K)]
    block_e = jnp.minimum(jnp.sum(pad_end[None, :N_EXPERTS] <= first_row[:, None], axis=1),
                          N_EXPERTS - 1).astype(jnp.int32)
    valid_end = (pad_start + counts)[block_e]
    block_valid = jnp.clip(valid_end - first_row, 0, MOE_ROWS).astype(jnp.int32)
    owns = counts[:N_EXPERTS] > 0
    expert_pos = jnp.cumsum(owns.astype(jnp.int32)) - 1
    slot = jnp.arange(N_EXPERTS, dtype=jnp.int32)
    used_experts = jnp.sum(jnp.where(owns[None, :] & (expert_pos[None, :] == slot[:, None]),
                                     slot[None, :], 0), axis=1).astype(jnp.int32)
    block_pos = expert_pos[block_e].astype(jnp.int32)
    n_used = jnp.stack([pad_end[N_EXPERTS - 1] // MOE_ROWS, jnp.sum(owns)]).astype(jnp.int32)
    xs = _sc_scatter(h2, dests, n_rows, grp["tok0"])
    y = _expert_call(layer, block_e, block_valid, block_pos, used_experts, n_used, xs, w_gu, b_gu, w_dn, b_dn)
    return _sc_gather(y, dest[:TOP_K].reshape(-1)).reshape(TOP_K, gt, dh)


def kernel(x, c, positions, ada_w, ada_b, norm_mix_g, norm_ffn_g, router_w, router_b, moe_w_gu, moe_b_gu,
           moe_w_dn, moe_b_dn, even_w_in, mla_q_norm_g, mla_w_uq, mla_kv_norm_g, mla_w_ukv, mla_q_head_g,
           mla_k_head_g, s5_a_re, s5_a_im, s5_log_dt, s5_b_re, s5_b_im, s5_c_re, s5_c_im, s5_d, s5_glu_w,
           s5_glu_b, even_w_out, odd_w_in, pool_w, pool_scale, sgu_norm_g, sgu_w, sgu_b, odd_w_out):
    bsz, seq, d = x.shape
    depth = ada_w.shape[0]
    mods = _ada_call(c, ada_w, ada_b).reshape(depth, bsz, 6, d)
    posf = positions.astype(F32).reshape(bsz, seq, 1)
    splits = MOE_SPLITS if bsz % MOE_SPLITS == 0 else 1
    gb = bsz // splits
    gt = gb * seq
    tiles_g = gt // min(ROW_TILE, seq)

    def settle(groups, mod):
        out = None
        for grp in groups:
            out = _combine_call(grp, mod, out, bsz, gb)
        return out

    pending = None
    for layer in range(depth):
        mod = mods[layer]
        i = layer // 2
        g_mix = norm_mix_g[layer].reshape(1, d)
        g_ffn = norm_ffn_g[layer].reshape(1, d)
        rw, rb = _router_pad(router_w[layer], router_b[layer])
        if layer % 2 == 0:
            if pending is not None:
                x = settle(pending, mods[layer - 1])
            prep = _prep_even(even_w_in[i], mla_q_norm_g[i], mla_w_uq[i], mla_kv_norm_g[i], mla_w_ukv[i],
                              mla_q_head_g[i], mla_k_head_g[i])
            q, k, v, u_t = _even_in_call(x, mod, posf, g_mix, prep)
            attn = _attn_call(q, k, v)
            disc = _s5_disc_call(s5_a_re[i], s5_a_im[i], s5_log_dt[i], s5_b_re[i], s5_b_im[i])
            ssm_t = _s5_call(u_t.reshape(seq, bsz, d // 2), disc, s5_c_re[i], s5_c_im[i], s5_d[i],
                             s5_glu_w[i], s5_glu_b[i])
            x_new, h2, te, tet, tile_cnt = _mix_out_call(x, attn, ssm_t.reshape(seq, bsz * (d // 2)), mod,
                                                         even_w_out[i].astype(BF16), g_ffn, rw, rb)
            groups = [dict(x=x_new, x_b0=gi * gb, h2=h2, te=te, tet=tet, tok0=gi * gt, mod_b0=gi * gb,
                           cnt=tile_cnt[gi * tiles_g:(gi + 1) * tiles_g]) for gi in range(splits)]
        else:
            groups = []
            for gi, grp in enumerate(pending):
                x_new, h2, te, tet, tile_cnt = _odd_call(grp, mods[layer - 1], mod, gb, g_mix, odd_w_in[i],
                                                         pool_w[i], pool_scale[i], sgu_norm_g[i], sgu_w[i],
                                                         sgu_b[i], odd_w_out[i], g_ffn, rw, rb)
                groups.append(dict(x=x_new, x_b0=0, h2=h2, te=te, tet=tet, tok0=0, mod_b0=gi * gb,
                                   cnt=tile_cnt))
        for grp in groups:
            grp["yg"] = _moe_rows(layer, grp, gt, moe_w_gu, moe_b_gu, moe_w_dn, moe_b_dn)
        pending = groups
    return settle(pending, mods[depth - 1])
```
